```python
import math
import jax
import jax.numpy as jnp
from jax import lax
import numpy as np

D_MODEL = 1024
BATCH = 8
SEQ = 2048
DEPTH = 2
DEC_BATCH = 32
DEC_SEQ = 8
PAST_LEN = 8192
PAGE_SIZE = 128

N_A_LAYERS = DEPTH // 2
N_B_LAYERS = DEPTH - N_A_LAYERS
N_DENSE_FFN = (DEPTH + 1) // 2
N_MOE_FFN = DEPTH // 2
N_MOD = 6

RET_HEADS = 4
RET_DK = D_MODEL // RET_HEADS
RET_DV = D_MODEL // RET_HEADS
RET_CHUNK = 128
ROPE_BASE = 10000.0

NSA_HEADS = 16
NSA_KV_HEADS = 4
NSA_GROUP = NSA_HEADS // NSA_KV_HEADS
NSA_HEAD_DIM = D_MODEL // NSA_HEADS
N_BRANCH = 3
N_KV_ROWS = 6
N_PAGED_ROWS = 4
CMP_BLOCK = 32
CMP_STRIDE = 16
CMP_HIDDEN = 2 * NSA_HEAD_DIM
SEL_BLOCK = 64
N_SEL = 16
FORCED_SCORE = 1.0e4
WINDOW = 512
Q_BLOCK = 128
SEL_Q_CHUNK = 16

FFN_DIM = 2816
N_EXPERTS = 8
TOP_K = 2
EXPERT_DIM = 3584
EPS = 1e-6

kernel_name = 'yoco_retnet_nsa_moe_decoder_step'


def rms_norm(x, g):
    xf = x.astype(jnp.float32)
    y = xf * lax.rsqrt(jnp.mean(xf * xf, axis=-1, keepdims=True) + EPS)
    return (y * g.astype(jnp.float32)).astype(x.dtype)


def modulate(x, shift, scale):
    return x * (1.0 + scale[:, None, :]) + shift[:, None, :]


def masked_softmax(s, mask):
    s = jnp.where(mask, s.astype(jnp.float32), -jnp.inf)
    m = jnp.max(s, axis=-1, keepdims=True)
    m = jnp.where(jnp.isfinite(m), m, 0.0)
    e = jnp.where(mask, jnp.exp(s - m), 0.0)
    return e / jnp.maximum(jnp.sum(e, axis=-1, keepdims=True), 1e-30)


def rotary(x, pos):
    half = x.shape[-1] // 2
    inv = ROPE_BASE ** (-jnp.arange(half, dtype=jnp.float32) / half)
    ang = pos.astype(jnp.float32)[:, None] * inv[None, :]
    cos = jnp.cos(ang)[None, :, None, :]
    sin = jnp.sin(ang)[None, :, None, :]
    xf = x.astype(jnp.float32)
    x1, x2 = xf[..., :half], xf[..., half:]
    return jnp.concatenate([x1 * cos - x2 * sin, x2 * cos + x1 * sin], axis=-1).astype(x.dtype)


def retention_scan(q, k, v, s0, chunk):
    b, t, nh, dk = q.shape
    dv = v.shape[-1]
    n = t // chunk
    f32 = jnp.float32
    log_g = jnp.log(1.0 - jnp.exp(jnp.linspace(math.log(1.0 / 32), math.log(1.0 / 512), nh, dtype=f32)))
    i = jnp.arange(chunk, dtype=f32)
    diff = i[:, None] - i[None, :]
    dmat = jnp.where(diff >= 0, jnp.exp(log_g[:, None, None] * jnp.maximum(diff, 0.0)), 0.0)
    xi = jnp.exp(log_g[:, None] * (i + 1.0))
    zeta = jnp.exp(log_g[:, None] * (chunk - 1.0 - i))
    g_chunk = jnp.exp(log_g * chunk)[None, :, None, None]
    qc = q.astype(f32).reshape(b, n, chunk, nh, dk)
    kc = k.astype(f32).reshape(b, n, chunk, nh, dk)
    vc = v.astype(f32).reshape(b, n, chunk, nh, dv)
    inner = jnp.einsum('bnihd,bnjhd->bnhij', qc, kc) * dmat
    o_inner = jnp.einsum('bnhij,bnjhe->bnihe', inner, vc)
    upd = jnp.einsum('bnjhd,hj,bnjhe->nbhde', kc, zeta, vc)

    def step(s, u):
        return g_chunk * s + u, s

    s_last, s_prev = lax.scan(step, s0.astype(f32), upd)
    o_cross = jnp.einsum('bnihd,nbhde,hi->bnihe', qc, s_prev, xi)
    return (o_inner + o_cross).reshape(b, t, nh, dv), s_last.astype(s0.dtype)


def retention_layer(h, pos, s0, w_in, gn_g, w_out):
    b, t, _ = h.shape
    nq = RET_HEADS * RET_DK
    nv = RET_HEADS * RET_DV
    proj = h @ w_in
    q = rotary(proj[..., :nq].reshape(b, t, RET_HEADS, RET_DK), pos)
    k = rotary(proj[..., nq:2 * nq].reshape(b, t, RET_HEADS, RET_DK), pos) * (RET_DK ** -0.5)
    v = proj[..., 2 * nq:2 * nq + nv].reshape(b, t, RET_HEADS, RET_DV)
    gate = proj[..., 2 * nq + nv:]
    o, s = retention_scan(q, k, v, s0, math.gcd(t, RET_CHUNK))
    mu = jnp.mean(o, axis=-1, keepdims=True)
    var = jnp.mean(jnp.square(o - mu), axis=-1, keepdims=True)
    o = ((o - mu) * lax.rsqrt(var + EPS)).reshape(b, t, nv) * gn_g.astype(jnp.float32)
    o = jax.nn.silu(gate.astype(jnp.float32)) * o
    return o.astype(h.dtype) @ w_out, s


def shared_kv_rows(x, sc, w_mod, b_mod, norm_g, w_kv):
    b, t, _ = x.shape
    shift, scale = jnp.split(sc @ w_mod + b_mod, 2, axis=-1)
    h = modulate(rms_norm(x, norm_g), shift, scale)
    return (h @ w_kv).reshape(b, t, N_KV_ROWS, NSA_KV_HEADS, NSA_HEAD_DIM)


def compress_blocks(k, pos_emb, w1, b1, w2):
    b, tk, g, d = k.shape
    ratio = CMP_BLOCK // CMP_STRIDE
    n_sub = -(-tk // CMP_STRIDE)
    n_cmp = n_sub - ratio + 1
    sub = jnp.pad(k, ((0, 0), (0, n_sub * CMP_STRIDE - tk), (0, 0), (0, 0))).reshape(b, n_sub, CMP_STRIDE, g, d)
    w1 = w1.reshape(CMP_BLOCK, d, CMP_HIDDEN)
    hid = b1
    for o in range(ratio):
        sl = slice(o * CMP_STRIDE, (o + 1) * CMP_STRIDE)
        part = jnp.einsum('bnsgd,sdh->bngh', sub + pos_emb[sl][None, None, :, None, :], w1[sl])
        hid = hid + part[:, o:o + n_cmp]
    return jax.nn.gelu(hid) @ w2


def block_cover_matrix(n_cmp, n_sel):
    i = np.arange(n_cmp)[:, None]
    j = np.arange(n_sel)[None, :]
    cover = (i * CMP_STRIDE < (j + 1) * SEL_BLOCK) & (i * CMP_STRIDE + CMP_BLOCK > j * SEL_BLOCK)
    return cover.astype(np.float32)


def selected_attention(q, k, v, idx, ok, qpos):
    b, t, g, r, d = q.shape
    tk = k.shape[1]
    n_sel = -(-tk // SEL_BLOCK)
    n_top = idx.shape[-1]
    pad = ((0, 0), (0, n_sel * SEL_BLOCK - tk), (0, 0), (0, 0))
    kb = jnp.pad(k, pad).reshape(b, n_sel, SEL_BLOCK, g, d).transpose(0, 3, 1, 2, 4)
    vb = jnp.pad(v, pad).reshape(b, n_sel, SEL_BLOCK, g, d).transpose(0, 3, 1, 2, 4)
    qc = math.gcd(t, SEL_Q_CHUNK)
    nc = t // qc
    q_c = q.reshape(b, nc, qc, g, r, d).transpose(1, 0, 2, 3, 4, 5)
    idx_c = idx.reshape(b, g, nc, qc, n_top).transpose(2, 0, 1, 3, 4)
    ok_c = ok.reshape(b, g, nc, qc, n_top).transpose(2, 0, 1, 3, 4)
    pos_c = qpos.reshape(nc, qc)
    offs = jnp.arange(SEL_BLOCK)
    gather = jax.vmap(jax.vmap(lambda blocks, ids: blocks[ids]))

    def one(args):
        qq, ii, oo, pp = args
        flat = ii.reshape(b, g, qc * n_top)
        kg = gather(kb, flat).reshape(b, g, qc, n_top, SEL_BLOCK, d)
        vg = gather(vb, flat).reshape(b, g, qc, n_top, SEL_BLOCK, d)
        tok = ii[..., None] * SEL_BLOCK + offs
        mask = oo[..., None] & (tok <= pp[None, None, :, None, None])
        s = jnp.einsum('bqgrd,bgqkld->bgrqkl', qq, kg).reshape(b, g, r, qc, n_top * SEL_BLOCK)
        p = masked_softmax(s, mask.reshape(b, g, 1, qc, n_top * SEL_BLOCK))
        p = p.reshape(b, g, r, qc, n_top, SEL_BLOCK).astype(vg.dtype)
        return jnp.einsum('bgrqkl,bgqkld->bqgrd', p, vg)

    out = lax.map(one, (q_c, idx_c, ok_c, pos_c))
    return out.transpose(1, 0, 2, 3, 4, 5).reshape(b, t, g, r, d)


def window_attention(q, k, v, kpos, qpos, qb):
    b, t, g, r, d = q.shape
    nb = t // qb
    span = k.shape[1] - (nb - 1) * qb
    q_b = q.reshape(b, nb, qb, g, r, d).transpose(1, 0, 2, 3, 4, 5)

    def one(args):
        qq, pp, st = args
        kk = lax.dynamic_slice_in_dim(k, st, span, axis=1)
        vv = lax.dynamic_slice_in_dim(v, st, span, axis=1)
        kp = lax.dynamic_slice_in_dim(kpos, st, span)[None, :]
        pq = pp[:, None]
        mask = (kp <= pq) & (kp > pq - WINDOW) & (kp >= 0)
        p = masked_softmax(jnp.einsum('bqgrd,bkgd->bgrqk', qq, kk), mask)
        return jnp.einsum('bgrqk,bkgd->bqgrd', p.astype(vv.dtype), vv)

    out = lax.map(one, (q_b, qpos.reshape(nb, qb), jnp.arange(nb) * qb))
    return out.transpose(1, 0, 2, 3, 4, 5).reshape(b, t, g, r, d)


def nsa_layer(h, qpos, kv_full, k_cmp, v_cmp, win_all, win_pos, win_qb, w_in, w_out):
    b, t, _ = h.shape
    nh, g, r, d = NSA_HEADS, NSA_KV_HEADS, NSA_GROUP, NSA_HEAD_DIM
    proj = h @ w_in
    q = (proj[..., :nh * d] * (d ** -0.5)).reshape(b, t, g, r, d)
    gates = jax.nn.sigmoid(proj[..., nh * d:].astype(jnp.float32)).reshape(b, t, g, r, N_BRANCH)
    tk = kv_full.shape[1]
    n_cmp = k_cmp.shape[1]
    cmp_end = jnp.arange(n_cmp) * CMP_STRIDE + CMP_BLOCK - 1
    p_cmp = masked_softmax(jnp.einsum('btgrd,bngd->bgrtn', q, k_cmp), cmp_end[None, :] <= qpos[:, None])
    o_cmp = jnp.einsum('bgrtn,bngd->btgrd', p_cmp.astype(v_cmp.dtype), v_cmp)
    n_sel = -(-tk // SEL_BLOCK)
    imp = jnp.einsum('bgrtn,nj->bgtj', p_cmp, jnp.asarray(block_cover_matrix(n_cmp, n_sel)))
    blk = jnp.arange(n_sel)[None, :]
    cur = (qpos // SEL_BLOCK)[:, None]
    forced = (blk == 0) | (blk == cur) | (blk == cur - 1)
    score = jnp.where(blk <= cur, jnp.where(forced, FORCED_SCORE, imp), -jnp.inf)
    top_score, top_idx = lax.top_k(score, min(N_SEL, n_sel))
    o_sel = selected_attention(q, kv_full[:, :, 2], kv_full[:, :, 3], top_idx, jnp.isfinite(top_score), qpos)
    o_win = window_attention(q, win_all[:, :, 0], win_all[:, :, 1], win_pos, qpos, win_qb)
    o = gates[..., 0:1] * o_cmp + gates[..., 1:2] * o_sel + gates[..., 2:3] * o_win
    return o.reshape(b, t, nh * d).astype(h.dtype) @ w_out


def swiglu(h, w_gu, w_down):
    gte, up = jnp.split(h @ w_gu, 2, axis=-1)
    return (jax.nn.silu(gte) * up) @ w_down


def moe_ffn(h, router_w, router_b, w_gu, w_down):
    logits = (h @ router_w + router_b).astype(jnp.float32)
    top_v, top_i = lax.top_k(logits, TOP_K)
    weights = jax.nn.softmax(top_v, axis=-1)
    gate = jnp.sum(jax.nn.one_hot(top_i, N_EXPERTS, dtype=jnp.float32) * weights[..., None], axis=-2).astype(h.dtype)
    out = jnp.zeros_like(h)
    for e in range(N_EXPERTS):
        out = out + gate[..., e:e + 1] * swiglu(h, w_gu[e], w_down[e])
    return out


def run_group(x, c, pos, ret_s0, past_kv, past_win, p):
    b, t, _ = x.shape
    sc = jax.nn.silu(c)
    new_ret = []
    kv_new = win_keep = None
    kv_full = k_cmp = v_cmp = win_all = win_pos = None
    win_qb = t
    for layer in range(DEPTH):
        mod = (sc @ p['w_mod'][layer] + p['b_mod'][layer]).reshape(b, N_MOD, D_MODEL)
        g = p['norm_g'][layer]
        h = modulate(rms_norm(x, g[0]), mod[:, 0], mod[:, 1])
        if layer < N_A_LAYERS:
            o, s = retention_layer(h, pos, ret_s0[layer], p['ret_w_in'][layer], p['ret_gn_g'][layer], p['ret_w_out'][layer])
            new_ret.append(s)
        else:
            if layer == N_A_LAYERS:
                rows = shared_kv_rows(x, sc, p['kv_w_mod'], p['kv_b_mod'], p['kv_norm_g'], p['kv_w'])
                kv_new = rows[:, :, :N_PAGED_ROWS]
                win_new = rows[:, :, N_PAGED_ROWS:]
                if past_kv is None:
                    kv_full = kv_new
                    win_all = jnp.pad(win_new, ((0, 0), (WINDOW, 0), (0, 0), (0, 0), (0, 0)))
                    win_pos = jnp.arange(-WINDOW, t)
                    win_qb = math.gcd(t, Q_BLOCK)
                    win_keep = win_new[:, t - min(WINDOW, t):]
                else:
                    n_past, n_win = past_kv.shape[1], past_win.shape[1]
                    kv_full = jnp.concatenate([past_kv, kv_new], axis=1)
                    win_all = jnp.concatenate([past_win, win_new], axis=1)
                    win_pos = n_past - n_win + jnp.arange(n_win + t)
                    win_qb = t
                    win_keep = win_all[:, t:]
                k_cmp = compress_blocks(kv_full[:, :, 0], p['cmp_pos'][0], p['cmp_w1'][0], p['cmp_b1'][0], p['cmp_w2'][0])
                v_cmp = compress_blocks(kv_full[:, :, 1], p['cmp_pos'][1], p['cmp_w1'][1], p['cmp_b1'][1], p['cmp_w2'][1])
            bl = layer - N_A_LAYERS
            o = nsa_layer(h, pos, kv_full, k_cmp, v_cmp, win_all, win_pos, win_qb, p['nsa_w_in'][bl], p['nsa_w_out'][bl])
        x = x + mod[:, 2][:, None] * rms_norm(o, g[1])
        h = modulate(rms_norm(x, g[2]), mod[:, 3], mod[:, 4])
        if layer % 2 == 0:
            f = swiglu(h, p['ffn_w_gu'][layer // 2], p['ffn_w_down'][layer // 2])
        else:
            f = moe_ffn(h, p['moe_router_w'][layer // 2], p['moe_router_b'][layer // 2], p['moe_w_gu'][layer // 2], p['moe_w_down'][layer // 2])
        x = x + mod[:, 5][:, None] * rms_norm(f, g[3])
    return x, jnp.stack(new_ret), kv_new, win_keep


def setup_inputs(seed: int = 0) -> dict:
    key = jax.random.key(seed)
    keys = iter(jax.random.split(key, 48))

    def nrm(shape, scale=1.0):
        return jax.random.normal(next(keys), shape, jnp.float32) * scale

    def gain(shape):
        return 1.0 + nrm(shape, 0.05)

    d = D_MODEL
    kvh, hd = NSA_KV_HEADS, NSA_HEAD_DIM
    n_pages = PAST_LEN // PAGE_SIZE
    n_pool = (DEC_BATCH * n_pages * 5) // 4
    win_len = min(WINDOW, PAST_LEN)
    page_table = jax.random.permutation(next(keys), n_pool)[:DEC_BATCH * n_pages].reshape(DEC_BATCH, n_pages).astype(jnp.int32)
    ret_in = 2 * RET_HEADS * RET_DK + RET_HEADS * RET_DV + d
    nsa_in = NSA_HEADS * hd + NSA_HEADS * N_BRANCH
    return {
        'x_prompt': nrm((BATCH, SEQ, d)),
        'x_sample': nrm((DEC_BATCH, DEC_SEQ, d)),
        'c_prompt': nrm((BATCH, d)),
        'c_sample': nrm((DEC_BATCH, d)),
        'state_ret': nrm((N_A_LAYERS, DEC_BATCH, RET_HEADS, RET_DK, RET_DV), 0.5),
        'cache_kv': nrm((n_pool, PAGE_SIZE, N_PAGED_ROWS, kvh, hd)),
        'cache_win': nrm((DEC_BATCH, win_len, N_KV_ROWS - N_PAGED_ROWS, kvh, hd)),
        'page_table': page_table,
        'w_mod': nrm((DEPTH, d, N_MOD * d), 0.5 * d ** -0.5),
        'b_mod': nrm((DEPTH, N_MOD * d), 0.01),
        'norm_g': gain((DEPTH, 4, d)),
        'ret_w_in': nrm((N_A_LAYERS, d, ret_in), d ** -0.5),
        'ret_gn_g': gain((N_A_LAYERS, RET_HEADS * RET_DV)),
        'ret_w_out': nrm((N_A_LAYERS, RET_HEADS * RET_DV, d), (RET_HEADS * RET_DV) ** -0.5),
        'kv_w_mod': nrm((d, 2 * d), 0.5 * d ** -0.5),
        'kv_b_mod': nrm((2 * d,), 0.01),
        'kv_norm_g': gain((d,)),
        'kv_w': nrm((d, N_KV_ROWS * kvh * hd), d ** -0.5),
        'cmp_pos': nrm((2, CMP_BLOCK, hd), 0.1),
        'cmp_w1': nrm((2, CMP_BLOCK * hd, CMP_HIDDEN), (CMP_BLOCK * hd) ** -0.5),
        'cmp_b1': nrm((2, CMP_HIDDEN), 0.01),
        'cmp_w2': nrm((2, CMP_HIDDEN, hd), CMP_HIDDEN ** -0.5),
        'nsa_w_in': nrm((N_B_LAYERS, d, nsa_in), d ** -0.5),
        'nsa_w_out': nrm((N_B_LAYERS, NSA_HEADS * hd, d), (NSA_HEADS * hd) ** -0.5),
        'ffn_w_gu': nrm((N_DENSE_FFN, d, 2 * FFN_DIM), d ** -0.5),
        'ffn_w_down': nrm((N_DENSE_FFN, FFN_DIM, d), FFN_DIM ** -0.5),
        'moe_router_w': nrm((N_MOE_FFN, d, N_EXPERTS), d ** -0.5),
        'moe_router_b': nrm((N_MOE_FFN, N_EXPERTS), 0.01),
        'moe_w_gu': nrm((N_MOE_FFN, N_EXPERTS, d, 2 * EXPERT_DIM), d ** -0.5),
        'moe_w_down': nrm((N_MOE_FFN, N_EXPERTS, EXPERT_DIM, d), EXPERT_DIM ** -0.5),
    }


def reference(x_prompt, x_sample, c_prompt, c_sample, state_ret, cache_kv, cache_win, page_table, w_mod, b_mod, norm_g, ret_w_in, ret_gn_g, ret_w_out, kv_w_mod, kv_b_mod, kv_norm_g, kv_w, cmp_pos, cmp_w1, cmp_b1, cmp_w2, nsa_w_in, nsa_w_out, ffn_w_gu, ffn_w_down, moe_router_w, moe_router_b, moe_w_gu, moe_w_down):
    params = {
        'w_mod': w_mod, 'b_mod': b_mod, 'norm_g': norm_g,
        'ret_w_in': ret_w_in, 'ret_gn_g': ret_gn_g, 'ret_w_out': ret_w_out,
        'kv_w_mod': kv_w_mod, 'kv_b_mod': kv_b_mod, 'kv_norm_g': kv_norm_g, 'kv_w': kv_w,
        'cmp_pos': cmp_pos, 'cmp_w1': cmp_w1, 'cmp_b1': cmp_b1, 'cmp_w2': cmp_w2,
        'nsa_w_in': nsa_w_in, 'nsa_w_out': nsa_w_out,
        'ffn_w_gu': ffn_w_gu, 'ffn_w_down': ffn_w_down,
        'moe_router_w': moe_router_w, 'moe_router_b': moe_router_b, 'moe_w_gu': moe_w_gu, 'moe_w_down': moe_w_down,
    }
    b, t, _ = x_prompt.shape
    db, dt, _ = x_sample.shape
    ret0 = jnp.zeros((N_A_LAYERS, b, RET_HEADS, RET_DK, RET_DV), x_prompt.dtype)
    y_prompt, ret_prompt, kv_prompt, win_prompt = run_group(x_prompt, c_prompt, jnp.arange(t), ret0, None, None, params)
    n_pages = page_table.shape[1]
    past_len = n_pages * cache_kv.shape[1]
    past_kv = cache_kv[page_table].reshape(db, past_len, N_PAGED_ROWS, NSA_KV_HEADS, NSA_HEAD_DIM)
    y_sample, ret_sample, kv_sample, win_sample = run_group(x_sample, c_sample, past_len + jnp.arange(dt), state_ret, past_kv, cache_win, params)
    return (y_prompt, y_sample, ret_prompt, ret_sample, kv_prompt, kv_sample, win_prompt, win_sample)
```

```python
import functools
import math

import numpy as np
import jax
import jax.numpy as jnp
from jax import lax
from jax.experimental import pallas as pl
from jax.experimental.pallas import tpu as pltpu

F32 = jnp.float32
BF16 = jnp.bfloat16
I32 = jnp.int32

D_MODEL = 1024
N_MOD = 6
RET_HEADS = 4
RET_DK = 256
RET_DV = 256
RET_CHUNK = 128
ROPE_BASE = 10000.0
NSA_HEADS = 16
NSA_KV_HEADS = 4
NSA_GROUP = 4
NSA_HEAD_DIM = 64
N_BRANCH = 3
N_KV_ROWS = 6
N_PAGED_ROWS = 4
CMP_BLOCK = 32
CMP_STRIDE = 16
CMP_HIDDEN = 128
SEL_BLOCK = 64
N_SEL = 16
FORCED_SCORE = 1.0e4
WINDOW = 512
Q_BLOCK = 128
N_EXPERTS = 8
TOP_K = 2
EPS = 1e-6

NEG = -1.0e30
KVW = NSA_KV_HEADS * NSA_HEAD_DIM
VMEM_LIMIT_BYTES = 56 * 1024 * 1024
HIGHEST = lax.Precision.HIGHEST


def _cparams(*sem):
    return pltpu.CompilerParams(dimension_semantics=sem, vmem_limit_bytes=VMEM_LIMIT_BYTES)


def _sigmoid(x):
    return 1.0 / (1.0 + jnp.exp(-x))


def _silu(x):
    return x * _sigmoid(x)


def _gelu_tanh(x):
    return x * (0.5 * (1.0 + jnp.tanh(math.sqrt(2.0 / math.pi) * (x + 0.044715 * (x * x * x)))))


def _norm_mod(x, g, shift, scale):
    ms = jnp.mean(x * x, axis=-1, keepdims=True)
    return (x * lax.rsqrt(ms + EPS) * g) * (1.0 + scale) + shift


def _rms_residual(x, gate, o, g):
    ms = jnp.mean(o * o, axis=-1, keepdims=True)
    return x + gate * (o * lax.rsqrt(ms + EPS) * g)


def _dot(a, b):
    return jnp.dot(a, b, preferred_element_type=F32)


def _dot_nt(a, b):
    return lax.dot_general(a, b, (((1,), (1,)), ((), ())), preferred_element_type=F32)


def _masked_softmax(s, mask):
    sm = jnp.where(mask, s, NEG)
    m = jnp.max(sm, axis=-1, keepdims=True)
    e = jnp.where(mask, jnp.exp(sm - m), 0.0)
    return e / jnp.maximum(jnp.sum(e, axis=-1, keepdims=True), 1e-30)


def _flash_init(m_ref, l_ref, acc_ref):
    m_ref[...] = jnp.full(m_ref.shape, NEG, F32)
    l_ref[...] = jnp.zeros(l_ref.shape, F32)
    acc_ref[...] = jnp.zeros(acc_ref.shape, F32)


def _flash_update(s, mask, v_b, m_ref, l_ref, acc_ref):
    sm = jnp.where(mask, s, NEG)
    m_old = m_ref[...]
    m_new = jnp.maximum(m_old, jnp.max(sm, axis=-1, keepdims=True))
    alpha = jnp.exp(m_old - m_new)
    p = jnp.where(mask, jnp.exp(sm - m_new), 0.0)
    l_ref[...] = alpha * l_ref[...] + jnp.sum(p, axis=-1, keepdims=True)
    acc_ref[...] = alpha * acc_ref[...] + _dot(p.astype(BF16), v_b)
    m_ref[...] = m_new


def _flash_result(l_ref, acc_ref):
    return acc_ref[...] / jnp.maximum(l_ref[...], 1e-30)


def _select_blocks(imp, cur, n_sel):
    rows, lanes = imp.shape
    blk = lax.broadcasted_iota(I32, (rows, lanes), 1)
    valid = (blk <= cur) & (blk < n_sel)
    forced = (blk == 0) | (blk == cur) | (blk == cur - 1)
    score = jnp.where(valid, jnp.where(forced, FORCED_SCORE, imp), -1.0)
    rank = jnp.zeros((rows, lanes), F32)
    for i in range(n_sel):
        ci = score[:, i:i + 1]
        beats = (ci > score) | ((ci == score) & (blk > i))
        rank = rank + jnp.where(beats, 1.0, 0.0)
    return jnp.where(valid & (rank < float(min(N_SEL, n_sel))), 1.0, 0.0)


def _cond_kernel(c_ref, w_ref, b_ref, o_ref):
    sc = _silu(c_ref[...])
    o_ref[...] = _dot(sc.astype(BF16), w_ref[...].astype(BF16)) + b_ref[...]


def cond_matmul(c, w, b):
    bc, d = c.shape
    n = w.shape[1]
    tn = 1024
    return pl.pallas_call(
        _cond_kernel,
        grid=(n // tn,),
        in_specs=[pl.BlockSpec((bc, d), lambda j: (0, 0)),
                  pl.BlockSpec((d, tn), lambda j: (0, j)),
                  pl.BlockSpec((1, tn), lambda j: (0, j))],
        out_specs=pl.BlockSpec((bc, tn), lambda j: (0, j)),
        out_shape=jax.ShapeDtypeStruct((bc, n), F32),
        compiler_params=_cparams("parallel"),
        name="cond_matmul",
    )(c, w, b.reshape(1, n))


def _mod_arg(m, per_token, tm, tiles_per_batch):
    d = m.shape[-1]
    if per_token:
        return m, pl.BlockSpec((tm, d), lambda i, *_: (i, 0))
    return m[:, None, :], pl.BlockSpec((None, 1, d), lambda i, *_: (i // tiles_per_batch, 0, 0))


def _vec_spec(d):
    return pl.BlockSpec((1, d), lambda i, *_: (0, 0))


def _nmm_kernel(x_ref, g_ref, sh_ref, sc_ref, w_ref, o_ref, h_ref):
    @pl.when(pl.program_id(1) == 0)
    def _():
        h_ref[...] = _norm_mod(x_ref[...], g_ref[...], sh_ref[...], sc_ref[...]).astype(BF16)

    o_ref[...] = _dot(h_ref[...], w_ref[...]).astype(o_ref.dtype)


def norm_mod_matmul(x, g, shift, scale, w_b, *, per_token, tm, tpb, tn=None):
    m, d = x.shape
    n = w_b.shape[1]
    tn = n if tn is None else tn
    sh, sh_spec = _mod_arg(shift, per_token, tm, tpb)
    sc, sc_spec = _mod_arg(scale, per_token, tm, tpb)
    return pl.pallas_call(
        _nmm_kernel,
        grid=(m // tm, n // tn),
        in_specs=[pl.BlockSpec((tm, d), lambda i, j: (i, 0)), _vec_spec(d), sh_spec, sc_spec,
                  pl.BlockSpec((d, tn), lambda i, j: (0, j))],
        out_specs=pl.BlockSpec((tm, tn), lambda i, j: (i, j)),
        out_shape=jax.ShapeDtypeStruct((m, n), F32),
        scratch_shapes=[pltpu.VMEM((tm, d), BF16)],
        compiler_params=_cparams("parallel", "arbitrary"),
        name="norm_mod_matmul",
    )(x, g.reshape(1, d), sh, sc, w_b)


def _mnr_kernel(a_ref, w_ref, x_ref, gate_ref, g_ref, o_ref):
    o = _dot(a_ref[...].astype(BF16), w_ref[...])
    o_ref[...] = _rms_residual(x_ref[...], gate_ref[...], o, g_ref[...])


def matmul_norm_residual(a, w_b, x, gate, g, *, per_token, tm, tpb):
    m, k = a.shape
    d = w_b.shape[1]
    gt, gt_spec = _mod_arg(gate, per_token, tm, tpb)
    return pl.pallas_call(
        _mnr_kernel,
        grid=(m // tm,),
        in_specs=[pl.BlockSpec((tm, k), lambda i: (i, 0)),
                  pl.BlockSpec((k, d), lambda i: (0, 0)),
                  pl.BlockSpec((tm, d), lambda i: (i, 0)), gt_spec, _vec_spec(d)],
        out_specs=pl.BlockSpec((tm, d), lambda i: (i, 0)),
        out_shape=jax.ShapeDtypeStruct((m, d), F32),
        compiler_params=_cparams("parallel"),
        name="matmul_norm_residual",
    )(a, w_b, x, gt, g.reshape(1, d))


def _nr_kernel(y_ref, x_ref, gate_ref, g_ref, o_ref):
    o_ref[...] = _rms_residual(x_ref[...], gate_ref[...], y_ref[...], g_ref[...])


def norm_residual(y, x, gate, g, *, per_token, tm, tpb):
    m, d = x.shape
    gt, gt_spec = _mod_arg(gate, per_token, tm, tpb)
    return pl.pallas_call(
        _nr_kernel,
        grid=(m // tm,),
        in_specs=[pl.BlockSpec((tm, d), lambda i: (i, 0)),
                  pl.BlockSpec((tm, d), lambda i: (i, 0)), gt_spec, _vec_spec(d)],
        out_specs=pl.BlockSpec((tm, d), lambda i: (i, 0)),
        out_shape=jax.ShapeDtypeStruct((m, d), F32),
        compiler_params=_cparams("parallel"),
        name="norm_residual",
    )(y, x, gt, g.reshape(1, d))


def _ffn_kernel(x_ref, g2_ref, sh_ref, sc_ref, wg_ref, wu_ref, wd_ref, gate_ref, g3_ref, o_ref,
                h_ref, acc_ref):
    f = pl.program_id(1)

    @pl.when(f == 0)
    def _():
        h_ref[...] = _norm_mod(x_ref[...], g2_ref[...], sh_ref[...], sc_ref[...]).astype(BF16)
        acc_ref[...] = jnp.zeros(acc_ref.shape, F32)

    h = h_ref[...]
    act = _silu(_dot(h, wg_ref[...])) * _dot(h, wu_ref[...])
    acc_ref[...] += _dot(act.astype(BF16), wd_ref[...])

    @pl.when(f == pl.num_programs(1) - 1)
    def _():
        o_ref[...] = _rms_residual(x_ref[...], gate_ref[...], acc_ref[...], g3_ref[...])


def ffn_sublayer(x, g2, shift, scale, w_gu_b, w_down_b, gate, g3, *, per_token, tm, tpb, tf):
    m, d = x.shape
    fdim = w_down_b.shape[0]
    nf = fdim // tf
    sh, sh_spec = _mod_arg(shift, per_token, tm, tpb)
    sc, sc_spec = _mod_arg(scale, per_token, tm, tpb)
    gt, gt_spec = _mod_arg(gate, per_token, tm, tpb)
    return pl.pallas_call(
        _ffn_kernel,
        grid=(m // tm, nf),
        in_specs=[pl.BlockSpec((tm, d), lambda i, f: (i, 0)), _vec_spec(d), sh_spec, sc_spec,
                  pl.BlockSpec((d, tf), lambda i, f: (0, f)),
                  pl.BlockSpec((d, tf), lambda i, f: (0, nf + f)),
                  pl.BlockSpec((tf, d), lambda i, f: (f, 0)),
                  gt_spec, _vec_spec(d)],
        out_specs=pl.BlockSpec((tm, d), lambda i, f: (i, 0)),
        out_shape=jax.ShapeDtypeStruct((m, d), F32),
        scratch_shapes=[pltpu.VMEM((tm, d), BF16), pltpu.VMEM((tm, d), F32)],
        compiler_params=_cparams("parallel", "arbitrary"),
        name="ffn_sublayer",
    )(x, g2.reshape(1, d), sh, sc, w_gu_b, w_gu_b, w_down_b, gt, g3.reshape(1, d))


def _ret_kernel(q_ref, k_ref, v_ref, gt_ref, cos_ref, sin_ref, dm_ref, xi_ref, zt_ref, gc_ref,
                gn_ref, s0_ref, o_ref, s_ref):
    @pl.when(pl.program_id(2) == 0)
    def _():
        s_ref[...] = s0_ref[...]

    cos = cos_ref[...]
    sin = sin_ref[...]
    half = RET_DK // 2

    def rot(x):
        x1 = x[:, :half]
        x2 = x[:, half:]
        return jnp.concatenate([x1 * cos - x2 * sin, x2 * cos + x1 * sin], axis=-1)

    q = rot(q_ref[...])
    k = rot(k_ref[...]) * (RET_DK ** -0.5)
    qb = q.astype(BF16)
    kb = k.astype(BF16)
    vb = v_ref[...].astype(BF16)
    state = s_ref[...]
    inner = _dot_nt(qb, kb) * dm_ref[...]
    o = _dot(inner.astype(BF16), vb) + _dot(qb, state.astype(BF16)) * xi_ref[...]
    kz = (k * zt_ref[...]).astype(BF16)
    upd = lax.dot_general(kz, vb, (((0,), (0,)), ((), ())), preferred_element_type=F32)
    s_ref[...] = gc_ref[...] * state + upd
    mu = jnp.mean(o, axis=-1, keepdims=True)
    dev = o - mu
    var = jnp.mean(dev * dev, axis=-1, keepdims=True)
    on = dev * lax.rsqrt(var + EPS) * gn_ref[...]
    o_ref[...] = _silu(gt_ref[...]) * on


def _ret_tables(chunk, rows):
    h = RET_HEADS
    log_g = jnp.log(1.0 - jnp.exp(jnp.linspace(math.log(1.0 / 32), math.log(1.0 / 512), h, dtype=F32)))
    i = jnp.arange(chunk, dtype=F32)
    diff = i[:, None] - i[None, :]
    dmat = jnp.where(diff >= 0, jnp.exp(log_g[:, None, None] * jnp.maximum(diff, 0.0)), 0.0)
    xi = jnp.exp(log_g[:, None] * (i + 1.0))
    zeta = jnp.exp(log_g[:, None] * (chunk - 1.0 - i))
    gch = jnp.exp(log_g * chunk)
    pad = rows - chunk
    dmat = jnp.pad(dmat, ((0, 0), (0, pad), (0, pad)))
    xi = jnp.pad(xi, ((0, 0), (0, pad)))[..., None]
    zeta = jnp.pad(zeta, ((0, 0), (0, pad)))[..., None]
    gch = jnp.broadcast_to(gch[:, None, None], (h, 1, RET_DV))
    return dmat, xi, zeta, gch


def _rope_tables(pos, rows):
    half = RET_DK // 2
    inv = ROPE_BASE ** (-jnp.arange(half, dtype=F32) / half)
    ang = pos.astype(F32)[:, None] * inv[None, :]
    pad = rows - pos.shape[0]
    return jnp.pad(jnp.cos(ang), ((0, pad), (0, 0))), jnp.pad(jnp.sin(ang), ((0, pad), (0, 0)))


def retention(proj, pos, s0, gn_g, chunk):
    b, t, _ = proj.shape
    c = RET_CHUNK
    n = t // c
    h = RET_HEADS
    dmat, xi, zeta, gch = _ret_tables(chunk, c)
    cos, sin = _rope_tables(pos, t)
    col = lambda off: pl.BlockSpec((None, c, RET_DK), lambda bi, hi, ni: (bi, ni, off + hi))
    o, s = pl.pallas_call(
        _ret_kernel,
        grid=(b, h, n),
        in_specs=[col(0), col(h), col(2 * h), col(3 * h),
                  pl.BlockSpec((c, RET_DK // 2), lambda bi, hi, ni: (ni, 0)),
                  pl.BlockSpec((c, RET_DK // 2), lambda bi, hi, ni: (ni, 0)),
                  pl.BlockSpec((None, c, c), lambda bi, hi, ni: (hi, 0, 0)),
                  pl.BlockSpec((None, c, 1), lambda bi, hi, ni: (hi, 0, 0)),
                  pl.BlockSpec((None, c, 1), lambda bi, hi, ni: (hi, 0, 0)),
                  pl.BlockSpec((None, 1, RET_DV), lambda bi, hi, ni: (hi, 0, 0)),
                  pl.BlockSpec((1, RET_DV), lambda bi, hi, ni: (0, hi)),
                  pl.BlockSpec((None, None, RET_DK, RET_DV), lambda bi, hi, ni: (bi, hi, 0, 0))],
        out_specs=[pl.BlockSpec((None, c, RET_DV), lambda bi, hi, ni: (bi, ni, hi)),
                   pl.BlockSpec((None, None, RET_DK, RET_DV), lambda bi, hi, ni: (bi, hi, 0, 0))],
        out_shape=[jax.ShapeDtypeStruct((b, t, h * RET_DV), F32),
                   jax.ShapeDtypeStruct((b, h, RET_DK, RET_DV), F32)],
        compiler_params=_cparams("parallel", "parallel", "arbitrary"),
        name="retention",
    )(proj, proj, proj, proj, cos, sin, dmat, xi, zeta, gch, gn_g.reshape(1, -1), s0)
    return o, s


def _compress_weights(cmp_pos, cmp_w1, cmp_b1, cmp_w2):
    g, d, hd = NSA_KV_HEADS, NSA_HEAD_DIM, CMP_HIDDEN
    eye = jnp.eye(g, dtype=F32)
    w1 = cmp_w1.reshape(2, CMP_BLOCK, d, hd)
    bd1 = jnp.einsum('ksdh,gj->ksgdjh', w1, eye).reshape(2, CMP_BLOCK, g * d, g * hd).astype(BF16)
    bd2 = jnp.einsum('khd,gj->kghjd', cmp_w2, eye).reshape(2, g * hd, g * d).astype(BF16)
    pos = jnp.tile(cmp_pos[:, :, None, :], (1, 1, g, 1)).reshape(2, CMP_BLOCK, 1, g * d)
    b1 = jnp.tile(cmp_b1[:, None, :], (1, g, 1)).reshape(2, 1, g * hd)
    return bd1, bd2, pos, b1


def _compress_finish(p0, p1, b1, bd2):
    rows = p0.shape[0]
    hid = b1 + p0 + pltpu.roll(p1, rows - 1, 0)
    return _dot(_gelu_tanh(hid).astype(BF16), bd2)


def _cmp_prompt_kernel(x_ref, bd1_ref, bd2_ref, pos_ref, b1_ref, o_ref, p0_ref, p1_ref):
    s = pl.program_id(2)

    @pl.when(s == 0)
    def _():
        p0_ref[...] = jnp.zeros(p0_ref.shape, F32)
        p1_ref[...] = jnp.zeros(p1_ref.shape, F32)

    xs = x_ref[...]
    p0_ref[...] += _dot((xs + pos_ref[0]).astype(BF16), bd1_ref[0])
    p1_ref[...] += _dot((xs + pos_ref[1]).astype(BF16), bd1_ref[1])

    @pl.when(s == pl.num_programs(2) - 1)
    def _():
        o_ref[...] = _compress_finish(p0_ref[...], p1_ref[...], b1_ref[...], bd2_ref[...])


def compress_prompt(rows, cw):
    bd1, bd2, pos, b1 = cw
    b, t, width = rows.shape
    n_sub = t // CMP_STRIDE
    kinds = width // KVW
    ratio = CMP_BLOCK // CMP_STRIDE
    hid = bd1.shape[-1]
    x = rows.reshape(b, n_sub, CMP_STRIDE * width)
    bd1 = bd1.reshape(2, ratio, CMP_STRIDE, KVW, hid)
    pos = pos.reshape(2, ratio, CMP_STRIDE, 1, KVW)
    return pl.pallas_call(
        _cmp_prompt_kernel,
        grid=(b, 2, CMP_STRIDE),
        in_specs=[pl.BlockSpec((None, n_sub, KVW), lambda bi, ki, si: (bi, 0, si * kinds + ki)),
                  pl.BlockSpec((None, ratio, None, KVW, hid), lambda bi, ki, si: (ki, 0, si, 0, 0)),
                  pl.BlockSpec((None,) + bd2.shape[1:], lambda bi, ki, si: (ki, 0, 0)),
                  pl.BlockSpec((None, ratio, None, 1, KVW), lambda bi, ki, si: (ki, 0, si, 0, 0)),
                  pl.BlockSpec((None,) + b1.shape[1:], lambda bi, ki, si: (ki, 0, 0))],
        out_specs=pl.BlockSpec((None, None, n_sub, KVW), lambda bi, ki, si: (ki, bi, 0, 0)),
        out_shape=jax.ShapeDtypeStruct((2, b, n_sub, KVW), F32),
        scratch_shapes=[pltpu.VMEM((n_sub, hid), F32), pltpu.VMEM((n_sub, hid), F32)],
        compiler_params=_cparams("parallel", "parallel", "arbitrary"),
        name="compress_prompt",
    )(x, bd1, bd2, pos, b1)


LANES = 128


def _cmp_sample_kernel(pt_ref, page_ref, new_ref, bd1_ref, bd2_ref, pos_ref, b1_ref, o_ref, sub_ref):
    p = pl.program_id(1)
    chunks = N_PAGED_ROWS * KVW // LANES
    per_page = page_ref.shape[0] // (chunks * CMP_STRIDE)
    n_cmp = o_ref.shape[1]
    row0 = pl.multiple_of(p * per_page, per_page)
    for kind in range(2):
        for s in range(CMP_STRIDE):
            for half in range(KVW // LANES):
                c = kind * (KVW // LANES) + half
                piece = page_ref[pl.ds(s * chunks + c, per_page, stride=CMP_STRIDE * chunks), :]
                sub_ref[kind, s, pl.ds(row0, per_page), half * LANES:(half + 1) * LANES] = piece

    @pl.when(p == pl.num_programs(1) - 1)
    def _():
        n_rows = sub_ref.shape[2]
        tail = n_rows - n_cmp
        hid = bd1_ref.shape[-1]
        for kind in range(2):
            p0 = jnp.zeros((n_rows, hid), F32)
            p1 = jnp.zeros((n_rows, hid), F32)
            for s in range(CMP_STRIDE):
                sub_ref[kind, s, pl.ds(n_cmp, tail), :] = jnp.broadcast_to(new_ref[kind, s:s + 1, :], (tail, KVW))
                xs = sub_ref[kind, s]
                p0 = p0 + _dot((xs + pos_ref[kind, s]).astype(BF16), bd1_ref[kind, s])
                p1 = p1 + _dot((xs + pos_ref[kind, CMP_STRIDE + s]).astype(BF16), bd1_ref[kind, CMP_STRIDE + s])
            o_ref[kind] = _compress_finish(p0, p1, b1_ref[kind], bd2_ref[kind])[:n_cmp]


def compress_sample(cache, page_table, new_sub, cw):
    bd1, bd2, pos, b1 = cw
    b, n_pages = page_table.shape
    n_pool, page, width = cache.shape
    n_cmp = n_pages * page // CMP_STRIDE
    pages = cache.reshape(n_pool, page * width // LANES, LANES)
    const = lambda a: pl.BlockSpec(a.shape, lambda bi, pi, pt: (0,) * a.ndim, pipeline_mode=pl.Buffered(1))
    grid_spec = pltpu.PrefetchScalarGridSpec(
        num_scalar_prefetch=1,
        grid=(b, n_pages),
        in_specs=[pl.BlockSpec((None,) + pages.shape[1:], lambda bi, pi, pt: (pt[bi, pi], 0, 0)),
                  pl.BlockSpec((2, None, CMP_STRIDE, KVW), lambda bi, pi, pt: (0, bi, 0, 0)),
                  const(bd1), const(bd2), const(pos), const(b1)],
        out_specs=pl.BlockSpec((2, None, n_cmp, KVW), lambda bi, pi, pt: (0, bi, 0, 0)),
        scratch_shapes=[pltpu.VMEM((2, CMP_STRIDE, n_cmp + 8, KVW), F32)],
    )
    return pl.pallas_call(
        _cmp_sample_kernel,
        grid_spec=grid_spec,
        out_shape=jax.ShapeDtypeStruct((2, b, n_cmp, KVW), F32),
        compiler_params=_cparams("parallel", "arbitrary"),
        name="compress_sample",
    )(page_table, pages, new_sub, bd1, bd2, pos, b1)


def _cover_matrix(n_cmp, n_sel, rows, cols):
    i = np.arange(n_cmp)[:, None]
    j = np.arange(n_sel)[None, :]
    cover = (i * CMP_STRIDE < (j + 1) * SEL_BLOCK) & (i * CMP_STRIDE + CMP_BLOCK > j * SEL_BLOCK)
    out = np.zeros((rows, cols), np.float32)
    out[:n_cmp, :n_sel] = cover
    return jnp.asarray(out)


SEL_CHUNK = 256


def _nsa_prompt_kernel(q_ref, gl_ref, kc_ref, vc_ref, cov_ref, ks_ref, vs_ref, kw_ref, vw_ref, o_ref,
                       m_ref, l_ref, acc_ref, out_ref, *, n_cmp, n_sel):
    r, tq, d = q_ref.shape
    rows = r * tq
    qi = pl.program_id(2)
    q0 = qi * tq
    qb = q_ref[...].reshape(rows, d).astype(BF16)
    gates = _sigmoid(gl_ref[...].reshape(rows, N_BRANCH))
    qpos_r = q0 + (lax.broadcasted_iota(I32, (rows, 1), 0) & (tq - 1))
    qpos = q0 + lax.broadcasted_iota(I32, (tq, 1), 0)

    ncp = kc_ref.shape[0]
    s = _dot_nt(qb, kc_ref[...].astype(BF16))
    nn = lax.broadcasted_iota(I32, (rows, ncp), 1)
    p_cmp = _masked_softmax(s, (nn * CMP_STRIDE + (CMP_BLOCK - 1) <= qpos_r) & (nn < n_cmp))
    out_ref[...] = gates[:, 0:1] * _dot(p_cmp.astype(BF16), vc_ref[...].astype(BF16))

    psum = jnp.sum(p_cmp.reshape(r, tq, ncp), axis=0)
    imp = jnp.dot(psum, cov_ref[...], precision=HIGHEST, preferred_element_type=F32)
    sel_b = _select_blocks(imp, qpos >> 6, n_sel).astype(BF16)
    nblk = sel_b.shape[1]

    _flash_init(m_ref, l_ref, acc_ref)

    def sel_body(c, carry):
        k0 = pl.multiple_of(c * SEL_CHUNK, SEL_CHUNK)
        kb = ks_ref[pl.ds(k0, SEL_CHUNK), :]
        vb = vs_ref[pl.ds(k0, SEL_CHUNK), :]
        sc = _dot_nt(qb, kb)
        jj = lax.broadcasted_iota(I32, (nblk, SEL_CHUNK), 0)
        tt = lax.broadcasted_iota(I32, (nblk, SEL_CHUNK), 1)
        expand = jnp.where(jj == ((k0 + tt) >> 6), 1.0, 0.0).astype(BF16)
        mtok = _dot(sel_b, expand)
        keypos = k0 + lax.broadcasted_iota(I32, (tq, SEL_CHUNK), 1)
        mk = jnp.where((mtok > 0.5) & (keypos <= qpos), 1.0, 0.0)
        mk = jnp.broadcast_to(mk[None], (r, tq, SEL_CHUNK)).reshape(rows, SEL_CHUNK)
        _flash_update(sc, mk > 0.5, vb, m_ref, l_ref, acc_ref)
        return carry

    lax.fori_loop(0, (q0 + tq + SEL_CHUNK - 1) // SEL_CHUNK, sel_body, 0)
    out_ref[...] += gates[:, 1:2] * _flash_result(l_ref, acc_ref)

    _flash_init(m_ref, l_ref, acc_ref)
    for c in range(WINDOW // tq + 1):
        ks0 = q0 - WINDOW + c * tq

        @pl.when(ks0 >= 0)
        def _():
            k0 = pl.multiple_of(ks0, tq)
            kb = kw_ref[pl.ds(k0, tq), :]
            vb = vw_ref[pl.ds(k0, tq), :]
            sc = _dot_nt(qb, kb)
            kp = k0 + lax.broadcasted_iota(I32, (rows, tq), 1)
            _flash_update(sc, (kp <= qpos_r) & (kp > qpos_r - WINDOW), vb, m_ref, l_ref, acc_ref)

    out = out_ref[...] + gates[:, 2:3] * _flash_result(l_ref, acc_ref)
    o_ref[...] = out.reshape(r, tq, d)


def nsa_prompt(qh, gl, kc, vc, ks, vs, kw, vw):
    b, g, r, t, d = qh.shape
    tq = Q_BLOCK
    ncp = kc.shape[2]
    n_cmp = t // CMP_STRIDE - 1
    n_sel = -(-t // SEL_BLOCK)
    cover = _cover_matrix(n_cmp, n_sel, ncp, 128)
    rows = r * tq
    full = lambda n: pl.BlockSpec((None, None, n, d), lambda bi, gi, qi: (bi, gi, 0, 0))
    return pl.pallas_call(
        functools.partial(_nsa_prompt_kernel, n_cmp=n_cmp, n_sel=n_sel),
        grid=(b, g, t // tq),
        in_specs=[pl.BlockSpec((None, None, r, tq, d), lambda bi, gi, qi: (bi, gi, 0, qi, 0)),
                  pl.BlockSpec((None, None, r, tq, N_BRANCH), lambda bi, gi, qi: (bi, gi, 0, qi, 0)),
                  full(ncp), full(ncp),
                  pl.BlockSpec(cover.shape, lambda bi, gi, qi: (0, 0)),
                  full(t), full(t), full(t), full(t)],
        out_specs=pl.BlockSpec((None, None, r, tq, d), lambda bi, gi, qi: (bi, gi, 0, qi, 0)),
        out_shape=jax.ShapeDtypeStruct((b, g, r, t, d), F32),
        scratch_shapes=[pltpu.VMEM((rows, 1), F32), pltpu.VMEM((rows, 1), F32),
                        pltpu.VMEM((rows, d), F32), pltpu.VMEM((rows, d), F32)],
        compiler_params=_cparams("parallel", "parallel", "arbitrary"),
        name="nsa_prompt",
    )(qh, gl, kc, vc, cover, ks, vs, kw, vw)


def _nsa_sample_select_kernel(q_ref, kc_ref, vc_ref, cov_ref, oc_ref, sel_ref, *, t, past, n_cmp, n_sel):
    rows = q_ref.shape[0]
    g, r = NSA_KV_HEADS, NSA_GROUP
    qb = q_ref[...].astype(BF16)
    qpos_r = past + (lax.broadcasted_iota(I32, (rows, 1), 0) & (t - 1))
    s = _dot_nt(qb, kc_ref[...].astype(BF16))
    nn = lax.broadcasted_iota(I32, s.shape, 1)
    p_cmp = _masked_softmax(s, (nn * CMP_STRIDE + (CMP_BLOCK - 1) <= qpos_r) & (nn < n_cmp))
    oc_ref[...] = _dot(p_cmp.astype(BF16), vc_ref[...].astype(BF16))
    psum = jnp.sum(p_cmp.reshape(g, r, t, s.shape[1]), axis=1).reshape(g * t, s.shape[1])
    imp = jnp.dot(psum, cov_ref[...], precision=HIGHEST, preferred_element_type=F32)
    qpos_gt = past + (lax.broadcasted_iota(I32, (g * t, 1), 0) & (t - 1))
    sel = _select_blocks(imp, qpos_gt >> 6, n_sel)
    lanes = sel.shape[1]
    sel_ref[...] = jnp.broadcast_to(sel.reshape(g, 1, t, lanes), (g, r, t, lanes)).reshape(rows, lanes)


def nsa_sample_select(q4, kc, vc, *, t, past):
    b, rows, _ = q4.shape
    n_cmp = kc.shape[1]
    n_sel = -(-(past + t) // SEL_BLOCK)
    lanes = -(-n_sel // 128) * 128
    cover = _cover_matrix(n_cmp, n_sel, n_cmp, lanes)
    return pl.pallas_call(
        functools.partial(_nsa_sample_select_kernel, t=t, past=past, n_cmp=n_cmp, n_sel=n_sel),
        grid=(b,),
        in_specs=[pl.BlockSpec((None, rows, KVW), lambda bi: (bi, 0, 0)),
                  pl.BlockSpec((None, n_cmp, KVW), lambda bi: (bi, 0, 0)),
                  pl.BlockSpec((None, n_cmp, KVW), lambda bi: (bi, 0, 0)),
                  pl.BlockSpec(cover.shape, lambda bi: (0, 0))],
        out_specs=[pl.BlockSpec((None, rows, KVW), lambda bi: (bi, 0, 0)),
                   pl.BlockSpec((None, rows, lanes), lambda bi: (bi, 0, 0))],
        out_shape=[jax.ShapeDtypeStruct((b, rows, KVW), F32),
                   jax.ShapeDtypeStruct((b, rows, lanes), F32)],
        compiler_params=_cparams("parallel"),
        name="nsa_sample_select",
    )(q4, kc, vc, cover)


def _nsa_sample_attend_kernel(pt_ref, q_ref, sel_ref, oc_ref, gl_ref, kp_ref, vp_ref, new_ref, win_ref,
                              o_ref, m_ref, l_ref, acc_ref, *, t, past):
    rows = q_ref.shape[0]
    p = pl.program_id(1)
    page = kp_ref.shape[0]
    qb = q_ref[...].astype(BF16)
    sel = sel_ref[...]
    nblk = sel.shape[1]

    @pl.when(p == 0)
    def _():
        _flash_init(m_ref, l_ref, acc_ref)

    sc = _dot_nt(qb, kp_ref[...].astype(BF16))
    jj = lax.broadcasted_iota(I32, (nblk, page), 0)
    tt = lax.broadcasted_iota(I32, (nblk, page), 1)
    expand = jnp.where(jj == ((p * page + tt) >> 6), 1.0, 0.0).astype(BF16)
    mtok = _dot(sel.astype(BF16), expand)
    _flash_update(sc, mtok > 0.5, vp_ref[...].astype(BF16), m_ref, l_ref, acc_ref)

    @pl.when(p == pl.num_programs(1) - 1)
    def _():
        qpos_r = past + (lax.broadcasted_iota(I32, (rows, 1), 0) & (t - 1))
        newr = new_ref[...]
        tp = newr.shape[0]
        jn = lax.broadcasted_iota(I32, (rows, tp), 1)
        new_ok = (jn < t) & (past + jn <= qpos_r)
        blk_new = past // SEL_BLOCK
        sel_new = sel[:, blk_new:blk_new + 1] > 0.5
        sc_n = _dot_nt(qb, newr[:, 2 * KVW:3 * KVW].astype(BF16))
        _flash_update(sc_n, new_ok & sel_new, newr[:, 3 * KVW:4 * KVW].astype(BF16), m_ref, l_ref, acc_ref)
        o_sel = _flash_result(l_ref, acc_ref)

        win = win_ref[...]
        n_win = win.shape[0]
        s_a = _dot_nt(qb, win[:, :KVW].astype(BF16))
        s_b = _dot_nt(qb, newr[:, 4 * KVW:5 * KVW].astype(BF16))
        kp_a = (past - n_win) + lax.broadcasted_iota(I32, (rows, n_win), 1)
        ok_a = (kp_a <= qpos_r) & (kp_a > qpos_r - WINDOW) & (kp_a >= 0)
        ok_b = new_ok & (past + jn > qpos_r - WINDOW)
        sm_a = jnp.where(ok_a, s_a, NEG)
        sm_b = jnp.where(ok_b, s_b, NEG)
        mx = jnp.maximum(jnp.max(sm_a, axis=-1, keepdims=True), jnp.max(sm_b, axis=-1, keepdims=True))
        e_a = jnp.where(ok_a, jnp.exp(sm_a - mx), 0.0)
        e_b = jnp.where(ok_b, jnp.exp(sm_b - mx), 0.0)
        den = jnp.maximum(jnp.sum(e_a, axis=-1, keepdims=True) + jnp.sum(e_b, axis=-1, keepdims=True), 1e-30)
        o_win = (_dot((e_a / den).astype(BF16), win[:, KVW:].astype(BF16))
                 + _dot((e_b / den).astype(BF16), newr[:, 5 * KVW:6 * KVW].astype(BF16)))

        gates = _sigmoid(gl_ref[...])
        o_ref[...] = gates[:, 0:1] * oc_ref[...] + gates[:, 1:2] * o_sel + gates[:, 2:3] * o_win


def nsa_sample_attend(q4, sel, ocmp, gl, cache, page_table, new_rows, cache_win, *, t, past):
    b, rows, _ = q4.shape
    n_pages = page_table.shape[1]
    page = cache.shape[1]
    lanes = sel.shape[2]
    tp = new_rows.shape[1]
    n_win = cache_win.shape[1]
    per_b = lambda shape: pl.BlockSpec((None,) + shape, lambda bi, pi, pt: (bi, 0, 0))
    grid_spec = pltpu.PrefetchScalarGridSpec(
        num_scalar_prefetch=1,
        grid=(b, n_pages),
        in_specs=[per_b((rows, KVW)), per_b((rows, lanes)), per_b((rows, KVW)), per_b((rows, N_BRANCH)),
                  pl.BlockSpec((None, page, KVW), lambda bi, pi, pt: (pt[bi, pi], 0, 2)),
                  pl.BlockSpec((None, page, KVW), lambda bi, pi, pt: (pt[bi, pi], 0, 3)),
                  per_b((tp, N_KV_ROWS * KVW)), per_b((n_win, 2 * KVW))],
        out_specs=per_b((rows, KVW)),
        scratch_shapes=[pltpu.VMEM((rows, 1), F32), pltpu.VMEM((rows, 1), F32), pltpu.VMEM((rows, KVW), F32)],
    )
    return pl.pallas_call(
        functools.partial(_nsa_sample_attend_kernel, t=t, past=past),
        grid_spec=grid_spec,
        out_shape=jax.ShapeDtypeStruct((b, rows, KVW), F32),
        compiler_params=_cparams("parallel", "arbitrary"),
        name="nsa_sample_attend",
    )(page_table, q4, sel, ocmp, gl, cache, cache, new_rows, cache_win)


def _route_kernel(x_ref, g_ref, sh_ref, sc_ref, rw_ref, rb_ref, h_ref, r_ref):
    h = _norm_mod(x_ref[...], g_ref[...], sh_ref[...], sc_ref[...])
    h_ref[...] = h.astype(BF16)
    logits = jnp.dot(h, rw_ref[...], precision=HIGHEST, preferred_element_type=F32) + rb_ref[...]
    lane = lax.broadcasted_iota(I32, logits.shape, 1)
    lane_f = lane.astype(F32)
    lg = jnp.where(lane < N_EXPERTS, logits, NEG)
    v1 = jnp.max(lg, axis=-1, keepdims=True)
    i1 = jnp.min(jnp.where(lg == v1, lane_f, 128.0), axis=-1, keepdims=True)
    lg2 = jnp.where(lane_f == i1, NEG, lg)
    v2 = jnp.max(lg2, axis=-1, keepdims=True)
    i2 = jnp.min(jnp.where(lg2 == v2, lane_f, 128.0), axis=-1, keepdims=True)
    e = jnp.exp(v2 - v1)
    w1 = 1.0 / (1.0 + e)
    w2 = e / (1.0 + e)
    r_ref[...] = jnp.where(lane == 0, i1, jnp.where(lane == 1, i2, jnp.where(lane == 2, w1, jnp.where(lane == 3, w2, 0.0))))


def moe_route(x, g, shift, scale, rw_pad, rb_pad, *, per_token, tm, tpb):
    m, d = x.shape
    sh, sh_spec = _mod_arg(shift, per_token, tm, tpb)
    sc, sc_spec = _mod_arg(scale, per_token, tm, tpb)
    return pl.pallas_call(
        _route_kernel,
        grid=(m // tm,),
        in_specs=[pl.BlockSpec((tm, d), lambda i: (i, 0)), _vec_spec(d), sh_spec, sc_spec,
                  pl.BlockSpec((d, 128), lambda i: (0, 0)), _vec_spec(128)],
        out_specs=[pl.BlockSpec((tm, d), lambda i: (i, 0)), pl.BlockSpec((tm, 128), lambda i: (i, 0))],
        out_shape=[jax.ShapeDtypeStruct((m, d), BF16), jax.ShapeDtypeStruct((m, 128), F32)],
        compiler_params=_cparams("parallel"),
        name="moe_route",
    )(x, g.reshape(1, d), sh, sc, rw_pad, rb_pad)


def _moe_kernel(te_ref, nv_ref, x_ref, wg_ref, wu_ref, wd_ref, ws_ref, o_ref, acc_ref):
    i = pl.program_id(0)
    f = pl.program_id(1)

    @pl.when(f == 0)
    def _():
        acc_ref[...] = jnp.zeros(acc_ref.shape, F32)

    @pl.when(i < nv_ref[0])
    def _():
        x = x_ref[...]
        act = _silu(_dot(x, wg_ref[...])) * _dot(x, wu_ref[...])
        acc_ref[...] += _dot(act.astype(BF16), wd_ref[...])

    @pl.when(f == pl.num_programs(1) - 1)
    def _():
        o_ref[...] = ws_ref[...] * acc_ref[...]


def moe_experts(xs, w_slot, tile_expert, n_valid, w_gu_b, w_down_b, *, tm, tf):
    p, d = xs.shape
    edim = w_down_b.shape[1]
    nf = edim // tf
    n_tiles = p // tm

    def wmap(off):
        def index(i, f, te, nv):
            ok = i < nv[0]
            return (te[i], 0, off + jnp.where(ok, f, nf - 1))
        return index

    def dmap(i, f, te, nv):
        return (te[i], jnp.where(i < nv[0], f, nf - 1), 0)

    grid_spec = pltpu.PrefetchScalarGridSpec(
        num_scalar_prefetch=2,
        grid=(n_tiles, nf),
        in_specs=[pl.BlockSpec((tm, d), lambda i, f, te, nv: (i, 0)),
                  pl.BlockSpec((None, d, tf), wmap(0)),
                  pl.BlockSpec((None, d, tf), wmap(nf)),
                  pl.BlockSpec((None, tf, d), dmap),
                  pl.BlockSpec((tm, 1), lambda i, f, te, nv: (i, 0))],
        out_specs=pl.BlockSpec((tm, d), lambda i, f, te, nv: (i, 0)),
        scratch_shapes=[pltpu.VMEM((tm, d), F32)],
    )
    return pl.pallas_call(
        _moe_kernel,
        grid_spec=grid_spec,
        out_shape=jax.ShapeDtypeStruct((p, d), F32),
        compiler_params=_cparams("parallel", "arbitrary"),
        name="moe_experts",
    )(tile_expert, n_valid, xs, w_gu_b, w_gu_b, w_down_b, w_slot)


def _moe_tables(route, tm):
    m = route.shape[0]
    na = TOP_K * m
    e = route[:, :TOP_K].astype(I32).reshape(na)
    w = route[:, TOP_K:2 * TOP_K].reshape(na)
    onehot = (e[:, None] == jnp.arange(N_EXPERTS, dtype=I32)[None, :]).astype(I32)
    within = jnp.sum((jnp.cumsum(onehot, axis=0) - onehot) * onehot, axis=1)
    counts = jnp.sum(onehot, axis=0)
    padded = ((counts + tm - 1) // tm) * tm
    ends = jnp.cumsum(padded)
    starts = ends - padded
    dest = starts[e] + within
    n_slots = (-(-na // tm) + N_EXPERTS) * tm
    src_tok = jnp.zeros((n_slots,), I32).at[dest].set(jnp.arange(na, dtype=I32) // TOP_K)
    w_slot = jnp.zeros((n_slots,), F32).at[dest].set(w)
    n_tiles = n_slots // tm
    n_valid = (ends[-1] // tm).astype(I32)
    tile_start = jnp.arange(n_tiles, dtype=I32) * tm
    tile_expert = jnp.sum((tile_start[:, None] >= ends[None, :]).astype(I32), axis=1)
    last = jnp.take(tile_expert, jnp.maximum(n_valid - 1, 0))
    tile_expert = jnp.where(jnp.arange(n_tiles) < n_valid, tile_expert, last).astype(I32)
    return src_tok, w_slot[:, None], tile_expert, n_valid.reshape(1), dest


def _split_mod(mod):
    return [mod[:, i * D_MODEL:(i + 1) * D_MODEL] for i in range(mod.shape[1] // D_MODEL)]


def kernel(x_prompt, x_sample, c_prompt, c_sample, state_ret, cache_kv, cache_win, page_table, w_mod, b_mod, norm_g, ret_w_in, ret_gn_g, ret_w_out, kv_w_mod, kv_b_mod, kv_norm_g, kv_w, cmp_pos, cmp_w1, cmp_b1, cmp_w2, nsa_w_in, nsa_w_out, ffn_w_gu, ffn_w_down, moe_router_w, moe_router_b, moe_w_gu, moe_w_down):
    bp, t, d = x_prompt.shape
    bs, ts, _ = x_sample.shape
    mp, ms = bp * t, bs * ts
    n_pool, page = cache_kv.shape[:2]
    past = page_table.shape[1] * page
    g, r, hd = NSA_KV_HEADS, NSA_GROUP, NSA_HEAD_DIM

    ret_w_in_b = ret_w_in[0].astype(BF16)
    ret_w_out_b = ret_w_out[0].astype(BF16)
    kv_w_b = kv_w.astype(BF16)
    nsa_cols = nsa_w_in.shape[2]
    nsa_cols_pad = -(-nsa_cols // 128) * 128
    nsa_w_in_b = jnp.pad(nsa_w_in[0], ((0, 0), (0, nsa_cols_pad - nsa_cols))).astype(BF16)
    nsa_w_out_b = nsa_w_out[0].astype(BF16)
    ffn_w_gu_b = ffn_w_gu[0].astype(BF16)
    ffn_w_down_b = ffn_w_down[0].astype(BF16)
    moe_w_gu_b = moe_w_gu[0].astype(BF16)
    moe_w_down_b = moe_w_down[0].astype(BF16)
    rw_pad = jnp.pad(moe_router_w[0], ((0, 0), (0, 128 - N_EXPERTS)))
    rb_pad = jnp.pad(moe_router_b[0], (0, 128 - N_EXPERTS)).reshape(1, 128)
    cw = _compress_weights(cmp_pos, cmp_w1, cmp_b1, cmp_w2)

    c_all = jnp.concatenate([c_prompt, c_sample], axis=0)
    mods = [cond_matmul(c_all, w_mod[layer], b_mod[layer]) for layer in range(w_mod.shape[0])]
    kv_mod = cond_matmul(c_all, kv_w_mod, kv_b_mod)

    tm_p = min(512, t)
    tm_f = min(1024, t)
    groups = {
        'p': dict(x=x_prompt.reshape(mp, d), b=bp, t=t, per_token=False, tm=tm_p, tpb=t // tm_p,
                  tm_f=tm_f, tpb_f=t // tm_f,
                  mod=lambda a: a[:bp]),
        's': dict(x=x_sample.reshape(ms, d), b=bs, t=ts, per_token=True, tm=ms, tpb=1, tm_f=ms, tpb_f=1,
                  mod=lambda a: jnp.repeat(a[bp:], ts, axis=0)),
    }
    out = {}

    for name, gr in groups.items():
        kw = dict(per_token=gr['per_token'], tm=gr['tm'], tpb=gr['tpb'])
        m0 = [gr['mod'](a) for a in _split_mod(mods[0])]
        x = gr['x']
        proj = norm_mod_matmul(x, norm_g[0, 0], m0[0], m0[1], ret_w_in_b, tn=1024, **kw)
        proj = proj.reshape(gr['b'], gr['t'], -1)
        if name == 'p':
            pos = jnp.arange(t)
            s0 = jnp.zeros((bp, RET_HEADS, RET_DK, RET_DV), F32)
            o, s_new = retention(proj, pos, s0, ret_gn_g[0], math.gcd(t, RET_CHUNK))
        else:
            pos = past + jnp.arange(ts)
            proj = jnp.pad(proj, ((0, 0), (0, RET_CHUNK - ts), (0, 0)))
            o, s_new = retention(proj, pos, state_ret[0], ret_gn_g[0], math.gcd(ts, RET_CHUNK))
            o = o[:, :ts]
        out['ret_' + name] = s_new[None]
        x = matmul_norm_residual(o.reshape(-1, d), ret_w_out_b, x, m0[2], norm_g[0, 1], **kw)
        x = ffn_sublayer(x, norm_g[0, 2], m0[3], m0[4], ffn_w_gu_b, ffn_w_down_b, m0[5], norm_g[0, 3],
                         per_token=gr['per_token'], tm=gr['tm_f'], tpb=gr['tpb_f'], tf=256)
        gr['x1'] = x

    for name, gr in groups.items():
        kw = dict(per_token=gr['per_token'], tm=gr['tm'], tpb=gr['tpb'])
        m1 = [gr['mod'](a) for a in _split_mod(mods[1])]
        kvm = [gr['mod'](a) for a in _split_mod(kv_mod)]
        gr['m1'] = m1
        x = gr['x1']
        b_, t_ = gr['b'], gr['t']
        rows = norm_mod_matmul(x, kv_norm_g, kvm[0], kvm[1], kv_w_b, **kw)
        proj = norm_mod_matmul(x, norm_g[1, 0], m1[0], m1[1], nsa_w_in_b, **kw)
        q = (proj[:, :NSA_HEADS * hd] * (hd ** -0.5)).reshape(b_, t_, g, r, hd)
        gl = proj[:, NSA_HEADS * hd:nsa_cols].reshape(b_, t_, g, r, N_BRANCH)
        rows3 = rows.reshape(b_, t_, N_KV_ROWS * KVW)
        rows6 = rows.reshape(b_, t_, N_KV_ROWS, g, hd)
        out['kv_' + name] = rows6[:, :, :N_PAGED_ROWS]
        if name == 'p':
            out['win_p'] = rows6[:, t_ - min(WINDOW, t_):, N_PAGED_ROWS:]
            cmp = compress_prompt(rows3, cw)
            cmp = cmp.reshape(2, b_, -1, g, hd).transpose(0, 1, 3, 2, 4)
            heads = rows6.transpose(2, 0, 3, 1, 4).astype(BF16)
            o = nsa_prompt(q.transpose(0, 2, 3, 1, 4), gl.transpose(0, 2, 3, 1, 4), cmp[0], cmp[1],
                           heads[2], heads[3], heads[4], heads[5])
            o = o.transpose(0, 3, 1, 2, 4).reshape(mp, NSA_HEADS * hd)
        else:
            cache = cache_kv.reshape(n_pool, page, N_PAGED_ROWS * KVW)
            win_past = cache_win.reshape(b_, cache_win.shape[1], 2 * KVW)
            out['win_s'] = jnp.concatenate([cache_win, rows6[:, :, N_PAGED_ROWS:]], axis=1)[:, t_:]
            new_sub = jnp.pad(rows3[:, :, :2 * KVW], ((0, 0), (0, CMP_STRIDE - t_), (0, 0)))
            new_sub = new_sub.reshape(b_, CMP_STRIDE, 2, KVW).transpose(2, 0, 1, 3)
            cmp = compress_sample(cache, page_table, new_sub, cw)
            eye = jnp.eye(g, dtype=F32)
            qrows = q.transpose(0, 2, 3, 1, 4).reshape(b_, g, r * t_, hd)
            q4 = jnp.einsum('bgxd,gj->bgxjd', qrows, eye).reshape(b_, g * r * t_, KVW)
            glr = gl.transpose(0, 2, 3, 1, 4).reshape(b_, g * r * t_, N_BRANCH)
            ocmp, sel = nsa_sample_select(q4, cmp[0], cmp[1], t=t_, past=past)
            new_rows = jnp.pad(rows3, ((0, 0), (0, 128 - t_), (0, 0)))
            o4 = nsa_sample_attend(q4, sel, ocmp, glr, cache, page_table, new_rows, win_past, t=t_, past=past)
            o4 = o4.reshape(b_, g, r, t_, g, hd)
            o = jnp.einsum('bgrtjd,gj->btgrd', o4, eye).reshape(ms, NSA_HEADS * hd)
        gr['x2'] = matmul_norm_residual(o, nsa_w_out_b, x, m1[2], norm_g[1, 1], **kw)

    hs, routes = [], []
    for name, gr in groups.items():
        kw = dict(per_token=gr['per_token'], tm=gr['tm'], tpb=gr['tpb'])
        m1 = gr['m1']
        h, route = moe_route(gr['x2'], norm_g[1, 2], m1[3], m1[4], rw_pad, rb_pad, **kw)
        hs.append(h)
        routes.append(route)
    h_all = jnp.concatenate(hs, axis=0)
    route_all = jnp.concatenate(routes, axis=0)
    tm_e = 512 if mp >= 4096 else 128
    src_tok, w_slot, tile_expert, n_valid, dest = _moe_tables(route_all, tm_e)
    xs = jnp.take(h_all, src_tok, axis=0)
    ys = moe_experts(xs, w_slot, tile_expert, n_valid, moe_w_gu_b, moe_w_down_b, tm=tm_e, tf=512)
    y_all = jnp.take(ys, dest[0::2], axis=0) + jnp.take(ys, dest[1::2], axis=0)
    for name, gr, y in (('p', groups['p'], y_all[:mp]), ('s', groups['s'], y_all[mp:])):
        kw = dict(per_token=gr['per_token'], tm=gr['tm'], tpb=gr['tpb'])
        out['y_' + name] = norm_residual(y, gr['x2'], gr['m1'][5], norm_g[1, 3], **kw)

    return (out['y_p'].reshape(bp, t, d), out['y_s'].reshape(bs, ts, d),
            out['ret_p'], out['ret_s'], out['kv_p'], out['kv_s'], out['win_p'], out['win_s'])
```

```python
import functools
import math

import numpy as np
import jax
import jax.numpy as jnp
from jax import lax
from jax.experimental import pallas as pl
from jax.experimental.pallas import tpu as pltpu

F32 = jnp.float32
BF16 = jnp.bfloat16
I32 = jnp.int32

D_MODEL = 1024
N_MOD = 6
RET_HEADS = 4
RET_DK = 256
RET_DV = 256
RET_CHUNK = 128
ROPE_BASE = 10000.0
NSA_HEADS = 16
NSA_KV_HEADS = 4
NSA_GROUP = 4
NSA_HEAD_DIM = 64
N_BRANCH = 3
N_KV_ROWS = 6
N_PAGED_ROWS = 4
CMP_BLOCK = 32
CMP_STRIDE = 16
CMP_HIDDEN = 128
SEL_BLOCK = 64
N_SEL = 16
FORCED_SCORE = 1.0e4
WINDOW = 512
Q_BLOCK = 128
N_EXPERTS = 8
TOP_K = 2
EPS = 1e-6

NEG = -1.0e30
KVW = NSA_KV_HEADS * NSA_HEAD_DIM
VMEM_LIMIT_BYTES = 56 * 1024 * 1024
HIGHEST = lax.Precision.HIGHEST


def _cparams(*sem):
    return pltpu.CompilerParams(dimension_semantics=sem, vmem_limit_bytes=VMEM_LIMIT_BYTES)


def _sigmoid(x):
    return 1.0 / (1.0 + jnp.exp(-x))


def _silu(x):
    return x * _sigmoid(x)


def _gelu_tanh(x):
    return x * (0.5 * (1.0 + jnp.tanh(math.sqrt(2.0 / math.pi) * (x + 0.044715 * (x * x * x)))))


def _norm_mod(x, g, shift, scale):
    ms = jnp.mean(x * x, axis=-1, keepdims=True)
    return (x * lax.rsqrt(ms + EPS) * g) * (1.0 + scale) + shift


def _rms_residual(x, gate, o, g):
    ms = jnp.mean(o * o, axis=-1, keepdims=True)
    return x + gate * (o * lax.rsqrt(ms + EPS) * g)


def _dot(a, b):
    return jnp.dot(a, b, preferred_element_type=F32)


def _dot_nt(a, b):
    return lax.dot_general(a, b, (((1,), (1,)), ((), ())), preferred_element_type=F32)


def _masked_softmax(s, mask):
    sm = jnp.where(mask, s, NEG)
    m = jnp.max(sm, axis=-1, keepdims=True)
    e = jnp.where(mask, jnp.exp(sm - m), 0.0)
    return e / jnp.maximum(jnp.sum(e, axis=-1, keepdims=True), 1e-30)


def _flash_init(m_ref, l_ref, acc_ref):
    m_ref[...] = jnp.full(m_ref.shape, NEG, F32)
    l_ref[...] = jnp.zeros(l_ref.shape, F32)
    acc_ref[...] = jnp.zeros(acc_ref.shape, F32)


def _flash_update(s, mask, v_b, m_ref, l_ref, acc_ref):
    sm = jnp.where(mask, s, NEG)
    m_old = m_ref[...]
    m_new = jnp.maximum(m_old, jnp.max(sm, axis=-1, keepdims=True))
    alpha = jnp.exp(m_old - m_new)
    p = jnp.where(mask, jnp.exp(sm - m_new), 0.0)
    l_ref[...] = alpha * l_ref[...] + jnp.sum(p, axis=-1, keepdims=True)
    acc_ref[...] = alpha * acc_ref[...] + _dot(p.astype(BF16), v_b)
    m_ref[...] = m_new


def _flash_result(l_ref, acc_ref):
    return acc_ref[...] / jnp.maximum(l_ref[...], 1e-30)


def _select_blocks(imp, cur, n_sel):
    rows, lanes = imp.shape
    blk = lax.broadcasted_iota(I32, (rows, lanes), 1)
    valid = (blk <= cur) & (blk < n_sel)
    forced = (blk == 0) | (blk == cur) | (blk == cur - 1)
    score = jnp.where(valid, jnp.where(forced, FORCED_SCORE, imp), -1.0)
    rank = jnp.zeros((rows, lanes), F32)
    for i in range(n_sel):
        ci = score[:, i:i + 1]
        beats = (ci > score) | ((ci == score) & (blk > i))
        rank = rank + jnp.where(beats, 1.0, 0.0)
    return jnp.where(valid & (rank < float(min(N_SEL, n_sel))), 1.0, 0.0)


def _cond_kernel(c_ref, w_ref, b_ref, o_ref):
    sc = _silu(c_ref[...])
    o_ref[...] = _dot(sc.astype(BF16), w_ref[...].astype(BF16)) + b_ref[...]


def cond_matmul(c, w, b):
    bc, d = c.shape
    n = w.shape[1]
    tn = 1024
    return pl.pallas_call(
        _cond_kernel,
        grid=(n // tn,),
        in_specs=[pl.BlockSpec((bc, d), lambda j: (0, 0)),
                  pl.BlockSpec((d, tn), lambda j: (0, j)),
                  pl.BlockSpec((1, tn), lambda j: (0, j))],
        out_specs=pl.BlockSpec((bc, tn), lambda j: (0, j)),
        out_shape=jax.ShapeDtypeStruct((bc, n), F32),
        compiler_params=_cparams("parallel"),
        name="cond_matmul",
    )(c, w, b.reshape(1, n))


def _mod_arg(m, per_token, tm, tiles_per_batch):
    d = m.shape[-1]
    if per_token:
        return m, pl.BlockSpec((tm, d), lambda i, *_: (i, 0))
    return m[:, None, :], pl.BlockSpec((None, 1, d), lambda i, *_: (i // tiles_per_batch, 0, 0))


def _vec_spec(d):
    return pl.BlockSpec((1, d), lambda i, *_: (0, 0))


def _nmm_kernel(x_ref, g_ref, sh_ref, sc_ref, w_ref, o_ref, h_ref):
    @pl.when(pl.program_id(1) == 0)
    def _():
        h_ref[...] = _norm_mod(x_ref[...], g_ref[...], sh_ref[...], sc_ref[...]).astype(BF16)

    o_ref[...] = _dot(h_ref[...], w_ref[...]).astype(o_ref.dtype)


def norm_mod_matmul(x, g, shift, scale, w_b, *, per_token, tm, tpb, tn=None):
    m, d = x.shape
    n = w_b.shape[1]
    tn = n if tn is None else tn
    sh, sh_spec = _mod_arg(shift, per_token, tm, tpb)
    sc, sc_spec = _mod_arg(scale, per_token, tm, tpb)
    return pl.pallas_call(
        _nmm_kernel,
        grid=(m // tm, n // tn),
        in_specs=[pl.BlockSpec((tm, d), lambda i, j: (i, 0)), _vec_spec(d), sh_spec, sc_spec,
                  pl.BlockSpec((d, tn), lambda i, j: (0, j))],
        out_specs=pl.BlockSpec((tm, tn), lambda i, j: (i, j)),
        out_shape=jax.ShapeDtypeStruct((m, n), F32),
        scratch_shapes=[pltpu.VMEM((tm, d), BF16)],
        compiler_params=_cparams("parallel", "arbitrary"),
        name="norm_mod_matmul",
    )(x, g.reshape(1, d), sh, sc, w_b)


def _nmm_t_kernel(x_ref, g_ref, sh_ref, sc_ref, wt_ref, *o_refs):
    h = _norm_mod(x_ref[...], g_ref[...], sh_ref[...], sc_ref[...]).astype(BF16)
    o = _dot_nt(wt_ref[...], h)
    row = 0
    for o_ref in o_refs:
        o_ref[...] = o[row:row + o_ref.shape[0]]
        row += o_ref.shape[0]


def norm_mod_matmul_t(x, g, shift, scale, wt_b, splits, *, b, tm):
    m, d = x.shape
    t = m // b
    tpb = t // tm
    n = wt_b.shape[0]
    sh, sh_spec = _mod_arg(shift, False, tm, tpb)
    sc, sc_spec = _mod_arg(scale, False, tm, tpb)
    return pl.pallas_call(
        _nmm_t_kernel,
        grid=(m // tm,),
        in_specs=[pl.BlockSpec((tm, d), lambda i: (i, 0)), _vec_spec(d), sh_spec, sc_spec,
                  pl.BlockSpec((n, d), lambda i: (0, 0))],
        out_specs=[pl.BlockSpec((None, ni, tm), lambda i: (i // tpb, 0, i % tpb)) for ni in splits],
        out_shape=[jax.ShapeDtypeStruct((b, ni, t), F32) for ni in splits],
        compiler_params=_cparams("parallel"),
        name="norm_mod_matmul_t",
    )(x, g.reshape(1, d), sh, sc, wt_b)


def _mnr_kernel(a_ref, w_ref, x_ref, gate_ref, g_ref, o_ref):
    o = _dot(a_ref[...].astype(BF16), w_ref[...])
    o_ref[...] = _rms_residual(x_ref[...], gate_ref[...], o, g_ref[...])


def matmul_norm_residual(a, w_b, x, gate, g, *, per_token, tm, tpb):
    m, k = a.shape
    d = w_b.shape[1]
    gt, gt_spec = _mod_arg(gate, per_token, tm, tpb)
    return pl.pallas_call(
        _mnr_kernel,
        grid=(m // tm,),
        in_specs=[pl.BlockSpec((tm, k), lambda i: (i, 0)),
                  pl.BlockSpec((k, d), lambda i: (0, 0)),
                  pl.BlockSpec((tm, d), lambda i: (i, 0)), gt_spec, _vec_spec(d)],
        out_specs=pl.BlockSpec((tm, d), lambda i: (i, 0)),
        out_shape=jax.ShapeDtypeStruct((m, d), F32),
        compiler_params=_cparams("parallel"),
        name="matmul_norm_residual",
    )(a, w_b, x, gt, g.reshape(1, d))


def _nr_kernel(y_ref, x_ref, gate_ref, g_ref, o_ref):
    o_ref[...] = _rms_residual(x_ref[...], gate_ref[...], y_ref[...], g_ref[...])


def norm_residual(y, x, gate, g, *, per_token, tm, tpb):
    m, d = x.shape
    gt, gt_spec = _mod_arg(gate, per_token, tm, tpb)
    return pl.pallas_call(
        _nr_kernel,
        grid=(m // tm,),
        in_specs=[pl.BlockSpec((tm, d), lambda i: (i, 0)),
                  pl.BlockSpec((tm, d), lambda i: (i, 0)), gt_spec, _vec_spec(d)],
        out_specs=pl.BlockSpec((tm, d), lambda i: (i, 0)),
        out_shape=jax.ShapeDtypeStruct((m, d), F32),
        compiler_params=_cparams("parallel"),
        name="norm_residual",
    )(y, x, gt, g.reshape(1, d))


def _ffn_kernel(x_ref, g2_ref, sh_ref, sc_ref, wg_ref, wu_ref, wd_ref, gate_ref, g3_ref, o_ref,
                h_ref, acc_ref):
    f = pl.program_id(1)

    @pl.when(f == 0)
    def _():
        h_ref[...] = _norm_mod(x_ref[...], g2_ref[...], sh_ref[...], sc_ref[...]).astype(BF16)
        acc_ref[...] = jnp.zeros(acc_ref.shape, F32)

    h = h_ref[...]
    act = _silu(_dot(h, wg_ref[...])) * _dot(h, wu_ref[...])
    acc_ref[...] += _dot(act.astype(BF16), wd_ref[...])

    @pl.when(f == pl.num_programs(1) - 1)
    def _():
        o_ref[...] = _rms_residual(x_ref[...], gate_ref[...], acc_ref[...], g3_ref[...])


def ffn_sublayer(x, g2, shift, scale, w_gu_b, w_down_b, gate, g3, *, per_token, tm, tpb, tf):
    m, d = x.shape
    fdim = w_down_b.shape[0]
    nf = fdim // tf
    sh, sh_spec = _mod_arg(shift, per_token, tm, tpb)
    sc, sc_spec = _mod_arg(scale, per_token, tm, tpb)
    gt, gt_spec = _mod_arg(gate, per_token, tm, tpb)
    return pl.pallas_call(
        _ffn_kernel,
        grid=(m // tm, nf),
        in_specs=[pl.BlockSpec((tm, d), lambda i, f: (i, 0)), _vec_spec(d), sh_spec, sc_spec,
                  pl.BlockSpec((d, tf), lambda i, f: (0, f)),
                  pl.BlockSpec((d, tf), lambda i, f: (0, nf + f)),
                  pl.BlockSpec((tf, d), lambda i, f: (f, 0)),
                  gt_spec, _vec_spec(d)],
        out_specs=pl.BlockSpec((tm, d), lambda i, f: (i, 0)),
        out_shape=jax.ShapeDtypeStruct((m, d), F32),
        scratch_shapes=[pltpu.VMEM((tm, d), BF16), pltpu.VMEM((tm, d), F32)],
        compiler_params=_cparams("parallel", "arbitrary"),
        name="ffn_sublayer",
    )(x, g2.reshape(1, d), sh, sc, w_gu_b, w_gu_b, w_down_b, gt, g3.reshape(1, d))


def _ret_kernel(q_ref, k_ref, v_ref, gt_ref, cos_ref, sin_ref, dm_ref, xi_ref, zt_ref, gc_ref,
                gn_ref, s0_ref, o_ref, s_ref):
    @pl.when(pl.program_id(2) == 0)
    def _():
        s_ref[...] = s0_ref[...]

    cos = cos_ref[...]
    sin = sin_ref[...]
    half = RET_DK // 2

    def rot(x):
        x1 = x[:, :half]
        x2 = x[:, half:]
        return jnp.concatenate([x1 * cos - x2 * sin, x2 * cos + x1 * sin], axis=-1)

    q = rot(q_ref[...])
    k = rot(k_ref[...]) * (RET_DK ** -0.5)
    qb = q.astype(BF16)
    kb = k.astype(BF16)
    vb = v_ref[...].astype(BF16)
    state = s_ref[...]
    inner = _dot_nt(qb, kb) * dm_ref[...]
    o = _dot(inner.astype(BF16), vb) + _dot(qb, state.astype(BF16)) * xi_ref[...]
    kz = (k * zt_ref[...]).astype(BF16)
    upd = lax.dot_general(kz, vb, (((0,), (0,)), ((), ())), preferred_element_type=F32)
    s_ref[...] = gc_ref[...] * state + upd
    mu = jnp.mean(o, axis=-1, keepdims=True)
    dev = o - mu
    var = jnp.mean(dev * dev, axis=-1, keepdims=True)
    on = dev * lax.rsqrt(var + EPS) * gn_ref[...]
    o_ref[...] = _silu(gt_ref[...]) * on


def _ret_tables(chunk, rows):
    h = RET_HEADS
    log_g = jnp.log(1.0 - jnp.exp(jnp.linspace(math.log(1.0 / 32), math.log(1.0 / 512), h, dtype=F32)))
    i = jnp.arange(chunk, dtype=F32)
    diff = i[:, None] - i[None, :]
    dmat = jnp.where(diff >= 0, jnp.exp(log_g[:, None, None] * jnp.maximum(diff, 0.0)), 0.0)
    xi = jnp.exp(log_g[:, None] * (i + 1.0))
    zeta = jnp.exp(log_g[:, None] * (chunk - 1.0 - i))
    gch = jnp.exp(log_g * chunk)
    pad = rows - chunk
    dmat = jnp.pad(dmat, ((0, 0), (0, pad), (0, pad)))
    xi = jnp.pad(xi, ((0, 0), (0, pad)))[..., None]
    zeta = jnp.pad(zeta, ((0, 0), (0, pad)))[..., None]
    gch = jnp.broadcast_to(gch[:, None, None], (h, 1, RET_DV))
    return dmat, xi, zeta, gch


def _rope_tables(pos, rows):
    half = RET_DK // 2
    inv = ROPE_BASE ** (-jnp.arange(half, dtype=F32) / half)
    ang = pos.astype(F32)[:, None] * inv[None, :]
    pad = rows - pos.shape[0]
    return jnp.pad(jnp.cos(ang), ((0, pad), (0, 0))), jnp.pad(jnp.sin(ang), ((0, pad), (0, 0)))


def retention(proj, pos, s0, gn_g, chunk):
    b, t, _ = proj.shape
    c = RET_CHUNK
    n = t // c
    h = RET_HEADS
    dmat, xi, zeta, gch = _ret_tables(chunk, c)
    cos, sin = _rope_tables(pos, t)
    col = lambda off: pl.BlockSpec((None, c, RET_DK), lambda bi, hi, ni: (bi, ni, off + hi))
    o, s = pl.pallas_call(
        _ret_kernel,
        grid=(b, h, n),
        in_specs=[col(0), col(h), col(2 * h), col(3 * h),
                  pl.BlockSpec((c, RET_DK // 2), lambda bi, hi, ni: (ni, 0)),
                  pl.BlockSpec((c, RET_DK // 2), lambda bi, hi, ni: (ni, 0)),
                  pl.BlockSpec((None, c, c), lambda bi, hi, ni: (hi, 0, 0)),
                  pl.BlockSpec((None, c, 1), lambda bi, hi, ni: (hi, 0, 0)),
                  pl.BlockSpec((None, c, 1), lambda bi, hi, ni: (hi, 0, 0)),
                  pl.BlockSpec((None, 1, RET_DV), lambda bi, hi, ni: (hi, 0, 0)),
                  pl.BlockSpec((1, RET_DV), lambda bi, hi, ni: (0, hi)),
                  pl.BlockSpec((None, None, RET_DK, RET_DV), lambda bi, hi, ni: (bi, hi, 0, 0))],
        out_specs=[pl.BlockSpec((None, c, RET_DV), lambda bi, hi, ni: (bi, ni, hi)),
                   pl.BlockSpec((None, None, RET_DK, RET_DV), lambda bi, hi, ni: (bi, hi, 0, 0))],
        out_shape=[jax.ShapeDtypeStruct((b, t, h * RET_DV), F32),
                   jax.ShapeDtypeStruct((b, h, RET_DK, RET_DV), F32)],
        compiler_params=_cparams("parallel", "parallel", "arbitrary"),
        name="retention",
    )(proj, proj, proj, proj, cos, sin, dmat, xi, zeta, gch, gn_g.reshape(1, -1), s0)
    return o, s


def _compress_weights(cmp_pos, cmp_w1, cmp_b1, cmp_w2):
    g, d, hd = NSA_KV_HEADS, NSA_HEAD_DIM, CMP_HIDDEN
    eye = jnp.eye(g, dtype=F32)
    w1 = cmp_w1.reshape(2, CMP_BLOCK, d, hd)
    bd1 = jnp.einsum('ksdh,gj->ksgdjh', w1, eye).reshape(2, CMP_BLOCK, g * d, g * hd).astype(BF16)
    bd2 = jnp.einsum('khd,gj->kghjd', cmp_w2, eye).reshape(2, g * hd, g * d).astype(BF16)
    pos = jnp.tile(cmp_pos[:, :, None, :], (1, 1, g, 1)).reshape(2, CMP_BLOCK, 1, g * d)
    b1 = jnp.tile(cmp_b1[:, None, :], (1, g, 1)).reshape(2, 1, g * hd)
    return bd1, bd2, pos, b1


def _compress_finish(p0, p1, b1, bd2):
    rows = p0.shape[0]
    hid = b1 + p0 + pltpu.roll(p1, rows - 1, 0)
    return _dot(_gelu_tanh(hid).astype(BF16), bd2)


LANES = 128
CMP_PAGES_PER_STEP = 8
ATT_PAGES_PER_STEP = 16


def _tokens_to_sublanes(xt_ref, tr_ref):
    for half in range(KVW // LANES):
        tr_ref[half] = xt_ref[half * LANES:(half + 1) * LANES, :].T


def _cmp_prompt_kernel(xt_ref, bd1_ref, bd2_ref, pos_ref, b1_ref, o_ref, tr_ref):
    n_sub = xt_ref.shape[1] // CMP_STRIDE
    hid = bd1_ref.shape[-1]
    _tokens_to_sublanes(xt_ref, tr_ref)
    p0 = jnp.zeros((n_sub, hid), F32)
    p1 = jnp.zeros((n_sub, hid), F32)
    for s in range(CMP_STRIDE):
        xs = jnp.concatenate([tr_ref[half, pl.ds(s, n_sub, stride=CMP_STRIDE), :]
                              for half in range(KVW // LANES)], axis=-1)
        p0 = p0 + _dot((xs + pos_ref[s]).astype(BF16), bd1_ref[s])
        p1 = p1 + _dot((xs + pos_ref[CMP_STRIDE + s]).astype(BF16), bd1_ref[CMP_STRIDE + s])
    o_ref[...] = _compress_finish(p0, p1, b1_ref[...], bd2_ref[...])


def compress_prompt(kvt, cw):
    bd1, bd2, pos, b1 = cw
    b, _, t = kvt.shape
    n_sub = t // CMP_STRIDE
    return pl.pallas_call(
        _cmp_prompt_kernel,
        grid=(b, 2),
        in_specs=[pl.BlockSpec((None, KVW, t), lambda bi, ki: (bi, ki, 0)),
                  pl.BlockSpec((None,) + bd1.shape[1:], lambda bi, ki: (ki, 0, 0, 0)),
                  pl.BlockSpec((None,) + bd2.shape[1:], lambda bi, ki: (ki, 0, 0)),
                  pl.BlockSpec((None,) + pos.shape[1:], lambda bi, ki: (ki, 0, 0, 0)),
                  pl.BlockSpec((None,) + b1.shape[1:], lambda bi, ki: (ki, 0, 0))],
        out_specs=pl.BlockSpec((None, None, n_sub, KVW), lambda bi, ki: (ki, bi, 0, 0)),
        out_shape=jax.ShapeDtypeStruct((2, b, n_sub, KVW), F32),
        scratch_shapes=[pltpu.VMEM((KVW // LANES, t, LANES), F32)],
        compiler_params=_cparams("parallel", "parallel"),
        name="compress_prompt",
    )(kvt, bd1, bd2, pos, b1)


def _cmp_sample_kernel(pt_ref, *refs):
    page_refs = refs[:CMP_PAGES_PER_STEP]
    new_ref, bd1_ref, bd2_ref, pos_ref, b1_ref, o_ref, sub_ref, tr_ref = refs[CMP_PAGES_PER_STEP:]
    p = pl.program_id(1)
    page = page_refs[0].shape[2]
    per_page = page // CMP_STRIDE
    n_cmp = o_ref.shape[1]
    for j, page_ref in enumerate(page_refs):
        row0 = pl.multiple_of((p * CMP_PAGES_PER_STEP + j) * per_page, per_page)
        for kind in range(2):
            _tokens_to_sublanes(page_ref.at[kind], tr_ref.at[kind])
            for s in range(CMP_STRIDE):
                for half in range(KVW // LANES):
                    piece = tr_ref[kind, half, pl.ds(s, per_page, stride=CMP_STRIDE), :]
                    sub_ref[kind, s, pl.ds(row0, per_page), half * LANES:(half + 1) * LANES] = piece

    @pl.when(p == pl.num_programs(1) - 1)
    def _():
        n_rows = sub_ref.shape[2]
        tail = n_rows - n_cmp
        hid = bd1_ref.shape[-1]
        for kind in range(2):
            p0 = jnp.zeros((n_rows, hid), F32)
            p1 = jnp.zeros((n_rows, hid), F32)
            for s in range(CMP_STRIDE):
                sub_ref[kind, s, pl.ds(n_cmp, tail), :] = jnp.broadcast_to(new_ref[kind, s:s + 1, :], (tail, KVW))
                xs = sub_ref[kind, s]
                p0 = p0 + _dot((xs + pos_ref[kind, s]).astype(BF16), bd1_ref[kind, s])
                p1 = p1 + _dot((xs + pos_ref[kind, CMP_STRIDE + s]).astype(BF16), bd1_ref[kind, CMP_STRIDE + s])
            o_ref[kind] = _compress_finish(p0, p1, b1_ref[kind], bd2_ref[kind])[:n_cmp]


def _page_specs(n_per_step, kind_block, page):
    def spec(j):
        return pl.BlockSpec((None, 2, KVW, page),
                            lambda bi, pi, pt: (pt[bi, pi * n_per_step + j], kind_block, 0, 0))
    return [spec(j) for j in range(n_per_step)]


def compress_sample(cache_t, page_table, new_sub, cw):
    bd1, bd2, pos, b1 = cw
    b, n_pages = page_table.shape
    page = cache_t.shape[3]
    n_cmp = n_pages * page // CMP_STRIDE
    nps = CMP_PAGES_PER_STEP
    const = lambda a: pl.BlockSpec(a.shape, lambda bi, pi, pt: (0,) * a.ndim, pipeline_mode=pl.Buffered(1))
    grid_spec = pltpu.PrefetchScalarGridSpec(
        num_scalar_prefetch=1,
        grid=(b, n_pages // nps),
        in_specs=_page_specs(nps, 0, page) + [
            pl.BlockSpec((2, None, CMP_STRIDE, KVW), lambda bi, pi, pt: (0, bi, 0, 0)),
            const(bd1), const(bd2), const(pos), const(b1)],
        out_specs=pl.BlockSpec((2, None, n_cmp, KVW), lambda bi, pi, pt: (0, bi, 0, 0)),
        scratch_shapes=[pltpu.VMEM((2, CMP_STRIDE, n_cmp + 8, KVW), F32),
                        pltpu.VMEM((2, KVW // LANES, page, LANES), F32)],
    )
    return pl.pallas_call(
        _cmp_sample_kernel,
        grid_spec=grid_spec,
        out_shape=jax.ShapeDtypeStruct((2, b, n_cmp, KVW), F32),
        compiler_params=_cparams("parallel", "arbitrary"),
        name="compress_sample",
    )(page_table, *([cache_t] * nps), new_sub, bd1, bd2, pos, b1)


def _cover_matrix(n_cmp, n_sel, rows, cols):
    i = np.arange(n_cmp)[:, None]
    j = np.arange(n_sel)[None, :]
    cover = (i * CMP_STRIDE < (j + 1) * SEL_BLOCK) & (i * CMP_STRIDE + CMP_BLOCK > j * SEL_BLOCK)
    out = np.zeros((rows, cols), np.float32)
    out[:n_cmp, :n_sel] = cover
    return jnp.asarray(out)


SEL_CHUNK = 256


def _select_blocks_t(imp_t, cur, n_sel):
    nb, nq = imp_t.shape
    blk = lax.broadcasted_iota(I32, (nb, nq), 0)
    valid = (blk <= cur) & (blk < n_sel)
    forced = (blk == 0) | (blk == cur) | (blk == cur - 1)
    score = jnp.where(valid, jnp.where(forced, FORCED_SCORE, imp_t), -1.0)
    rank = jnp.zeros((nb, nq), F32)
    for i in range(n_sel):
        ci = score[i:i + 1, :]
        beats = (ci > score) | ((ci == score) & (blk > i))
        rank = rank + jnp.where(beats, 1.0, 0.0)
    return jnp.where(valid & (rank < float(min(N_SEL, n_sel))), 1.0, 0.0)


def _nsa_prompt_kernel(q_ref, gl_ref, kc_ref, vc_ref, covt_ref, exp_ref, ks_ref, vs_ref, kw_ref, vw_ref, o_ref,
                       s_ref, m_ref, l_ref, acc_ref, out_ref, *, n_cmp, n_sel):
    tq = q_ref.shape[0]
    r, d = NSA_GROUP, NSA_HEAD_DIM
    rows = r * tq
    t = ks_ref.shape[1]
    q0 = pl.program_id(2) * tq
    qall = q_ref[...] * (d ** -0.5)
    qb = jnp.concatenate([qall[:, h * d:(h + 1) * d] for h in range(r)], axis=0).astype(BF16)
    gl = _sigmoid(gl_ref[...])

    def gate(branch):
        return jnp.concatenate([gl[:, h * N_BRANCH + branch:h * N_BRANCH + branch + 1] for h in range(r)], axis=0)

    qpos_r = q0 + (lax.broadcasted_iota(I32, (rows, 1), 0) & (tq - 1))
    qpos = q0 + lax.broadcasted_iota(I32, (tq, 1), 0)
    qpos_l = q0 + lax.broadcasted_iota(I32, (1, tq), 1)

    ncp = kc_ref.shape[0]
    s = _dot_nt(qb, kc_ref[...].astype(BF16))
    nn = lax.broadcasted_iota(I32, (rows, ncp), 1)
    p_cmp = _masked_softmax(s, (nn * CMP_STRIDE + (CMP_BLOCK - 1) <= qpos_r) & (nn < n_cmp))
    out_ref[...] = gate(0) * _dot(p_cmp.astype(BF16), vc_ref[...].astype(BF16))

    psum = jnp.sum(p_cmp.reshape(r, tq, ncp), axis=0)
    imp_t = lax.dot_general(covt_ref[...], psum, (((1,), (1,)), ((), ())), precision=HIGHEST,
                            preferred_element_type=F32)
    sel_t = _select_blocks_t(imp_t, qpos_l >> 6, n_sel).astype(BF16)

    m_ref[...] = jnp.full(m_ref.shape, NEG, F32)
    n_chunks = (q0 + tq + SEL_CHUNK - 1) // SEL_CHUNK

    def pass1(c, carry):
        k0 = pl.multiple_of(c * SEL_CHUNK, SEL_CHUNK)
        sc = _dot(qb, ks_ref[:, pl.ds(k0, SEL_CHUNK)].astype(BF16))
        mtok = lax.dot_general(sel_t, exp_ref[:, pl.ds(k0, SEL_CHUNK)], (((0,), (0,)), ((), ())),
                               preferred_element_type=F32)
        keypos = k0 + lax.broadcasted_iota(I32, (tq, SEL_CHUNK), 1)
        bias = jnp.where((mtok > 0.5) & (keypos <= qpos), 0.0, NEG)
        sm = (sc.reshape(r, tq, SEL_CHUNK) + bias[None]).reshape(rows, SEL_CHUNK)
        s_ref[:, pl.ds(k0, SEL_CHUNK)] = sm
        m_ref[...] = jnp.maximum(m_ref[...], jnp.maximum(sm[:, :LANES], sm[:, LANES:]))
        return carry

    lax.fori_loop(0, n_chunks, pass1, 0)
    m_ref[...] = jnp.broadcast_to(jnp.max(m_ref[...], axis=-1, keepdims=True), m_ref.shape)
    l_ref[...] = jnp.zeros(l_ref.shape, F32)
    acc_ref[...] = jnp.zeros(acc_ref.shape, F32)

    def pass2(c, carry):
        k0 = pl.multiple_of(c * SEL_CHUNK, SEL_CHUNK)
        mb = m_ref[...]
        p = jnp.exp(s_ref[:, pl.ds(k0, SEL_CHUNK)] - jnp.concatenate([mb, mb], axis=-1))
        l_ref[...] += p[:, :LANES] + p[:, LANES:]
        acc_ref[...] += _dot_nt(p.astype(BF16), vs_ref[:, pl.ds(k0, SEL_CHUNK)].astype(BF16))
        return carry

    lax.fori_loop(0, n_chunks, pass2, 0)
    l_sel = jnp.maximum(jnp.sum(l_ref[...], axis=-1, keepdims=True), 1e-30)
    out_ref[...] += gate(1) * (acc_ref[...] / l_sel)

    span = min(WINDOW + tq, t)
    w0 = pl.multiple_of(jnp.maximum(q0 + tq - span, 0), tq)
    sc = _dot(qb, kw_ref[:, pl.ds(w0, span)].astype(BF16))
    kp = w0 + lax.broadcasted_iota(I32, (tq, span), 1)
    bias = jnp.where((kp <= qpos) & (kp > qpos - WINDOW), 0.0, NEG)
    sm = (sc.reshape(r, tq, span) + bias[None]).reshape(rows, span)
    p = jnp.exp(sm - jnp.max(sm, axis=-1, keepdims=True))
    l_win = jnp.maximum(jnp.sum(p, axis=-1, keepdims=True), 1e-30)
    o_win = _dot_nt(p.astype(BF16), vw_ref[:, pl.ds(w0, span)].astype(BF16)) / l_win
    out = out_ref[...] + gate(2) * o_win
    o_ref[...] = jnp.concatenate([out[h * tq:(h + 1) * tq] for h in range(r)], axis=-1)


def nsa_prompt(proj, kc, vc, kvt, wint, *, b):
    m = proj.shape[0]
    t = m // b
    g, r, d = NSA_KV_HEADS, NSA_GROUP, NSA_HEAD_DIM
    tq = Q_BLOCK
    nt = t // tq
    ncp = kc.shape[2]
    n_cmp = t // CMP_STRIDE - 1
    n_sel = -(-t // SEL_BLOCK)
    nb = -(-n_sel // 8) * 8
    covt = _cover_matrix(n_cmp, n_sel, ncp, nb).T
    expand = jnp.asarray(np.arange(nb)[:, None] == (np.arange(t)[None, :] // SEL_BLOCK), BF16)
    rows = r * tq
    cmp_spec = pl.BlockSpec((None, None, ncp, d), lambda bi, gi, qi: (bi, gi, 0, 0))
    kv_spec = lambda blk: pl.BlockSpec((None, d, t), lambda bi, gi, qi: (bi, blk * g + gi, 0))
    return pl.pallas_call(
        functools.partial(_nsa_prompt_kernel, n_cmp=n_cmp, n_sel=n_sel),
        grid=(b, g, nt),
        in_specs=[pl.BlockSpec((tq, r * d), lambda bi, gi, qi: (bi * nt + qi, gi)),
                  pl.BlockSpec((tq, LANES), lambda bi, gi, qi: (bi * nt + qi, NSA_HEADS * d // LANES + gi)),
                  cmp_spec, cmp_spec,
                  pl.BlockSpec(covt.shape, lambda bi, gi, qi: (0, 0)),
                  pl.BlockSpec(expand.shape, lambda bi, gi, qi: (0, 0)),
                  kv_spec(2), kv_spec(3), kv_spec(0), kv_spec(1)],
        out_specs=pl.BlockSpec((tq, r * d), lambda bi, gi, qi: (bi * nt + qi, gi)),
        out_shape=jax.ShapeDtypeStruct((m, NSA_HEADS * d), F32),
        scratch_shapes=[pltpu.VMEM((rows, t), F32), pltpu.VMEM((rows, LANES), F32), pltpu.VMEM((rows, LANES), F32),
                        pltpu.VMEM((rows, d), F32), pltpu.VMEM((rows, d), F32)],
        compiler_params=_cparams("parallel", "parallel", "arbitrary"),
        name="nsa_prompt",
    )(proj, proj, kc, vc, covt, expand, kvt, kvt, wint, wint)


def _nsa_sample_select_kernel(q_ref, kc_ref, vc_ref, cov_ref, oc_ref, sel_ref, *, t, past, n_cmp, n_sel):
    rows = q_ref.shape[0]
    g, r = NSA_KV_HEADS, NSA_GROUP
    qb = q_ref[...].astype(BF16)
    qpos_r = past + (lax.broadcasted_iota(I32, (rows, 1), 0) & (t - 1))
    s = _dot_nt(qb, kc_ref[...].astype(BF16))
    nn = lax.broadcasted_iota(I32, s.shape, 1)
    p_cmp = _masked_softmax(s, (nn * CMP_STRIDE + (CMP_BLOCK - 1) <= qpos_r) & (nn < n_cmp))
    oc_ref[...] = _dot(p_cmp.astype(BF16), vc_ref[...].astype(BF16))
    psum = jnp.sum(p_cmp.reshape(g, r, t, s.shape[1]), axis=1).reshape(g * t, s.shape[1])
    imp = jnp.dot(psum, cov_ref[...], precision=HIGHEST, preferred_element_type=F32)
    qpos_gt = past + (lax.broadcasted_iota(I32, (g * t, 1), 0) & (t - 1))
    sel = _select_blocks(imp, qpos_gt >> 6, n_sel)
    lanes = sel.shape[1]
    sel_ref[...] = jnp.broadcast_to(sel.reshape(g, 1, t, lanes), (g, r, t, lanes)).reshape(rows, lanes)


def nsa_sample_select(q4, kc, vc, *, t, past):
    b, rows, _ = q4.shape
    n_cmp = kc.shape[1]
    n_sel = -(-(past + t) // SEL_BLOCK)
    lanes = -(-n_sel // 128) * 128
    cover = _cover_matrix(n_cmp, n_sel, n_cmp, lanes)
    return pl.pallas_call(
        functools.partial(_nsa_sample_select_kernel, t=t, past=past, n_cmp=n_cmp, n_sel=n_sel),
        grid=(b,),
        in_specs=[pl.BlockSpec((None, rows, KVW), lambda bi: (bi, 0, 0)),
                  pl.BlockSpec((None, n_cmp, KVW), lambda bi: (bi, 0, 0)),
                  pl.BlockSpec((None, n_cmp, KVW), lambda bi: (bi, 0, 0)),
                  pl.BlockSpec(cover.shape, lambda bi: (0, 0))],
        out_specs=[pl.BlockSpec((None, rows, KVW), lambda bi: (bi, 0, 0)),
                   pl.BlockSpec((None, rows, lanes), lambda bi: (bi, 0, 0))],
        out_shape=[jax.ShapeDtypeStruct((b, rows, KVW), F32),
                   jax.ShapeDtypeStruct((b, rows, lanes), F32)],
        compiler_params=_cparams("parallel"),
        name="nsa_sample_select",
    )(q4, kc, vc, cover)


def _nsa_sample_attend_kernel(pt_ref, q_ref, sel_ref, oc_ref, gl_ref, *refs, t, past):
    page_refs = refs[:ATT_PAGES_PER_STEP]
    new_ref, win_ref, o_ref, m_ref, l_ref, acc_ref = refs[ATT_PAGES_PER_STEP:]
    rows = q_ref.shape[0]
    p = pl.program_id(1)
    page = page_refs[0].shape[2]
    width = ATT_PAGES_PER_STEP * page
    qb = q_ref[...].astype(BF16)
    sel = sel_ref[...]
    nblk = sel.shape[1]

    @pl.when(p == 0)
    def _():
        _flash_init(m_ref, l_ref, acc_ref)

    sc = jnp.concatenate([_dot(qb, pr[0].astype(BF16)) for pr in page_refs], axis=-1)
    jj = lax.broadcasted_iota(I32, (nblk, width), 0)
    tt = lax.broadcasted_iota(I32, (nblk, width), 1)
    expand = jnp.where(jj == ((p * width + tt) >> 6), 1.0, 0.0).astype(BF16)
    mask = _dot(sel.astype(BF16), expand) > 0.5
    sm = jnp.where(mask, sc, NEG)
    m_old = m_ref[...]
    m_new = jnp.maximum(m_old, jnp.max(sm, axis=-1, keepdims=True))
    alpha = jnp.exp(m_old - m_new)
    pr_b = jnp.where(mask, jnp.exp(sm - m_new), 0.0)
    l_ref[...] = alpha * l_ref[...] + jnp.sum(pr_b, axis=-1, keepdims=True)
    pr_b = pr_b.astype(BF16)
    pv = _dot_nt(pr_b[:, :page], page_refs[0][1].astype(BF16))
    for j in range(1, ATT_PAGES_PER_STEP):
        pv = pv + _dot_nt(pr_b[:, j * page:(j + 1) * page], page_refs[j][1].astype(BF16))
    acc_ref[...] = alpha * acc_ref[...] + pv
    m_ref[...] = m_new

    @pl.when(p == pl.num_programs(1) - 1)
    def _():
        qpos_r = past + (lax.broadcasted_iota(I32, (rows, 1), 0) & (t - 1))
        newr = new_ref[...]
        tp = newr.shape[0]
        jn = lax.broadcasted_iota(I32, (rows, tp), 1)
        new_ok = (jn < t) & (past + jn <= qpos_r)
        blk_new = past // SEL_BLOCK
        sel_new = sel[:, blk_new:blk_new + 1] > 0.5
        sc_n = _dot_nt(qb, newr[:, 2 * KVW:3 * KVW].astype(BF16))
        _flash_update(sc_n, new_ok & sel_new, newr[:, 3 * KVW:4 * KVW].astype(BF16), m_ref, l_ref, acc_ref)
        o_sel = _flash_result(l_ref, acc_ref)

        n_win = win_ref.shape[2]
        s_a = _dot(qb, win_ref[0].astype(BF16))
        s_b = _dot_nt(qb, newr[:, 4 * KVW:5 * KVW].astype(BF16))
        kp_a = (past - n_win) + lax.broadcasted_iota(I32, (rows, n_win), 1)
        ok_a = (kp_a <= qpos_r) & (kp_a > qpos_r - WINDOW) & (kp_a >= 0)
        ok_b = new_ok & (past + jn > qpos_r - WINDOW)
        sm_a = jnp.where(ok_a, s_a, NEG)
        sm_b = jnp.where(ok_b, s_b, NEG)
        mx = jnp.maximum(jnp.max(sm_a, axis=-1, keepdims=True), jnp.max(sm_b, axis=-1, keepdims=True))
        e_a = jnp.where(ok_a, jnp.exp(sm_a - mx), 0.0)
        e_b = jnp.where(ok_b, jnp.exp(sm_b - mx), 0.0)
        den = jnp.maximum(jnp.sum(e_a, axis=-1, keepdims=True) + jnp.sum(e_b, axis=-1, keepdims=True), 1e-30)
        o_win = (_dot_nt((e_a / den).astype(BF16), win_ref[1].astype(BF16))
                 + _dot((e_b / den).astype(BF16), newr[:, 5 * KVW:6 * KVW].astype(BF16)))

        gates = _sigmoid(gl_ref[...])
        o_ref[...] = gates[:, 0:1] * oc_ref[...] + gates[:, 1:2] * o_sel + gates[:, 2:3] * o_win


def nsa_sample_attend(q4, sel, ocmp, gl, cache_t, page_table, new_rows, win_t, *, t, past):
    b, rows, _ = q4.shape
    n_pages = page_table.shape[1]
    page = cache_t.shape[3]
    lanes = sel.shape[2]
    tp = new_rows.shape[1]
    n_win = win_t.shape[3]
    nps = ATT_PAGES_PER_STEP
    per_b = lambda shape: pl.BlockSpec((None,) + shape, lambda bi, pi, pt: (bi,) + (0,) * len(shape))
    grid_spec = pltpu.PrefetchScalarGridSpec(
        num_scalar_prefetch=1,
        grid=(b, n_pages // nps),
        in_specs=[per_b((rows, KVW)), per_b((rows, lanes)), per_b((rows, KVW)), per_b((rows, N_BRANCH))]
        + _page_specs(nps, 1, page)
        + [per_b((tp, N_KV_ROWS * KVW)), per_b((2, KVW, n_win))],
        out_specs=per_b((rows, KVW)),
        scratch_shapes=[pltpu.VMEM((rows, 1), F32), pltpu.VMEM((rows, 1), F32), pltpu.VMEM((rows, KVW), F32)],
    )
    return pl.pallas_call(
        functools.partial(_nsa_sample_attend_kernel, t=t, past=past),
        grid_spec=grid_spec,
        out_shape=jax.ShapeDtypeStruct((b, rows, KVW), F32),
        compiler_params=_cparams("parallel", "arbitrary"),
        name="nsa_sample_attend",
    )(page_table, q4, sel, ocmp, gl, *([cache_t] * nps), new_rows, win_t)


def _route_kernel(x_ref, g_ref, sh_ref, sc_ref, rw_ref, rb_ref, h_ref, r_ref):
    h = _norm_mod(x_ref[...], g_ref[...], sh_ref[...], sc_ref[...])
    h_ref[...] = h.astype(BF16)
    logits = jnp.dot(h, rw_ref[...], precision=HIGHEST, preferred_element_type=F32) + rb_ref[...]
    lane = lax.broadcasted_iota(I32, logits.shape, 1)
    lane_f = lane.astype(F32)
    lg = jnp.where(lane < N_EXPERTS, logits, NEG)
    v1 = jnp.max(lg, axis=-1, keepdims=True)
    i1 = jnp.min(jnp.where(lg == v1, lane_f, 128.0), axis=-1, keepdims=True)
    lg2 = jnp.where(lane_f == i1, NEG, lg)
    v2 = jnp.max(lg2, axis=-1, keepdims=True)
    i2 = jnp.min(jnp.where(lg2 == v2, lane_f, 128.0), axis=-1, keepdims=True)
    e = jnp.exp(v2 - v1)
    w1 = 1.0 / (1.0 + e)
    w2 = e / (1.0 + e)
    r_ref[...] = jnp.where(lane == 0, i1, jnp.where(lane == 1, i2, jnp.where(lane == 2, w1, jnp.where(lane == 3, w2, 0.0))))


def moe_route(x, g, shift, scale, rw_pad, rb_pad, *, per_token, tm, tpb):
    m, d = x.shape
    sh, sh_spec = _mod_arg(shift, per_token, tm, tpb)
    sc, sc_spec = _mod_arg(scale, per_token, tm, tpb)
    return pl.pallas_call(
        _route_kernel,
        grid=(m // tm,),
        in_specs=[pl.BlockSpec((tm, d), lambda i: (i, 0)), _vec_spec(d), sh_spec, sc_spec,
                  pl.BlockSpec((d, 128), lambda i: (0, 0)), _vec_spec(128)],
        out_specs=[pl.BlockSpec((tm, d), lambda i: (i, 0)), pl.BlockSpec((tm, 128), lambda i: (i, 0))],
        out_shape=[jax.ShapeDtypeStruct((m, d), BF16), jax.ShapeDtypeStruct((m, 128), F32)],
        compiler_params=_cparams("parallel"),
        name="moe_route",
    )(x, g.reshape(1, d), sh, sc, rw_pad, rb_pad)


def _moe_kernel(te_ref, nv_ref, x_ref, wg_ref, wu_ref, wd_ref, ws_ref, o_ref, acc_ref):
    i = pl.program_id(0)
    f = pl.program_id(1)

    @pl.when(f == 0)
    def _():
        acc_ref[...] = jnp.zeros(acc_ref.shape, F32)

    @pl.when(i < nv_ref[0])
    def _():
        x = x_ref[...]
        act = _silu(_dot(x, wg_ref[...])) * _dot(x, wu_ref[...])
        acc_ref[...] += _dot(act.astype(BF16), wd_ref[...])

    @pl.when(f == pl.num_programs(1) - 1)
    def _():
        o_ref[...] = ws_ref[...] * acc_ref[...]


def moe_experts(xs, w_slot, tile_expert, n_valid, w_gu_b, w_down_b, *, tm, tf):
    p, d = xs.shape
    edim = w_down_b.shape[1]
    nf = edim // tf
    n_tiles = p // tm

    def wmap(off):
        def index(i, f, te, nv):
            ok = i < nv[0]
            return (te[i], 0, off + jnp.where(ok, f, nf - 1))
        return index

    def dmap(i, f, te, nv):
        return (te[i], jnp.where(i < nv[0], f, nf - 1), 0)

    grid_spec = pltpu.PrefetchScalarGridSpec(
        num_scalar_prefetch=2,
        grid=(n_tiles, nf),
        in_specs=[pl.BlockSpec((tm, d), lambda i, f, te, nv: (i, 0)),
                  pl.BlockSpec((None, d, tf), wmap(0)),
                  pl.BlockSpec((None, d, tf), wmap(nf)),
                  pl.BlockSpec((None, tf, d), dmap),
                  pl.BlockSpec((tm, 1), lambda i, f, te, nv: (i, 0))],
        out_specs=pl.BlockSpec((tm, d), lambda i, f, te, nv: (i, 0)),
        scratch_shapes=[pltpu.VMEM((tm, d), F32)],
    )
    return pl.pallas_call(
        _moe_kernel,
        grid_spec=grid_spec,
        out_shape=jax.ShapeDtypeStruct((p, d), F32),
        compiler_params=_cparams("parallel", "arbitrary"),
        name="moe_experts",
    )(tile_expert, n_valid, xs, w_gu_b, w_gu_b, w_down_b, w_slot)


def _moe_tables(route, tm):
    m = route.shape[0]
    na = TOP_K * m
    e = route[:, :TOP_K].astype(I32).reshape(na)
    w = route[:, TOP_K:2 * TOP_K].reshape(na)
    onehot = (e[:, None] == jnp.arange(N_EXPERTS, dtype=I32)[None, :]).astype(I32)
    within = jnp.sum((jnp.cumsum(onehot, axis=0) - onehot) * onehot, axis=1)
    counts = jnp.sum(onehot, axis=0)
    padded = ((counts + tm - 1) // tm) * tm
    ends = jnp.cumsum(padded)
    starts = ends - padded
    dest = starts[e] + within
    n_slots = (-(-na // tm) + N_EXPERTS) * tm
    src_tok = jnp.zeros((n_slots,), I32).at[dest].set(jnp.arange(na, dtype=I32) // TOP_K)
    w_slot = jnp.zeros((n_slots,), F32).at[dest].set(w)
    n_tiles = n_slots // tm
    n_valid = (ends[-1] // tm).astype(I32)
    tile_start = jnp.arange(n_tiles, dtype=I32) * tm
    tile_expert = jnp.sum((tile_start[:, None] >= ends[None, :]).astype(I32), axis=1)
    last = jnp.take(tile_expert, jnp.maximum(n_valid - 1, 0))
    tile_expert = jnp.where(jnp.arange(n_tiles) < n_valid, tile_expert, last).astype(I32)
    return src_tok, w_slot[:, None], tile_expert, n_valid.reshape(1), dest


def _split_mod(mod):
    return [mod[:, i * D_MODEL:(i + 1) * D_MODEL] for i in range(mod.shape[1] // D_MODEL)]


def kernel(x_prompt, x_sample, c_prompt, c_sample, state_ret, cache_kv, cache_win, page_table, w_mod, b_mod, norm_g, ret_w_in, ret_gn_g, ret_w_out, kv_w_mod, kv_b_mod, kv_norm_g, kv_w, cmp_pos, cmp_w1, cmp_b1, cmp_w2, nsa_w_in, nsa_w_out, ffn_w_gu, ffn_w_down, moe_router_w, moe_router_b, moe_w_gu, moe_w_down):
    bp, t, d = x_prompt.shape
    bs, ts, _ = x_sample.shape
    mp, ms = bp * t, bs * ts
    n_pool, page = cache_kv.shape[:2]
    past = page_table.shape[1] * page
    g, r, hd = NSA_KV_HEADS, NSA_GROUP, NSA_HEAD_DIM

    ret_w_in_b = ret_w_in[0].astype(BF16)
    ret_w_out_b = ret_w_out[0].astype(BF16)
    kv_w_b = kv_w.astype(BF16)
    kv_wt_b = kv_w.T.astype(BF16)
    nq = NSA_HEADS * hd
    gate_w = nsa_w_in[0][:, nq:].reshape(d, g, r * N_BRANCH)
    gate_w = jnp.pad(gate_w, ((0, 0), (0, 0), (0, LANES - r * N_BRANCH))).reshape(d, g * LANES)
    nsa_w_in_b = jnp.concatenate([nsa_w_in[0][:, :nq], gate_w], axis=1).astype(BF16)
    nsa_w_out_b = nsa_w_out[0].astype(BF16)
    ffn_w_gu_b = ffn_w_gu[0].astype(BF16)
    ffn_w_down_b = ffn_w_down[0].astype(BF16)
    moe_w_gu_b = moe_w_gu[0].astype(BF16)
    moe_w_down_b = moe_w_down[0].astype(BF16)
    rw_pad = jnp.pad(moe_router_w[0], ((0, 0), (0, 128 - N_EXPERTS)))
    rb_pad = jnp.pad(moe_router_b[0], (0, 128 - N_EXPERTS)).reshape(1, 128)
    cw = _compress_weights(cmp_pos, cmp_w1, cmp_b1, cmp_w2)

    c_all = jnp.concatenate([c_prompt, c_sample], axis=0)
    mods = [cond_matmul(c_all, w_mod[layer], b_mod[layer]) for layer in range(w_mod.shape[0])]
    kv_mod = cond_matmul(c_all, kv_w_mod, kv_b_mod)

    tm_p = min(512, t)
    tm_f = min(1024, t)
    groups = {
        'p': dict(x=x_prompt.reshape(mp, d), b=bp, t=t, per_token=False, tm=tm_p, tpb=t // tm_p,
                  tm_f=tm_f, tpb_f=t // tm_f,
                  mod=lambda a: a[:bp]),
        's': dict(x=x_sample.reshape(ms, d), b=bs, t=ts, per_token=True, tm=ms, tpb=1, tm_f=ms, tpb_f=1,
                  mod=lambda a: jnp.repeat(a[bp:], ts, axis=0)),
    }
    out = {}

    for name, gr in groups.items():
        kw = dict(per_token=gr['per_token'], tm=gr['tm'], tpb=gr['tpb'])
        m0 = [gr['mod'](a) for a in _split_mod(mods[0])]
        x = gr['x']
        proj = norm_mod_matmul(x, norm_g[0, 0], m0[0], m0[1], ret_w_in_b, tn=1024, **kw)
        proj = proj.reshape(gr['b'], gr['t'], -1)
        if name == 'p':
            pos = jnp.arange(t)
            s0 = jnp.zeros((bp, RET_HEADS, RET_DK, RET_DV), F32)
            o, s_new = retention(proj, pos, s0, ret_gn_g[0], math.gcd(t, RET_CHUNK))
        else:
            pos = past + jnp.arange(ts)
            proj = jnp.pad(proj, ((0, 0), (0, RET_CHUNK - ts), (0, 0)))
            o, s_new = retention(proj, pos, state_ret[0], ret_gn_g[0], math.gcd(ts, RET_CHUNK))
            o = o[:, :ts]
        out['ret_' + name] = s_new[None]
        x = matmul_norm_residual(o.reshape(-1, d), ret_w_out_b, x, m0[2], norm_g[0, 1], **kw)
        x = ffn_sublayer(x, norm_g[0, 2], m0[3], m0[4], ffn_w_gu_b, ffn_w_down_b, m0[5], norm_g[0, 3],
                         per_token=gr['per_token'], tm=gr['tm_f'], tpb=gr['tpb_f'], tf=256)
        gr['x1'] = x

    for name, gr in groups.items():
        kw = dict(per_token=gr['per_token'], tm=gr['tm'], tpb=gr['tpb'])
        m1 = [gr['mod'](a) for a in _split_mod(mods[1])]
        kvm = [gr['mod'](a) for a in _split_mod(kv_mod)]
        gr['m1'] = m1
        x = gr['x1']
        b_, t_ = gr['b'], gr['t']
        proj = norm_mod_matmul(x, norm_g[1, 0], m1[0], m1[1], nsa_w_in_b, **kw)
        if name == 'p':
            kvt, wint = norm_mod_matmul_t(x, kv_norm_g, kvm[0], kvm[1], kv_wt_b,
                                          (N_PAGED_ROWS * KVW, (N_KV_ROWS - N_PAGED_ROWS) * KVW), b=b_, tm=gr['tm'])
            out['kv_p'] = kvt.reshape(b_, N_PAGED_ROWS, g, hd, t_).transpose(0, 4, 1, 2, 3)
            n_keep = min(WINDOW, t_)
            out['win_p'] = wint[:, :, t_ - n_keep:].reshape(b_, 2, g, hd, n_keep).transpose(0, 4, 1, 2, 3)
            cmp = compress_prompt(kvt, cw)
            cmp = cmp.reshape(2, b_, -1, g, hd).transpose(0, 1, 3, 2, 4)
            o = nsa_prompt(proj, cmp[0], cmp[1], kvt, wint, b=b_)
        else:
            rows = norm_mod_matmul(x, kv_norm_g, kvm[0], kvm[1], kv_w_b, **kw)
            q = (proj[:, :NSA_HEADS * hd] * (hd ** -0.5)).reshape(b_, t_, g, r, hd)
            gl = proj[:, NSA_HEADS * hd:].reshape(b_, t_, g, LANES)[..., :r * N_BRANCH]
            gl = gl.reshape(b_, t_, g, r, N_BRANCH)
            rows3 = rows.reshape(b_, t_, N_KV_ROWS * KVW)
            rows6 = rows.reshape(b_, t_, N_KV_ROWS, g, hd)
            out['kv_s'] = rows6[:, :, :N_PAGED_ROWS]
            cache_t = cache_kv.transpose(0, 2, 3, 4, 1).reshape(n_pool, N_PAGED_ROWS, KVW, page)
            win_t5 = cache_win.transpose(0, 2, 3, 4, 1)
            n_win = cache_win.shape[1]
            new_win_t = rows6[:, :, N_PAGED_ROWS:].transpose(0, 2, 3, 4, 1)
            out['win_s'] = jnp.concatenate([win_t5, new_win_t], axis=-1)[..., t_:].transpose(0, 4, 1, 2, 3)
            new_sub = jnp.pad(rows3[:, :, :2 * KVW], ((0, 0), (0, CMP_STRIDE - t_), (0, 0)))
            new_sub = new_sub.reshape(b_, CMP_STRIDE, 2, KVW).transpose(2, 0, 1, 3)
            cmp = compress_sample(cache_t, page_table, new_sub, cw)
            eye = jnp.eye(g, dtype=F32)
            qrows = q.transpose(0, 2, 3, 1, 4).reshape(b_, g, r * t_, hd)
            q4 = jnp.einsum('bgxd,gj->bgxjd', qrows, eye).reshape(b_, g * r * t_, KVW)
            glr = gl.transpose(0, 2, 3, 1, 4).reshape(b_, g * r * t_, N_BRANCH)
            ocmp, sel = nsa_sample_select(q4, cmp[0], cmp[1], t=t_, past=past)
            new_rows = jnp.pad(rows3, ((0, 0), (0, 128 - t_), (0, 0)))
            o4 = nsa_sample_attend(q4, sel, ocmp, glr, cache_t, page_table, new_rows,
                                   win_t5.reshape(b_, 2, KVW, n_win), t=t_, past=past)
            o4 = o4.reshape(b_, g, r, t_, g, hd)
            o = jnp.einsum('bgrtjd,gj->btgrd', o4, eye).reshape(ms, NSA_HEADS * hd)
        gr['x2'] = matmul_norm_residual(o, nsa_w_out_b, x, m1[2], norm_g[1, 1], **kw)

    hs, routes = [], []
    for name, gr in groups.items():
        kw = dict(per_token=gr['per_token'], tm=gr['tm'], tpb=gr['tpb'])
        m1 = gr['m1']
        h, route = moe_route(gr['x2'], norm_g[1, 2], m1[3], m1[4], rw_pad, rb_pad, **kw)
        hs.append(h)
        routes.append(route)
    h_all = jnp.concatenate(hs, axis=0)
    route_all = jnp.concatenate(routes, axis=0)
    tm_e = 512 if mp >= 4096 else 128
    src_tok, w_slot, tile_expert, n_valid, dest = _moe_tables(route_all, tm_e)
    xs = jnp.take(h_all, src_tok, axis=0)
    ys = moe_experts(xs, w_slot, tile_expert, n_valid, moe_w_gu_b, moe_w_down_b, tm=tm_e, tf=512)
    y_all = jnp.take(ys, dest[0::2], axis=0) + jnp.take(ys, dest[1::2], axis=0)
    for name, gr, y in (('p', groups['p'], y_all[:mp]), ('s', groups['s'], y_all[mp:])):
        kw = dict(per_token=gr['per_token'], tm=gr['tm'], tpb=gr['tpb'])
        out['y_' + name] = norm_residual(y, gr['x2'], gr['m1'][5], norm_g[1, 3], **kw)

    return (out['y_p'].reshape(bp, t, d), out['y_s'].reshape(bs, ts, d),
            out['ret_p'], out['ret_s'], out['kv_p'], out['kv_s'], out['win_p'], out['win_s'])
```

```python
import functools
import math

import numpy as np
import jax
import jax.numpy as jnp
from jax import lax
from jax.experimental import pallas as pl
from jax.experimental.pallas import tpu as pltpu

F32 = jnp.float32
BF16 = jnp.bfloat16
I32 = jnp.int32

D_MODEL = 1024
N_MOD = 6
RET_HEADS = 4
RET_DK = 256
RET_DV = 256
RET_CHUNK = 128
ROPE_BASE = 10000.0
NSA_HEADS = 16
NSA_KV_HEADS = 4
NSA_GROUP = 4
NSA_HEAD_DIM = 64
N_BRANCH = 3
N_KV_ROWS = 6
N_PAGED_ROWS = 4
CMP_BLOCK = 32
CMP_STRIDE = 16
CMP_HIDDEN = 128
SEL_BLOCK = 64
N_SEL = 16
FORCED_SCORE = 1.0e4
WINDOW = 512
Q_BLOCK = 128
N_EXPERTS = 8
TOP_K = 2
EPS = 1e-6

NEG = -1.0e30
KVW = NSA_KV_HEADS * NSA_HEAD_DIM
VMEM_LIMIT_BYTES = 56 * 1024 * 1024
HIGHEST = lax.Precision.HIGHEST


def _cparams(*sem):
    return pltpu.CompilerParams(dimension_semantics=sem, vmem_limit_bytes=VMEM_LIMIT_BYTES)


def _sigmoid(x):
    return 1.0 / (1.0 + jnp.exp(-x))


def _silu(x):
    return x * _sigmoid(x)


def _gelu_tanh(x):
    return x * (0.5 * (1.0 + jnp.tanh(math.sqrt(2.0 / math.pi) * (x + 0.044715 * (x * x * x)))))


def _norm_mod(x, g, shift, scale):
    ms = jnp.mean(x * x, axis=-1, keepdims=True)
    return (x * lax.rsqrt(ms + EPS) * g) * (1.0 + scale) + shift


def _rms_residual(x, gate, o, g):
    ms = jnp.mean(o * o, axis=-1, keepdims=True)
    return x + gate * (o * lax.rsqrt(ms + EPS) * g)


def _dot(a, b):
    return jnp.dot(a, b, preferred_element_type=F32)


def _dot_nt(a, b):
    return lax.dot_general(a, b, (((1,), (1,)), ((), ())), preferred_element_type=F32)


def _masked_softmax(s, mask):
    sm = jnp.where(mask, s, NEG)
    m = jnp.max(sm, axis=-1, keepdims=True)
    e = jnp.where(mask, jnp.exp(sm - m), 0.0)
    return e / jnp.maximum(jnp.sum(e, axis=-1, keepdims=True), 1e-30)


def _flash_init(m_ref, l_ref, acc_ref):
    m_ref[...] = jnp.full(m_ref.shape, NEG, F32)
    l_ref[...] = jnp.zeros(l_ref.shape, F32)
    acc_ref[...] = jnp.zeros(acc_ref.shape, F32)


def _flash_update(s, mask, v_b, m_ref, l_ref, acc_ref):
    sm = jnp.where(mask, s, NEG)
    m_old = m_ref[...]
    m_new = jnp.maximum(m_old, jnp.max(sm, axis=-1, keepdims=True))
    alpha = jnp.exp(m_old - m_new)
    p = jnp.where(mask, jnp.exp(sm - m_new), 0.0)
    l_ref[...] = alpha * l_ref[...] + jnp.sum(p, axis=-1, keepdims=True)
    acc_ref[...] = alpha * acc_ref[...] + _dot(p.astype(BF16), v_b)
    m_ref[...] = m_new


def _flash_result(l_ref, acc_ref):
    return acc_ref[...] / jnp.maximum(l_ref[...], 1e-30)


def _select_blocks(imp, cur, n_sel):
    rows, lanes = imp.shape
    blk = lax.broadcasted_iota(I32, (rows, lanes), 1)
    valid = (blk <= cur) & (blk < n_sel)
    forced = (blk == 0) | (blk == cur) | (blk == cur - 1)
    score = jnp.where(valid, jnp.where(forced, FORCED_SCORE, imp), -1.0)
    rank = jnp.zeros((rows, lanes), F32)
    for i in range(n_sel):
        ci = score[:, i:i + 1]
        beats = (ci > score) | ((ci == score) & (blk > i))
        rank = rank + jnp.where(beats, 1.0, 0.0)
    return jnp.where(valid & (rank < float(min(N_SEL, n_sel))), 1.0, 0.0)


def _cond_kernel(c_ref, w_ref, b_ref, o_ref):
    sc = _silu(c_ref[...])
    o_ref[...] = _dot(sc.astype(BF16), w_ref[...].astype(BF16)) + b_ref[...]


def cond_matmul(c, w, b):
    bc, d = c.shape
    n = w.shape[1]
    tn = 1024
    return pl.pallas_call(
        _cond_kernel,
        grid=(n // tn,),
        in_specs=[pl.BlockSpec((bc, d), lambda j: (0, 0)),
                  pl.BlockSpec((d, tn), lambda j: (0, j)),
                  pl.BlockSpec((1, tn), lambda j: (0, j))],
        out_specs=pl.BlockSpec((bc, tn), lambda j: (0, j)),
        out_shape=jax.ShapeDtypeStruct((bc, n), F32),
        compiler_params=_cparams("parallel"),
        name="cond_matmul",
    )(c, w, b.reshape(1, n))


def _mod_arg(m, per_token, tm, tiles_per_batch):
    d = m.shape[-1]
    if per_token:
        return m, pl.BlockSpec((tm, d), lambda i, *_: (i, 0))
    return m[:, None, :], pl.BlockSpec((None, 1, d), lambda i, *_: (i // tiles_per_batch, 0, 0))


def _vec_spec(d):
    return pl.BlockSpec((1, d), lambda i, *_: (0, 0))


def _nmm_kernel(x_ref, g_ref, sh_ref, sc_ref, w_ref, o_ref, h_ref):
    @pl.when(pl.program_id(1) == 0)
    def _():
        h_ref[...] = _norm_mod(x_ref[...], g_ref[...], sh_ref[...], sc_ref[...]).astype(BF16)

    o_ref[...] = _dot(h_ref[...], w_ref[...]).astype(o_ref.dtype)


def norm_mod_matmul(x, g, shift, scale, w_b, *, per_token, tm, tpb, tn=None, out_dtype=F32):
    m, d = x.shape
    n = w_b.shape[1]
    tn = n if tn is None else tn
    sh, sh_spec = _mod_arg(shift, per_token, tm, tpb)
    sc, sc_spec = _mod_arg(scale, per_token, tm, tpb)
    return pl.pallas_call(
        _nmm_kernel,
        grid=(m // tm, n // tn),
        in_specs=[pl.BlockSpec((tm, d), lambda i, j: (i, 0)), _vec_spec(d), sh_spec, sc_spec,
                  pl.BlockSpec((d, tn), lambda i, j: (0, j))],
        out_specs=pl.BlockSpec((tm, tn), lambda i, j: (i, j)),
        out_shape=jax.ShapeDtypeStruct((m, n), out_dtype),
        scratch_shapes=[pltpu.VMEM((tm, d), BF16)],
        compiler_params=_cparams("parallel", "arbitrary"),
        name="norm_mod_matmul",
    )(x, g.reshape(1, d), sh, sc, w_b)


def _nmm_t_kernel(x_ref, g_ref, sh_ref, sc_ref, wt_ref, *o_refs):
    h = _norm_mod(x_ref[...], g_ref[...], sh_ref[...], sc_ref[...]).astype(BF16)
    o = _dot_nt(wt_ref[...], h)
    row = 0
    for o_ref in o_refs:
        o_ref[...] = o[row:row + o_ref.shape[0]]
        row += o_ref.shape[0]


def norm_mod_matmul_t(x, g, shift, scale, wt_b, splits, *, b, tm):
    m, d = x.shape
    t = m // b
    tpb = t // tm
    n = wt_b.shape[0]
    sh, sh_spec = _mod_arg(shift, False, tm, tpb)
    sc, sc_spec = _mod_arg(scale, False, tm, tpb)
    return pl.pallas_call(
        _nmm_t_kernel,
        grid=(m // tm,),
        in_specs=[pl.BlockSpec((tm, d), lambda i: (i, 0)), _vec_spec(d), sh_spec, sc_spec,
                  pl.BlockSpec((n, d), lambda i: (0, 0))],
        out_specs=[pl.BlockSpec((None, ni, tm), lambda i: (i // tpb, 0, i % tpb)) for ni in splits],
        out_shape=[jax.ShapeDtypeStruct((b, ni, t), F32) for ni in splits],
        compiler_params=_cparams("parallel"),
        name="norm_mod_matmul_t",
    )(x, g.reshape(1, d), sh, sc, wt_b)


def _mnr_kernel(a_ref, w_ref, x_ref, gate_ref, g_ref, o_ref):
    o = _dot(a_ref[...].astype(BF16), w_ref[...])
    o_ref[...] = _rms_residual(x_ref[...], gate_ref[...], o, g_ref[...])


def matmul_norm_residual(a, w_b, x, gate, g, *, per_token, tm, tpb):
    m, k = a.shape
    d = w_b.shape[1]
    gt, gt_spec = _mod_arg(gate, per_token, tm, tpb)
    return pl.pallas_call(
        _mnr_kernel,
        grid=(m // tm,),
        in_specs=[pl.BlockSpec((tm, k), lambda i: (i, 0)),
                  pl.BlockSpec((k, d), lambda i: (0, 0)),
                  pl.BlockSpec((tm, d), lambda i: (i, 0)), gt_spec, _vec_spec(d)],
        out_specs=pl.BlockSpec((tm, d), lambda i: (i, 0)),
        out_shape=jax.ShapeDtypeStruct((m, d), F32),
        compiler_params=_cparams("parallel"),
        name="matmul_norm_residual",
    )(a, w_b, x, gt, g.reshape(1, d))


def _nr_kernel(y_ref, x_ref, gate_ref, g_ref, o_ref):
    o_ref[...] = _rms_residual(x_ref[...], gate_ref[...], y_ref[...], g_ref[...])


def norm_residual(y, x, gate, g, *, per_token, tm, tpb):
    m, d = x.shape
    gt, gt_spec = _mod_arg(gate, per_token, tm, tpb)
    return pl.pallas_call(
        _nr_kernel,
        grid=(m // tm,),
        in_specs=[pl.BlockSpec((tm, d), lambda i: (i, 0)),
                  pl.BlockSpec((tm, d), lambda i: (i, 0)), gt_spec, _vec_spec(d)],
        out_specs=pl.BlockSpec((tm, d), lambda i: (i, 0)),
        out_shape=jax.ShapeDtypeStruct((m, d), F32),
        compiler_params=_cparams("parallel"),
        name="norm_residual",
    )(y, x, gt, g.reshape(1, d))


def _ffn_kernel(x_ref, g2_ref, sh_ref, sc_ref, wg_ref, wu_ref, wd_ref, gate_ref, g3_ref, o_ref,
                h_ref, acc_ref):
    f = pl.program_id(1)

    @pl.when(f == 0)
    def _():
        h_ref[...] = _norm_mod(x_ref[...], g2_ref[...], sh_ref[...], sc_ref[...]).astype(BF16)
        acc_ref[...] = jnp.zeros(acc_ref.shape, F32)

    h = h_ref[...]
    act = _silu(_dot(h, wg_ref[...])) * _dot(h, wu_ref[...])
    acc_ref[...] += _dot(act.astype(BF16), wd_ref[...])

    @pl.when(f == pl.num_programs(1) - 1)
    def _():
        o_ref[...] = _rms_residual(x_ref[...], gate_ref[...], acc_ref[...], g3_ref[...])


def ffn_sublayer(x, g2, shift, scale, w_gu_b, w_down_b, gate, g3, *, per_token, tm, tpb, tf):
    m, d = x.shape
    fdim = w_down_b.shape[0]
    nf = fdim // tf
    sh, sh_spec = _mod_arg(shift, per_token, tm, tpb)
    sc, sc_spec = _mod_arg(scale, per_token, tm, tpb)
    gt, gt_spec = _mod_arg(gate, per_token, tm, tpb)
    return pl.pallas_call(
        _ffn_kernel,
        grid=(m // tm, nf),
        in_specs=[pl.BlockSpec((tm, d), lambda i, f: (i, 0)), _vec_spec(d), sh_spec, sc_spec,
                  pl.BlockSpec((d, tf), lambda i, f: (0, f)),
                  pl.BlockSpec((d, tf), lambda i, f: (0, nf + f)),
                  pl.BlockSpec((tf, d), lambda i, f: (f, 0)),
                  gt_spec, _vec_spec(d)],
        out_specs=pl.BlockSpec((tm, d), lambda i, f: (i, 0)),
        out_shape=jax.ShapeDtypeStruct((m, d), F32),
        scratch_shapes=[pltpu.VMEM((tm, d), BF16), pltpu.VMEM((tm, d), F32)],
        compiler_params=_cparams("parallel", "arbitrary"),
        name="ffn_sublayer",
    )(x, g2.reshape(1, d), sh, sc, w_gu_b, w_gu_b, w_down_b, gt, g3.reshape(1, d))


def _ret_kernel(q_ref, k_ref, v_ref, gt_ref, cos_ref, sin_ref, dm_ref, xi_ref, zt_ref, gc_ref,
                gn_ref, s0_ref, o_ref, s_ref):
    @pl.when(pl.program_id(1) == 0)
    def _():
        s_ref[...] = s0_ref[...]

    cos = cos_ref[...]
    sin = sin_ref[...]
    half = RET_DK // 2

    def rot(x):
        x1 = x[:, :half]
        x2 = x[:, half:]
        return jnp.concatenate([x1 * cos - x2 * sin, x2 * cos + x1 * sin], axis=-1)

    for h in range(RET_HEADS):
        kcols = slice(h * RET_DK, (h + 1) * RET_DK)
        vcols = slice(h * RET_DV, (h + 1) * RET_DV)
        q = rot(q_ref[:, kcols])
        k = rot(k_ref[:, kcols]) * (RET_DK ** -0.5)
        qb = q.astype(BF16)
        kb = k.astype(BF16)
        vb = v_ref[:, vcols].astype(BF16)
        state = s_ref[h]
        inner = _dot_nt(qb, kb) * dm_ref[h]
        o = _dot(inner.astype(BF16), vb) + _dot(qb, state.astype(BF16)) * xi_ref[h]
        kz = (k * zt_ref[h]).astype(BF16)
        upd = lax.dot_general(kz, vb, (((0,), (0,)), ((), ())), preferred_element_type=F32)
        s_ref[h] = gc_ref[h] * state + upd
        mu = jnp.mean(o, axis=-1, keepdims=True)
        dev = o - mu
        var = jnp.mean(dev * dev, axis=-1, keepdims=True)
        on = dev * lax.rsqrt(var + EPS) * gn_ref[:, vcols]
        o_ref[:, vcols] = _silu(gt_ref[:, vcols]) * on


def _ret_tables(chunk, rows):
    h = RET_HEADS
    log_g = jnp.log(1.0 - jnp.exp(jnp.linspace(math.log(1.0 / 32), math.log(1.0 / 512), h, dtype=F32)))
    i = jnp.arange(chunk, dtype=F32)
    diff = i[:, None] - i[None, :]
    dmat = jnp.where(diff >= 0, jnp.exp(log_g[:, None, None] * jnp.maximum(diff, 0.0)), 0.0)
    xi = jnp.exp(log_g[:, None] * (i + 1.0))
    zeta = jnp.exp(log_g[:, None] * (chunk - 1.0 - i))
    gch = jnp.exp(log_g * chunk)
    pad = rows - chunk
    dmat = jnp.pad(dmat, ((0, 0), (0, pad), (0, pad)))
    xi = jnp.pad(xi, ((0, 0), (0, pad)))[..., None]
    zeta = jnp.pad(zeta, ((0, 0), (0, pad)))[..., None]
    gch = jnp.broadcast_to(gch[:, None, None], (h, 1, RET_DV))
    return dmat, xi, zeta, gch


def _rope_tables(pos, rows):
    half = RET_DK // 2
    inv = ROPE_BASE ** (-jnp.arange(half, dtype=F32) / half)
    ang = pos.astype(F32)[:, None] * inv[None, :]
    pad = rows - pos.shape[0]
    return jnp.pad(jnp.cos(ang), ((0, pad), (0, 0))), jnp.pad(jnp.sin(ang), ((0, pad), (0, 0)))


def retention(proj, pos, s0, gn_g, chunk):
    b, t, _ = proj.shape
    c = RET_CHUNK
    n = t // c
    h = RET_HEADS
    dmat, xi, zeta, gch = _ret_tables(chunk, c)
    cos, sin = _rope_tables(pos, t)
    col = lambda j: pl.BlockSpec((None, c, h * RET_DK), lambda bi, ni: (bi, ni, j))
    const = lambda a: pl.BlockSpec(a.shape, lambda bi, ni: (0,) * a.ndim)
    state_spec = pl.BlockSpec((None, h, RET_DK, RET_DV), lambda bi, ni: (bi, 0, 0, 0))
    gn = gn_g.reshape(1, -1)
    o, s = pl.pallas_call(
        _ret_kernel,
        grid=(b, n),
        in_specs=[col(0), col(1), col(2), col(3),
                  pl.BlockSpec((c, RET_DK // 2), lambda bi, ni: (ni, 0)),
                  pl.BlockSpec((c, RET_DK // 2), lambda bi, ni: (ni, 0)),
                  const(dmat), const(xi), const(zeta), const(gch), const(gn), state_spec],
        out_specs=[pl.BlockSpec((None, c, h * RET_DV), lambda bi, ni: (bi, ni, 0)), state_spec],
        out_shape=[jax.ShapeDtypeStruct((b, t, h * RET_DV), F32),
                   jax.ShapeDtypeStruct((b, h, RET_DK, RET_DV), F32)],
        compiler_params=_cparams("parallel", "arbitrary"),
        name="retention",
    )(proj, proj, proj, proj, cos, sin, dmat, xi, zeta, gch, gn, s0)
    return o, s


def _compress_weights(cmp_pos, cmp_w1, cmp_b1, cmp_w2):
    g, d, hd = NSA_KV_HEADS, NSA_HEAD_DIM, CMP_HIDDEN
    eye = jnp.eye(g, dtype=F32)
    w1 = cmp_w1.reshape(2, CMP_BLOCK, d, hd)
    bd1 = jnp.einsum('ksdh,gj->ksgdjh', w1, eye).reshape(2, CMP_BLOCK, g * d, g * hd).astype(BF16)
    bd2 = jnp.einsum('khd,gj->kghjd', cmp_w2, eye).reshape(2, g * hd, g * d).astype(BF16)
    pos = jnp.tile(cmp_pos[:, :, None, :], (1, 1, g, 1)).reshape(2, CMP_BLOCK, 1, g * d)
    b1 = jnp.tile(cmp_b1[:, None, :], (1, g, 1)).reshape(2, 1, g * hd)
    return bd1, bd2, pos, b1


def _compress_finish(p0, p1, b1, bd2):
    rows = p0.shape[0]
    hid = b1 + p0 + pltpu.roll(p1, rows - 1, 0)
    return _dot(_gelu_tanh(hid).astype(BF16), bd2)


LANES = 128
CMP_PAGES_PER_STEP = 8
ATT_PAGES_PER_STEP = 16


def _tokens_to_sublanes(xt_ref, tr_ref):
    for half in range(KVW // LANES):
        tr_ref[half] = xt_ref[half * LANES:(half + 1) * LANES, :].T


def _cmp_prompt_kernel(xt_ref, bd1_ref, bd2_ref, pos_ref, b1_ref, o_ref, tr_ref):
    n_sub = xt_ref.shape[1] // CMP_STRIDE
    hid = bd1_ref.shape[-1]
    _tokens_to_sublanes(xt_ref, tr_ref)
    p0 = jnp.zeros((n_sub, hid), F32)
    p1 = jnp.zeros((n_sub, hid), F32)
    for s in range(CMP_STRIDE):
        xs = jnp.concatenate([tr_ref[half, pl.ds(s, n_sub, stride=CMP_STRIDE), :]
                              for half in range(KVW // LANES)], axis=-1)
        p0 = p0 + _dot((xs + pos_ref[s]).astype(BF16), bd1_ref[s])
        p1 = p1 + _dot((xs + pos_ref[CMP_STRIDE + s]).astype(BF16), bd1_ref[CMP_STRIDE + s])
    o_ref[...] = _compress_finish(p0, p1, b1_ref[...], bd2_ref[...])


def compress_prompt(kvt, cw):
    bd1, bd2, pos, b1 = cw
    b, _, t = kvt.shape
    n_sub = t // CMP_STRIDE
    return pl.pallas_call(
        _cmp_prompt_kernel,
        grid=(b, 2),
        in_specs=[pl.BlockSpec((None, KVW, t), lambda bi, ki: (bi, ki, 0)),
                  pl.BlockSpec((None,) + bd1.shape[1:], lambda bi, ki: (ki, 0, 0, 0)),
                  pl.BlockSpec((None,) + bd2.shape[1:], lambda bi, ki: (ki, 0, 0)),
                  pl.BlockSpec((None,) + pos.shape[1:], lambda bi, ki: (ki, 0, 0, 0)),
                  pl.BlockSpec((None,) + b1.shape[1:], lambda bi, ki: (ki, 0, 0))],
        out_specs=pl.BlockSpec((None, None, n_sub, KVW), lambda bi, ki: (ki, bi, 0, 0)),
        out_shape=jax.ShapeDtypeStruct((2, b, n_sub, KVW), F32),
        scratch_shapes=[pltpu.VMEM((KVW // LANES, t, LANES), F32)],
        compiler_params=_cparams("parallel", "parallel"),
        name="compress_prompt",
    )(kvt, bd1, bd2, pos, b1)


def _cmp_sample_kernel(pt_ref, *refs):
    page_refs = refs[:CMP_PAGES_PER_STEP]
    new_ref, bd1_ref, bd2_ref, pos_ref, b1_ref, o_ref, sub_ref, tr_ref = refs[CMP_PAGES_PER_STEP:]
    p = pl.program_id(1)
    page = page_refs[0].shape[2]
    per_page = page // CMP_STRIDE
    n_cmp = o_ref.shape[1]
    for j, page_ref in enumerate(page_refs):
        row0 = pl.multiple_of((p * CMP_PAGES_PER_STEP + j) * per_page, per_page)
        for kind in range(2):
            _tokens_to_sublanes(page_ref.at[kind], tr_ref.at[kind])
            for s in range(CMP_STRIDE):
                for half in range(KVW // LANES):
                    piece = tr_ref[kind, half, pl.ds(s, per_page, stride=CMP_STRIDE), :]
                    sub_ref[kind, s, pl.ds(row0, per_page), half * LANES:(half + 1) * LANES] = piece

    @pl.when(p == pl.num_programs(1) - 1)
    def _():
        n_rows = sub_ref.shape[2]
        tail = n_rows - n_cmp
        hid = bd1_ref.shape[-1]
        for kind in range(2):
            p0 = jnp.zeros((n_rows, hid), F32)
            p1 = jnp.zeros((n_rows, hid), F32)
            for s in range(CMP_STRIDE):
                sub_ref[kind, s, pl.ds(n_cmp, tail), :] = jnp.broadcast_to(new_ref[kind, s:s + 1, :], (tail, KVW))
                xs = sub_ref[kind, s]
                p0 = p0 + _dot((xs + pos_ref[kind, s]).astype(BF16), bd1_ref[kind, s])
                p1 = p1 + _dot((xs + pos_ref[kind, CMP_STRIDE + s]).astype(BF16), bd1_ref[kind, CMP_STRIDE + s])
            o_ref[kind] = _compress_finish(p0, p1, b1_ref[kind], bd2_ref[kind])[:n_cmp]


def _page_specs(n_per_step, kind_block, page):
    def spec(j):
        return pl.BlockSpec((None, 2, KVW, page),
                            lambda bi, pi, pt: (pt[bi, pi * n_per_step + j], kind_block, 0, 0))
    return [spec(j) for j in range(n_per_step)]


def compress_sample(cache_t, page_table, new_sub, cw):
    bd1, bd2, pos, b1 = cw
    b, n_pages = page_table.shape
    page = cache_t.shape[3]
    n_cmp = n_pages * page // CMP_STRIDE
    nps = CMP_PAGES_PER_STEP
    const = lambda a: pl.BlockSpec(a.shape, lambda bi, pi, pt: (0,) * a.ndim, pipeline_mode=pl.Buffered(1))
    grid_spec = pltpu.PrefetchScalarGridSpec(
        num_scalar_prefetch=1,
        grid=(b, n_pages // nps),
        in_specs=_page_specs(nps, 0, page) + [
            pl.BlockSpec((2, None, CMP_STRIDE, KVW), lambda bi, pi, pt: (0, bi, 0, 0)),
            const(bd1), const(bd2), const(pos), const(b1)],
        out_specs=pl.BlockSpec((2, None, n_cmp, KVW), lambda bi, pi, pt: (0, bi, 0, 0)),
        scratch_shapes=[pltpu.VMEM((2, CMP_STRIDE, n_cmp + 8, KVW), F32),
                        pltpu.VMEM((2, KVW // LANES, page, LANES), F32)],
    )
    return pl.pallas_call(
        _cmp_sample_kernel,
        grid_spec=grid_spec,
        out_shape=jax.ShapeDtypeStruct((2, b, n_cmp, KVW), F32),
        compiler_params=_cparams("parallel", "arbitrary"),
        name="compress_sample",
    )(page_table, *([cache_t] * nps), new_sub, bd1, bd2, pos, b1)


def _cover_matrix(n_cmp, n_sel, rows, cols):
    i = np.arange(n_cmp)[:, None]
    j = np.arange(n_sel)[None, :]
    cover = (i * CMP_STRIDE < (j + 1) * SEL_BLOCK) & (i * CMP_STRIDE + CMP_BLOCK > j * SEL_BLOCK)
    out = np.zeros((rows, cols), np.float32)
    out[:n_cmp, :n_sel] = cover
    return jnp.asarray(out)


SEL_CHUNK = 256
SEL_CLASS = 512


def _select_blocks_t(imp_t, cur, n_sel):
    nb, nq = imp_t.shape
    blk = lax.broadcasted_iota(I32, (nb, nq), 0)
    valid = (blk <= cur) & (blk < n_sel)
    forced = (blk == 0) | (blk == cur) | (blk == cur - 1)
    score = jnp.where(valid, jnp.where(forced, FORCED_SCORE, imp_t), -1.0)
    rank = jnp.zeros((nb, nq), F32)
    for i in range(n_sel):
        ci = score[i:i + 1, :]
        beats = (ci > score) | ((ci == score) & (blk > i))
        rank = rank + jnp.where(beats, 1.0, 0.0)
    return jnp.where(valid & (rank < float(min(N_SEL, n_sel))), 1.0, 0.0)


def _nsa_prompt_kernel(qt_ref, glt_ref, kc_ref, vct_ref, covt_ref, expt_ref, ks_ref, vst_ref, kw_ref, vwt_ref,
                       o_ref, s_ref, osel_ref, *, n_cmp, n_sel):
    tq = qt_ref.shape[1]
    r, d = NSA_GROUP, NSA_HEAD_DIM
    cols = r * tq
    t = ks_ref.shape[0]
    q0 = pl.program_id(2) * tq
    qt = qt_ref[...] * (d ** -0.5)
    qcat = jnp.concatenate([qt[h * d:(h + 1) * d, :] for h in range(r)], axis=-1)
    qb = jnp.concatenate([qcat, jnp.zeros_like(qcat)], axis=0).astype(BF16)
    gl = _sigmoid(glt_ref[...])

    def gate(branch):
        return jnp.concatenate([jnp.broadcast_to(gl[h * N_BRANCH + branch:h * N_BRANCH + branch + 1, :], (d, tq))
                                for h in range(r)], axis=0)

    def stack(o_t):
        return jnp.concatenate([o_t[:, h * tq:(h + 1) * tq] for h in range(r)], axis=0)

    def tile(x):
        return jnp.concatenate([x] * r, axis=-1)

    qpos = q0 + lax.broadcasted_iota(I32, (1, tq), 1)

    ncp = kc_ref.shape[0]
    nn = lax.broadcasted_iota(I32, (ncp, tq), 0)
    ok = tile(jnp.where((nn * CMP_STRIDE + (CMP_BLOCK - 1) <= qpos) & (nn < n_cmp), 1.0, 0.0))
    sm = _dot(kc_ref[...].astype(BF16), qb) + (ok - 1.0) * (-NEG)
    e = jnp.exp(sm - jnp.max(sm, axis=0, keepdims=True)) * ok
    p_cmp = e / jnp.maximum(jnp.sum(e, axis=0, keepdims=True), 1e-30)
    out = gate(0) * stack(_dot(vct_ref[...].astype(BF16), p_cmp.astype(BF16)))

    psum_t = p_cmp[:, :tq]
    for h in range(1, r):
        psum_t = psum_t + p_cmp[:, h * tq:(h + 1) * tq]
    imp_t = jnp.dot(covt_ref[...], psum_t, precision=HIGHEST, preferred_element_type=F32)
    sel_t = _select_blocks_t(imp_t, qpos >> 6, n_sel).astype(BF16)

    groups = SEL_CHUNK // 8

    def sel_branch(n_keys):
        m_run = jnp.full((8, cols), NEG, F32)
        for k0 in range(0, n_keys, SEL_CHUNK):
            sc = _dot(ks_ref[k0:k0 + SEL_CHUNK, :], qb)
            mtok = _dot(expt_ref[k0:k0 + SEL_CHUNK, :], sel_t)
            keypos = k0 + lax.broadcasted_iota(I32, (SEL_CHUNK, tq), 0)
            sm = sc + tile(jnp.where((mtok > 0.5) & (keypos <= qpos), 0.0, NEG))
            s_ref[k0:k0 + SEL_CHUNK, :] = sm
            m_run = jnp.maximum(m_run, jnp.max(sm.reshape(groups, 8, cols), axis=0))
        m_sel = jnp.max(m_run, axis=0, keepdims=True)
        l_run = jnp.zeros((8, cols), F32)
        acc_t = jnp.zeros((d, cols), F32)
        for k0 in range(0, n_keys, SEL_CHUNK):
            p = jnp.exp(s_ref[k0:k0 + SEL_CHUNK, :] - m_sel)
            l_run = l_run + jnp.sum(p.reshape(groups, 8, cols), axis=0)
            acc_t = acc_t + _dot(vst_ref[:, k0:k0 + SEL_CHUNK].astype(BF16), p.astype(BF16))
        l_sel = jnp.maximum(jnp.sum(l_run, axis=0, keepdims=True), 1e-30)
        osel_ref[...] = acc_t / l_sel

    sel_class = SEL_CLASS if t % SEL_CLASS == 0 else t
    cls_id = pl.program_id(2) // (sel_class // tq)
    for cls in range(t // sel_class):
        pl.when(cls_id == cls)(functools.partial(sel_branch, (cls + 1) * sel_class))
    out = out + gate(1) * stack(osel_ref[...])

    span = min(WINDOW + tq, t)
    w0 = pl.multiple_of(jnp.maximum(q0 + tq - span, 0), tq)
    kp = w0 + lax.broadcasted_iota(I32, (span, tq), 0)
    sm = _dot(kw_ref[pl.ds(w0, span), :], qb) + tile(jnp.where((kp <= qpos) & (kp > qpos - WINDOW), 0.0, NEG))
    p = jnp.exp(sm - jnp.max(sm, axis=0, keepdims=True))
    l_win = jnp.maximum(jnp.sum(p, axis=0, keepdims=True), 1e-30)
    o_win_t = _dot(vwt_ref[:, pl.ds(w0, span)].astype(BF16), p.astype(BF16)) / l_win
    out = out + gate(2) * stack(o_win_t)
    o_ref[...] = out.T


GATE_ROWS = 16


def nsa_prompt(qt, glt, kc, vct, kpad, kvt, wint):
    b, _, t = qt.shape
    g, r, d = NSA_KV_HEADS, NSA_GROUP, NSA_HEAD_DIM
    tq = Q_BLOCK
    nt = t // tq
    ncp = kc.shape[2]
    n_cmp = t // CMP_STRIDE - 1
    n_sel = -(-t // SEL_BLOCK)
    nb = -(-n_sel // 8) * 8
    covt = _cover_matrix(n_cmp, n_sel, ncp, nb).T
    expand_t = jnp.asarray((np.arange(t)[:, None] // SEL_BLOCK) == np.arange(nb)[None, :], BF16)
    per_bg = lambda shape: pl.BlockSpec((None, None) + shape, lambda bi, gi, qi: (bi, gi, 0, 0))
    const = lambda a: pl.BlockSpec(a.shape, lambda bi, gi, qi: (0,) * a.ndim)
    return pl.pallas_call(
        functools.partial(_nsa_prompt_kernel, n_cmp=n_cmp, n_sel=n_sel),
        grid=(b, g, nt),
        in_specs=[pl.BlockSpec((None, r * d, tq), lambda bi, gi, qi: (bi, gi, qi)),
                  pl.BlockSpec((None, GATE_ROWS, tq), lambda bi, gi, qi: (bi, gi, qi)),
                  per_bg((ncp, LANES)), per_bg((d, ncp)), const(covt), const(expand_t),
                  pl.BlockSpec((t, LANES), lambda bi, gi, qi: (bi, gi)),
                  pl.BlockSpec((None, d, t), lambda bi, gi, qi: (bi, 3 * g + gi, 0)),
                  pl.BlockSpec((t, LANES), lambda bi, gi, qi: (bi, g + gi)),
                  pl.BlockSpec((None, d, t), lambda bi, gi, qi: (bi, g + gi, 0))],
        out_specs=pl.BlockSpec((tq, r * d), lambda bi, gi, qi: (bi * nt + qi, gi)),
        out_shape=jax.ShapeDtypeStruct((b * t, NSA_HEADS * d), F32),
        scratch_shapes=[pltpu.VMEM((t, r * tq), F32), pltpu.VMEM((d, r * tq), F32)],
        compiler_params=_cparams("parallel", "parallel", "arbitrary"),
        name="nsa_prompt",
    )(qt, glt, kc, vct, covt, expand_t, kpad, kvt, kpad, wint)


def _nsa_sample_select_kernel(q_ref, kc_ref, vc_ref, cov_ref, oc_ref, sel_ref, *, t, past, n_cmp, n_sel):
    rows = q_ref.shape[0]
    g, r = NSA_KV_HEADS, NSA_GROUP
    qb = q_ref[...].astype(BF16)
    qpos_r = past + (lax.broadcasted_iota(I32, (rows, 1), 0) & (t - 1))
    s = _dot_nt(qb, kc_ref[...].astype(BF16))
    nn = lax.broadcasted_iota(I32, s.shape, 1)
    p_cmp = _masked_softmax(s, (nn * CMP_STRIDE + (CMP_BLOCK - 1) <= qpos_r) & (nn < n_cmp))
    oc_ref[...] = _dot(p_cmp.astype(BF16), vc_ref[...].astype(BF16))
    psum = jnp.sum(p_cmp.reshape(g, r, t, s.shape[1]), axis=1).reshape(g * t, s.shape[1])
    imp = jnp.dot(psum, cov_ref[...], precision=HIGHEST, preferred_element_type=F32)
    qpos_gt = past + (lax.broadcasted_iota(I32, (g * t, 1), 0) & (t - 1))
    sel = _select_blocks(imp, qpos_gt >> 6, n_sel)
    lanes = sel.shape[1]
    sel_ref[...] = jnp.broadcast_to(sel.reshape(g, 1, t, lanes), (g, r, t, lanes)).reshape(rows, lanes)


def nsa_sample_select(q4, kc, vc, *, t, past):
    b, rows, _ = q4.shape
    n_cmp = kc.shape[1]
    n_sel = -(-(past + t) // SEL_BLOCK)
    lanes = -(-n_sel // 128) * 128
    cover = _cover_matrix(n_cmp, n_sel, n_cmp, lanes)
    return pl.pallas_call(
        functools.partial(_nsa_sample_select_kernel, t=t, past=past, n_cmp=n_cmp, n_sel=n_sel),
        grid=(b,),
        in_specs=[pl.BlockSpec((None, rows, KVW), lambda bi: (bi, 0, 0)),
                  pl.BlockSpec((None, n_cmp, KVW), lambda bi: (bi, 0, 0)),
                  pl.BlockSpec((None, n_cmp, KVW), lambda bi: (bi, 0, 0)),
                  pl.BlockSpec(cover.shape, lambda bi: (0, 0))],
        out_specs=[pl.BlockSpec((None, rows, KVW), lambda bi: (bi, 0, 0)),
                   pl.BlockSpec((None, rows, lanes), lambda bi: (bi, 0, 0))],
        out_shape=[jax.ShapeDtypeStruct((b, rows, KVW), F32),
                   jax.ShapeDtypeStruct((b, rows, lanes), F32)],
        compiler_params=_cparams("parallel"),
        name="nsa_sample_select",
    )(q4, kc, vc, cover)


def _nsa_sample_attend_kernel(pt_ref, q_ref, sel_ref, oc_ref, gl_ref, *refs, t, past):
    page_refs = refs[:ATT_PAGES_PER_STEP]
    new_ref, win_ref, o_ref, m_ref, l_ref, acc_ref = refs[ATT_PAGES_PER_STEP:]
    rows = q_ref.shape[0]
    p = pl.program_id(1)
    page = page_refs[0].shape[2]
    width = ATT_PAGES_PER_STEP * page
    qb = q_ref[...].astype(BF16)
    sel = sel_ref[...]
    nblk = sel.shape[1]

    @pl.when(p == 0)
    def _():
        _flash_init(m_ref, l_ref, acc_ref)

    sc = jnp.concatenate([_dot(qb, pr[0].astype(BF16)) for pr in page_refs], axis=-1)
    jj = lax.broadcasted_iota(I32, (nblk, width), 0)
    tt = lax.broadcasted_iota(I32, (nblk, width), 1)
    expand = jnp.where(jj == ((p * width + tt) >> 6), 1.0, 0.0).astype(BF16)
    mask = _dot(sel.astype(BF16), expand) > 0.5
    sm = jnp.where(mask, sc, NEG)
    m_old = m_ref[...]
    m_new = jnp.maximum(m_old, jnp.max(sm, axis=-1, keepdims=True))
    alpha = jnp.exp(m_old - m_new)
    pr_b = jnp.where(mask, jnp.exp(sm - m_new), 0.0)
    l_ref[...] = alpha * l_ref[...] + jnp.sum(pr_b, axis=-1, keepdims=True)
    pr_b = pr_b.astype(BF16)
    pv = _dot_nt(pr_b[:, :page], page_refs[0][1].astype(BF16))
    for j in range(1, ATT_PAGES_PER_STEP):
        pv = pv + _dot_nt(pr_b[:, j * page:(j + 1) * page], page_refs[j][1].astype(BF16))
    acc_ref[...] = alpha * acc_ref[...] + pv
    m_ref[...] = m_new

    @pl.when(p == pl.num_programs(1) - 1)
    def _():
        qpos_r = past + (lax.broadcasted_iota(I32, (rows, 1), 0) & (t - 1))
        newr = new_ref[...]
        tp = newr.shape[0]
        jn = lax.broadcasted_iota(I32, (rows, tp), 1)
        new_ok = (jn < t) & (past + jn <= qpos_r)
        blk_new = past // SEL_BLOCK
        sel_new = sel[:, blk_new:blk_new + 1] > 0.5
        sc_n = _dot_nt(qb, newr[:, 2 * KVW:3 * KVW].astype(BF16))
        _flash_update(sc_n, new_ok & sel_new, newr[:, 3 * KVW:4 * KVW].astype(BF16), m_ref, l_ref, acc_ref)
        o_sel = _flash_result(l_ref, acc_ref)

        n_win = win_ref.shape[2]
        s_a = _dot(qb, win_ref[0].astype(BF16))
        s_b = _dot_nt(qb, newr[:, 4 * KVW:5 * KVW].astype(BF16))
        kp_a = (past - n_win) + lax.broadcasted_iota(I32, (rows, n_win), 1)
        ok_a = (kp_a <= qpos_r) & (kp_a > qpos_r - WINDOW) & (kp_a >= 0)
        ok_b = new_ok & (past + jn > qpos_r - WINDOW)
        sm_a = jnp.where(ok_a, s_a, NEG)
        sm_b = jnp.where(ok_b, s_b, NEG)
        mx = jnp.maximum(jnp.max(sm_a, axis=-1, keepdims=True), jnp.max(sm_b, axis=-1, keepdims=True))
        e_a = jnp.where(ok_a, jnp.exp(sm_a - mx), 0.0)
        e_b = jnp.where(ok_b, jnp.exp(sm_b - mx), 0.0)
        den = jnp.maximum(jnp.sum(e_a, axis=-1, keepdims=True) + jnp.sum(e_b, axis=-1, keepdims=True), 1e-30)
        o_win = (_dot_nt((e_a / den).astype(BF16), win_ref[1].astype(BF16))
                 + _dot((e_b / den).astype(BF16), newr[:, 5 * KVW:6 * KVW].astype(BF16)))

        gates = _sigmoid(gl_ref[...])
        o_ref[...] = gates[:, 0:1] * oc_ref[...] + gates[:, 1:2] * o_sel + gates[:, 2:3] * o_win


def nsa_sample_attend(q4, sel, ocmp, gl, cache_t, page_table, new_rows, win_t, *, t, past):
    b, rows, _ = q4.shape
    n_pages = page_table.shape[1]
    page = cache_t.shape[3]
    lanes = sel.shape[2]
    tp = new_rows.shape[1]
    n_win = win_t.shape[3]
    nps = ATT_PAGES_PER_STEP
    per_b = lambda shape: pl.BlockSpec((None,) + shape, lambda bi, pi, pt: (bi,) + (0,) * len(shape))
    grid_spec = pltpu.PrefetchScalarGridSpec(
        num_scalar_prefetch=1,
        grid=(b, n_pages // nps),
        in_specs=[per_b((rows, KVW)), per_b((rows, lanes)), per_b((rows, KVW)), per_b((rows, N_BRANCH))]
        + _page_specs(nps, 1, page)
        + [per_b((tp, N_KV_ROWS * KVW)), per_b((2, KVW, n_win))],
        out_specs=per_b((rows, KVW)),
        scratch_shapes=[pltpu.VMEM((rows, 1), F32), pltpu.VMEM((rows, 1), F32), pltpu.VMEM((rows, KVW), F32)],
    )
    return pl.pallas_call(
        functools.partial(_nsa_sample_attend_kernel, t=t, past=past),
        grid_spec=grid_spec,
        out_shape=jax.ShapeDtypeStruct((b, rows, KVW), F32),
        compiler_params=_cparams("parallel", "arbitrary"),
        name="nsa_sample_attend",
    )(page_table, q4, sel, ocmp, gl, *([cache_t] * nps), new_rows, win_t)


def _route_kernel(x_ref, g_ref, sh_ref, sc_ref, rw_ref, rb_ref, h_ref, r_ref):
    h = _norm_mod(x_ref[...], g_ref[...], sh_ref[...], sc_ref[...])
    h_ref[...] = h.astype(BF16)
    logits = jnp.dot(h, rw_ref[...], precision=HIGHEST, preferred_element_type=F32) + rb_ref[...]
    lane = lax.broadcasted_iota(I32, logits.shape, 1)
    lane_f = lane.astype(F32)
    lg = jnp.where(lane < N_EXPERTS, logits, NEG)
    v1 = jnp.max(lg, axis=-1, keepdims=True)
    i1 = jnp.min(jnp.where(lg == v1, lane_f, 128.0), axis=-1, keepdims=True)
    lg2 = jnp.where(lane_f == i1, NEG, lg)
    v2 = jnp.max(lg2, axis=-1, keepdims=True)
    i2 = jnp.min(jnp.where(lg2 == v2, lane_f, 128.0), axis=-1, keepdims=True)
    e = jnp.exp(v2 - v1)
    w1 = 1.0 / (1.0 + e)
    w2 = e / (1.0 + e)
    r_ref[...] = jnp.where(lane == 0, i1, jnp.where(lane == 1, i2, jnp.where(lane == 2, w1, jnp.where(lane == 3, w2, 0.0))))


def moe_route(x, g, shift, scale, rw_pad, rb_pad, *, per_token, tm, tpb):
    m, d = x.shape
    sh, sh_spec = _mod_arg(shift, per_token, tm, tpb)
    sc, sc_spec = _mod_arg(scale, per_token, tm, tpb)
    return pl.pallas_call(
        _route_kernel,
        grid=(m // tm,),
        in_specs=[pl.BlockSpec((tm, d), lambda i: (i, 0)), _vec_spec(d), sh_spec, sc_spec,
                  pl.BlockSpec((d, 128), lambda i: (0, 0)), _vec_spec(128)],
        out_specs=[pl.BlockSpec((tm, d), lambda i: (i, 0)), pl.BlockSpec((tm, 128), lambda i: (i, 0))],
        out_shape=[jax.ShapeDtypeStruct((m, d), BF16), jax.ShapeDtypeStruct((m, 128), F32)],
        compiler_params=_cparams("parallel"),
        name="moe_route",
    )(x, g.reshape(1, d), sh, sc, rw_pad, rb_pad)


def _moe_kernel(te_ref, nv_ref, x_ref, wg_ref, wu_ref, wd_ref, ws_ref, o_ref, acc_ref):
    i = pl.program_id(0)
    f = pl.program_id(1)

    @pl.when(f == 0)
    def _():
        acc_ref[...] = jnp.zeros(acc_ref.shape, F32)

    @pl.when(i < nv_ref[0])
    def _():
        x = x_ref[...]
        act = _silu(_dot(x, wg_ref[...])) * _dot(x, wu_ref[...])
        acc_ref[...] += _dot(act.astype(BF16), wd_ref[...])

    @pl.when(f == pl.num_programs(1) - 1)
    def _():
        o_ref[...] = ws_ref[...] * acc_ref[...]


def moe_experts(xs, w_slot, tile_expert, n_valid, w_gu_b, w_down_b, *, tm, tf):
    p, d = xs.shape
    edim = w_down_b.shape[1]
    nf = edim // tf
    n_tiles = p // tm

    def wmap(off):
        def index(i, f, te, nv):
            ok = i < nv[0]
            return (te[i], 0, off + jnp.where(ok, f, nf - 1))
        return index

    def dmap(i, f, te, nv):
        return (te[i], jnp.where(i < nv[0], f, nf - 1), 0)

    grid_spec = pltpu.PrefetchScalarGridSpec(
        num_scalar_prefetch=2,
        grid=(n_tiles, nf),
        in_specs=[pl.BlockSpec((tm, d), lambda i, f, te, nv: (i, 0)),
                  pl.BlockSpec((None, d, tf), wmap(0)),
                  pl.BlockSpec((None, d, tf), wmap(nf)),
                  pl.BlockSpec((None, tf, d), dmap),
                  pl.BlockSpec((tm, 1), lambda i, f, te, nv: (i, 0))],
        out_specs=pl.BlockSpec((tm, d), lambda i, f, te, nv: (i, 0)),
        scratch_shapes=[pltpu.VMEM((tm, d), F32)],
    )
    return pl.pallas_call(
        _moe_kernel,
        grid_spec=grid_spec,
        out_shape=jax.ShapeDtypeStruct((p, d), F32),
        compiler_params=_cparams("parallel", "arbitrary"),
        name="moe_experts",
    )(tile_expert, n_valid, xs, w_gu_b, w_gu_b, w_down_b, w_slot)


def _moe_tables(route, tm):
    m = route.shape[0]
    na = TOP_K * m
    e = route[:, :TOP_K].astype(I32).reshape(na)
    w = route[:, TOP_K:2 * TOP_K].reshape(na)
    onehot = (e[:, None] == jnp.arange(N_EXPERTS, dtype=I32)[None, :]).astype(I32)
    within = jnp.sum((jnp.cumsum(onehot, axis=0) - onehot) * onehot, axis=1)
    counts = jnp.sum(onehot, axis=0)
    padded = ((counts + tm - 1) // tm) * tm
    ends = jnp.cumsum(padded)
    starts = ends - padded
    dest = starts[e] + within
    n_slots = (-(-na // tm) + N_EXPERTS) * tm
    src_tok = jnp.zeros((n_slots,), I32).at[dest].set(jnp.arange(na, dtype=I32) // TOP_K)
    w_slot = jnp.zeros((n_slots,), F32).at[dest].set(w)
    n_tiles = n_slots // tm
    n_valid = (ends[-1] // tm).astype(I32)
    tile_start = jnp.arange(n_tiles, dtype=I32) * tm
    tile_expert = jnp.sum((tile_start[:, None] >= ends[None, :]).astype(I32), axis=1)
    last = jnp.take(tile_expert, jnp.maximum(n_valid - 1, 0))
    tile_expert = jnp.where(jnp.arange(n_tiles) < n_valid, tile_expert, last).astype(I32)
    return src_tok, w_slot[:, None], tile_expert, n_valid.reshape(1), dest


def _split_mod(mod):
    return [mod[:, i * D_MODEL:(i + 1) * D_MODEL] for i in range(mod.shape[1] // D_MODEL)]


def kernel(x_prompt, x_sample, c_prompt, c_sample, state_ret, cache_kv, cache_win, page_table, w_mod, b_mod, norm_g, ret_w_in, ret_gn_g, ret_w_out, kv_w_mod, kv_b_mod, kv_norm_g, kv_w, cmp_pos, cmp_w1, cmp_b1, cmp_w2, nsa_w_in, nsa_w_out, ffn_w_gu, ffn_w_down, moe_router_w, moe_router_b, moe_w_gu, moe_w_down):
    bp, t, d = x_prompt.shape
    bs, ts, _ = x_sample.shape
    mp, ms = bp * t, bs * ts
    n_pool, page = cache_kv.shape[:2]
    past = page_table.shape[1] * page
    g, r, hd = NSA_KV_HEADS, NSA_GROUP, NSA_HEAD_DIM

    ret_w_in_b = ret_w_in[0].astype(BF16)
    ret_w_out_b = ret_w_out[0].astype(BF16)
    kv_w_b = kv_w.astype(BF16)
    kv_wt_b = kv_w.T.astype(BF16)
    nq = NSA_HEADS * hd
    gate_w = nsa_w_in[0][:, nq:].reshape(d, g, r * N_BRANCH)
    gate_w = jnp.pad(gate_w, ((0, 0), (0, 0), (0, LANES - r * N_BRANCH))).reshape(d, g * LANES)
    nsa_w_in_b = jnp.concatenate([nsa_w_in[0][:, :nq], gate_w], axis=1).astype(BF16)
    gate_wt = nsa_w_in[0][:, nq:].T.reshape(g, r * N_BRANCH, d)
    gate_wt = jnp.pad(gate_wt, ((0, 0), (0, GATE_ROWS - r * N_BRANCH), (0, 0))).reshape(g * GATE_ROWS, d)
    nsa_wt_b = jnp.concatenate([nsa_w_in[0][:, :nq].T, gate_wt], axis=0).astype(BF16)
    def pad_heads(w):
        return jnp.pad(w.reshape(d, g, hd), ((0, 0), (0, 0), (0, LANES - hd))).reshape(d, g * LANES)
    kpad_w_b = jnp.concatenate([pad_heads(kv_w[:, 2 * KVW:3 * KVW]), pad_heads(kv_w[:, 4 * KVW:5 * KVW])],
                               axis=1).astype(BF16)
    nsa_w_out_b = nsa_w_out[0].astype(BF16)
    ffn_w_gu_b = ffn_w_gu[0].astype(BF16)
    ffn_w_down_b = ffn_w_down[0].astype(BF16)
    moe_w_gu_b = moe_w_gu[0].astype(BF16)
    moe_w_down_b = moe_w_down[0].astype(BF16)
    rw_pad = jnp.pad(moe_router_w[0], ((0, 0), (0, 128 - N_EXPERTS)))
    rb_pad = jnp.pad(moe_router_b[0], (0, 128 - N_EXPERTS)).reshape(1, 128)
    cw = _compress_weights(cmp_pos, cmp_w1, cmp_b1, cmp_w2)

    c_all = jnp.concatenate([c_prompt, c_sample], axis=0)
    mods = [cond_matmul(c_all, w_mod[layer], b_mod[layer]) for layer in range(w_mod.shape[0])]
    kv_mod = cond_matmul(c_all, kv_w_mod, kv_b_mod)

    tm_p = min(512, t)
    tm_f = min(1024, t)
    groups = {
        'p': dict(x=x_prompt.reshape(mp, d), b=bp, t=t, per_token=False, tm=tm_p, tpb=t // tm_p,
                  tm_f=tm_f, tpb_f=t // tm_f,
                  mod=lambda a: a[:bp]),
        's': dict(x=x_sample.reshape(ms, d), b=bs, t=ts, per_token=True, tm=ms, tpb=1, tm_f=ms, tpb_f=1,
                  mod=lambda a: jnp.repeat(a[bp:], ts, axis=0)),
    }
    out = {}

    for name, gr in groups.items():
        kw = dict(per_token=gr['per_token'], tm=gr['tm'], tpb=gr['tpb'])
        m0 = [gr['mod'](a) for a in _split_mod(mods[0])]
        x = gr['x']
        proj = norm_mod_matmul(x, norm_g[0, 0], m0[0], m0[1], ret_w_in_b, tn=1024, **kw)
        proj = proj.reshape(gr['b'], gr['t'], -1)
        if name == 'p':
            pos = jnp.arange(t)
            s0 = jnp.zeros((bp, RET_HEADS, RET_DK, RET_DV), F32)
            o, s_new = retention(proj, pos, s0, ret_gn_g[0], math.gcd(t, RET_CHUNK))
        else:
            pos = past + jnp.arange(ts)
            proj = jnp.pad(proj, ((0, 0), (0, RET_CHUNK - ts), (0, 0)))
            o, s_new = retention(proj, pos, state_ret[0], ret_gn_g[0], math.gcd(ts, RET_CHUNK))
            o = o[:, :ts]
        out['ret_' + name] = s_new[None]
        x = matmul_norm_residual(o.reshape(-1, d), ret_w_out_b, x, m0[2], norm_g[0, 1], **kw)
        x = ffn_sublayer(x, norm_g[0, 2], m0[3], m0[4], ffn_w_gu_b, ffn_w_down_b, m0[5], norm_g[0, 3],
                         per_token=gr['per_token'], tm=gr['tm_f'], tpb=gr['tpb_f'], tf=256)
        gr['x1'] = x

    for name, gr in groups.items():
        kw = dict(per_token=gr['per_token'], tm=gr['tm'], tpb=gr['tpb'])
        m1 = [gr['mod'](a) for a in _split_mod(mods[1])]
        kvm = [gr['mod'](a) for a in _split_mod(kv_mod)]
        gr['m1'] = m1
        x = gr['x1']
        b_, t_ = gr['b'], gr['t']
        if name == 'p':
            kvt, wint = norm_mod_matmul_t(x, kv_norm_g, kvm[0], kvm[1], kv_wt_b,
                                          (N_PAGED_ROWS * KVW, (N_KV_ROWS - N_PAGED_ROWS) * KVW), b=b_, tm=gr['tm'])
            out['kv_p'] = kvt.reshape(b_, N_PAGED_ROWS, g, hd, t_).transpose(0, 4, 1, 2, 3)
            n_keep = min(WINDOW, t_)
            out['win_p'] = wint[:, :, t_ - n_keep:].reshape(b_, 2, g, hd, n_keep).transpose(0, 4, 1, 2, 3)
            kpad = norm_mod_matmul(x, kv_norm_g, kvm[0], kvm[1], kpad_w_b, out_dtype=BF16, **kw)
            qt, glt = norm_mod_matmul_t(x, norm_g[1, 0], m1[0], m1[1], nsa_wt_b, (nq, g * GATE_ROWS),
                                        b=b_, tm=gr['tm'])
            cmp = compress_prompt(kvt, cw).reshape(2, b_, -1, g, hd)
            kc = jnp.pad(cmp[0].transpose(0, 2, 1, 3), ((0, 0), (0, 0), (0, 0), (0, LANES - hd)))
            o = nsa_prompt(qt, glt, kc, cmp[1].transpose(0, 2, 3, 1), kpad, kvt, wint)
        else:
            proj = norm_mod_matmul(x, norm_g[1, 0], m1[0], m1[1], nsa_w_in_b, **kw)
            rows = norm_mod_matmul(x, kv_norm_g, kvm[0], kvm[1], kv_w_b, **kw)
            q = (proj[:, :NSA_HEADS * hd] * (hd ** -0.5)).reshape(b_, t_, g, r, hd)
            gl = proj[:, NSA_HEADS * hd:].reshape(b_, t_, g, LANES)[..., :r * N_BRANCH]
            gl = gl.reshape(b_, t_, g, r, N_BRANCH)
            rows3 = rows.reshape(b_, t_, N_KV_ROWS * KVW)
            rows6 = rows.reshape(b_, t_, N_KV_ROWS, g, hd)
            out['kv_s'] = rows6[:, :, :N_PAGED_ROWS]
            cache_t = cache_kv.transpose(0, 2, 3, 4, 1).reshape(n_pool, N_PAGED_ROWS, KVW, page)
            win_t5 = cache_win.transpose(0, 2, 3, 4, 1)
            n_win = cache_win.shape[1]
            new_win_t = rows6[:, :, N_PAGED_ROWS:].transpose(0, 2, 3, 4, 1)
            out['win_s'] = jnp.concatenate([win_t5, new_win_t], axis=-1)[..., t_:].transpose(0, 4, 1, 2, 3)
            new_sub = jnp.pad(rows3[:, :, :2 * KVW], ((0, 0), (0, CMP_STRIDE - t_), (0, 0)))
            new_sub = new_sub.reshape(b_, CMP_STRIDE, 2, KVW).transpose(2, 0, 1, 3)
            cmp = compress_sample(cache_t, page_table, new_sub, cw)
            eye = jnp.eye(g, dtype=F32)
            qrows = q.transpose(0, 2, 3, 1, 4).reshape(b_, g, r * t_, hd)
            q4 = jnp.einsum('bgxd,gj->bgxjd', qrows, eye).reshape(b_, g * r * t_, KVW)
            glr = gl.transpose(0, 2, 3, 1, 4).reshape(b_, g * r * t_, N_BRANCH)
            ocmp, sel = nsa_sample_select(q4, cmp[0], cmp[1], t=t_, past=past)
            new_rows = jnp.pad(rows3, ((0, 0), (0, 128 - t_), (0, 0)))
            o4 = nsa_sample_attend(q4, sel, ocmp, glr, cache_t, page_table, new_rows,
                                   win_t5.reshape(b_, 2, KVW, n_win), t=t_, past=past)
            o4 = o4.reshape(b_, g, r, t_, g, hd)
            o = jnp.einsum('bgrtjd,gj->btgrd', o4, eye).reshape(ms, NSA_HEADS * hd)
        gr['x2'] = matmul_norm_residual(o, nsa_w_out_b, x, m1[2], norm_g[1, 1], **kw)

    hs, routes = [], []
    for name, gr in groups.items():
        kw = dict(per_token=gr['per_token'], tm=gr['tm'], tpb=gr['tpb'])
        m1 = gr['m1']
        h, route = moe_route(gr['x2'], norm_g[1, 2], m1[3], m1[4], rw_pad, rb_pad, **kw)
        hs.append(h)
        routes.append(route)
    h_all = jnp.concatenate(hs, axis=0)
    route_all = jnp.concatenate(routes, axis=0)
    tm_e = 512 if mp >= 4096 else 128
    src_tok, w_slot, tile_expert, n_valid, dest = _moe_tables(route_all, tm_e)
    xs = jnp.take(h_all, src_tok, axis=0)
    ys = moe_experts(xs, w_slot, tile_expert, n_valid, moe_w_gu_b, moe_w_down_b, tm=tm_e, tf=512)
    y_all = jnp.take(ys, dest[0::2], axis=0) + jnp.take(ys, dest[1::2], axis=0)
    for name, gr, y in (('p', groups['p'], y_all[:mp]), ('s', groups['s'], y_all[mp:])):
        kw = dict(per_token=gr['per_token'], tm=gr['tm'], tpb=gr['tpb'])
        out['y_' + name] = norm_residual(y, gr['x2'], gr['m1'][5], norm_g[1, 3], **kw)

    return (out['y_p'].reshape(bp, t, d), out['y_s'].reshape(bs, ts, d),
            out['ret_p'], out['ret_s'], out['kv_p'], out['kv_s'], out['win_p'], out['win_s'])
```

```python
import functools
import math

import numpy as np
import jax
import jax.numpy as jnp
from jax import lax
from jax.experimental import pallas as pl
from jax.experimental.pallas import tpu as pltpu

F32 = jnp.float32
BF16 = jnp.bfloat16
I32 = jnp.int32

D_MODEL = 1024
N_MOD = 6
RET_HEADS = 4
RET_DK = 256
RET_DV = 256
RET_CHUNK = 128
ROPE_BASE = 10000.0
NSA_HEADS = 16
NSA_KV_HEADS = 4
NSA_GROUP = 4
NSA_HEAD_DIM = 64
N_BRANCH = 3
N_KV_ROWS = 6
N_PAGED_ROWS = 4
CMP_BLOCK = 32
CMP_STRIDE = 16
CMP_HIDDEN = 128
SEL_BLOCK = 64
N_SEL = 16
FORCED_SCORE = 1.0e4
WINDOW = 512
Q_BLOCK = 128
N_EXPERTS = 8
TOP_K = 2
EPS = 1e-6

NEG = -1.0e30
KVW = NSA_KV_HEADS * NSA_HEAD_DIM
VMEM_LIMIT_BYTES = 56 * 1024 * 1024
HIGHEST = lax.Precision.HIGHEST


def _cparams(*sem):
    return pltpu.CompilerParams(dimension_semantics=sem, vmem_limit_bytes=VMEM_LIMIT_BYTES)


def _sigmoid(x):
    return 1.0 / (1.0 + jnp.exp(-x))


def _silu(x):
    return x * _sigmoid(x)


def _gelu_tanh(x):
    return x * (0.5 * (1.0 + jnp.tanh(math.sqrt(2.0 / math.pi) * (x + 0.044715 * (x * x * x)))))


def _norm_mod(x, g, shift, scale):
    ms = jnp.mean(x * x, axis=-1, keepdims=True)
    return (x * lax.rsqrt(ms + EPS) * g) * (1.0 + scale) + shift


def _rms_residual(x, gate, o, g):
    ms = jnp.mean(o * o, axis=-1, keepdims=True)
    return x + gate * (o * lax.rsqrt(ms + EPS) * g)


def _dot(a, b):
    return jnp.dot(a, b, preferred_element_type=F32)


def _dot_nt(a, b):
    return lax.dot_general(a, b, (((1,), (1,)), ((), ())), preferred_element_type=F32)


def _masked_softmax(s, mask):
    sm = jnp.where(mask, s, NEG)
    m = jnp.max(sm, axis=-1, keepdims=True)
    e = jnp.where(mask, jnp.exp(sm - m), 0.0)
    return e / jnp.maximum(jnp.sum(e, axis=-1, keepdims=True), 1e-30)


def _flash_init(m_ref, l_ref, acc_ref):
    m_ref[...] = jnp.full(m_ref.shape, NEG, F32)
    l_ref[...] = jnp.zeros(l_ref.shape, F32)
    acc_ref[...] = jnp.zeros(acc_ref.shape, F32)


def _flash_update(s, mask, v_b, m_ref, l_ref, acc_ref):
    sm = jnp.where(mask, s, NEG)
    m_old = m_ref[...]
    m_new = jnp.maximum(m_old, jnp.max(sm, axis=-1, keepdims=True))
    alpha = jnp.exp(m_old - m_new)
    p = jnp.where(mask, jnp.exp(sm - m_new), 0.0)
    l_ref[...] = alpha * l_ref[...] + jnp.sum(p, axis=-1, keepdims=True)
    acc_ref[...] = alpha * acc_ref[...] + _dot(p.astype(BF16), v_b)
    m_ref[...] = m_new


def _flash_result(l_ref, acc_ref):
    return acc_ref[...] / jnp.maximum(l_ref[...], 1e-30)


def _select_blocks(imp, cur, n_sel):
    rows, lanes = imp.shape
    blk = lax.broadcasted_iota(I32, (rows, lanes), 1)
    valid = (blk <= cur) & (blk < n_sel)
    forced = (blk == 0) | (blk == cur) | (blk == cur - 1)
    score = jnp.where(valid, jnp.where(forced, FORCED_SCORE, imp), -1.0)
    rank = jnp.zeros((rows, lanes), F32)
    for i in range(n_sel):
        ci = score[:, i:i + 1]
        beats = (ci > score) | ((ci == score) & (blk > i))
        rank = rank + jnp.where(beats, 1.0, 0.0)
    return jnp.where(valid & (rank < float(min(N_SEL, n_sel))), 1.0, 0.0)


def _cond_kernel(c_ref, w_ref, b_ref, o_ref):
    sc = _silu(c_ref[...])
    o_ref[...] = _dot(sc.astype(BF16), w_ref[...].astype(BF16)) + b_ref[...]


def cond_matmul(c, w, b):
    bc, d = c.shape
    n = w.shape[1]
    tn = 1024
    return pl.pallas_call(
        _cond_kernel,
        grid=(n // tn,),
        in_specs=[pl.BlockSpec((bc, d), lambda j: (0, 0)),
                  pl.BlockSpec((d, tn), lambda j: (0, j)),
                  pl.BlockSpec((1, tn), lambda j: (0, j))],
        out_specs=pl.BlockSpec((bc, tn), lambda j: (0, j)),
        out_shape=jax.ShapeDtypeStruct((bc, n), F32),
        compiler_params=_cparams("parallel"),
        name="cond_matmul",
    )(c, w, b.reshape(1, n))


def _mod_arg(m, per_token, tm, tiles_per_batch):
    d = m.shape[-1]
    if per_token:
        return m, pl.BlockSpec((tm, d), lambda i, *_: (i, 0))
    return m[:, None, :], pl.BlockSpec((None, 1, d), lambda i, *_: (i // tiles_per_batch, 0, 0))


def _vec_spec(d):
    return pl.BlockSpec((1, d), lambda i, *_: (0, 0))


def _nmm_kernel(x_ref, g_ref, sh_ref, sc_ref, w_ref, o_ref, h_ref):
    @pl.when(pl.program_id(1) == 0)
    def _():
        h_ref[...] = _norm_mod(x_ref[...], g_ref[...], sh_ref[...], sc_ref[...]).astype(BF16)

    o_ref[...] = _dot(h_ref[...], w_ref[...]).astype(o_ref.dtype)


def norm_mod_matmul(x, g, shift, scale, w_b, *, per_token, tm, tpb, tn=None, out_dtype=F32):
    m, d = x.shape
    n = w_b.shape[1]
    tn = n if tn is None else tn
    sh, sh_spec = _mod_arg(shift, per_token, tm, tpb)
    sc, sc_spec = _mod_arg(scale, per_token, tm, tpb)
    return pl.pallas_call(
        _nmm_kernel,
        grid=(m // tm, n // tn),
        in_specs=[pl.BlockSpec((tm, d), lambda i, j: (i, 0)), _vec_spec(d), sh_spec, sc_spec,
                  pl.BlockSpec((d, tn), lambda i, j: (0, j))],
        out_specs=pl.BlockSpec((tm, tn), lambda i, j: (i, j)),
        out_shape=jax.ShapeDtypeStruct((m, n), out_dtype),
        scratch_shapes=[pltpu.VMEM((tm, d), BF16)],
        compiler_params=_cparams("parallel", "arbitrary"),
        name="norm_mod_matmul",
    )(x, g.reshape(1, d), sh, sc, w_b)


def _nmm_t_kernel(x_ref, g_ref, sh_ref, sc_ref, wt_ref, *o_refs):
    h = _norm_mod(x_ref[...], g_ref[...], sh_ref[...], sc_ref[...]).astype(BF16)
    o = _dot_nt(wt_ref[...], h)
    row = 0
    for o_ref in o_refs:
        o_ref[...] = o[row:row + o_ref.shape[0]]
        row += o_ref.shape[0]


def norm_mod_matmul_t(x, g, shift, scale, wt_b, splits, *, b, tm):
    m, d = x.shape
    t = m // b
    tpb = t // tm
    n = wt_b.shape[0]
    sh, sh_spec = _mod_arg(shift, False, tm, tpb)
    sc, sc_spec = _mod_arg(scale, False, tm, tpb)
    return pl.pallas_call(
        _nmm_t_kernel,
        grid=(m // tm,),
        in_specs=[pl.BlockSpec((tm, d), lambda i: (i, 0)), _vec_spec(d), sh_spec, sc_spec,
                  pl.BlockSpec((n, d), lambda i: (0, 0))],
        out_specs=[pl.BlockSpec((None, ni, tm), lambda i: (i // tpb, 0, i % tpb)) for ni in splits],
        out_shape=[jax.ShapeDtypeStruct((b, ni, t), F32) for ni in splits],
        compiler_params=_cparams("parallel"),
        name="norm_mod_matmul_t",
    )(x, g.reshape(1, d), sh, sc, wt_b)


def _mnr_kernel(a_ref, w_ref, x_ref, gate_ref, g_ref, o_ref):
    o = _dot(a_ref[...].astype(BF16), w_ref[...])
    o_ref[...] = _rms_residual(x_ref[...], gate_ref[...], o, g_ref[...])


def matmul_norm_residual(a, w_b, x, gate, g, *, per_token, tm, tpb):
    m, k = a.shape
    d = w_b.shape[1]
    gt, gt_spec = _mod_arg(gate, per_token, tm, tpb)
    return pl.pallas_call(
        _mnr_kernel,
        grid=(m // tm,),
        in_specs=[pl.BlockSpec((tm, k), lambda i: (i, 0)),
                  pl.BlockSpec((k, d), lambda i: (0, 0)),
                  pl.BlockSpec((tm, d), lambda i: (i, 0)), gt_spec, _vec_spec(d)],
        out_specs=pl.BlockSpec((tm, d), lambda i: (i, 0)),
        out_shape=jax.ShapeDtypeStruct((m, d), F32),
        compiler_params=_cparams("parallel"),
        name="matmul_norm_residual",
    )(a, w_b, x, gt, g.reshape(1, d))


def _combine_kernel(y1_ref, y2_ref, r_ref, x_ref, gate_ref, g_ref, o_ref):
    route = r_ref[...]
    y = route[:, TOP_K:TOP_K + 1] * y1_ref[...] + route[:, TOP_K + 1:TOP_K + 2] * y2_ref[...]
    o_ref[...] = _rms_residual(x_ref[...], gate_ref[...], y, g_ref[...])


def moe_combine(y1, y2, route, x, gate, g, *, per_token, tm, tpb):
    m, d = x.shape
    gt, gt_spec = _mod_arg(gate, per_token, tm, tpb)
    row = lambda n: pl.BlockSpec((tm, n), lambda i: (i, 0))
    return pl.pallas_call(
        _combine_kernel,
        grid=(m // tm,),
        in_specs=[row(d), row(d), row(route.shape[1]), row(d), gt_spec, _vec_spec(d)],
        out_specs=row(d),
        out_shape=jax.ShapeDtypeStruct((m, d), F32),
        compiler_params=_cparams("parallel"),
        name="moe_combine",
    )(y1, y2, route, x, gt, g.reshape(1, d))


def _ffn_kernel(x_ref, g2_ref, sh_ref, sc_ref, wg_ref, wu_ref, wd_ref, gate_ref, g3_ref, o_ref,
                h_ref, acc_ref):
    f = pl.program_id(1)

    @pl.when(f == 0)
    def _():
        h_ref[...] = _norm_mod(x_ref[...], g2_ref[...], sh_ref[...], sc_ref[...]).astype(BF16)
        acc_ref[...] = jnp.zeros(acc_ref.shape, F32)

    h = h_ref[...]
    act = _silu(_dot(h, wg_ref[...])) * _dot(h, wu_ref[...])
    acc_ref[...] += _dot(act.astype(BF16), wd_ref[...])

    @pl.when(f == pl.num_programs(1) - 1)
    def _():
        o_ref[...] = _rms_residual(x_ref[...], gate_ref[...], acc_ref[...], g3_ref[...])


def ffn_sublayer(x, g2, shift, scale, w_gu_b, w_down_b, gate, g3, *, per_token, tm, tpb, tf):
    m, d = x.shape
    fdim = w_down_b.shape[0]
    nf = fdim // tf
    sh, sh_spec = _mod_arg(shift, per_token, tm, tpb)
    sc, sc_spec = _mod_arg(scale, per_token, tm, tpb)
    gt, gt_spec = _mod_arg(gate, per_token, tm, tpb)
    return pl.pallas_call(
        _ffn_kernel,
        grid=(m // tm, nf),
        in_specs=[pl.BlockSpec((tm, d), lambda i, f: (i, 0)), _vec_spec(d), sh_spec, sc_spec,
                  pl.BlockSpec((d, tf), lambda i, f: (0, f)),
                  pl.BlockSpec((d, tf), lambda i, f: (0, nf + f)),
                  pl.BlockSpec((tf, d), lambda i, f: (f, 0)),
                  gt_spec, _vec_spec(d)],
        out_specs=pl.BlockSpec((tm, d), lambda i, f: (i, 0)),
        out_shape=jax.ShapeDtypeStruct((m, d), F32),
        scratch_shapes=[pltpu.VMEM((tm, d), BF16), pltpu.VMEM((tm, d), F32)],
        compiler_params=_cparams("parallel", "arbitrary"),
        name="ffn_sublayer",
    )(x, g2.reshape(1, d), sh, sc, w_gu_b, w_gu_b, w_down_b, gt, g3.reshape(1, d))


def _ret_kernel(q_ref, k_ref, v_ref, gt_ref, cos_ref, sin_ref, dm_ref, xi_ref, zt_ref, gc_ref,
                gn_ref, s0_ref, o_ref, s_ref):
    @pl.when(pl.program_id(1) == 0)
    def _():
        s_ref[...] = s0_ref[...]

    cos = cos_ref[...]
    sin = sin_ref[...]
    half = RET_DK // 2

    def rot(x):
        x1 = x[:, :half]
        x2 = x[:, half:]
        return jnp.concatenate([x1 * cos - x2 * sin, x2 * cos + x1 * sin], axis=-1)

    for h in range(RET_HEADS):
        kcols = slice(h * RET_DK, (h + 1) * RET_DK)
        vcols = slice(h * RET_DV, (h + 1) * RET_DV)
        q = rot(q_ref[:, kcols])
        k = rot(k_ref[:, kcols]) * (RET_DK ** -0.5)
        qb = q.astype(BF16)
        kb = k.astype(BF16)
        vb = v_ref[:, vcols].astype(BF16)
        state = s_ref[h]
        inner = _dot_nt(qb, kb) * dm_ref[h]
        o = _dot(inner.astype(BF16), vb) + _dot(qb, state.astype(BF16)) * xi_ref[h]
        kz = (k * zt_ref[h]).astype(BF16)
        upd = lax.dot_general(kz, vb, (((0,), (0,)), ((), ())), preferred_element_type=F32)
        s_ref[h] = gc_ref[h] * state + upd
        mu = jnp.mean(o, axis=-1, keepdims=True)
        dev = o - mu
        var = jnp.mean(dev * dev, axis=-1, keepdims=True)
        on = dev * lax.rsqrt(var + EPS) * gn_ref[:, vcols]
        o_ref[:, vcols] = _silu(gt_ref[:, vcols]) * on


def _ret_tables(chunk, rows):
    h = RET_HEADS
    log_g = jnp.log(1.0 - jnp.exp(jnp.linspace(math.log(1.0 / 32), math.log(1.0 / 512), h, dtype=F32)))
    i = jnp.arange(chunk, dtype=F32)
    diff = i[:, None] - i[None, :]
    dmat = jnp.where(diff >= 0, jnp.exp(log_g[:, None, None] * jnp.maximum(diff, 0.0)), 0.0)
    xi = jnp.exp(log_g[:, None] * (i + 1.0))
    zeta = jnp.exp(log_g[:, None] * (chunk - 1.0 - i))
    gch = jnp.exp(log_g * chunk)
    pad = rows - chunk
    dmat = jnp.pad(dmat, ((0, 0), (0, pad), (0, pad)))
    xi = jnp.pad(xi, ((0, 0), (0, pad)))[..., None]
    zeta = jnp.pad(zeta, ((0, 0), (0, pad)))[..., None]
    gch = jnp.broadcast_to(gch[:, None, None], (h, 1, RET_DV))
    return dmat, xi, zeta, gch


def _rope_tables(pos, rows):
    half = RET_DK // 2
    inv = ROPE_BASE ** (-jnp.arange(half, dtype=F32) / half)
    ang = pos.astype(F32)[:, None] * inv[None, :]
    pad = rows - pos.shape[0]
    return jnp.pad(jnp.cos(ang), ((0, pad), (0, 0))), jnp.pad(jnp.sin(ang), ((0, pad), (0, 0)))


def retention(proj, pos, s0, gn_g, chunk):
    b, t, _ = proj.shape
    c = RET_CHUNK
    n = t // c
    h = RET_HEADS
    dmat, xi, zeta, gch = _ret_tables(chunk, c)
    cos, sin = _rope_tables(pos, t)
    col = lambda j: pl.BlockSpec((None, c, h * RET_DK), lambda bi, ni: (bi, ni, j))
    const = lambda a: pl.BlockSpec(a.shape, lambda bi, ni: (0,) * a.ndim)
    state_spec = pl.BlockSpec((None, h, RET_DK, RET_DV), lambda bi, ni: (bi, 0, 0, 0))
    gn = gn_g.reshape(1, -1)
    o, s = pl.pallas_call(
        _ret_kernel,
        grid=(b, n),
        in_specs=[col(0), col(1), col(2), col(3),
                  pl.BlockSpec((c, RET_DK // 2), lambda bi, ni: (ni, 0)),
                  pl.BlockSpec((c, RET_DK // 2), lambda bi, ni: (ni, 0)),
                  const(dmat), const(xi), const(zeta), const(gch), const(gn), state_spec],
        out_specs=[pl.BlockSpec((None, c, h * RET_DV), lambda bi, ni: (bi, ni, 0)), state_spec],
        out_shape=[jax.ShapeDtypeStruct((b, t, h * RET_DV), F32),
                   jax.ShapeDtypeStruct((b, h, RET_DK, RET_DV), F32)],
        compiler_params=_cparams("parallel", "arbitrary"),
        name="retention",
    )(proj, proj, proj, proj, cos, sin, dmat, xi, zeta, gch, gn, s0)
    return o, s


def _compress_weights(cmp_pos, cmp_w1, cmp_b1, cmp_w2):
    g, d, hd = NSA_KV_HEADS, NSA_HEAD_DIM, CMP_HIDDEN
    eye = jnp.eye(g, dtype=F32)
    w1 = cmp_w1.reshape(2, CMP_BLOCK, d, hd)
    bd1 = jnp.einsum('ksdh,gj->ksgdjh', w1, eye).reshape(2, CMP_BLOCK, g * d, g * hd).astype(BF16)
    bd2 = jnp.einsum('khd,gj->kghjd', cmp_w2, eye).reshape(2, g * hd, g * d).astype(BF16)
    pos = jnp.tile(cmp_pos[:, :, None, :], (1, 1, g, 1)).reshape(2, CMP_BLOCK, 1, g * d)
    b1 = jnp.tile(cmp_b1[:, None, :], (1, g, 1)).reshape(2, 1, g * hd)
    return bd1, bd2, pos, b1


def _compress_finish(p0, p1, b1, bd2):
    rows = p0.shape[0]
    hid = b1 + p0 + pltpu.roll(p1, rows - 1, 0)
    return _dot(_gelu_tanh(hid).astype(BF16), bd2)


LANES = 128
CMP_PAGES_PER_STEP = 8
ATT_PAGES_PER_STEP = 16


def _tokens_to_sublanes(xt_ref, tr_ref):
    for half in range(KVW // LANES):
        tr_ref[half] = xt_ref[half * LANES:(half + 1) * LANES, :].T


def _cmp_prompt_kernel(xt_ref, bd1_ref, bd2_ref, pos_ref, b1_ref, o_ref, tr_ref):
    n_sub = xt_ref.shape[1] // CMP_STRIDE
    hid = bd1_ref.shape[-1]
    _tokens_to_sublanes(xt_ref, tr_ref)
    p0 = jnp.zeros((n_sub, hid), F32)
    p1 = jnp.zeros((n_sub, hid), F32)
    for s in range(CMP_STRIDE):
        xs = jnp.concatenate([tr_ref[half, pl.ds(s, n_sub, stride=CMP_STRIDE), :]
                              for half in range(KVW // LANES)], axis=-1)
        p0 = p0 + _dot((xs + pos_ref[s]).astype(BF16), bd1_ref[s])
        p1 = p1 + _dot((xs + pos_ref[CMP_STRIDE + s]).astype(BF16), bd1_ref[CMP_STRIDE + s])
    o_ref[...] = _compress_finish(p0, p1, b1_ref[...], bd2_ref[...])


def compress_prompt(kvt, cw):
    bd1, bd2, pos, b1 = cw
    b, _, t = kvt.shape
    n_sub = t // CMP_STRIDE
    return pl.pallas_call(
        _cmp_prompt_kernel,
        grid=(b, 2),
        in_specs=[pl.BlockSpec((None, KVW, t), lambda bi, ki: (bi, ki, 0)),
                  pl.BlockSpec((None,) + bd1.shape[1:], lambda bi, ki: (ki, 0, 0, 0)),
                  pl.BlockSpec((None,) + bd2.shape[1:], lambda bi, ki: (ki, 0, 0)),
                  pl.BlockSpec((None,) + pos.shape[1:], lambda bi, ki: (ki, 0, 0, 0)),
                  pl.BlockSpec((None,) + b1.shape[1:], lambda bi, ki: (ki, 0, 0))],
        out_specs=pl.BlockSpec((None, None, n_sub, KVW), lambda bi, ki: (ki, bi, 0, 0)),
        out_shape=jax.ShapeDtypeStruct((2, b, n_sub, KVW), F32),
        scratch_shapes=[pltpu.VMEM((KVW // LANES, t, LANES), F32)],
        compiler_params=_cparams("parallel", "parallel"),
        name="compress_prompt",
    )(kvt, bd1, bd2, pos, b1)


def _cmp_sample_kernel(pt_ref, *refs):
    page_refs = refs[:CMP_PAGES_PER_STEP]
    new_ref, bd1_ref, bd2_ref, pos_ref, b1_ref, o_ref, sub_ref, tr_ref = refs[CMP_PAGES_PER_STEP:]
    p = pl.program_id(1)
    page = page_refs[0].shape[2]
    per_page = page // CMP_STRIDE
    n_cmp = o_ref.shape[1]
    for j, page_ref in enumerate(page_refs):
        row0 = pl.multiple_of((p * CMP_PAGES_PER_STEP + j) * per_page, per_page)
        for kind in range(2):
            _tokens_to_sublanes(page_ref.at[kind], tr_ref.at[kind])
            for s in range(CMP_STRIDE):
                for half in range(KVW // LANES):
                    piece = tr_ref[kind, half, pl.ds(s, per_page, stride=CMP_STRIDE), :]
                    sub_ref[kind, s, pl.ds(row0, per_page), half * LANES:(half + 1) * LANES] = piece

    @pl.when(p == pl.num_programs(1) - 1)
    def _():
        n_rows = sub_ref.shape[2]
        tail = n_rows - n_cmp
        hid = bd1_ref.shape[-1]
        for kind in range(2):
            p0 = jnp.zeros((n_rows, hid), F32)
            p1 = jnp.zeros((n_rows, hid), F32)
            for s in range(CMP_STRIDE):
                sub_ref[kind, s, pl.ds(n_cmp, tail), :] = jnp.broadcast_to(new_ref[kind, s:s + 1, :], (tail, KVW))
                xs = sub_ref[kind, s]
                p0 = p0 + _dot((xs + pos_ref[kind, s]).astype(BF16), bd1_ref[kind, s])
                p1 = p1 + _dot((xs + pos_ref[kind, CMP_STRIDE + s]).astype(BF16), bd1_ref[kind, CMP_STRIDE + s])
            o_ref[kind] = _compress_finish(p0, p1, b1_ref[kind], bd2_ref[kind])[:n_cmp]


def _page_specs(n_per_step, kind_block, page):
    def spec(j):
        return pl.BlockSpec((None, 2, KVW, page),
                            lambda bi, pi, pt: (pt[bi, pi * n_per_step + j], kind_block, 0, 0))
    return [spec(j) for j in range(n_per_step)]


def compress_sample(cache_t, page_table, new_sub, cw):
    bd1, bd2, pos, b1 = cw
    b, n_pages = page_table.shape
    page = cache_t.shape[3]
    n_cmp = n_pages * page // CMP_STRIDE
    nps = CMP_PAGES_PER_STEP
    const = lambda a: pl.BlockSpec(a.shape, lambda bi, pi, pt: (0,) * a.ndim, pipeline_mode=pl.Buffered(1))
    grid_spec = pltpu.PrefetchScalarGridSpec(
        num_scalar_prefetch=1,
        grid=(b, n_pages // nps),
        in_specs=_page_specs(nps, 0, page) + [
            pl.BlockSpec((2, None, CMP_STRIDE, KVW), lambda bi, pi, pt: (0, bi, 0, 0)),
            const(bd1), const(bd2), const(pos), const(b1)],
        out_specs=pl.BlockSpec((2, None, n_cmp, KVW), lambda bi, pi, pt: (0, bi, 0, 0)),
        scratch_shapes=[pltpu.VMEM((2, CMP_STRIDE, n_cmp + 8, KVW), F32),
                        pltpu.VMEM((2, KVW // LANES, page, LANES), F32)],
    )
    return pl.pallas_call(
        _cmp_sample_kernel,
        grid_spec=grid_spec,
        out_shape=jax.ShapeDtypeStruct((2, b, n_cmp, KVW), F32),
        compiler_params=_cparams("parallel", "arbitrary"),
        name="compress_sample",
    )(page_table, *([cache_t] * nps), new_sub, bd1, bd2, pos, b1)


def _cover_matrix(n_cmp, n_sel, rows, cols):
    i = np.arange(n_cmp)[:, None]
    j = np.arange(n_sel)[None, :]
    cover = (i * CMP_STRIDE < (j + 1) * SEL_BLOCK) & (i * CMP_STRIDE + CMP_BLOCK > j * SEL_BLOCK)
    out = np.zeros((rows, cols), np.float32)
    out[:n_cmp, :n_sel] = cover
    return jnp.asarray(out)


SEL_CHUNK = 256
SEL_CLASS = 512


def _select_blocks_t(imp_t, cur, n_sel):
    nb, nq = imp_t.shape
    blk = lax.broadcasted_iota(I32, (nb, nq), 0)
    valid = (blk <= cur) & (blk < n_sel)
    forced = (blk == 0) | (blk == cur) | (blk == cur - 1)
    score = jnp.where(valid, jnp.where(forced, FORCED_SCORE, imp_t), -1.0)
    rank = jnp.zeros((nb, nq), F32)
    for i in range(n_sel):
        ci = score[i:i + 1, :]
        beats = (ci > score) | ((ci == score) & (blk > i))
        rank = rank + jnp.where(beats, 1.0, 0.0)
    return jnp.where(valid & (rank < float(min(N_SEL, n_sel))), 1.0, 0.0)


def _nsa_prompt_kernel(qt_ref, glt_ref, kc_ref, vct_ref, covt_ref, expt_ref, ks_ref, vst_ref, kw_ref, vwt_ref,
                       o_ref, s_ref, osel_ref, *, n_cmp, n_sel):
    tq = qt_ref.shape[1]
    r, d = NSA_GROUP, NSA_HEAD_DIM
    cols = r * tq
    t = ks_ref.shape[0]
    q0 = pl.program_id(2) * tq
    qt = qt_ref[...] * (d ** -0.5)
    qcat = jnp.concatenate([qt[h * d:(h + 1) * d, :] for h in range(r)], axis=-1)
    qb = jnp.concatenate([qcat, jnp.zeros_like(qcat)], axis=0).astype(BF16)
    gl = _sigmoid(glt_ref[...])

    def gate(branch):
        return jnp.concatenate([jnp.broadcast_to(gl[h * N_BRANCH + branch:h * N_BRANCH + branch + 1, :], (d, tq))
                                for h in range(r)], axis=0)

    def stack(o_t):
        return jnp.concatenate([o_t[:, h * tq:(h + 1) * tq] for h in range(r)], axis=0)

    def tile(x):
        return jnp.concatenate([x] * r, axis=-1)

    qpos = q0 + lax.broadcasted_iota(I32, (1, tq), 1)

    span = min(WINDOW + tq, t)
    w0 = pl.multiple_of(jnp.maximum(q0 + tq - span, 0), tq)
    kp = w0 + lax.broadcasted_iota(I32, (span, tq), 0)
    sm = _dot(kw_ref[pl.ds(w0, span), :], qb) + tile(jnp.where((kp <= qpos) & (kp > qpos - WINDOW), 0.0, NEG))
    p = jnp.exp(sm - jnp.max(sm, axis=0, keepdims=True))
    l_win = jnp.maximum(jnp.sum(p, axis=0, keepdims=True), 1e-30)
    o_win_t = _dot(vwt_ref[:, pl.ds(w0, span)].astype(BF16), p.astype(BF16)) / l_win
    out = gate(2) * stack(o_win_t)

    ncp = kc_ref.shape[0]
    nn = lax.broadcasted_iota(I32, (ncp, tq), 0)
    ok = tile(jnp.where((nn * CMP_STRIDE + (CMP_BLOCK - 1) <= qpos) & (nn < n_cmp), 1.0, 0.0))
    sm = _dot(kc_ref[...].astype(BF16), qb) + (ok - 1.0) * (-NEG)
    e = jnp.exp(sm - jnp.max(sm, axis=0, keepdims=True)) * ok
    p_cmp = e / jnp.maximum(jnp.sum(e, axis=0, keepdims=True), 1e-30)
    out = out + gate(0) * stack(_dot(vct_ref[...].astype(BF16), p_cmp.astype(BF16)))

    psum_t = p_cmp[:, :tq]
    for h in range(1, r):
        psum_t = psum_t + p_cmp[:, h * tq:(h + 1) * tq]
    imp_t = jnp.dot(covt_ref[...], psum_t, precision=HIGHEST, preferred_element_type=F32)
    sel_t = _select_blocks_t(imp_t, qpos >> 6, n_sel).astype(BF16)

    groups = SEL_CHUNK // 8

    def sel_branch(n_keys):
        m_run = jnp.full((8, cols), NEG, F32)
        for k0 in range(0, n_keys, SEL_CHUNK):
            sc = _dot(ks_ref[k0:k0 + SEL_CHUNK, :], qb)
            mtok = _dot(expt_ref[k0:k0 + SEL_CHUNK, :], sel_t)
            keypos = k0 + lax.broadcasted_iota(I32, (SEL_CHUNK, tq), 0)
            sm = sc + tile(jnp.where((mtok > 0.5) & (keypos <= qpos), 0.0, NEG))
            s_ref[k0:k0 + SEL_CHUNK, :] = sm
            m_run = jnp.maximum(m_run, jnp.max(sm.reshape(groups, 8, cols), axis=0))
        m_sel = jnp.max(m_run, axis=0, keepdims=True)
        l_run = jnp.zeros((8, cols), F32)
        acc_t = jnp.zeros((d, cols), F32)
        for k0 in range(0, n_keys, SEL_CHUNK):
            p = jnp.exp(s_ref[k0:k0 + SEL_CHUNK, :] - m_sel)
            l_run = l_run + jnp.sum(p.reshape(groups, 8, cols), axis=0)
            acc_t = acc_t + _dot(vst_ref[:, k0:k0 + SEL_CHUNK].astype(BF16), p.astype(BF16))
        l_sel = jnp.maximum(jnp.sum(l_run, axis=0, keepdims=True), 1e-30)
        osel_ref[...] = acc_t / l_sel

    sel_class = SEL_CLASS if t % SEL_CLASS == 0 else t
    cls_id = pl.program_id(2) // (sel_class // tq)
    for cls in range(t // sel_class):
        pl.when(cls_id == cls)(functools.partial(sel_branch, (cls + 1) * sel_class))
    o_ref[...] = (out + gate(1) * stack(osel_ref[...])).T


GATE_ROWS = 16


def nsa_prompt(qt, glt, kc, vct, kpad, kvt, wint):
    b, _, t = qt.shape
    g, r, d = NSA_KV_HEADS, NSA_GROUP, NSA_HEAD_DIM
    tq = Q_BLOCK
    nt = t // tq
    ncp = kc.shape[2]
    n_cmp = t // CMP_STRIDE - 1
    n_sel = -(-t // SEL_BLOCK)
    nb = -(-n_sel // 8) * 8
    covt = _cover_matrix(n_cmp, n_sel, ncp, nb).T
    expand_t = jnp.asarray((np.arange(t)[:, None] // SEL_BLOCK) == np.arange(nb)[None, :], BF16)
    per_bg = lambda shape: pl.BlockSpec((None, None) + shape, lambda bi, gi, qi: (bi, gi, 0, 0))
    const = lambda a: pl.BlockSpec(a.shape, lambda bi, gi, qi: (0,) * a.ndim)
    return pl.pallas_call(
        functools.partial(_nsa_prompt_kernel, n_cmp=n_cmp, n_sel=n_sel),
        grid=(b, g, nt),
        in_specs=[pl.BlockSpec((None, r * d, tq), lambda bi, gi, qi: (bi, gi, qi)),
                  pl.BlockSpec((None, GATE_ROWS, tq), lambda bi, gi, qi: (bi, gi, qi)),
                  per_bg((ncp, LANES)), per_bg((d, ncp)), const(covt), const(expand_t),
                  pl.BlockSpec((t, LANES), lambda bi, gi, qi: (bi, gi)),
                  pl.BlockSpec((None, d, t), lambda bi, gi, qi: (bi, 3 * g + gi, 0)),
                  pl.BlockSpec((t, LANES), lambda bi, gi, qi: (bi, g + gi)),
                  pl.BlockSpec((None, d, t), lambda bi, gi, qi: (bi, g + gi, 0))],
        out_specs=pl.BlockSpec((tq, r * d), lambda bi, gi, qi: (bi * nt + qi, gi)),
        out_shape=jax.ShapeDtypeStruct((b * t, NSA_HEADS * d), F32),
        scratch_shapes=[pltpu.VMEM((t, r * tq), F32), pltpu.VMEM((d, r * tq), F32)],
        compiler_params=_cparams("parallel", "parallel", "arbitrary"),
        name="nsa_prompt",
    )(qt, glt, kc, vct, covt, expand_t, kpad, kvt, kpad, wint)


def _nsa_sample_select_kernel(q_ref, kc_ref, vc_ref, cov_ref, oc_ref, sel_ref, *, t, past, n_cmp, n_sel):
    rows = q_ref.shape[0]
    g, r = NSA_KV_HEADS, NSA_GROUP
    qb = q_ref[...].astype(BF16)
    qpos_r = past + (lax.broadcasted_iota(I32, (rows, 1), 0) & (t - 1))
    s = _dot_nt(qb, kc_ref[...].astype(BF16))
    nn = lax.broadcasted_iota(I32, s.shape, 1)
    p_cmp = _masked_softmax(s, (nn * CMP_STRIDE + (CMP_BLOCK - 1) <= qpos_r) & (nn < n_cmp))
    oc_ref[...] = _dot(p_cmp.astype(BF16), vc_ref[...].astype(BF16))
    psum = jnp.sum(p_cmp.reshape(g, r, t, s.shape[1]), axis=1).reshape(g * t, s.shape[1])
    imp = jnp.dot(psum, cov_ref[...], precision=HIGHEST, preferred_element_type=F32)
    qpos_gt = past + (lax.broadcasted_iota(I32, (g * t, 1), 0) & (t - 1))
    sel = _select_blocks(imp, qpos_gt >> 6, n_sel)
    lanes = sel.shape[1]
    sel_ref[...] = jnp.broadcast_to(sel.reshape(g, 1, t, lanes), (g, r, t, lanes)).reshape(rows, lanes)


def nsa_sample_select(q4, kc, vc, *, t, past):
    b, rows, _ = q4.shape
    n_cmp = kc.shape[1]
    n_sel = -(-(past + t) // SEL_BLOCK)
    lanes = -(-n_sel // 128) * 128
    cover = _cover_matrix(n_cmp, n_sel, n_cmp, lanes)
    return pl.pallas_call(
        functools.partial(_nsa_sample_select_kernel, t=t, past=past, n_cmp=n_cmp, n_sel=n_sel),
        grid=(b,),
        in_specs=[pl.BlockSpec((None, rows, KVW), lambda bi: (bi, 0, 0)),
                  pl.BlockSpec((None, n_cmp, KVW), lambda bi: (bi, 0, 0)),
                  pl.BlockSpec((None, n_cmp, KVW), lambda bi: (bi, 0, 0)),
                  pl.BlockSpec(cover.shape, lambda bi: (0, 0))],
        out_specs=[pl.BlockSpec((None, rows, KVW), lambda bi: (bi, 0, 0)),
                   pl.BlockSpec((None, rows, lanes), lambda bi: (bi, 0, 0))],
        out_shape=[jax.ShapeDtypeStruct((b, rows, KVW), F32),
                   jax.ShapeDtypeStruct((b, rows, lanes), F32)],
        compiler_params=_cparams("parallel"),
        name="nsa_sample_select",
    )(q4, kc, vc, cover)


def _nsa_sample_attend_kernel(pt_ref, q_ref, sel_ref, oc_ref, gl_ref, *refs, t, past):
    page_refs = refs[:ATT_PAGES_PER_STEP]
    new_ref, win_ref, o_ref, m_ref, l_ref, acc_ref = refs[ATT_PAGES_PER_STEP:]
    rows = q_ref.shape[0]
    p = pl.program_id(1)
    page = page_refs[0].shape[2]
    width = ATT_PAGES_PER_STEP * page
    qb = q_ref[...].astype(BF16)
    sel = sel_ref[...]
    nblk = sel.shape[1]

    @pl.when(p == 0)
    def _():
        _flash_init(m_ref, l_ref, acc_ref)

    sc = jnp.concatenate([_dot(qb, pr[0].astype(BF16)) for pr in page_refs], axis=-1)
    jj = lax.broadcasted_iota(I32, (nblk, width), 0)
    tt = lax.broadcasted_iota(I32, (nblk, width), 1)
    expand = jnp.where(jj == ((p * width + tt) >> 6), 1.0, 0.0).astype(BF16)
    mask = _dot(sel.astype(BF16), expand) > 0.5
    sm = jnp.where(mask, sc, NEG)
    m_old = m_ref[...]
    m_new = jnp.maximum(m_old, jnp.max(sm, axis=-1, keepdims=True))
    alpha = jnp.exp(m_old - m_new)
    pr_b = jnp.where(mask, jnp.exp(sm - m_new), 0.0)
    l_ref[...] = alpha * l_ref[...] + jnp.sum(pr_b, axis=-1, keepdims=True)
    pr_b = pr_b.astype(BF16)
    pv = _dot_nt(pr_b[:, :page], page_refs[0][1].astype(BF16))
    for j in range(1, ATT_PAGES_PER_STEP):
        pv = pv + _dot_nt(pr_b[:, j * page:(j + 1) * page], page_refs[j][1].astype(BF16))
    acc_ref[...] = alpha * acc_ref[...] + pv
    m_ref[...] = m_new

    @pl.when(p == pl.num_programs(1) - 1)
    def _():
        qpos_r = past + (lax.broadcasted_iota(I32, (rows, 1), 0) & (t - 1))
        newr = new_ref[...]
        tp = newr.shape[0]
        jn = lax.broadcasted_iota(I32, (rows, tp), 1)
        new_ok = (jn < t) & (past + jn <= qpos_r)
        blk_new = past // SEL_BLOCK
        sel_new = sel[:, blk_new:blk_new + 1] > 0.5
        sc_n = _dot_nt(qb, newr[:, 2 * KVW:3 * KVW].astype(BF16))
        _flash_update(sc_n, new_ok & sel_new, newr[:, 3 * KVW:4 * KVW].astype(BF16), m_ref, l_ref, acc_ref)
        o_sel = _flash_result(l_ref, acc_ref)

        n_win = win_ref.shape[2]
        s_a = _dot(qb, win_ref[0].astype(BF16))
        s_b = _dot_nt(qb, newr[:, 4 * KVW:5 * KVW].astype(BF16))
        kp_a = (past - n_win) + lax.broadcasted_iota(I32, (rows, n_win), 1)
        ok_a = (kp_a <= qpos_r) & (kp_a > qpos_r - WINDOW) & (kp_a >= 0)
        ok_b = new_ok & (past + jn > qpos_r - WINDOW)
        sm_a = jnp.where(ok_a, s_a, NEG)
        sm_b = jnp.where(ok_b, s_b, NEG)
        mx = jnp.maximum(jnp.max(sm_a, axis=-1, keepdims=True), jnp.max(sm_b, axis=-1, keepdims=True))
        e_a = jnp.where(ok_a, jnp.exp(sm_a - mx), 0.0)
        e_b = jnp.where(ok_b, jnp.exp(sm_b - mx), 0.0)
        den = jnp.maximum(jnp.sum(e_a, axis=-1, keepdims=True) + jnp.sum(e_b, axis=-1, keepdims=True), 1e-30)
        o_win = (_dot_nt((e_a / den).astype(BF16), win_ref[1].astype(BF16))
                 + _dot((e_b / den).astype(BF16), newr[:, 5 * KVW:6 * KVW].astype(BF16)))

        gates = _sigmoid(gl_ref[...])
        o_ref[...] = gates[:, 0:1] * oc_ref[...] + gates[:, 1:2] * o_sel + gates[:, 2:3] * o_win


def nsa_sample_attend(q4, sel, ocmp, gl, cache_t, page_table, new_rows, win_t, *, t, past):
    b, rows, _ = q4.shape
    n_pages = page_table.shape[1]
    page = cache_t.shape[3]
    lanes = sel.shape[2]
    tp = new_rows.shape[1]
    n_win = win_t.shape[3]
    nps = ATT_PAGES_PER_STEP
    per_b = lambda shape: pl.BlockSpec((None,) + shape, lambda bi, pi, pt: (bi,) + (0,) * len(shape))
    grid_spec = pltpu.PrefetchScalarGridSpec(
        num_scalar_prefetch=1,
        grid=(b, n_pages // nps),
        in_specs=[per_b((rows, KVW)), per_b((rows, lanes)), per_b((rows, KVW)), per_b((rows, N_BRANCH))]
        + _page_specs(nps, 1, page)
        + [per_b((tp, N_KV_ROWS * KVW)), per_b((2, KVW, n_win))],
        out_specs=per_b((rows, KVW)),
        scratch_shapes=[pltpu.VMEM((rows, 1), F32), pltpu.VMEM((rows, 1), F32), pltpu.VMEM((rows, KVW), F32)],
    )
    return pl.pallas_call(
        functools.partial(_nsa_sample_attend_kernel, t=t, past=past),
        grid_spec=grid_spec,
        out_shape=jax.ShapeDtypeStruct((b, rows, KVW), F32),
        compiler_params=_cparams("parallel", "arbitrary"),
        name="nsa_sample_attend",
    )(page_table, q4, sel, ocmp, gl, *([cache_t] * nps), new_rows, win_t)


def _route_kernel(x_ref, g_ref, sh_ref, sc_ref, rw_ref, rb_ref, h_ref, r_ref):
    h = _norm_mod(x_ref[...], g_ref[...], sh_ref[...], sc_ref[...])
    h_ref[...] = h
    logits = jnp.dot(h, rw_ref[...], precision=HIGHEST, preferred_element_type=F32) + rb_ref[...]
    lane = lax.broadcasted_iota(I32, logits.shape, 1)
    lane_f = lane.astype(F32)
    lg = jnp.where(lane < N_EXPERTS, logits, NEG)
    v1 = jnp.max(lg, axis=-1, keepdims=True)
    i1 = jnp.min(jnp.where(lg == v1, lane_f, 128.0), axis=-1, keepdims=True)
    lg2 = jnp.where(lane_f == i1, NEG, lg)
    v2 = jnp.max(lg2, axis=-1, keepdims=True)
    i2 = jnp.min(jnp.where(lg2 == v2, lane_f, 128.0), axis=-1, keepdims=True)
    e = jnp.exp(v2 - v1)
    w1 = 1.0 / (1.0 + e)
    w2 = e / (1.0 + e)
    r_ref[...] = jnp.where(lane == 0, i1, jnp.where(lane == 1, i2, jnp.where(lane == 2, w1, jnp.where(lane == 3, w2, 0.0))))


def moe_route(x, g, shift, scale, rw_pad, rb_pad, *, per_token, tm, tpb):
    m, d = x.shape
    sh, sh_spec = _mod_arg(shift, per_token, tm, tpb)
    sc, sc_spec = _mod_arg(scale, per_token, tm, tpb)
    return pl.pallas_call(
        _route_kernel,
        grid=(m // tm,),
        in_specs=[pl.BlockSpec((tm, d), lambda i: (i, 0)), _vec_spec(d), sh_spec, sc_spec,
                  pl.BlockSpec((d, 128), lambda i: (0, 0)), _vec_spec(128)],
        out_specs=[pl.BlockSpec((tm, d), lambda i: (i, 0)), pl.BlockSpec((tm, 128), lambda i: (i, 0))],
        out_shape=[jax.ShapeDtypeStruct((m, d), F32), jax.ShapeDtypeStruct((m, 128), F32)],
        compiler_params=_cparams("parallel"),
        name="moe_route",
    )(x, g.reshape(1, d), sh, sc, rw_pad, rb_pad)


def _moe_kernel(te_ref, nv_ref, x_ref, wg_ref, wu_ref, wd_ref, o_ref, xb_ref, acc_ref):
    i = pl.program_id(0)
    f = pl.program_id(1)

    @pl.when(f == 0)
    def _():
        acc_ref[...] = jnp.zeros(acc_ref.shape, F32)
        xb_ref[...] = x_ref[...].astype(BF16)

    @pl.when(i < nv_ref[0])
    def _():
        x = xb_ref[...]
        act = _silu(_dot(x, wg_ref[...])) * _dot(x, wu_ref[...])
        acc_ref[...] += _dot(act.astype(BF16), wd_ref[...])

    @pl.when(f == pl.num_programs(1) - 1)
    def _():
        o_ref[...] = acc_ref[...]


def moe_experts(xs, tile_expert, n_valid, w_gu_b, w_down_b, *, tm, tf):
    p, d = xs.shape
    edim = w_down_b.shape[1]
    nf = edim // tf
    n_tiles = p // tm

    def wmap(off):
        def index(i, f, te, nv):
            ok = i < nv[0]
            return (te[i], 0, off + jnp.where(ok, f, nf - 1))
        return index

    def dmap(i, f, te, nv):
        return (te[i], jnp.where(i < nv[0], f, nf - 1), 0)

    grid_spec = pltpu.PrefetchScalarGridSpec(
        num_scalar_prefetch=2,
        grid=(n_tiles, nf),
        in_specs=[pl.BlockSpec((tm, d), lambda i, f, te, nv: (i, 0)),
                  pl.BlockSpec((None, d, tf), wmap(0)),
                  pl.BlockSpec((None, d, tf), wmap(nf)),
                  pl.BlockSpec((None, tf, d), dmap)],
        out_specs=pl.BlockSpec((tm, d), lambda i, f, te, nv: (i, 0)),
        scratch_shapes=[pltpu.VMEM((tm, d), BF16), pltpu.VMEM((tm, d), F32)],
    )
    return pl.pallas_call(
        _moe_kernel,
        grid_spec=grid_spec,
        out_shape=jax.ShapeDtypeStruct((p, d), F32),
        compiler_params=_cparams("parallel", "arbitrary"),
        name="moe_experts",
    )(tile_expert, n_valid, xs, w_gu_b, w_gu_b, w_down_b)


def _moe_tables(route, tm):
    m = route.shape[0]
    na = TOP_K * m
    e = route[:, :TOP_K].astype(I32).reshape(na)
    onehot = (e[:, None] == jnp.arange(N_EXPERTS, dtype=I32)[None, :]).astype(I32)
    within = jnp.sum((jnp.cumsum(onehot, axis=0) - onehot) * onehot, axis=1)
    counts = jnp.sum(onehot, axis=0)
    padded = ((counts + tm - 1) // tm) * tm
    ends = jnp.cumsum(padded)
    starts = ends - padded
    dest = starts[e] + within
    n_slots = (-(-na // tm) + N_EXPERTS) * tm
    src_tok = jnp.zeros((n_slots,), I32).at[dest].set(jnp.arange(na, dtype=I32) // TOP_K,
                                                       mode='promise_in_bounds', unique_indices=True)
    n_tiles = n_slots // tm
    n_valid = (ends[-1] // tm).astype(I32)
    tile_start = jnp.arange(n_tiles, dtype=I32) * tm
    tile_expert = jnp.sum((tile_start[:, None] >= ends[None, :]).astype(I32), axis=1)
    last = jnp.take(tile_expert, jnp.maximum(n_valid - 1, 0))
    tile_expert = jnp.where(jnp.arange(n_tiles) < n_valid, tile_expert, last).astype(I32)
    return src_tok, tile_expert, n_valid.reshape(1), dest


def _split_mod(mod):
    return [mod[:, i * D_MODEL:(i + 1) * D_MODEL] for i in range(mod.shape[1] // D_MODEL)]


def kernel(x_prompt, x_sample, c_prompt, c_sample, state_ret, cache_kv, cache_win, page_table, w_mod, b_mod, norm_g, ret_w_in, ret_gn_g, ret_w_out, kv_w_mod, kv_b_mod, kv_norm_g, kv_w, cmp_pos, cmp_w1, cmp_b1, cmp_w2, nsa_w_in, nsa_w_out, ffn_w_gu, ffn_w_down, moe_router_w, moe_router_b, moe_w_gu, moe_w_down):
    bp, t, d = x_prompt.shape
    bs, ts, _ = x_sample.shape
    mp, ms = bp * t, bs * ts
    n_pool, page = cache_kv.shape[:2]
    past = page_table.shape[1] * page
    g, r, hd = NSA_KV_HEADS, NSA_GROUP, NSA_HEAD_DIM

    ret_w_in_b = ret_w_in[0].astype(BF16)
    ret_w_out_b = ret_w_out[0].astype(BF16)
    kv_w_b = kv_w.astype(BF16)
    kv_wt_b = kv_w.T.astype(BF16)
    nq = NSA_HEADS * hd
    gate_w = nsa_w_in[0][:, nq:].reshape(d, g, r * N_BRANCH)
    gate_w = jnp.pad(gate_w, ((0, 0), (0, 0), (0, LANES - r * N_BRANCH))).reshape(d, g * LANES)
    nsa_w_in_b = jnp.concatenate([nsa_w_in[0][:, :nq], gate_w], axis=1).astype(BF16)
    gate_wt = nsa_w_in[0][:, nq:].T.reshape(g, r * N_BRANCH, d)
    gate_wt = jnp.pad(gate_wt, ((0, 0), (0, GATE_ROWS - r * N_BRANCH), (0, 0))).reshape(g * GATE_ROWS, d)
    nsa_wt_b = jnp.concatenate([nsa_w_in[0][:, :nq].T, gate_wt], axis=0).astype(BF16)
    def pad_heads(w):
        return jnp.pad(w.reshape(d, g, hd), ((0, 0), (0, 0), (0, LANES - hd))).reshape(d, g * LANES)
    kpad_w_b = jnp.concatenate([pad_heads(kv_w[:, 2 * KVW:3 * KVW]), pad_heads(kv_w[:, 4 * KVW:5 * KVW])],
                               axis=1).astype(BF16)
    nsa_w_out_b = nsa_w_out[0].astype(BF16)
    ffn_w_gu_b = ffn_w_gu[0].astype(BF16)
    ffn_w_down_b = ffn_w_down[0].astype(BF16)
    moe_w_gu_b = moe_w_gu[0].astype(BF16)
    moe_w_down_b = moe_w_down[0].astype(BF16)
    rw_pad = jnp.pad(moe_router_w[0], ((0, 0), (0, 128 - N_EXPERTS)))
    rb_pad = jnp.pad(moe_router_b[0], (0, 128 - N_EXPERTS)).reshape(1, 128)
    cw = _compress_weights(cmp_pos, cmp_w1, cmp_b1, cmp_w2)

    c_all = jnp.concatenate([c_prompt, c_sample], axis=0)
    mods = [cond_matmul(c_all, w_mod[layer], b_mod[layer]) for layer in range(w_mod.shape[0])]
    kv_mod = cond_matmul(c_all, kv_w_mod, kv_b_mod)

    tm_p = min(512, t)
    tm_f = min(1024, t)
    groups = {
        'p': dict(x=x_prompt.reshape(mp, d), b=bp, t=t, per_token=False, tm=tm_p, tpb=t // tm_p,
                  tm_f=tm_f, tpb_f=t // tm_f,
                  mod=lambda a: a[:bp]),
        's': dict(x=x_sample.reshape(ms, d), b=bs, t=ts, per_token=True, tm=ms, tpb=1, tm_f=ms, tpb_f=1,
                  mod=lambda a: jnp.repeat(a[bp:], ts, axis=0)),
    }
    out = {}

    for name, gr in groups.items():
        kw = dict(per_token=gr['per_token'], tm=gr['tm'], tpb=gr['tpb'])
        m0 = [gr['mod'](a) for a in _split_mod(mods[0])]
        x = gr['x']
        proj = norm_mod_matmul(x, norm_g[0, 0], m0[0], m0[1], ret_w_in_b, tn=1024, **kw)
        proj = proj.reshape(gr['b'], gr['t'], -1)
        if name == 'p':
            pos = jnp.arange(t)
            s0 = jnp.zeros((bp, RET_HEADS, RET_DK, RET_DV), F32)
            o, s_new = retention(proj, pos, s0, ret_gn_g[0], math.gcd(t, RET_CHUNK))
        else:
            pos = past + jnp.arange(ts)
            proj = jnp.pad(proj, ((0, 0), (0, RET_CHUNK - ts), (0, 0)))
            o, s_new = retention(proj, pos, state_ret[0], ret_gn_g[0], math.gcd(ts, RET_CHUNK))
            o = o[:, :ts]
        out['ret_' + name] = s_new[None]
        x = matmul_norm_residual(o.reshape(-1, d), ret_w_out_b, x, m0[2], norm_g[0, 1], **kw)
        x = ffn_sublayer(x, norm_g[0, 2], m0[3], m0[4], ffn_w_gu_b, ffn_w_down_b, m0[5], norm_g[0, 3],
                         per_token=gr['per_token'], tm=gr['tm_f'], tpb=gr['tpb_f'], tf=256)
        gr['x1'] = x

    for name, gr in groups.items():
        kw = dict(per_token=gr['per_token'], tm=gr['tm'], tpb=gr['tpb'])
        m1 = [gr['mod'](a) for a in _split_mod(mods[1])]
        kvm = [gr['mod'](a) for a in _split_mod(kv_mod)]
        gr['m1'] = m1
        x = gr['x1']
        b_, t_ = gr['b'], gr['t']
        if name == 'p':
            kvt, wint = norm_mod_matmul_t(x, kv_norm_g, kvm[0], kvm[1], kv_wt_b,
                                          (N_PAGED_ROWS * KVW, (N_KV_ROWS - N_PAGED_ROWS) * KVW), b=b_, tm=gr['tm'])
            out['kv_p'] = kvt.reshape(b_, N_PAGED_ROWS, g, hd, t_).transpose(0, 4, 1, 2, 3)
            n_keep = min(WINDOW, t_)
            out['win_p'] = wint[:, :, t_ - n_keep:].reshape(b_, 2, g, hd, n_keep).transpose(0, 4, 1, 2, 3)
            kpad = norm_mod_matmul(x, kv_norm_g, kvm[0], kvm[1], kpad_w_b, out_dtype=BF16, **kw)
            qt, glt = norm_mod_matmul_t(x, norm_g[1, 0], m1[0], m1[1], nsa_wt_b, (nq, g * GATE_ROWS),
                                        b=b_, tm=gr['tm'])
            cmp = compress_prompt(kvt, cw).reshape(2, b_, -1, g, hd)
            kc = jnp.pad(cmp[0].transpose(0, 2, 1, 3), ((0, 0), (0, 0), (0, 0), (0, LANES - hd)))
            o = nsa_prompt(qt, glt, kc, cmp[1].transpose(0, 2, 3, 1), kpad, kvt, wint)
        else:
            proj = norm_mod_matmul(x, norm_g[1, 0], m1[0], m1[1], nsa_w_in_b, **kw)
            rows = norm_mod_matmul(x, kv_norm_g, kvm[0], kvm[1], kv_w_b, **kw)
            q = (proj[:, :NSA_HEADS * hd] * (hd ** -0.5)).reshape(b_, t_, g, r, hd)
            gl = proj[:, NSA_HEADS * hd:].reshape(b_, t_, g, LANES)[..., :r * N_BRANCH]
            gl = gl.reshape(b_, t_, g, r, N_BRANCH)
            rows3 = rows.reshape(b_, t_, N_KV_ROWS * KVW)
            rows6 = rows.reshape(b_, t_, N_KV_ROWS, g, hd)
            out['kv_s'] = rows6[:, :, :N_PAGED_ROWS]
            cache_t = cache_kv.transpose(0, 2, 3, 4, 1).reshape(n_pool, N_PAGED_ROWS, KVW, page)
            win_t5 = cache_win.transpose(0, 2, 3, 4, 1)
            n_win = cache_win.shape[1]
            new_win_t = rows6[:, :, N_PAGED_ROWS:].transpose(0, 2, 3, 4, 1)
            out['win_s'] = jnp.concatenate([win_t5, new_win_t], axis=-1)[..., t_:].transpose(0, 4, 1, 2, 3)
            new_sub = jnp.pad(rows3[:, :, :2 * KVW], ((0, 0), (0, CMP_STRIDE - t_), (0, 0)))
            new_sub = new_sub.reshape(b_, CMP_STRIDE, 2, KVW).transpose(2, 0, 1, 3)
            cmp = compress_sample(cache_t, page_table, new_sub, cw)
            eye = jnp.eye(g, dtype=F32)
            qrows = q.transpose(0, 2, 3, 1, 4).reshape(b_, g, r * t_, hd)
            q4 = jnp.einsum('bgxd,gj->bgxjd', qrows, eye).reshape(b_, g * r * t_, KVW)
            glr = gl.transpose(0, 2, 3, 1, 4).reshape(b_, g * r * t_, N_BRANCH)
            ocmp, sel = nsa_sample_select(q4, cmp[0], cmp[1], t=t_, past=past)
            new_rows = jnp.pad(rows3, ((0, 0), (0, 128 - t_), (0, 0)))
            o4 = nsa_sample_attend(q4, sel, ocmp, glr, cache_t, page_table, new_rows,
                                   win_t5.reshape(b_, 2, KVW, n_win), t=t_, past=past)
            o4 = o4.reshape(b_, g, r, t_, g, hd)
            o = jnp.einsum('bgrtjd,gj->btgrd', o4, eye).reshape(ms, NSA_HEADS * hd)
        gr['x2'] = matmul_norm_residual(o, nsa_w_out_b, x, m1[2], norm_g[1, 1], **kw)

    hs, routes = [], []
    for name, gr in groups.items():
        kw = dict(per_token=gr['per_token'], tm=gr['tm'], tpb=gr['tpb'])
        m1 = gr['m1']
        h, route = moe_route(gr['x2'], norm_g[1, 2], m1[3], m1[4], rw_pad, rb_pad, **kw)
        hs.append(h)
        routes.append(route)
    h_all = jnp.concatenate(hs, axis=0)
    route_all = jnp.concatenate(routes, axis=0)
    tm_e = 512 if mp >= 4096 else 128
    src_tok, tile_expert, n_valid, dest = _moe_tables(route_all, tm_e)
    take = lambda a, idx: a.at[idx].get(mode='promise_in_bounds')
    xs = take(h_all, src_tok)
    ys = moe_experts(xs, tile_expert, n_valid, moe_w_gu_b, moe_w_down_b, tm=tm_e, tf=512)
    dest2 = dest.reshape(-1, TOP_K)
    for name, gr, lo, hi, route in (('p', groups['p'], 0, mp, routes[0]), ('s', groups['s'], mp, mp + ms, routes[1])):
        kw = dict(per_token=gr['per_token'], tm=gr['tm'], tpb=gr['tpb'])
        out['y_' + name] = moe_combine(take(ys, dest2[lo:hi, 0]), take(ys, dest2[lo:hi, 1]), route,
                                       gr['x2'], gr['m1'][5], norm_g[1, 3], **kw)

    return (out['y_p'].reshape(bp, t, d), out['y_s'].reshape(bs, ts, d),
            out['ret_p'], out['ret_s'], out['kv_p'], out['kv_s'], out['win_p'], out['win_s'])
```

```python
import functools
import math

import numpy as np
import jax
import jax.numpy as jnp
from jax import lax
from jax.experimental import pallas as pl
from jax.experimental.pallas import tpu as pltpu

F32 = jnp.float32
BF16 = jnp.bfloat16
I32 = jnp.int32

D_MODEL = 1024
N_MOD = 6
RET_HEADS = 4
RET_DK = 256
RET_DV = 256
RET_CHUNK = 128
ROPE_BASE = 10000.0
NSA_HEADS = 16
NSA_KV_HEADS = 4
NSA_GROUP = 4
NSA_HEAD_DIM = 64
N_BRANCH = 3
N_KV_ROWS = 6
N_PAGED_ROWS = 4
CMP_BLOCK = 32
CMP_STRIDE = 16
CMP_HIDDEN = 128
SEL_BLOCK = 64
N_SEL = 16
FORCED_SCORE = 1.0e4
WINDOW = 512
Q_BLOCK = 128
N_EXPERTS = 8
TOP_K = 2
EPS = 1e-6

NEG = -1.0e30
KVW = NSA_KV_HEADS * NSA_HEAD_DIM
VMEM_LIMIT_BYTES = 56 * 1024 * 1024
HIGHEST = lax.Precision.HIGHEST


LANES = 128
FFN_HIDDEN_TILE_CAP = 1408
MOE_HIDDEN_TILE_CAP = 896


def _largest_tile(n, cap):
    best = LANES
    for k in range(LANES, cap + 1, LANES):
        if n % k == 0:
            best = k
    return best


def _cparams(*sem):
    return pltpu.CompilerParams(dimension_semantics=sem, vmem_limit_bytes=VMEM_LIMIT_BYTES)


def _sigmoid(x):
    return 1.0 / (1.0 + jnp.exp(-x))


def _silu(x):
    return x * _sigmoid(x)


def _gelu_tanh(x):
    return x * (0.5 * (1.0 + jnp.tanh(math.sqrt(2.0 / math.pi) * (x + 0.044715 * (x * x * x)))))


def _norm_mod(x, g, shift, scale):
    ms = jnp.mean(x * x, axis=-1, keepdims=True)
    return (x * lax.rsqrt(ms + EPS) * g) * (1.0 + scale) + shift


def _rms_residual(x, gate, o, g):
    ms = jnp.mean(o * o, axis=-1, keepdims=True)
    return x + gate * (o * lax.rsqrt(ms + EPS) * g)


def _dot(a, b):
    return jnp.dot(a, b, preferred_element_type=F32)


def _dot_nt(a, b):
    return lax.dot_general(a, b, (((1,), (1,)), ((), ())), preferred_element_type=F32)


def _masked_softmax(s, mask):
    sm = jnp.where(mask, s, NEG)
    m = jnp.max(sm, axis=-1, keepdims=True)
    e = jnp.where(mask, jnp.exp(sm - m), 0.0)
    return e / jnp.maximum(jnp.sum(e, axis=-1, keepdims=True), 1e-30)


def _flash_init(m_ref, l_ref, acc_ref):
    m_ref[...] = jnp.full(m_ref.shape, NEG, F32)
    l_ref[...] = jnp.zeros(l_ref.shape, F32)
    acc_ref[...] = jnp.zeros(acc_ref.shape, F32)


def _flash_update(s, mask, v_b, m_ref, l_ref, acc_ref):
    sm = jnp.where(mask, s, NEG)
    m_old = m_ref[...]
    m_new = jnp.maximum(m_old, jnp.max(sm, axis=-1, keepdims=True))
    alpha = jnp.exp(m_old - m_new)
    p = jnp.where(mask, jnp.exp(sm - m_new), 0.0)
    l_ref[...] = alpha * l_ref[...] + jnp.sum(p, axis=-1, keepdims=True)
    acc_ref[...] = alpha * acc_ref[...] + _dot(p.astype(BF16), v_b)
    m_ref[...] = m_new


def _flash_result(l_ref, acc_ref):
    return acc_ref[...] / jnp.maximum(l_ref[...], 1e-30)


def _select_blocks(imp, cur, n_sel):
    rows, lanes = imp.shape
    blk = lax.broadcasted_iota(I32, (rows, lanes), 1)
    valid = (blk <= cur) & (blk < n_sel)
    forced = (blk == 0) | (blk == cur) | (blk == cur - 1)
    score = jnp.where(valid, jnp.where(forced, FORCED_SCORE, imp), -1.0)
    rank = jnp.zeros((rows, lanes), F32)
    for i in range(n_sel):
        ci = score[:, i:i + 1]
        beats = (ci > score) | ((ci == score) & (blk > i))
        rank = rank + jnp.where(beats, 1.0, 0.0)
    return jnp.where(valid & (rank < float(min(N_SEL, n_sel))), 1.0, 0.0)


def _cond_kernel(c_ref, w_ref, b_ref, o_ref):
    sc = _silu(c_ref[...])
    o_ref[...] = _dot(sc.astype(BF16), w_ref[...].astype(BF16)) + b_ref[...]


def cond_matmul(c, w, b):
    bc, d = c.shape
    n = w.shape[1]
    tn = 1024
    return pl.pallas_call(
        _cond_kernel,
        grid=(n // tn,),
        in_specs=[pl.BlockSpec((bc, d), lambda j: (0, 0)),
                  pl.BlockSpec((d, tn), lambda j: (0, j)),
                  pl.BlockSpec((1, tn), lambda j: (0, j))],
        out_specs=pl.BlockSpec((bc, tn), lambda j: (0, j)),
        out_shape=jax.ShapeDtypeStruct((bc, n), F32),
        compiler_params=_cparams("parallel"),
        name="cond_matmul",
    )(c, w, b.reshape(1, n))


def _mod_arg(m, per_token, tm, tiles_per_batch):
    d = m.shape[-1]
    if per_token:
        return m, pl.BlockSpec((tm, d), lambda i, *_: (i, 0))
    return m[:, None, :], pl.BlockSpec((None, 1, d), lambda i, *_: (i // tiles_per_batch, 0, 0))


def _vec_spec(d):
    return pl.BlockSpec((1, d), lambda i, *_: (0, 0))


def _nmm_kernel(x_ref, g_ref, sh_ref, sc_ref, w_ref, o_ref, h_ref):
    @pl.when(pl.program_id(1) == 0)
    def _():
        h_ref[...] = _norm_mod(x_ref[...], g_ref[...], sh_ref[...], sc_ref[...]).astype(BF16)

    o_ref[...] = _dot(h_ref[...], w_ref[...]).astype(o_ref.dtype)


def norm_mod_matmul(x, g, shift, scale, w_b, *, per_token, tm, tpb, tn=None, out_dtype=F32):
    m, d = x.shape
    n = w_b.shape[1]
    tn = n if tn is None else tn
    sh, sh_spec = _mod_arg(shift, per_token, tm, tpb)
    sc, sc_spec = _mod_arg(scale, per_token, tm, tpb)
    return pl.pallas_call(
        _nmm_kernel,
        grid=(m // tm, n // tn),
        in_specs=[pl.BlockSpec((tm, d), lambda i, j: (i, 0)), _vec_spec(d), sh_spec, sc_spec,
                  pl.BlockSpec((d, tn), lambda i, j: (0, j))],
        out_specs=pl.BlockSpec((tm, tn), lambda i, j: (i, j)),
        out_shape=jax.ShapeDtypeStruct((m, n), out_dtype),
        scratch_shapes=[pltpu.VMEM((tm, d), BF16)],
        compiler_params=_cparams("parallel", "arbitrary"),
        name="norm_mod_matmul",
    )(x, g.reshape(1, d), sh, sc, w_b)


def _nmm_t_kernel(x_ref, g_ref, sh_ref, sc_ref, wt_ref, *o_refs):
    h = _norm_mod(x_ref[...], g_ref[...], sh_ref[...], sc_ref[...]).astype(BF16)
    o = _dot_nt(wt_ref[...], h)
    row = 0
    for o_ref in o_refs:
        o_ref[...] = o[row:row + o_ref.shape[0]]
        row += o_ref.shape[0]


def norm_mod_matmul_t(x, g, shift, scale, wt_b, splits, *, b, tm):
    m, d = x.shape
    t = m // b
    tpb = t // tm
    n = wt_b.shape[0]
    sh, sh_spec = _mod_arg(shift, False, tm, tpb)
    sc, sc_spec = _mod_arg(scale, False, tm, tpb)
    return pl.pallas_call(
        _nmm_t_kernel,
        grid=(m // tm,),
        in_specs=[pl.BlockSpec((tm, d), lambda i: (i, 0)), _vec_spec(d), sh_spec, sc_spec,
                  pl.BlockSpec((n, d), lambda i: (0, 0))],
        out_specs=[pl.BlockSpec((None, ni, tm), lambda i: (i // tpb, 0, i % tpb)) for ni in splits],
        out_shape=[jax.ShapeDtypeStruct((b, ni, t), F32) for ni in splits],
        compiler_params=_cparams("parallel"),
        name="norm_mod_matmul_t",
    )(x, g.reshape(1, d), sh, sc, wt_b)


def _mnr_kernel(a_ref, w_ref, x_ref, gate_ref, g_ref, o_ref):
    o = _dot(a_ref[...].astype(BF16), w_ref[...])
    o_ref[...] = _rms_residual(x_ref[...], gate_ref[...], o, g_ref[...])


def matmul_norm_residual(a, w_b, x, gate, g, *, per_token, tm, tpb):
    m, k = a.shape
    d = w_b.shape[1]
    gt, gt_spec = _mod_arg(gate, per_token, tm, tpb)
    return pl.pallas_call(
        _mnr_kernel,
        grid=(m // tm,),
        in_specs=[pl.BlockSpec((tm, k), lambda i: (i, 0)),
                  pl.BlockSpec((k, d), lambda i: (0, 0)),
                  pl.BlockSpec((tm, d), lambda i: (i, 0)), gt_spec, _vec_spec(d)],
        out_specs=pl.BlockSpec((tm, d), lambda i: (i, 0)),
        out_shape=jax.ShapeDtypeStruct((m, d), F32),
        compiler_params=_cparams("parallel"),
        name="matmul_norm_residual",
    )(a, w_b, x, gt, g.reshape(1, d))


def _combine_kernel(y1_ref, y2_ref, r_ref, x_ref, gate_ref, g_ref, o_ref):
    route = r_ref[...]
    y = route[:, TOP_K:TOP_K + 1] * y1_ref[...] + route[:, TOP_K + 1:TOP_K + 2] * y2_ref[...]
    o_ref[...] = _rms_residual(x_ref[...], gate_ref[...], y, g_ref[...])


def moe_combine(y1, y2, route, x, gate, g, *, per_token, tm, tpb):
    m, d = x.shape
    gt, gt_spec = _mod_arg(gate, per_token, tm, tpb)
    row = lambda n: pl.BlockSpec((tm, n), lambda i: (i, 0))
    return pl.pallas_call(
        _combine_kernel,
        grid=(m // tm,),
        in_specs=[row(d), row(d), row(route.shape[1]), row(d), gt_spec, _vec_spec(d)],
        out_specs=row(d),
        out_shape=jax.ShapeDtypeStruct((m, d), F32),
        compiler_params=_cparams("parallel"),
        name="moe_combine",
    )(y1, y2, route, x, gt, g.reshape(1, d))


def _ffn_kernel(x_ref, g2_ref, sh_ref, sc_ref, wg_ref, wu_ref, wd_ref, gate_ref, g3_ref, o_ref,
                h_ref, acc_ref):
    f = pl.program_id(1)

    @pl.when(f == 0)
    def _():
        h_ref[...] = _norm_mod(x_ref[...], g2_ref[...], sh_ref[...], sc_ref[...]).astype(BF16)
        acc_ref[...] = jnp.zeros(acc_ref.shape, F32)

    h = h_ref[...]
    act = _silu(_dot(h, wg_ref[...])) * _dot(h, wu_ref[...])
    acc_ref[...] += _dot(act.astype(BF16), wd_ref[...])

    @pl.when(f == pl.num_programs(1) - 1)
    def _():
        o_ref[...] = _rms_residual(x_ref[...], gate_ref[...], acc_ref[...], g3_ref[...])


def ffn_sublayer(x, g2, shift, scale, w_gu_b, w_down_b, gate, g3, *, per_token, tm, tpb, tf):
    m, d = x.shape
    fdim = w_down_b.shape[0]
    nf = fdim // tf
    sh, sh_spec = _mod_arg(shift, per_token, tm, tpb)
    sc, sc_spec = _mod_arg(scale, per_token, tm, tpb)
    gt, gt_spec = _mod_arg(gate, per_token, tm, tpb)
    return pl.pallas_call(
        _ffn_kernel,
        grid=(m // tm, nf),
        in_specs=[pl.BlockSpec((tm, d), lambda i, f: (i, 0)), _vec_spec(d), sh_spec, sc_spec,
                  pl.BlockSpec((d, tf), lambda i, f: (0, f)),
                  pl.BlockSpec((d, tf), lambda i, f: (0, nf + f)),
                  pl.BlockSpec((tf, d), lambda i, f: (f, 0)),
                  gt_spec, _vec_spec(d)],
        out_specs=pl.BlockSpec((tm, d), lambda i, f: (i, 0)),
        out_shape=jax.ShapeDtypeStruct((m, d), F32),
        scratch_shapes=[pltpu.VMEM((tm, d), BF16), pltpu.VMEM((tm, d), F32)],
        compiler_params=_cparams("parallel", "arbitrary"),
        name="ffn_sublayer",
    )(x, g2.reshape(1, d), sh, sc, w_gu_b, w_gu_b, w_down_b, gt, g3.reshape(1, d))


def _ret_kernel(q_ref, k_ref, v_ref, gt_ref, cos_ref, sin_ref, dm_ref, xi_ref, zt_ref, gc_ref,
                gn_ref, s0_ref, o_ref, s_ref):
    @pl.when(pl.program_id(1) == 0)
    def _():
        s_ref[...] = s0_ref[...]

    cos = cos_ref[...]
    sin = sin_ref[...]
    half = RET_DK // 2

    def rot(x):
        x1 = x[:, :half]
        x2 = x[:, half:]
        return jnp.concatenate([x1 * cos - x2 * sin, x2 * cos + x1 * sin], axis=-1)

    for h in range(RET_HEADS):
        kcols = slice(h * RET_DK, (h + 1) * RET_DK)
        vcols = slice(h * RET_DV, (h + 1) * RET_DV)
        q = rot(q_ref[:, kcols])
        k = rot(k_ref[:, kcols]) * (RET_DK ** -0.5)
        qb = q.astype(BF16)
        kb = k.astype(BF16)
        vb = v_ref[:, vcols].astype(BF16)
        state = s_ref[h]
        inner = _dot_nt(qb, kb) * dm_ref[h]
        o = _dot(inner.astype(BF16), vb) + _dot(qb, state.astype(BF16)) * xi_ref[h]
        kz = (k * zt_ref[h]).astype(BF16)
        upd = lax.dot_general(kz, vb, (((0,), (0,)), ((), ())), preferred_element_type=F32)
        s_ref[h] = gc_ref[h] * state + upd
        mu = jnp.mean(o, axis=-1, keepdims=True)
        dev = o - mu
        var = jnp.mean(dev * dev, axis=-1, keepdims=True)
        on = dev * lax.rsqrt(var + EPS) * gn_ref[:, vcols]
        o_ref[:, vcols] = _silu(gt_ref[:, vcols]) * on


def _ret_tables(chunk, rows):
    h = RET_HEADS
    log_g = jnp.log(1.0 - jnp.exp(jnp.linspace(math.log(1.0 / 32), math.log(1.0 / 512), h, dtype=F32)))
    i = jnp.arange(chunk, dtype=F32)
    diff = i[:, None] - i[None, :]
    dmat = jnp.where(diff >= 0, jnp.exp(log_g[:, None, None] * jnp.maximum(diff, 0.0)), 0.0)
    xi = jnp.exp(log_g[:, None] * (i + 1.0))
    zeta = jnp.exp(log_g[:, None] * (chunk - 1.0 - i))
    gch = jnp.exp(log_g * chunk)
    pad = rows - chunk
    dmat = jnp.pad(dmat, ((0, 0), (0, pad), (0, pad)))
    xi = jnp.pad(xi, ((0, 0), (0, pad)))[..., None]
    zeta = jnp.pad(zeta, ((0, 0), (0, pad)))[..., None]
    gch = jnp.broadcast_to(gch[:, None, None], (h, 1, RET_DV))
    return dmat, xi, zeta, gch


def _rope_tables(pos, rows):
    half = RET_DK // 2
    inv = ROPE_BASE ** (-jnp.arange(half, dtype=F32) / half)
    ang = pos.astype(F32)[:, None] * inv[None, :]
    pad = rows - pos.shape[0]
    return jnp.pad(jnp.cos(ang), ((0, pad), (0, 0))), jnp.pad(jnp.sin(ang), ((0, pad), (0, 0)))


def retention(proj, pos, s0, gn_g, chunk):
    b, t, _ = proj.shape
    c = RET_CHUNK
    n = t // c
    h = RET_HEADS
    dmat, xi, zeta, gch = _ret_tables(chunk, c)
    cos, sin = _rope_tables(pos, t)
    col = lambda j: pl.BlockSpec((None, c, h * RET_DK), lambda bi, ni: (bi, ni, j))
    const = lambda a: pl.BlockSpec(a.shape, lambda bi, ni: (0,) * a.ndim)
    state_spec = pl.BlockSpec((None, h, RET_DK, RET_DV), lambda bi, ni: (bi, 0, 0, 0))
    gn = gn_g.reshape(1, -1)
    o, s = pl.pallas_call(
        _ret_kernel,
        grid=(b, n),
        in_specs=[col(0), col(1), col(2), col(3),
                  pl.BlockSpec((c, RET_DK // 2), lambda bi, ni: (ni, 0)),
                  pl.BlockSpec((c, RET_DK // 2), lambda bi, ni: (ni, 0)),
                  const(dmat), const(xi), const(zeta), const(gch), const(gn), state_spec],
        out_specs=[pl.BlockSpec((None, c, h * RET_DV), lambda bi, ni: (bi, ni, 0)), state_spec],
        out_shape=[jax.ShapeDtypeStruct((b, t, h * RET_DV), F32),
                   jax.ShapeDtypeStruct((b, h, RET_DK, RET_DV), F32)],
        compiler_params=_cparams("parallel", "arbitrary"),
        name="retention",
    )(proj, proj, proj, proj, cos, sin, dmat, xi, zeta, gch, gn, s0)
    return o, s


def _compress_weights(cmp_pos, cmp_w1, cmp_b1, cmp_w2):
    g, d, hd = NSA_KV_HEADS, NSA_HEAD_DIM, CMP_HIDDEN
    eye = jnp.eye(g, dtype=F32)
    w1 = cmp_w1.reshape(2, CMP_BLOCK, d, hd)
    bd1 = jnp.einsum('ksdh,gj->ksgdjh', w1, eye).reshape(2, CMP_BLOCK, g * d, g * hd).astype(BF16)
    bd2 = jnp.einsum('khd,gj->kghjd', cmp_w2, eye).reshape(2, g * hd, g * d).astype(BF16)
    pos = jnp.tile(cmp_pos[:, :, None, :], (1, 1, g, 1)).reshape(2, CMP_BLOCK, 1, g * d)
    b1 = jnp.tile(cmp_b1[:, None, :], (1, g, 1)).reshape(2, 1, g * hd)
    return bd1, bd2, pos, b1


def _compress_finish(p0, p1, b1, bd2):
    rows = p0.shape[0]
    hid = b1 + p0 + pltpu.roll(p1, rows - 1, 0)
    return _dot(_gelu_tanh(hid).astype(BF16), bd2)


CMP_PAGES_PER_STEP = 8
ATT_PAGES_PER_STEP = 16


def _tokens_to_sublanes(xt_ref, tr_ref):
    for half in range(KVW // LANES):
        tr_ref[half] = xt_ref[half * LANES:(half + 1) * LANES, :].T


def _cmp_prompt_kernel(xt_ref, bd1_ref, bd2_ref, pos_ref, b1_ref, o_ref, tr_ref):
    n_sub = xt_ref.shape[1] // CMP_STRIDE
    hid = bd1_ref.shape[-1]
    _tokens_to_sublanes(xt_ref, tr_ref)
    p0 = jnp.zeros((n_sub, hid), F32)
    p1 = jnp.zeros((n_sub, hid), F32)
    for s in range(CMP_STRIDE):
        xs = jnp.concatenate([tr_ref[half, pl.ds(s, n_sub, stride=CMP_STRIDE), :]
                              for half in range(KVW // LANES)], axis=-1)
        p0 = p0 + _dot((xs + pos_ref[s]).astype(BF16), bd1_ref[s])
        p1 = p1 + _dot((xs + pos_ref[CMP_STRIDE + s]).astype(BF16), bd1_ref[CMP_STRIDE + s])
    o_ref[...] = _compress_finish(p0, p1, b1_ref[...], bd2_ref[...])


def compress_prompt(kvt, cw):
    bd1, bd2, pos, b1 = cw
    b, _, t = kvt.shape
    n_sub = t // CMP_STRIDE
    return pl.pallas_call(
        _cmp_prompt_kernel,
        grid=(b, 2),
        in_specs=[pl.BlockSpec((None, KVW, t), lambda bi, ki: (bi, ki, 0)),
                  pl.BlockSpec((None,) + bd1.shape[1:], lambda bi, ki: (ki, 0, 0, 0)),
                  pl.BlockSpec((None,) + bd2.shape[1:], lambda bi, ki: (ki, 0, 0)),
                  pl.BlockSpec((None,) + pos.shape[1:], lambda bi, ki: (ki, 0, 0, 0)),
                  pl.BlockSpec((None,) + b1.shape[1:], lambda bi, ki: (ki, 0, 0))],
        out_specs=pl.BlockSpec((None, None, n_sub, KVW), lambda bi, ki: (ki, bi, 0, 0)),
        out_shape=jax.ShapeDtypeStruct((2, b, n_sub, KVW), F32),
        scratch_shapes=[pltpu.VMEM((KVW // LANES, t, LANES), F32)],
        compiler_params=_cparams("parallel", "parallel"),
        name="compress_prompt",
    )(kvt, bd1, bd2, pos, b1)


def _cmp_sample_kernel(pt_ref, *refs):
    page_refs = refs[:CMP_PAGES_PER_STEP]
    new_ref, bd1_ref, bd2_ref, pos_ref, b1_ref, o_ref, sub_ref, tr_ref = refs[CMP_PAGES_PER_STEP:]
    p = pl.program_id(1)
    page = page_refs[0].shape[2]
    per_page = page // CMP_STRIDE
    n_cmp = o_ref.shape[1]
    for j, page_ref in enumerate(page_refs):
        row0 = pl.multiple_of((p * CMP_PAGES_PER_STEP + j) * per_page, per_page)
        for kind in range(2):
            _tokens_to_sublanes(page_ref.at[kind], tr_ref.at[kind])
            for s in range(CMP_STRIDE):
                for half in range(KVW // LANES):
                    piece = tr_ref[kind, half, pl.ds(s, per_page, stride=CMP_STRIDE), :]
                    sub_ref[kind, s, pl.ds(row0, per_page), half * LANES:(half + 1) * LANES] = piece

    @pl.when(p == pl.num_programs(1) - 1)
    def _():
        n_rows = sub_ref.shape[2]
        tail = n_rows - n_cmp
        hid = bd1_ref.shape[-1]
        for kind in range(2):
            p0 = jnp.zeros((n_rows, hid), F32)
            p1 = jnp.zeros((n_rows, hid), F32)
            for s in range(CMP_STRIDE):
                sub_ref[kind, s, pl.ds(n_cmp, tail), :] = jnp.broadcast_to(new_ref[kind, s:s + 1, :], (tail, KVW))
                xs = sub_ref[kind, s]
                p0 = p0 + _dot((xs + pos_ref[kind, s]).astype(BF16), bd1_ref[kind, s])
                p1 = p1 + _dot((xs + pos_ref[kind, CMP_STRIDE + s]).astype(BF16), bd1_ref[kind, CMP_STRIDE + s])
            o_ref[kind] = _compress_finish(p0, p1, b1_ref[kind], bd2_ref[kind])[:n_cmp]


def _page_specs(n_per_step, kind_block, page):
    def spec(j):
        return pl.BlockSpec((None, 2, KVW, page),
                            lambda bi, pi, pt: (pt[bi, pi * n_per_step + j], kind_block, 0, 0))
    return [spec(j) for j in range(n_per_step)]


def compress_sample(cache_t, page_table, new_sub, cw):
    bd1, bd2, pos, b1 = cw
    b, n_pages = page_table.shape
    page = cache_t.shape[3]
    n_cmp = n_pages * page // CMP_STRIDE
    nps = CMP_PAGES_PER_STEP
    const = lambda a: pl.BlockSpec(a.shape, lambda bi, pi, pt: (0,) * a.ndim, pipeline_mode=pl.Buffered(1))
    grid_spec = pltpu.PrefetchScalarGridSpec(
        num_scalar_prefetch=1,
        grid=(b, n_pages // nps),
        in_specs=_page_specs(nps, 0, page) + [
            pl.BlockSpec((2, None, CMP_STRIDE, KVW), lambda bi, pi, pt: (0, bi, 0, 0)),
            const(bd1), const(bd2), const(pos), const(b1)],
        out_specs=pl.BlockSpec((2, None, n_cmp, KVW), lambda bi, pi, pt: (0, bi, 0, 0)),
        scratch_shapes=[pltpu.VMEM((2, CMP_STRIDE, n_cmp + 8, KVW), F32),
                        pltpu.VMEM((2, KVW // LANES, page, LANES), F32)],
    )
    return pl.pallas_call(
        _cmp_sample_kernel,
        grid_spec=grid_spec,
        out_shape=jax.ShapeDtypeStruct((2, b, n_cmp, KVW), F32),
        compiler_params=_cparams("parallel", "arbitrary"),
        name="compress_sample",
    )(page_table, *([cache_t] * nps), new_sub, bd1, bd2, pos, b1)


def _cover_matrix(n_cmp, n_sel, rows, cols):
    i = np.arange(n_cmp)[:, None]
    j = np.arange(n_sel)[None, :]
    cover = (i * CMP_STRIDE < (j + 1) * SEL_BLOCK) & (i * CMP_STRIDE + CMP_BLOCK > j * SEL_BLOCK)
    out = np.zeros((rows, cols), np.float32)
    out[:n_cmp, :n_sel] = cover
    return jnp.asarray(out)


SEL_CHUNK = 256
SEL_CLASS = 512


def _select_blocks_t(imp_t, cur, n_sel):
    nb, nq = imp_t.shape
    blk = lax.broadcasted_iota(I32, (nb, nq), 0)
    valid = (blk <= cur) & (blk < n_sel)
    forced = (blk == 0) | (blk == cur) | (blk == cur - 1)
    score = jnp.where(valid, jnp.where(forced, FORCED_SCORE, imp_t), -1.0)
    rank = jnp.zeros((nb, nq), F32)
    for i in range(n_sel):
        ci = score[i:i + 1, :]
        beats = (ci > score) | ((ci == score) & (blk > i))
        rank = rank + jnp.where(beats, 1.0, 0.0)
    return jnp.where(valid & (rank < float(min(N_SEL, n_sel))), 1.0, 0.0)


def _nsa_prompt_kernel(qt_ref, glt_ref, kc_ref, vct_ref, covt_ref, kblk_ref, wb_ref, ks_ref, vst_ref, kw_ref, vwt_ref,
                       o_ref, s_ref, osel_ref, *, n_cmp, n_sel):
    tq = qt_ref.shape[1]
    r, d = NSA_GROUP, NSA_HEAD_DIM
    cols = r * tq
    t = ks_ref.shape[0]
    q0 = pl.program_id(2) * tq
    qt = qt_ref[...] * (d ** -0.5)
    qcat = jnp.concatenate([qt[h * d:(h + 1) * d, :] for h in range(r)], axis=-1)
    qb = jnp.concatenate([qcat, jnp.zeros_like(qcat)], axis=0).astype(BF16)
    gl = _sigmoid(glt_ref[...])

    def gate(branch):
        return jnp.concatenate([jnp.broadcast_to(gl[h * N_BRANCH + branch:h * N_BRANCH + branch + 1, :], (d, tq))
                                for h in range(r)], axis=0)

    def stack(o_t):
        return jnp.concatenate([o_t[:, h * tq:(h + 1) * tq] for h in range(r)], axis=0)

    def tile(x):
        return jnp.concatenate([x] * r, axis=-1)

    ones_rows = 16

    def with_ones(vt):
        return jnp.concatenate([vt.astype(BF16), jnp.ones((ones_rows, vt.shape[1]), BF16)], axis=0)

    qpos = q0 + lax.broadcasted_iota(I32, (1, tq), 1)

    span = min(WINDOW + tq, t)
    w0 = pl.multiple_of(jnp.maximum(q0 + tq - span, 0), tq)
    sm = _dot(kw_ref[pl.ds(w0, span), :], qb) + tile(wb_ref[(q0 - w0) // tq])
    p = jnp.exp(sm - jnp.max(sm, axis=0, keepdims=True))
    o_win_t = _dot(with_ones(vwt_ref[:, pl.ds(w0, span)]), p.astype(BF16))
    out = gate(2) * stack(o_win_t[:d] / jnp.maximum(o_win_t[d:d + 1], 1e-30))

    ncp = kc_ref.shape[0]
    nn = lax.broadcasted_iota(I32, (ncp, tq), 0)
    ok = tile(jnp.where((nn * CMP_STRIDE + (CMP_BLOCK - 1) <= qpos) & (nn < n_cmp), 1.0, 0.0))
    sm = _dot(kc_ref[...].astype(BF16), qb) + (ok - 1.0) * (-NEG)
    e = jnp.exp(sm - jnp.max(sm, axis=0, keepdims=True)) * ok
    p_cmp = e / jnp.maximum(jnp.sum(e, axis=0, keepdims=True), 1e-30)
    out = out + gate(0) * stack(_dot(vct_ref[...].astype(BF16), p_cmp.astype(BF16)))

    psum_t = p_cmp[:, :tq]
    for h in range(1, r):
        psum_t = psum_t + p_cmp[:, h * tq:(h + 1) * tq]
    imp_t = jnp.dot(covt_ref[...], psum_t, precision=HIGHEST, preferred_element_type=F32)
    sel_t = _select_blocks_t(imp_t, qpos >> 6, n_sel)
    nb = sel_t.shape[0]
    q_sel = jnp.concatenate([qcat, tile((sel_t - 1.0) * (-NEG)), jnp.zeros((LANES - d - nb, cols), F32)],
                            axis=0).astype(BF16)

    groups = SEL_CHUNK // 8
    sel_class = SEL_CLASS if t % SEL_CLASS == 0 else t

    def sel_branch(n_keys):
        m_run = jnp.full((8, cols), NEG, F32)
        for k0 in range(0, n_keys, SEL_CHUNK):
            k_aug = ks_ref[k0:k0 + SEL_CHUNK, :] + kblk_ref[k0:k0 + SEL_CHUNK, :]
            sm = _dot(k_aug, q_sel)
            if k0 + SEL_CHUNK > n_keys - sel_class:
                keypos = k0 + lax.broadcasted_iota(I32, (SEL_CHUNK, tq), 0)
                sm = sm + tile(jnp.where(keypos <= qpos, 0.0, NEG))
            s_ref[k0:k0 + SEL_CHUNK, :] = sm
            m_run = jnp.maximum(m_run, jnp.max(sm.reshape(groups, 8, cols), axis=0))
        m_sel = jnp.max(m_run, axis=0, keepdims=True)
        acc_t = jnp.zeros((d + ones_rows, cols), F32)
        for k0 in range(0, n_keys, SEL_CHUNK):
            p = jnp.exp(s_ref[k0:k0 + SEL_CHUNK, :] - m_sel)
            acc_t = acc_t + _dot(with_ones(vst_ref[:, k0:k0 + SEL_CHUNK]), p.astype(BF16))
        osel_ref[...] = acc_t[:d] / jnp.maximum(acc_t[d:d + 1], 1e-30)

    cls_id = pl.program_id(2) // (sel_class // tq)
    for cls in range(t // sel_class):
        pl.when(cls_id == cls)(functools.partial(sel_branch, (cls + 1) * sel_class))
    o_ref[...] = (out + gate(1) * stack(osel_ref[...])).T


GATE_ROWS = 16


def nsa_prompt(qt, glt, kc, vct, kpad, kvt, wint):
    b, _, t = qt.shape
    g, r, d = NSA_KV_HEADS, NSA_GROUP, NSA_HEAD_DIM
    tq = Q_BLOCK
    nt = t // tq
    ncp = kc.shape[2]
    n_cmp = t // CMP_STRIDE - 1
    n_sel = -(-t // SEL_BLOCK)
    nb = -(-n_sel // 8) * 8
    covt = _cover_matrix(n_cmp, n_sel, ncp, nb).T
    assert d + nb <= LANES, "block one-hot must fit in the keys' padding lanes"
    kblk = np.zeros((t, LANES), np.float32)
    kblk[np.arange(t), d + np.arange(t) // SEL_BLOCK] = 1.0
    kblk = jnp.asarray(kblk, BF16)
    span = min(WINDOW + tq, t)
    i = np.arange(span)[None, :, None]
    j = np.arange(tq)[None, None, :]
    rel = i - j - np.arange(0, span - tq + 1, tq)[:, None, None]
    wbias = jnp.asarray(np.where((rel <= 0) & (rel > -WINDOW), 0.0, NEG), F32)
    per_bg = lambda shape: pl.BlockSpec((None, None) + shape, lambda bi, gi, qi: (bi, gi, 0, 0))
    const = lambda a: pl.BlockSpec(a.shape, lambda bi, gi, qi: (0,) * a.ndim)
    return pl.pallas_call(
        functools.partial(_nsa_prompt_kernel, n_cmp=n_cmp, n_sel=n_sel),
        grid=(b, g, nt),
        in_specs=[pl.BlockSpec((None, r * d, tq), lambda bi, gi, qi: (bi, gi, qi)),
                  pl.BlockSpec((None, GATE_ROWS, tq), lambda bi, gi, qi: (bi, gi, qi)),
                  per_bg((ncp, LANES)), per_bg((d, ncp)), const(covt), const(kblk), const(wbias),
                  pl.BlockSpec((t, LANES), lambda bi, gi, qi: (bi, gi)),
                  pl.BlockSpec((None, d, t), lambda bi, gi, qi: (bi, 3 * g + gi, 0)),
                  pl.BlockSpec((t, LANES), lambda bi, gi, qi: (bi, g + gi)),
                  pl.BlockSpec((None, d, t), lambda bi, gi, qi: (bi, g + gi, 0))],
        out_specs=pl.BlockSpec((tq, r * d), lambda bi, gi, qi: (bi * nt + qi, gi)),
        out_shape=jax.ShapeDtypeStruct((b * t, NSA_HEADS * d), F32),
        scratch_shapes=[pltpu.VMEM((t, r * tq), F32), pltpu.VMEM((d, r * tq), F32)],
        compiler_params=_cparams("parallel", "parallel", "arbitrary"),
        name="nsa_prompt",
    )(qt, glt, kc, vct, covt, kblk, wbias, kpad, kvt, kpad, wint)


def _nsa_sample_select_kernel(q_ref, kc_ref, vc_ref, cov_ref, oc_ref, sel_ref, *, t, past, n_cmp, n_sel):
    rows = q_ref.shape[0]
    g, r = NSA_KV_HEADS, NSA_GROUP
    qb = q_ref[...].astype(BF16)
    qpos_r = past + (lax.broadcasted_iota(I32, (rows, 1), 0) & (t - 1))
    s = _dot_nt(qb, kc_ref[...].astype(BF16))
    nn = lax.broadcasted_iota(I32, s.shape, 1)
    p_cmp = _masked_softmax(s, (nn * CMP_STRIDE + (CMP_BLOCK - 1) <= qpos_r) & (nn < n_cmp))
    oc_ref[...] = _dot(p_cmp.astype(BF16), vc_ref[...].astype(BF16))
    psum = jnp.sum(p_cmp.reshape(g, r, t, s.shape[1]), axis=1).reshape(g * t, s.shape[1])
    imp = jnp.dot(psum, cov_ref[...], precision=HIGHEST, preferred_element_type=F32)
    qpos_gt = past + (lax.broadcasted_iota(I32, (g * t, 1), 0) & (t - 1))
    sel = _select_blocks(imp, qpos_gt >> 6, n_sel)
    lanes = sel.shape[1]
    sel_ref[...] = jnp.broadcast_to(sel.reshape(g, 1, t, lanes), (g, r, t, lanes)).reshape(rows, lanes)


def nsa_sample_select(q4, kc, vc, *, t, past):
    b, rows, _ = q4.shape
    n_cmp = kc.shape[1]
    n_sel = -(-(past + t) // SEL_BLOCK)
    lanes = -(-n_sel // 128) * 128
    cover = _cover_matrix(n_cmp, n_sel, n_cmp, lanes)
    return pl.pallas_call(
        functools.partial(_nsa_sample_select_kernel, t=t, past=past, n_cmp=n_cmp, n_sel=n_sel),
        grid=(b,),
        in_specs=[pl.BlockSpec((None, rows, KVW), lambda bi: (bi, 0, 0)),
                  pl.BlockSpec((None, n_cmp, KVW), lambda bi: (bi, 0, 0)),
                  pl.BlockSpec((None, n_cmp, KVW), lambda bi: (bi, 0, 0)),
                  pl.BlockSpec(cover.shape, lambda bi: (0, 0))],
        out_specs=[pl.BlockSpec((None, rows, KVW), lambda bi: (bi, 0, 0)),
                   pl.BlockSpec((None, rows, lanes), lambda bi: (bi, 0, 0))],
        out_shape=[jax.ShapeDtypeStruct((b, rows, KVW), F32),
                   jax.ShapeDtypeStruct((b, rows, lanes), F32)],
        compiler_params=_cparams("parallel"),
        name="nsa_sample_select",
    )(q4, kc, vc, cover)


def _nsa_sample_attend_kernel(pt_ref, q_ref, sel_ref, oc_ref, gl_ref, *refs, t, past):
    page_refs = refs[:ATT_PAGES_PER_STEP]
    new_ref, win_ref, o_ref, m_ref, l_ref, acc_ref = refs[ATT_PAGES_PER_STEP:]
    rows = q_ref.shape[0]
    p = pl.program_id(1)
    page = page_refs[0].shape[2]
    width = ATT_PAGES_PER_STEP * page
    qb = q_ref[...].astype(BF16)
    sel = sel_ref[...]
    nblk = sel.shape[1]

    @pl.when(p == 0)
    def _():
        _flash_init(m_ref, l_ref, acc_ref)

    sc = jnp.concatenate([_dot(qb, pr[0].astype(BF16)) for pr in page_refs], axis=-1)
    jj = lax.broadcasted_iota(I32, (nblk, width), 0)
    tt = lax.broadcasted_iota(I32, (nblk, width), 1)
    expand = jnp.where(jj == ((p * width + tt) >> 6), 1.0, 0.0).astype(BF16)
    mask = _dot(sel.astype(BF16), expand) > 0.5
    sm = jnp.where(mask, sc, NEG)
    m_old = m_ref[...]
    m_new = jnp.maximum(m_old, jnp.max(sm, axis=-1, keepdims=True))
    alpha = jnp.exp(m_old - m_new)
    pr_b = jnp.where(mask, jnp.exp(sm - m_new), 0.0)
    l_ref[...] = alpha * l_ref[...] + jnp.sum(pr_b, axis=-1, keepdims=True)
    pr_b = pr_b.astype(BF16)
    pv = _dot_nt(pr_b[:, :page], page_refs[0][1].astype(BF16))
    for j in range(1, ATT_PAGES_PER_STEP):
        pv = pv + _dot_nt(pr_b[:, j * page:(j + 1) * page], page_refs[j][1].astype(BF16))
    acc_ref[...] = alpha * acc_ref[...] + pv
    m_ref[...] = m_new

    @pl.when(p == pl.num_programs(1) - 1)
    def _():
        qpos_r = past + (lax.broadcasted_iota(I32, (rows, 1), 0) & (t - 1))
        newr = new_ref[...]
        tp = newr.shape[0]
        jn = lax.broadcasted_iota(I32, (rows, tp), 1)
        new_ok = (jn < t) & (past + jn <= qpos_r)
        blk_new = past // SEL_BLOCK
        sel_new = sel[:, blk_new:blk_new + 1] > 0.5
        sc_n = _dot_nt(qb, newr[:, 2 * KVW:3 * KVW].astype(BF16))
        _flash_update(sc_n, new_ok & sel_new, newr[:, 3 * KVW:4 * KVW].astype(BF16), m_ref, l_ref, acc_ref)
        o_sel = _flash_result(l_ref, acc_ref)

        n_win = win_ref.shape[2]
        s_a = _dot(qb, win_ref[0].astype(BF16))
        s_b = _dot_nt(qb, newr[:, 4 * KVW:5 * KVW].astype(BF16))
        kp_a = (past - n_win) + lax.broadcasted_iota(I32, (rows, n_win), 1)
        ok_a = (kp_a <= qpos_r) & (kp_a > qpos_r - WINDOW) & (kp_a >= 0)
        ok_b = new_ok & (past + jn > qpos_r - WINDOW)
        sm_a = jnp.where(ok_a, s_a, NEG)
        sm_b = jnp.where(ok_b, s_b, NEG)
        mx = jnp.maximum(jnp.max(sm_a, axis=-1, keepdims=True), jnp.max(sm_b, axis=-1, keepdims=True))
        e_a = jnp.where(ok_a, jnp.exp(sm_a - mx), 0.0)
        e_b = jnp.where(ok_b, jnp.exp(sm_b - mx), 0.0)
        den = jnp.maximum(jnp.sum(e_a, axis=-1, keepdims=True) + jnp.sum(e_b, axis=-1, keepdims=True), 1e-30)
        o_win = (_dot_nt((e_a / den).astype(BF16), win_ref[1].astype(BF16))
                 + _dot((e_b / den).astype(BF16), newr[:, 5 * KVW:6 * KVW].astype(BF16)))

        gates = _sigmoid(gl_ref[...])
        o_ref[...] = gates[:, 0:1] * oc_ref[...] + gates[:, 1:2] * o_sel + gates[:, 2:3] * o_win


def nsa_sample_attend(q4, sel, ocmp, gl, cache_t, page_table, new_rows, win_t, *, t, past):
    b, rows, _ = q4.shape
    n_pages = page_table.shape[1]
    page = cache_t.shape[3]
    lanes = sel.shape[2]
    tp = new_rows.shape[1]
    n_win = win_t.shape[3]
    nps = ATT_PAGES_PER_STEP
    per_b = lambda shape: pl.BlockSpec((None,) + shape, lambda bi, pi, pt: (bi,) + (0,) * len(shape))
    grid_spec = pltpu.PrefetchScalarGridSpec(
        num_scalar_prefetch=1,
        grid=(b, n_pages // nps),
        in_specs=[per_b((rows, KVW)), per_b((rows, lanes)), per_b((rows, KVW)), per_b((rows, N_BRANCH))]
        + _page_specs(nps, 1, page)
        + [per_b((tp, N_KV_ROWS * KVW)), per_b((2, KVW, n_win))],
        out_specs=per_b((rows, KVW)),
        scratch_shapes=[pltpu.VMEM((rows, 1), F32), pltpu.VMEM((rows, 1), F32), pltpu.VMEM((rows, KVW), F32)],
    )
    return pl.pallas_call(
        functools.partial(_nsa_sample_attend_kernel, t=t, past=past),
        grid_spec=grid_spec,
        out_shape=jax.ShapeDtypeStruct((b, rows, KVW), F32),
        compiler_params=_cparams("parallel", "arbitrary"),
        name="nsa_sample_attend",
    )(page_table, q4, sel, ocmp, gl, *([cache_t] * nps), new_rows, win_t)


def _route_kernel(x_ref, g_ref, sh_ref, sc_ref, rw_ref, rb_ref, h_ref, r_ref):
    h = _norm_mod(x_ref[...], g_ref[...], sh_ref[...], sc_ref[...])
    h_ref[...] = h
    logits = jnp.dot(h, rw_ref[...], precision=HIGHEST, preferred_element_type=F32) + rb_ref[...]
    lane = lax.broadcasted_iota(I32, logits.shape, 1)
    lane_f = lane.astype(F32)
    lg = jnp.where(lane < N_EXPERTS, logits, NEG)
    v1 = jnp.max(lg, axis=-1, keepdims=True)
    i1 = jnp.min(jnp.where(lg == v1, lane_f, 128.0), axis=-1, keepdims=True)
    lg2 = jnp.where(lane_f == i1, NEG, lg)
    v2 = jnp.max(lg2, axis=-1, keepdims=True)
    i2 = jnp.min(jnp.where(lg2 == v2, lane_f, 128.0), axis=-1, keepdims=True)
    e = jnp.exp(v2 - v1)
    w1 = 1.0 / (1.0 + e)
    w2 = e / (1.0 + e)
    r_ref[...] = jnp.where(lane == 0, i1, jnp.where(lane == 1, i2, jnp.where(lane == 2, w1, jnp.where(lane == 3, w2, 0.0))))


def moe_route(x, g, shift, scale, rw_pad, rb_pad, *, per_token, tm, tpb):
    m, d = x.shape
    sh, sh_spec = _mod_arg(shift, per_token, tm, tpb)
    sc, sc_spec = _mod_arg(scale, per_token, tm, tpb)
    return pl.pallas_call(
        _route_kernel,
        grid=(m // tm,),
        in_specs=[pl.BlockSpec((tm, d), lambda i: (i, 0)), _vec_spec(d), sh_spec, sc_spec,
                  pl.BlockSpec((d, 128), lambda i: (0, 0)), _vec_spec(128)],
        out_specs=[pl.BlockSpec((tm, d), lambda i: (i, 0)), pl.BlockSpec((tm, 128), lambda i: (i, 0))],
        out_shape=[jax.ShapeDtypeStruct((m, d), F32), jax.ShapeDtypeStruct((m, 128), F32)],
        compiler_params=_cparams("parallel"),
        name="moe_route",
    )(x, g.reshape(1, d), sh, sc, rw_pad, rb_pad)


def _moe_kernel(te_ref, nv_ref, x_ref, wg_ref, wu_ref, wd_ref, o_ref, xb_ref, acc_ref):
    i = pl.program_id(0)
    f = pl.program_id(1)

    @pl.when(f == 0)
    def _():
        acc_ref[...] = jnp.zeros(acc_ref.shape, F32)
        xb_ref[...] = x_ref[...].astype(BF16)

    @pl.when(i < nv_ref[0])
    def _():
        x = xb_ref[...]
        act = _silu(_dot(x, wg_ref[...])) * _dot(x, wu_ref[...])
        acc_ref[...] += _dot(act.astype(BF16), wd_ref[...])

    @pl.when(f == pl.num_programs(1) - 1)
    def _():
        o_ref[...] = acc_ref[...]


def moe_experts(xs, tile_expert, n_valid, w_gu_b, w_down_b, *, tm, tf):
    p, d = xs.shape
    edim = w_down_b.shape[1]
    nf = edim // tf
    n_tiles = p // tm

    def wmap(off):
        def index(i, f, te, nv):
            ok = i < nv[0]
            return (te[i], 0, off + jnp.where(ok, f, nf - 1))
        return index

    def dmap(i, f, te, nv):
        return (te[i], jnp.where(i < nv[0], f, nf - 1), 0)

    grid_spec = pltpu.PrefetchScalarGridSpec(
        num_scalar_prefetch=2,
        grid=(n_tiles, nf),
        in_specs=[pl.BlockSpec((tm, d), lambda i, f, te, nv: (i, 0)),
                  pl.BlockSpec((None, d, tf), wmap(0)),
                  pl.BlockSpec((None, d, tf), wmap(nf)),
                  pl.BlockSpec((None, tf, d), dmap)],
        out_specs=pl.BlockSpec((tm, d), lambda i, f, te, nv: (i, 0)),
        scratch_shapes=[pltpu.VMEM((tm, d), BF16), pltpu.VMEM((tm, d), F32)],
    )
    return pl.pallas_call(
        _moe_kernel,
        grid_spec=grid_spec,
        out_shape=jax.ShapeDtypeStruct((p, d), F32),
        compiler_params=_cparams("parallel", "arbitrary"),
        name="moe_experts",
    )(tile_expert, n_valid, xs, w_gu_b, w_gu_b, w_down_b)


def _moe_tables(route, tm):
    m = route.shape[0]
    na = TOP_K * m
    e = route[:, :TOP_K].astype(I32).reshape(na)
    onehot = (e[:, None] == jnp.arange(N_EXPERTS, dtype=I32)[None, :]).astype(I32)
    within = jnp.sum((jnp.cumsum(onehot, axis=0) - onehot) * onehot, axis=1)
    counts = jnp.sum(onehot, axis=0)
    padded = ((counts + tm - 1) // tm) * tm
    ends = jnp.cumsum(padded)
    starts = ends - padded
    dest = starts[e] + within
    n_slots = (-(-na // tm) + N_EXPERTS) * tm
    src_tok = jnp.zeros((n_slots,), I32).at[dest].set(jnp.arange(na, dtype=I32) // TOP_K,
                                                       mode='promise_in_bounds', unique_indices=True)
    n_tiles = n_slots // tm
    n_valid = (ends[-1] // tm).astype(I32)
    tile_start = jnp.arange(n_tiles, dtype=I32) * tm
    tile_expert = jnp.sum((tile_start[:, None] >= ends[None, :]).astype(I32), axis=1)
    last = jnp.take(tile_expert, jnp.maximum(n_valid - 1, 0))
    tile_expert = jnp.where(jnp.arange(n_tiles) < n_valid, tile_expert, last).astype(I32)
    return src_tok, tile_expert, n_valid.reshape(1), dest


def _split_mod(mod):
    return [mod[:, i * D_MODEL:(i + 1) * D_MODEL] for i in range(mod.shape[1] // D_MODEL)]


def kernel(x_prompt, x_sample, c_prompt, c_sample, state_ret, cache_kv, cache_win, page_table, w_mod, b_mod, norm_g, ret_w_in, ret_gn_g, ret_w_out, kv_w_mod, kv_b_mod, kv_norm_g, kv_w, cmp_pos, cmp_w1, cmp_b1, cmp_w2, nsa_w_in, nsa_w_out, ffn_w_gu, ffn_w_down, moe_router_w, moe_router_b, moe_w_gu, moe_w_down):
    bp, t, d = x_prompt.shape
    bs, ts, _ = x_sample.shape
    mp, ms = bp * t, bs * ts
    n_pool, page = cache_kv.shape[:2]
    past = page_table.shape[1] * page
    g, r, hd = NSA_KV_HEADS, NSA_GROUP, NSA_HEAD_DIM

    ret_w_in_b = ret_w_in[0].astype(BF16)
    ret_w_out_b = ret_w_out[0].astype(BF16)
    kv_w_b = kv_w.astype(BF16)
    kv_wt_b = kv_w.T.astype(BF16)
    nq = NSA_HEADS * hd
    gate_w = nsa_w_in[0][:, nq:].reshape(d, g, r * N_BRANCH)
    gate_w = jnp.pad(gate_w, ((0, 0), (0, 0), (0, LANES - r * N_BRANCH))).reshape(d, g * LANES)
    nsa_w_in_b = jnp.concatenate([nsa_w_in[0][:, :nq], gate_w], axis=1).astype(BF16)
    gate_wt = nsa_w_in[0][:, nq:].T.reshape(g, r * N_BRANCH, d)
    gate_wt = jnp.pad(gate_wt, ((0, 0), (0, GATE_ROWS - r * N_BRANCH), (0, 0))).reshape(g * GATE_ROWS, d)
    nsa_wt_b = jnp.concatenate([nsa_w_in[0][:, :nq].T, gate_wt], axis=0).astype(BF16)
    def pad_heads(w):
        return jnp.pad(w.reshape(d, g, hd), ((0, 0), (0, 0), (0, LANES - hd))).reshape(d, g * LANES)
    kpad_w_b = jnp.concatenate([pad_heads(kv_w[:, 2 * KVW:3 * KVW]), pad_heads(kv_w[:, 4 * KVW:5 * KVW])],
                               axis=1).astype(BF16)
    nsa_w_out_b = nsa_w_out[0].astype(BF16)
    ffn_w_gu_b = ffn_w_gu[0].astype(BF16)
    ffn_w_down_b = ffn_w_down[0].astype(BF16)
    moe_w_gu_b = moe_w_gu[0].astype(BF16)
    moe_w_down_b = moe_w_down[0].astype(BF16)
    rw_pad = jnp.pad(moe_router_w[0], ((0, 0), (0, 128 - N_EXPERTS)))
    rb_pad = jnp.pad(moe_router_b[0], (0, 128 - N_EXPERTS)).reshape(1, 128)
    cw = _compress_weights(cmp_pos, cmp_w1, cmp_b1, cmp_w2)

    c_all = jnp.concatenate([c_prompt, c_sample], axis=0)
    mods = [cond_matmul(c_all, w_mod[layer], b_mod[layer]) for layer in range(w_mod.shape[0])]
    kv_mod = cond_matmul(c_all, kv_w_mod, kv_b_mod)

    tm_p = min(512, t)
    tm_f = min(1024, t)
    groups = {
        'p': dict(x=x_prompt.reshape(mp, d), b=bp, t=t, per_token=False, tm=tm_p, tpb=t // tm_p,
                  tm_f=tm_f, tpb_f=t // tm_f,
                  mod=lambda a: a[:bp]),
        's': dict(x=x_sample.reshape(ms, d), b=bs, t=ts, per_token=True, tm=ms, tpb=1, tm_f=ms, tpb_f=1,
                  mod=lambda a: jnp.repeat(a[bp:], ts, axis=0)),
    }
    out = {}

    for name, gr in groups.items():
        kw = dict(per_token=gr['per_token'], tm=gr['tm'], tpb=gr['tpb'])
        m0 = [gr['mod'](a) for a in _split_mod(mods[0])]
        x = gr['x']
        proj = norm_mod_matmul(x, norm_g[0, 0], m0[0], m0[1], ret_w_in_b, tn=1024, **kw)
        proj = proj.reshape(gr['b'], gr['t'], -1)
        if name == 'p':
            pos = jnp.arange(t)
            s0 = jnp.zeros((bp, RET_HEADS, RET_DK, RET_DV), F32)
            o, s_new = retention(proj, pos, s0, ret_gn_g[0], math.gcd(t, RET_CHUNK))
        else:
            pos = past + jnp.arange(ts)
            proj = jnp.pad(proj, ((0, 0), (0, RET_CHUNK - ts), (0, 0)))
            o, s_new = retention(proj, pos, state_ret[0], ret_gn_g[0], math.gcd(ts, RET_CHUNK))
            o = o[:, :ts]
        out['ret_' + name] = s_new[None]
        x = matmul_norm_residual(o.reshape(-1, d), ret_w_out_b, x, m0[2], norm_g[0, 1], **kw)
        x = ffn_sublayer(x, norm_g[0, 2], m0[3], m0[4], ffn_w_gu_b, ffn_w_down_b, m0[5], norm_g[0, 3],
                         per_token=gr['per_token'], tm=gr['tm_f'], tpb=gr['tpb_f'],
                         tf=_largest_tile(ffn_w_down.shape[1], FFN_HIDDEN_TILE_CAP))
        gr['x1'] = x

    for name, gr in groups.items():
        kw = dict(per_token=gr['per_token'], tm=gr['tm'], tpb=gr['tpb'])
        m1 = [gr['mod'](a) for a in _split_mod(mods[1])]
        kvm = [gr['mod'](a) for a in _split_mod(kv_mod)]
        gr['m1'] = m1
        x = gr['x1']
        b_, t_ = gr['b'], gr['t']
        if name == 'p':
            kvt, wint = norm_mod_matmul_t(x, kv_norm_g, kvm[0], kvm[1], kv_wt_b,
                                          (N_PAGED_ROWS * KVW, (N_KV_ROWS - N_PAGED_ROWS) * KVW), b=b_, tm=gr['tm'])
            out['kv_p'] = kvt.reshape(b_, N_PAGED_ROWS, g, hd, t_).transpose(0, 4, 1, 2, 3)
            n_keep = min(WINDOW, t_)
            out['win_p'] = wint[:, :, t_ - n_keep:].reshape(b_, 2, g, hd, n_keep).transpose(0, 4, 1, 2, 3)
            kpad = norm_mod_matmul(x, kv_norm_g, kvm[0], kvm[1], kpad_w_b, out_dtype=BF16, **kw)
            qt, glt = norm_mod_matmul_t(x, norm_g[1, 0], m1[0], m1[1], nsa_wt_b, (nq, g * GATE_ROWS),
                                        b=b_, tm=gr['tm'])
            cmp = compress_prompt(kvt, cw).reshape(2, b_, -1, g, hd)
            kc = jnp.pad(cmp[0].transpose(0, 2, 1, 3), ((0, 0), (0, 0), (0, 0), (0, LANES - hd)))
            o = nsa_prompt(qt, glt, kc, cmp[1].transpose(0, 2, 3, 1), kpad, kvt, wint)
        else:
            proj = norm_mod_matmul(x, norm_g[1, 0], m1[0], m1[1], nsa_w_in_b, **kw)
            rows = norm_mod_matmul(x, kv_norm_g, kvm[0], kvm[1], kv_w_b, **kw)
            q = (proj[:, :NSA_HEADS * hd] * (hd ** -0.5)).reshape(b_, t_, g, r, hd)
            gl = proj[:, NSA_HEADS * hd:].reshape(b_, t_, g, LANES)[..., :r * N_BRANCH]
            gl = gl.reshape(b_, t_, g, r, N_BRANCH)
            rows3 = rows.reshape(b_, t_, N_KV_ROWS * KVW)
            rows6 = rows.reshape(b_, t_, N_KV_ROWS, g, hd)
            out['kv_s'] = rows6[:, :, :N_PAGED_ROWS]
            cache_t = cache_kv.transpose(0, 2, 3, 4, 1).reshape(n_pool, N_PAGED_ROWS, KVW, page)
            win_t5 = cache_win.transpose(0, 2, 3, 4, 1)
            n_win = cache_win.shape[1]
            new_win_t = rows6[:, :, N_PAGED_ROWS:].transpose(0, 2, 3, 4, 1)
            out['win_s'] = jnp.concatenate([win_t5, new_win_t], axis=-1)[..., t_:].transpose(0, 4, 1, 2, 3)
            new_sub = jnp.pad(rows3[:, :, :2 * KVW], ((0, 0), (0, CMP_STRIDE - t_), (0, 0)))
            new_sub = new_sub.reshape(b_, CMP_STRIDE, 2, KVW).transpose(2, 0, 1, 3)
            cmp = compress_sample(cache_t, page_table, new_sub, cw)
            eye = jnp.eye(g, dtype=F32)
            qrows = q.transpose(0, 2, 3, 1, 4).reshape(b_, g, r * t_, hd)
            q4 = jnp.einsum('bgxd,gj->bgxjd', qrows, eye).reshape(b_, g * r * t_, KVW)
            glr = gl.transpose(0, 2, 3, 1, 4).reshape(b_, g * r * t_, N_BRANCH)
            ocmp, sel = nsa_sample_select(q4, cmp[0], cmp[1], t=t_, past=past)
            new_rows = jnp.pad(rows3, ((0, 0), (0, 128 - t_), (0, 0)))
            o4 = nsa_sample_attend(q4, sel, ocmp, glr, cache_t, page_table, new_rows,
                                   win_t5.reshape(b_, 2, KVW, n_win), t=t_, past=past)
            o4 = o4.reshape(b_, g, r, t_, g, hd)
            o = jnp.einsum('bgrtjd,gj->btgrd', o4, eye).reshape(ms, NSA_HEADS * hd)
        gr['x2'] = matmul_norm_residual(o, nsa_w_out_b, x, m1[2], norm_g[1, 1], **kw)

    hs, routes = [], []
    for name, gr in groups.items():
        kw = dict(per_token=gr['per_token'], tm=gr['tm'], tpb=gr['tpb'])
        m1 = gr['m1']
        h, route = moe_route(gr['x2'], norm_g[1, 2], m1[3], m1[4], rw_pad, rb_pad, **kw)
        hs.append(h)
        routes.append(route)
    h_all = jnp.concatenate(hs, axis=0)
    route_all = jnp.concatenate(routes, axis=0)
    tm_e = 512 if mp >= 4096 else 128
    src_tok, tile_expert, n_valid, dest = _moe_tables(route_all, tm_e)
    take = lambda a, idx: a.at[idx].get(mode='promise_in_bounds')
    xs = take(h_all, src_tok)
    ys = moe_experts(xs, tile_expert, n_valid, moe_w_gu_b, moe_w_down_b, tm=tm_e,
                     tf=_largest_tile(moe_w_down.shape[2], MOE_HIDDEN_TILE_CAP))
    dest2 = dest.reshape(-1, TOP_K)
    for name, gr, lo, hi, route in (('p', groups['p'], 0, mp, routes[0]), ('s', groups['s'], mp, mp + ms, routes[1])):
        kw = dict(per_token=gr['per_token'], tm=gr['tm'], tpb=gr['tpb'])
        out['y_' + name] = moe_combine(take(ys, dest2[lo:hi, 0]), take(ys, dest2[lo:hi, 1]), route,
                                       gr['x2'], gr['m1'][5], norm_g[1, 3], **kw)

    return (out['y_p'].reshape(bp, t, d), out['y_s'].reshape(bs, ts, d),
            out['ret_p'], out['ret_s'], out['kv_p'], out['kv_s'], out['win_p'], out['win_s'])
```

```python
import functools
import math

import numpy as np
import jax
import jax.numpy as jnp
from jax import lax
from jax.experimental import pallas as pl
from jax.experimental.pallas import tpu as pltpu

F32 = jnp.float32
BF16 = jnp.bfloat16
I32 = jnp.int32

D_MODEL = 1024
N_MOD = 6
RET_HEADS = 4
RET_DK = 256
RET_DV = 256
RET_CHUNK = 128
ROPE_BASE = 10000.0
NSA_HEADS = 16
NSA_KV_HEADS = 4
NSA_GROUP = 4
NSA_HEAD_DIM = 64
N_BRANCH = 3
N_KV_ROWS = 6
N_PAGED_ROWS = 4
CMP_BLOCK = 32
CMP_STRIDE = 16
CMP_HIDDEN = 128
SEL_BLOCK = 64
N_SEL = 16
FORCED_SCORE = 1.0e4
WINDOW = 512
Q_BLOCK = 128
N_EXPERTS = 8
TOP_K = 2
EPS = 1e-6

NEG = -1.0e30
KVW = NSA_KV_HEADS * NSA_HEAD_DIM
VMEM_LIMIT_BYTES = 56 * 1024 * 1024
HIGHEST = lax.Precision.HIGHEST


LANES = 128
FFN_HIDDEN_TILE_CAP = 1408
MOE_HIDDEN_TILE_CAP = 896


def _largest_tile(n, cap):
    best = LANES
    for k in range(LANES, cap + 1, LANES):
        if n % k == 0:
            best = k
    return best


def _cparams(*sem):
    return pltpu.CompilerParams(dimension_semantics=sem, vmem_limit_bytes=VMEM_LIMIT_BYTES)


def _sigmoid(x):
    return 1.0 / (1.0 + jnp.exp(-x))


def _silu(x):
    return x * _sigmoid(x)


def _gelu_tanh(x):
    return x * (0.5 * (1.0 + jnp.tanh(math.sqrt(2.0 / math.pi) * (x + 0.044715 * (x * x * x)))))


def _norm_mod(x, g, shift, scale):
    ms = jnp.mean(x * x, axis=-1, keepdims=True)
    return (x * lax.rsqrt(ms + EPS) * g) * (1.0 + scale) + shift


def _rms_residual(x, gate, o, g):
    ms = jnp.mean(o * o, axis=-1, keepdims=True)
    return x + gate * (o * lax.rsqrt(ms + EPS) * g)


def _dot(a, b):
    return jnp.dot(a, b, preferred_element_type=F32)


def _dot_nt(a, b):
    return lax.dot_general(a, b, (((1,), (1,)), ((), ())), preferred_element_type=F32)


def _masked_softmax(s, mask):
    sm = jnp.where(mask, s, NEG)
    m = jnp.max(sm, axis=-1, keepdims=True)
    e = jnp.where(mask, jnp.exp(sm - m), 0.0)
    return e / jnp.maximum(jnp.sum(e, axis=-1, keepdims=True), 1e-30)


def _flash_init(m_ref, l_ref, acc_ref):
    m_ref[...] = jnp.full(m_ref.shape, NEG, F32)
    l_ref[...] = jnp.zeros(l_ref.shape, F32)
    acc_ref[...] = jnp.zeros(acc_ref.shape, F32)


def _flash_update(s, mask, v_b, m_ref, l_ref, acc_ref):
    sm = jnp.where(mask, s, NEG)
    m_old = m_ref[...]
    m_new = jnp.maximum(m_old, jnp.max(sm, axis=-1, keepdims=True))
    alpha = jnp.exp(m_old - m_new)
    p = jnp.where(mask, jnp.exp(sm - m_new), 0.0)
    l_ref[...] = alpha * l_ref[...] + jnp.sum(p, axis=-1, keepdims=True)
    acc_ref[...] = alpha * acc_ref[...] + _dot(p.astype(BF16), v_b)
    m_ref[...] = m_new


def _flash_result(l_ref, acc_ref):
    return acc_ref[...] / jnp.maximum(l_ref[...], 1e-30)


def _select_blocks(imp, cur, n_sel):
    rows, lanes = imp.shape
    blk = lax.broadcasted_iota(I32, (rows, lanes), 1)
    valid = (blk <= cur) & (blk < n_sel)
    forced = (blk == 0) | (blk == cur) | (blk == cur - 1)
    score = jnp.where(valid, jnp.where(forced, FORCED_SCORE, imp), -1.0)
    rank = jnp.zeros((rows, lanes), F32)
    for i in range(n_sel):
        ci = score[:, i:i + 1]
        beats = (ci > score) | ((ci == score) & (blk > i))
        rank = rank + jnp.where(beats, 1.0, 0.0)
    return jnp.where(valid & (rank < float(min(N_SEL, n_sel))), 1.0, 0.0)


def _cond_kernel(c_ref, w_ref, b_ref, o_ref):
    sc = _silu(c_ref[...])
    o_ref[...] = _dot(sc.astype(BF16), w_ref[...].astype(BF16)) + b_ref[...]


def cond_matmul(c, w, b, layer):
    bc, d = c.shape
    n_layers, _, n = w.shape
    tn = 1024
    return pl.pallas_call(
        _cond_kernel,
        grid=(n // tn,),
        in_specs=[pl.BlockSpec((bc, d), lambda j: (0, 0)),
                  pl.BlockSpec((None, d, tn), lambda j: (layer, 0, j)),
                  pl.BlockSpec((None, 1, tn), lambda j: (layer, 0, j))],
        out_specs=pl.BlockSpec((bc, tn), lambda j: (0, j)),
        out_shape=jax.ShapeDtypeStruct((bc, n), F32),
        compiler_params=_cparams("parallel"),
        name="cond_matmul",
    )(c, w, b.reshape(n_layers, 1, n))


def _mod_arg(m, per_token, tm, tiles_per_batch):
    d = m.shape[-1]
    if per_token:
        return m, pl.BlockSpec((tm, d), lambda i, *_: (i, 0))
    return m[:, None, :], pl.BlockSpec((None, 1, d), lambda i, *_: (i // tiles_per_batch, 0, 0))


def _vec_spec(d):
    return pl.BlockSpec((1, d), lambda i, *_: (0, 0))


def _nmm_kernel(x_ref, g_ref, sh_ref, sc_ref, w_ref, o_ref, h_ref):
    @pl.when(pl.program_id(1) == 0)
    def _():
        h_ref[...] = _norm_mod(x_ref[...], g_ref[...], sh_ref[...], sc_ref[...]).astype(BF16)

    o_ref[...] = _dot(h_ref[...], w_ref[...]).astype(o_ref.dtype)


def norm_mod_matmul(x, g, shift, scale, w_b, *, per_token, tm, tpb, tn=None, out_dtype=F32):
    m, d = x.shape
    n = w_b.shape[1]
    tn = n if tn is None else tn
    sh, sh_spec = _mod_arg(shift, per_token, tm, tpb)
    sc, sc_spec = _mod_arg(scale, per_token, tm, tpb)
    return pl.pallas_call(
        _nmm_kernel,
        grid=(m // tm, n // tn),
        in_specs=[pl.BlockSpec((tm, d), lambda i, j: (i, 0)), _vec_spec(d), sh_spec, sc_spec,
                  pl.BlockSpec((d, tn), lambda i, j: (0, j))],
        out_specs=pl.BlockSpec((tm, tn), lambda i, j: (i, j)),
        out_shape=jax.ShapeDtypeStruct((m, n), out_dtype),
        scratch_shapes=[pltpu.VMEM((tm, d), BF16)],
        compiler_params=_cparams("parallel", "arbitrary"),
        name="norm_mod_matmul",
    )(x, g.reshape(1, d), sh, sc, w_b)


def _nmm_t_kernel(x_ref, g_ref, sh_ref, sc_ref, wt_ref, *o_refs):
    h = _norm_mod(x_ref[...], g_ref[...], sh_ref[...], sc_ref[...]).astype(BF16)
    o = _dot_nt(wt_ref[...], h)
    row = 0
    for o_ref in o_refs:
        o_ref[...] = o[row:row + o_ref.shape[0]]
        row += o_ref.shape[0]


def norm_mod_matmul_t(x, g, shift, scale, wt_b, splits, *, b, tm):
    m, d = x.shape
    t = m // b
    tpb = t // tm
    n = wt_b.shape[0]
    sh, sh_spec = _mod_arg(shift, False, tm, tpb)
    sc, sc_spec = _mod_arg(scale, False, tm, tpb)
    return pl.pallas_call(
        _nmm_t_kernel,
        grid=(m // tm,),
        in_specs=[pl.BlockSpec((tm, d), lambda i: (i, 0)), _vec_spec(d), sh_spec, sc_spec,
                  pl.BlockSpec((n, d), lambda i: (0, 0))],
        out_specs=[pl.BlockSpec((None, ni, tm), lambda i: (i // tpb, 0, i % tpb)) for ni in splits],
        out_shape=[jax.ShapeDtypeStruct((b, ni, t), F32) for ni in splits],
        compiler_params=_cparams("parallel"),
        name="norm_mod_matmul_t",
    )(x, g.reshape(1, d), sh, sc, wt_b)


def _mnr_kernel(a_ref, w_ref, x_ref, gate_ref, g_ref, o_ref):
    o = _dot(a_ref[...].astype(BF16), w_ref[...])
    o_ref[...] = _rms_residual(x_ref[...], gate_ref[...], o, g_ref[...])


def matmul_norm_residual(a, w_b, x, gate, g, *, per_token, tm, tpb):
    m, k = a.shape
    d = w_b.shape[1]
    gt, gt_spec = _mod_arg(gate, per_token, tm, tpb)
    return pl.pallas_call(
        _mnr_kernel,
        grid=(m // tm,),
        in_specs=[pl.BlockSpec((tm, k), lambda i: (i, 0)),
                  pl.BlockSpec((k, d), lambda i: (0, 0)),
                  pl.BlockSpec((tm, d), lambda i: (i, 0)), gt_spec, _vec_spec(d)],
        out_specs=pl.BlockSpec((tm, d), lambda i: (i, 0)),
        out_shape=jax.ShapeDtypeStruct((m, d), F32),
        compiler_params=_cparams("parallel"),
        name="matmul_norm_residual",
    )(a, w_b, x, gt, g.reshape(1, d))


def _combine_kernel(y1_ref, y2_ref, r_ref, x_ref, gate_ref, g_ref, o_ref):
    route = r_ref[...]
    y = route[:, TOP_K:TOP_K + 1] * y1_ref[...] + route[:, TOP_K + 1:TOP_K + 2] * y2_ref[...]
    o_ref[...] = _rms_residual(x_ref[...], gate_ref[...], y, g_ref[...])


def moe_combine(y1, y2, route, x, gate, g, *, per_token, tm, tpb):
    m, d = x.shape
    gt, gt_spec = _mod_arg(gate, per_token, tm, tpb)
    row = lambda n: pl.BlockSpec((tm, n), lambda i: (i, 0))
    return pl.pallas_call(
        _combine_kernel,
        grid=(m // tm,),
        in_specs=[row(d), row(d), row(route.shape[1]), row(d), gt_spec, _vec_spec(d)],
        out_specs=row(d),
        out_shape=jax.ShapeDtypeStruct((m, d), F32),
        compiler_params=_cparams("parallel"),
        name="moe_combine",
    )(y1, y2, route, x, gt, g.reshape(1, d))


def _ffn_kernel(x_ref, g2_ref, sh_ref, sc_ref, wg_ref, wu_ref, wd_ref, gate_ref, g3_ref, o_ref,
                h_ref, acc_ref):
    f = pl.program_id(1)

    @pl.when(f == 0)
    def _():
        h_ref[...] = _norm_mod(x_ref[...], g2_ref[...], sh_ref[...], sc_ref[...]).astype(BF16)
        acc_ref[...] = jnp.zeros(acc_ref.shape, F32)

    h = h_ref[...]
    act = _silu(_dot(h, wg_ref[...])) * _dot(h, wu_ref[...])
    acc_ref[...] += _dot(act.astype(BF16), wd_ref[...])

    @pl.when(f == pl.num_programs(1) - 1)
    def _():
        o_ref[...] = _rms_residual(x_ref[...], gate_ref[...], acc_ref[...], g3_ref[...])


def ffn_sublayer(x, g2, shift, scale, w_gu_b, w_down_b, gate, g3, *, per_token, tm, tpb, tf):
    m, d = x.shape
    fdim = w_down_b.shape[0]
    nf = fdim // tf
    sh, sh_spec = _mod_arg(shift, per_token, tm, tpb)
    sc, sc_spec = _mod_arg(scale, per_token, tm, tpb)
    gt, gt_spec = _mod_arg(gate, per_token, tm, tpb)
    return pl.pallas_call(
        _ffn_kernel,
        grid=(m // tm, nf),
        in_specs=[pl.BlockSpec((tm, d), lambda i, f: (i, 0)), _vec_spec(d), sh_spec, sc_spec,
                  pl.BlockSpec((d, tf), lambda i, f: (0, f)),
                  pl.BlockSpec((d, tf), lambda i, f: (0, nf + f)),
                  pl.BlockSpec((tf, d), lambda i, f: (f, 0)),
                  gt_spec, _vec_spec(d)],
        out_specs=pl.BlockSpec((tm, d), lambda i, f: (i, 0)),
        out_shape=jax.ShapeDtypeStruct((m, d), F32),
        scratch_shapes=[pltpu.VMEM((tm, d), BF16), pltpu.VMEM((tm, d), F32)],
        compiler_params=_cparams("parallel", "arbitrary"),
        name="ffn_sublayer",
    )(x, g2.reshape(1, d), sh, sc, w_gu_b, w_gu_b, w_down_b, gt, g3.reshape(1, d))


def _ret_kernel(q_ref, k_ref, v_ref, gt_ref, cos_ref, sin_ref, dm_ref, xi_ref, zt_ref, gc_ref,
                gn_ref, s0_ref, o_ref, s_ref):
    @pl.when(pl.program_id(1) == 0)
    def _():
        s_ref[...] = s0_ref[...]

    rows = q_ref.shape[0]

    def pad(x):
        if rows == RET_CHUNK:
            return x
        return jnp.concatenate([x, jnp.zeros((RET_CHUNK - rows, x.shape[1]), x.dtype)], axis=0)

    cos = pad(cos_ref[...])
    sin = pad(sin_ref[...])
    half = RET_DK // 2

    def rot(x):
        x1 = x[:, :half]
        x2 = x[:, half:]
        return jnp.concatenate([x1 * cos - x2 * sin, x2 * cos + x1 * sin], axis=-1)

    for h in range(RET_HEADS):
        kcols = slice(h * RET_DK, (h + 1) * RET_DK)
        vcols = slice(h * RET_DV, (h + 1) * RET_DV)
        q = rot(pad(q_ref[:, kcols]))
        k = rot(pad(k_ref[:, kcols])) * (RET_DK ** -0.5)
        qb = q.astype(BF16)
        kb = k.astype(BF16)
        vb = pad(v_ref[:, vcols]).astype(BF16)
        state = s_ref[h]
        inner = _dot_nt(qb, kb) * dm_ref[h]
        o = _dot(inner.astype(BF16), vb) + _dot(qb, state.astype(BF16)) * xi_ref[h]
        kz = (k * zt_ref[h]).astype(BF16)
        upd = lax.dot_general(kz, vb, (((0,), (0,)), ((), ())), preferred_element_type=F32)
        s_ref[h] = gc_ref[h] * state + upd
        mu = jnp.mean(o, axis=-1, keepdims=True)
        dev = o - mu
        var = jnp.mean(dev * dev, axis=-1, keepdims=True)
        on = dev * lax.rsqrt(var + EPS) * gn_ref[:, vcols]
        o_ref[:, vcols] = _silu(gt_ref[:, vcols]) * on[:rows]


def _ret_tables(chunk, rows):
    h = RET_HEADS
    log_g = jnp.log(1.0 - jnp.exp(jnp.linspace(math.log(1.0 / 32), math.log(1.0 / 512), h, dtype=F32)))
    i = jnp.arange(chunk, dtype=F32)
    diff = i[:, None] - i[None, :]
    dmat = jnp.where(diff >= 0, jnp.exp(log_g[:, None, None] * jnp.maximum(diff, 0.0)), 0.0)
    xi = jnp.exp(log_g[:, None] * (i + 1.0))
    zeta = jnp.exp(log_g[:, None] * (chunk - 1.0 - i))
    gch = jnp.exp(log_g * chunk)
    pad = rows - chunk
    dmat = jnp.pad(dmat, ((0, 0), (0, pad), (0, pad)))
    xi = jnp.pad(xi, ((0, 0), (0, pad)))[..., None]
    zeta = jnp.pad(zeta, ((0, 0), (0, pad)))[..., None]
    gch = jnp.broadcast_to(gch[:, None, None], (h, 1, RET_DV))
    return dmat, xi, zeta, gch


def _rope_tables(pos, rows):
    half = RET_DK // 2
    inv = ROPE_BASE ** (-jnp.arange(half, dtype=F32) / half)
    ang = pos.astype(F32)[:, None] * inv[None, :]
    pad = rows - pos.shape[0]
    return jnp.pad(jnp.cos(ang), ((0, pad), (0, 0))), jnp.pad(jnp.sin(ang), ((0, pad), (0, 0)))


def retention(proj, pos, s0, gn_g, chunk):
    b, t, _ = proj.shape
    c = min(t, RET_CHUNK)
    n = t // c
    h = RET_HEADS
    dmat, xi, zeta, gch = _ret_tables(chunk, RET_CHUNK)
    cos, sin = _rope_tables(pos, t)
    col = lambda j: pl.BlockSpec((None, c, h * RET_DK), lambda bi, ni: (bi, ni, j))
    const = lambda a: pl.BlockSpec(a.shape, lambda bi, ni: (0,) * a.ndim)
    state_spec = pl.BlockSpec((None, h, RET_DK, RET_DV), lambda bi, ni: (bi, 0, 0, 0))
    gn = gn_g.reshape(1, -1)
    o, s = pl.pallas_call(
        _ret_kernel,
        grid=(b, n),
        in_specs=[col(0), col(1), col(2), col(3),
                  pl.BlockSpec((c, RET_DK // 2), lambda bi, ni: (ni, 0)),
                  pl.BlockSpec((c, RET_DK // 2), lambda bi, ni: (ni, 0)),
                  const(dmat), const(xi), const(zeta), const(gch), const(gn), state_spec],
        out_specs=[pl.BlockSpec((None, c, h * RET_DV), lambda bi, ni: (bi, ni, 0)), state_spec],
        out_shape=[jax.ShapeDtypeStruct((b, t, h * RET_DV), F32),
                   jax.ShapeDtypeStruct((b, h, RET_DK, RET_DV), F32)],
        compiler_params=_cparams("parallel", "arbitrary"),
        name="retention",
    )(proj, proj, proj, proj, cos, sin, dmat, xi, zeta, gch, gn, s0)
    return o, s


def _compress_weights(cmp_pos, cmp_w1, cmp_b1, cmp_w2):
    g, d, hd = NSA_KV_HEADS, NSA_HEAD_DIM, CMP_HIDDEN
    eye = jnp.eye(g, dtype=F32)
    w1 = cmp_w1.reshape(2, CMP_BLOCK, d, hd)
    bd1 = jnp.einsum('ksdh,gj->ksgdjh', w1, eye).reshape(2, CMP_BLOCK, g * d, g * hd).astype(BF16)
    bd2 = jnp.einsum('khd,gj->kghjd', cmp_w2, eye).reshape(2, g * hd, g * d).astype(BF16)
    pos = jnp.tile(cmp_pos[:, :, None, :], (1, 1, g, 1)).reshape(2, CMP_BLOCK, 1, g * d)
    b1 = jnp.tile(cmp_b1[:, None, :], (1, g, 1)).reshape(2, 1, g * hd)
    return bd1, bd2, pos, b1


def _compress_finish(p0, p1, b1, bd2):
    rows = p0.shape[0]
    hid = b1 + p0 + pltpu.roll(p1, rows - 1, 0)
    return _dot(_gelu_tanh(hid).astype(BF16), bd2)


CMP_PAGES_PER_STEP = 8
ATT_PAGES_PER_STEP = 16


def _tokens_to_sublanes(xt_ref, tr_ref):
    for half in range(KVW // LANES):
        tr_ref[half] = xt_ref[half * LANES:(half + 1) * LANES, :].T


def _cmp_prompt_kernel(xt_ref, bd1_ref, bd2_ref, pos_ref, b1_ref, o_ref, tr_ref):
    n_sub = xt_ref.shape[1] // CMP_STRIDE
    hid = bd1_ref.shape[-1]
    _tokens_to_sublanes(xt_ref, tr_ref)
    p0 = jnp.zeros((n_sub, hid), F32)
    p1 = jnp.zeros((n_sub, hid), F32)
    for s in range(CMP_STRIDE):
        xs = jnp.concatenate([tr_ref[half, pl.ds(s, n_sub, stride=CMP_STRIDE), :]
                              for half in range(KVW // LANES)], axis=-1)
        p0 = p0 + _dot((xs + pos_ref[s]).astype(BF16), bd1_ref[s])
        p1 = p1 + _dot((xs + pos_ref[CMP_STRIDE + s]).astype(BF16), bd1_ref[CMP_STRIDE + s])
    o_ref[...] = _compress_finish(p0, p1, b1_ref[...], bd2_ref[...])


def compress_prompt(kvt, cw):
    bd1, bd2, pos, b1 = cw
    b, _, t = kvt.shape
    n_sub = t // CMP_STRIDE
    return pl.pallas_call(
        _cmp_prompt_kernel,
        grid=(b, 2),
        in_specs=[pl.BlockSpec((None, KVW, t), lambda bi, ki: (bi, ki, 0)),
                  pl.BlockSpec((None,) + bd1.shape[1:], lambda bi, ki: (ki, 0, 0, 0)),
                  pl.BlockSpec((None,) + bd2.shape[1:], lambda bi, ki: (ki, 0, 0)),
                  pl.BlockSpec((None,) + pos.shape[1:], lambda bi, ki: (ki, 0, 0, 0)),
                  pl.BlockSpec((None,) + b1.shape[1:], lambda bi, ki: (ki, 0, 0))],
        out_specs=pl.BlockSpec((None, None, n_sub, KVW), lambda bi, ki: (ki, bi, 0, 0)),
        out_shape=jax.ShapeDtypeStruct((2, b, n_sub, KVW), F32),
        scratch_shapes=[pltpu.VMEM((KVW // LANES, t, LANES), F32)],
        compiler_params=_cparams("parallel", "parallel"),
        name="compress_prompt",
    )(kvt, bd1, bd2, pos, b1)


def _cmp_sample_kernel(pt_ref, *refs):
    page_refs = refs[:CMP_PAGES_PER_STEP]
    new_ref, bd1_ref, bd2_ref, pos_ref, b1_ref, o_ref, sub_ref, tr_ref = refs[CMP_PAGES_PER_STEP:]
    p = pl.program_id(1)
    page = page_refs[0].shape[2]
    per_page = page // CMP_STRIDE
    n_cmp = o_ref.shape[1]
    for j, page_ref in enumerate(page_refs):
        row0 = pl.multiple_of((p * CMP_PAGES_PER_STEP + j) * per_page, per_page)
        for kind in range(2):
            _tokens_to_sublanes(page_ref.at[kind], tr_ref.at[kind])
            for s in range(CMP_STRIDE):
                for half in range(KVW // LANES):
                    piece = tr_ref[kind, half, pl.ds(s, per_page, stride=CMP_STRIDE), :]
                    sub_ref[kind, s, pl.ds(row0, per_page), half * LANES:(half + 1) * LANES] = piece

    @pl.when(p == pl.num_programs(1) - 1)
    def _():
        n_rows = sub_ref.shape[2]
        tail = n_rows - n_cmp
        hid = bd1_ref.shape[-1]
        for kind in range(2):
            p0 = jnp.zeros((n_rows, hid), F32)
            p1 = jnp.zeros((n_rows, hid), F32)
            for s in range(CMP_STRIDE):
                sub_ref[kind, s, pl.ds(n_cmp, tail), :] = jnp.broadcast_to(new_ref[kind, s:s + 1, :], (tail, KVW))
                xs = sub_ref[kind, s]
                p0 = p0 + _dot((xs + pos_ref[kind, s]).astype(BF16), bd1_ref[kind, s])
                p1 = p1 + _dot((xs + pos_ref[kind, CMP_STRIDE + s]).astype(BF16), bd1_ref[kind, CMP_STRIDE + s])
            o_ref[kind] = _compress_finish(p0, p1, b1_ref[kind], bd2_ref[kind])[:n_cmp]


def _page_specs(n_per_step, kind_block, page):
    def spec(j):
        return pl.BlockSpec((None, 2, KVW, page),
                            lambda bi, pi, pt: (pt[bi, pi * n_per_step + j], kind_block, 0, 0))
    return [spec(j) for j in range(n_per_step)]


def compress_sample(cache_t, page_table, new_sub, cw):
    bd1, bd2, pos, b1 = cw
    b, n_pages = page_table.shape
    page = cache_t.shape[3]
    n_cmp = n_pages * page // CMP_STRIDE
    nps = CMP_PAGES_PER_STEP
    const = lambda a: pl.BlockSpec(a.shape, lambda bi, pi, pt: (0,) * a.ndim, pipeline_mode=pl.Buffered(1))
    grid_spec = pltpu.PrefetchScalarGridSpec(
        num_scalar_prefetch=1,
        grid=(b, n_pages // nps),
        in_specs=_page_specs(nps, 0, page) + [
            pl.BlockSpec((2, None, CMP_STRIDE, KVW), lambda bi, pi, pt: (0, bi, 0, 0)),
            const(bd1), const(bd2), const(pos), const(b1)],
        out_specs=pl.BlockSpec((2, None, n_cmp, KVW), lambda bi, pi, pt: (0, bi, 0, 0)),
        scratch_shapes=[pltpu.VMEM((2, CMP_STRIDE, n_cmp + 8, KVW), F32),
                        pltpu.VMEM((2, KVW // LANES, page, LANES), F32)],
    )
    return pl.pallas_call(
        _cmp_sample_kernel,
        grid_spec=grid_spec,
        out_shape=jax.ShapeDtypeStruct((2, b, n_cmp, KVW), F32),
        compiler_params=_cparams("parallel", "arbitrary"),
        name="compress_sample",
    )(page_table, *([cache_t] * nps), new_sub, bd1, bd2, pos, b1)


def _cover_matrix(n_cmp, n_sel, rows, cols):
    i = np.arange(n_cmp)[:, None]
    j = np.arange(n_sel)[None, :]
    cover = (i * CMP_STRIDE < (j + 1) * SEL_BLOCK) & (i * CMP_STRIDE + CMP_BLOCK > j * SEL_BLOCK)
    out = np.zeros((rows, cols), np.float32)
    out[:n_cmp, :n_sel] = cover
    return jnp.asarray(out)


SEL_CHUNK = 256
SEL_CLASS = 512


def _select_blocks_t(imp_t, cur, n_sel):
    nb, nq = imp_t.shape
    blk = lax.broadcasted_iota(I32, (nb, nq), 0)
    valid = (blk <= cur) & (blk < n_sel)
    forced = (blk == 0) | (blk == cur) | (blk == cur - 1)
    score = jnp.where(valid, jnp.where(forced, FORCED_SCORE, imp_t), -1.0)
    rank = jnp.zeros((nb, nq), F32)
    for i in range(n_sel):
        ci = score[i:i + 1, :]
        beats = (ci > score) | ((ci == score) & (blk > i))
        rank = rank + jnp.where(beats, 1.0, 0.0)
    return jnp.where(valid & (rank < float(min(N_SEL, n_sel))), 1.0, 0.0)


def _nsa_prompt_kernel(qt_ref, glt_ref, kc_ref, vct_ref, covt_ref, kblk_ref, wb_ref, ks_ref, vst_ref, kw_ref, vwt_ref,
                       o_ref, s_ref, osel_ref, *, n_cmp, n_sel):
    tq = qt_ref.shape[1]
    r, d = NSA_GROUP, NSA_HEAD_DIM
    cols = r * tq
    t = ks_ref.shape[0]
    q0 = pl.program_id(2) * tq
    qt = qt_ref[...] * (d ** -0.5)
    qcat = jnp.concatenate([qt[h * d:(h + 1) * d, :] for h in range(r)], axis=-1)
    qb = jnp.concatenate([qcat, jnp.zeros_like(qcat)], axis=0).astype(BF16)
    gl = _sigmoid(glt_ref[...])

    def gate(branch):
        return jnp.concatenate([jnp.broadcast_to(gl[h * N_BRANCH + branch:h * N_BRANCH + branch + 1, :], (d, tq))
                                for h in range(r)], axis=0)

    def stack(o_t):
        return jnp.concatenate([o_t[:, h * tq:(h + 1) * tq] for h in range(r)], axis=0)

    def tile(x):
        return jnp.concatenate([x] * r, axis=-1)

    ones_rows = 16

    def with_ones(vt):
        return jnp.concatenate([vt.astype(BF16), jnp.ones((ones_rows, vt.shape[1]), BF16)], axis=0)

    qpos = q0 + lax.broadcasted_iota(I32, (1, tq), 1)

    span = min(WINDOW + tq, t)
    w0 = pl.multiple_of(jnp.maximum(q0 + tq - span, 0), tq)
    sm = _dot(kw_ref[pl.ds(w0, span), :], qb) + tile(wb_ref[(q0 - w0) // tq])
    p = jnp.exp(sm - jnp.max(sm, axis=0, keepdims=True))
    o_win_t = _dot(with_ones(vwt_ref[:, pl.ds(w0, span)]), p.astype(BF16))
    out = gate(2) * stack(o_win_t[:d] / jnp.maximum(o_win_t[d:d + 1], 1e-30))

    ncp = kc_ref.shape[0]
    nn = lax.broadcasted_iota(I32, (ncp, tq), 0)
    ok = tile(jnp.where((nn * CMP_STRIDE + (CMP_BLOCK - 1) <= qpos) & (nn < n_cmp), 1.0, 0.0))
    sm = _dot(kc_ref[...].astype(BF16), qb) + (ok - 1.0) * (-NEG)
    e = jnp.exp(sm - jnp.max(sm, axis=0, keepdims=True)) * ok
    p_cmp = e / jnp.maximum(jnp.sum(e, axis=0, keepdims=True), 1e-30)
    out = out + gate(0) * stack(_dot(vct_ref[...].astype(BF16), p_cmp.astype(BF16)))

    psum_t = p_cmp[:, :tq]
    for h in range(1, r):
        psum_t = psum_t + p_cmp[:, h * tq:(h + 1) * tq]
    imp_t = jnp.dot(covt_ref[...], psum_t, precision=HIGHEST, preferred_element_type=F32)
    sel_t = _select_blocks_t(imp_t, qpos >> 6, n_sel)
    nb = sel_t.shape[0]
    q_sel = jnp.concatenate([qcat, tile((sel_t - 1.0) * (-NEG)), jnp.zeros((LANES - d - nb, cols), F32)],
                            axis=0).astype(BF16)

    groups = SEL_CHUNK // 8
    sel_class = SEL_CLASS if t % SEL_CLASS == 0 else t

    def sel_branch(n_keys):
        m_run = jnp.full((8, cols), NEG, F32)
        for k0 in range(0, n_keys, SEL_CHUNK):
            k_aug = ks_ref[k0:k0 + SEL_CHUNK, :] + kblk_ref[k0:k0 + SEL_CHUNK, :]
            sm = _dot(k_aug, q_sel)
            if k0 + SEL_CHUNK > n_keys - sel_class:
                keypos = k0 + lax.broadcasted_iota(I32, (SEL_CHUNK, tq), 0)
                sm = sm + tile(jnp.where(keypos <= qpos, 0.0, NEG))
            s_ref[k0:k0 + SEL_CHUNK, :] = sm
            m_run = jnp.maximum(m_run, jnp.max(sm.reshape(groups, 8, cols), axis=0))
        m_sel = jnp.max(m_run, axis=0, keepdims=True)
        acc_t = jnp.zeros((d + ones_rows, cols), F32)
        for k0 in range(0, n_keys, SEL_CHUNK):
            p = jnp.exp(s_ref[k0:k0 + SEL_CHUNK, :] - m_sel)
            acc_t = acc_t + _dot(with_ones(vst_ref[:, k0:k0 + SEL_CHUNK]), p.astype(BF16))
        osel_ref[...] = acc_t[:d] / jnp.maximum(acc_t[d:d + 1], 1e-30)

    cls_id = pl.program_id(2) // (sel_class // tq)
    for cls in range(t // sel_class):
        pl.when(cls_id == cls)(functools.partial(sel_branch, (cls + 1) * sel_class))
    o_ref[...] = (out + gate(1) * stack(osel_ref[...])).T


GATE_ROWS = 16


def nsa_prompt(qt, glt, kc, vct, kpad, kvt, wint):
    b, _, t = qt.shape
    g, r, d = NSA_KV_HEADS, NSA_GROUP, NSA_HEAD_DIM
    tq = Q_BLOCK
    nt = t // tq
    ncp = kc.shape[2]
    n_cmp = t // CMP_STRIDE - 1
    n_sel = -(-t // SEL_BLOCK)
    nb = -(-n_sel // 8) * 8
    covt = _cover_matrix(n_cmp, n_sel, ncp, nb).T
    assert d + nb <= LANES, "block one-hot must fit in the keys' padding lanes"
    kblk = np.zeros((t, LANES), np.float32)
    kblk[np.arange(t), d + np.arange(t) // SEL_BLOCK] = 1.0
    kblk = jnp.asarray(kblk, BF16)
    span = min(WINDOW + tq, t)
    i = np.arange(span)[None, :, None]
    j = np.arange(tq)[None, None, :]
    rel = i - j - np.arange(0, span - tq + 1, tq)[:, None, None]
    wbias = jnp.asarray(np.where((rel <= 0) & (rel > -WINDOW), 0.0, NEG), F32)
    per_bg = lambda shape: pl.BlockSpec((None, None) + shape, lambda bi, gi, qi: (bi, gi, 0, 0))
    const = lambda a: pl.BlockSpec(a.shape, lambda bi, gi, qi: (0,) * a.ndim)
    return pl.pallas_call(
        functools.partial(_nsa_prompt_kernel, n_cmp=n_cmp, n_sel=n_sel),
        grid=(b, g, nt),
        in_specs=[pl.BlockSpec((None, r * d, tq), lambda bi, gi, qi: (bi, gi, qi)),
                  pl.BlockSpec((None, GATE_ROWS, tq), lambda bi, gi, qi: (bi, gi, qi)),
                  per_bg((ncp, LANES)), per_bg((d, ncp)), const(covt), const(kblk), const(wbias),
                  pl.BlockSpec((t, LANES), lambda bi, gi, qi: (bi, gi)),
                  pl.BlockSpec((None, d, t), lambda bi, gi, qi: (bi, 3 * g + gi, 0)),
                  pl.BlockSpec((t, LANES), lambda bi, gi, qi: (bi, g + gi)),
                  pl.BlockSpec((None, d, t), lambda bi, gi, qi: (bi, g + gi, 0))],
        out_specs=pl.BlockSpec((tq, r * d), lambda bi, gi, qi: (bi * nt + qi, gi)),
        out_shape=jax.ShapeDtypeStruct((b * t, NSA_HEADS * d), F32),
        scratch_shapes=[pltpu.VMEM((t, r * tq), F32), pltpu.VMEM((d, r * tq), F32)],
        compiler_params=_cparams("parallel", "parallel", "arbitrary"),
        name="nsa_prompt",
    )(qt, glt, kc, vct, covt, kblk, wbias, kpad, kvt, kpad, wint)


def _nsa_sample_select_kernel(q_ref, kc_ref, vc_ref, cov_ref, oc_ref, sel_ref, *, t, past, n_cmp, n_sel):
    rows = q_ref.shape[0]
    g, r = NSA_KV_HEADS, NSA_GROUP
    qb = q_ref[...].astype(BF16)
    qpos_r = past + (lax.broadcasted_iota(I32, (rows, 1), 0) & (t - 1))
    s = _dot_nt(qb, kc_ref[...].astype(BF16))
    nn = lax.broadcasted_iota(I32, s.shape, 1)
    p_cmp = _masked_softmax(s, (nn * CMP_STRIDE + (CMP_BLOCK - 1) <= qpos_r) & (nn < n_cmp))
    oc_ref[...] = _dot(p_cmp.astype(BF16), vc_ref[...].astype(BF16))
    psum = jnp.sum(p_cmp.reshape(g, r, t, s.shape[1]), axis=1).reshape(g * t, s.shape[1])
    imp = jnp.dot(psum, cov_ref[...], precision=HIGHEST, preferred_element_type=F32)
    qpos_gt = past + (lax.broadcasted_iota(I32, (g * t, 1), 0) & (t - 1))
    sel = _select_blocks(imp, qpos_gt >> 6, n_sel)
    lanes = sel.shape[1]
    sel_ref[...] = jnp.broadcast_to(sel.reshape(g, 1, t, lanes), (g, r, t, lanes)).reshape(rows, lanes)


def nsa_sample_select(q4, cmp, *, t, past):
    b, rows, _ = q4.shape
    n_cmp = cmp.shape[2]
    n_sel = -(-(past + t) // SEL_BLOCK)
    lanes = -(-n_sel // 128) * 128
    cover = _cover_matrix(n_cmp, n_sel, n_cmp, lanes)
    return pl.pallas_call(
        functools.partial(_nsa_sample_select_kernel, t=t, past=past, n_cmp=n_cmp, n_sel=n_sel),
        grid=(b,),
        in_specs=[pl.BlockSpec((None, rows, KVW), lambda bi: (bi, 0, 0)),
                  pl.BlockSpec((None, None, n_cmp, KVW), lambda bi: (0, bi, 0, 0)),
                  pl.BlockSpec((None, None, n_cmp, KVW), lambda bi: (1, bi, 0, 0)),
                  pl.BlockSpec(cover.shape, lambda bi: (0, 0))],
        out_specs=[pl.BlockSpec((None, rows, KVW), lambda bi: (bi, 0, 0)),
                   pl.BlockSpec((None, rows, lanes), lambda bi: (bi, 0, 0))],
        out_shape=[jax.ShapeDtypeStruct((b, rows, KVW), F32),
                   jax.ShapeDtypeStruct((b, rows, lanes), F32)],
        compiler_params=_cparams("parallel"),
        name="nsa_sample_select",
    )(q4, cmp, cmp, cover)


def _nsa_sample_attend_kernel(pt_ref, q_ref, sel_ref, oc_ref, gl_ref, *refs, t, past):
    page_refs = refs[:ATT_PAGES_PER_STEP]
    new_ref, win_ref, o_ref, m_ref, l_ref, acc_ref = refs[ATT_PAGES_PER_STEP:]
    rows = q_ref.shape[0]
    p = pl.program_id(1)
    page = page_refs[0].shape[2]
    width = ATT_PAGES_PER_STEP * page
    qb = q_ref[...].astype(BF16)
    sel = sel_ref[...]
    nblk = sel.shape[1]

    @pl.when(p == 0)
    def _():
        _flash_init(m_ref, l_ref, acc_ref)

    sc = jnp.concatenate([_dot(qb, pr[0].astype(BF16)) for pr in page_refs], axis=-1)
    jj = lax.broadcasted_iota(I32, (nblk, width), 0)
    tt = lax.broadcasted_iota(I32, (nblk, width), 1)
    expand = jnp.where(jj == ((p * width + tt) >> 6), 1.0, 0.0).astype(BF16)
    mask = _dot(sel.astype(BF16), expand) > 0.5
    sm = jnp.where(mask, sc, NEG)
    m_old = m_ref[...]
    m_new = jnp.maximum(m_old, jnp.max(sm, axis=-1, keepdims=True))
    alpha = jnp.exp(m_old - m_new)
    pr_b = jnp.where(mask, jnp.exp(sm - m_new), 0.0)
    l_ref[...] = alpha * l_ref[...] + jnp.sum(pr_b, axis=-1, keepdims=True)
    pr_b = pr_b.astype(BF16)
    pv = _dot_nt(pr_b[:, :page], page_refs[0][1].astype(BF16))
    for j in range(1, ATT_PAGES_PER_STEP):
        pv = pv + _dot_nt(pr_b[:, j * page:(j + 1) * page], page_refs[j][1].astype(BF16))
    acc_ref[...] = alpha * acc_ref[...] + pv
    m_ref[...] = m_new

    @pl.when(p == pl.num_programs(1) - 1)
    def _():
        qpos_r = past + (lax.broadcasted_iota(I32, (rows, 1), 0) & (t - 1))
        tp = LANES
        newr = jnp.concatenate([new_ref[...], jnp.zeros((tp - t, new_ref.shape[1]), F32)], axis=0)
        jn = lax.broadcasted_iota(I32, (rows, tp), 1)
        new_ok = (jn < t) & (past + jn <= qpos_r)
        blk_new = past // SEL_BLOCK
        sel_new = sel[:, blk_new:blk_new + 1] > 0.5
        sc_n = _dot_nt(qb, newr[:, 2 * KVW:3 * KVW].astype(BF16))
        _flash_update(sc_n, new_ok & sel_new, newr[:, 3 * KVW:4 * KVW].astype(BF16), m_ref, l_ref, acc_ref)
        o_sel = _flash_result(l_ref, acc_ref)

        n_win = win_ref.shape[2]
        s_a = _dot(qb, win_ref[0].astype(BF16))
        s_b = _dot_nt(qb, newr[:, 4 * KVW:5 * KVW].astype(BF16))
        kp_a = (past - n_win) + lax.broadcasted_iota(I32, (rows, n_win), 1)
        ok_a = (kp_a <= qpos_r) & (kp_a > qpos_r - WINDOW) & (kp_a >= 0)
        ok_b = new_ok & (past + jn > qpos_r - WINDOW)
        sm_a = jnp.where(ok_a, s_a, NEG)
        sm_b = jnp.where(ok_b, s_b, NEG)
        mx = jnp.maximum(jnp.max(sm_a, axis=-1, keepdims=True), jnp.max(sm_b, axis=-1, keepdims=True))
        e_a = jnp.where(ok_a, jnp.exp(sm_a - mx), 0.0)
        e_b = jnp.where(ok_b, jnp.exp(sm_b - mx), 0.0)
        den = jnp.maximum(jnp.sum(e_a, axis=-1, keepdims=True) + jnp.sum(e_b, axis=-1, keepdims=True), 1e-30)
        o_win = (_dot_nt((e_a / den).astype(BF16), win_ref[1].astype(BF16))
                 + _dot((e_b / den).astype(BF16), newr[:, 5 * KVW:6 * KVW].astype(BF16)))

        gates = _sigmoid(gl_ref[...])
        o_ref[...] = gates[:, 0:1] * oc_ref[...] + gates[:, 1:2] * o_sel + gates[:, 2:3] * o_win


def nsa_sample_attend(q4, sel, ocmp, gl, cache_t, page_table, new_rows, win_t, *, t, past):
    b, rows, _ = q4.shape
    n_pages = page_table.shape[1]
    page = cache_t.shape[3]
    lanes = sel.shape[2]
    tp = new_rows.shape[1]
    n_win = win_t.shape[3]
    nps = ATT_PAGES_PER_STEP
    per_b = lambda shape: pl.BlockSpec((None,) + shape, lambda bi, pi, pt: (bi,) + (0,) * len(shape))
    grid_spec = pltpu.PrefetchScalarGridSpec(
        num_scalar_prefetch=1,
        grid=(b, n_pages // nps),
        in_specs=[per_b((rows, KVW)), per_b((rows, lanes)), per_b((rows, KVW)), per_b((rows, N_BRANCH))]
        + _page_specs(nps, 1, page)
        + [per_b((tp, N_KV_ROWS * KVW)), per_b((2, KVW, n_win))],
        out_specs=per_b((rows, KVW)),
        scratch_shapes=[pltpu.VMEM((rows, 1), F32), pltpu.VMEM((rows, 1), F32), pltpu.VMEM((rows, KVW), F32)],
    )
    return pl.pallas_call(
        functools.partial(_nsa_sample_attend_kernel, t=t, past=past),
        grid_spec=grid_spec,
        out_shape=jax.ShapeDtypeStruct((b, rows, KVW), F32),
        compiler_params=_cparams("parallel", "arbitrary"),
        name="nsa_sample_attend",
    )(page_table, q4, sel, ocmp, gl, *([cache_t] * nps), new_rows, win_t)


def _route_kernel(x_ref, g_ref, sh_ref, sc_ref, rw_ref, rb_ref, h_ref, r_ref):
    h = _norm_mod(x_ref[...], g_ref[...], sh_ref[...], sc_ref[...])
    h_ref[...] = h
    logits = jnp.dot(h, rw_ref[...], precision=HIGHEST, preferred_element_type=F32) + rb_ref[...]
    lane = lax.broadcasted_iota(I32, logits.shape, 1)
    lane_f = lane.astype(F32)
    lg = jnp.where(lane < N_EXPERTS, logits, NEG)
    v1 = jnp.max(lg, axis=-1, keepdims=True)
    i1 = jnp.min(jnp.where(lg == v1, lane_f, 128.0), axis=-1, keepdims=True)
    lg2 = jnp.where(lane_f == i1, NEG, lg)
    v2 = jnp.max(lg2, axis=-1, keepdims=True)
    i2 = jnp.min(jnp.where(lg2 == v2, lane_f, 128.0), axis=-1, keepdims=True)
    e = jnp.exp(v2 - v1)
    w1 = 1.0 / (1.0 + e)
    w2 = e / (1.0 + e)
    r_ref[...] = jnp.where(lane == 0, i1, jnp.where(lane == 1, i2, jnp.where(lane == 2, w1, jnp.where(lane == 3, w2, 0.0))))


def moe_route(x, g, shift, scale, rw_pad, rb_pad, *, per_token, tm, tpb):
    m, d = x.shape
    sh, sh_spec = _mod_arg(shift, per_token, tm, tpb)
    sc, sc_spec = _mod_arg(scale, per_token, tm, tpb)
    return pl.pallas_call(
        _route_kernel,
        grid=(m // tm,),
        in_specs=[pl.BlockSpec((tm, d), lambda i: (i, 0)), _vec_spec(d), sh_spec, sc_spec,
                  pl.BlockSpec((d, 128), lambda i: (0, 0)), _vec_spec(128)],
        out_specs=[pl.BlockSpec((tm, d), lambda i: (i, 0)), pl.BlockSpec((tm, 128), lambda i: (i, 0))],
        out_shape=[jax.ShapeDtypeStruct((m, d), F32), jax.ShapeDtypeStruct((m, 128), F32)],
        compiler_params=_cparams("parallel"),
        name="moe_route",
    )(x, g.reshape(1, d), sh, sc, rw_pad, rb_pad)


def _moe_kernel(te_ref, nv_ref, x_ref, wg_ref, wu_ref, wd_ref, o_ref, xb_ref, acc_ref):
    i = pl.program_id(0)
    f = pl.program_id(1)

    @pl.when(f == 0)
    def _():
        acc_ref[...] = jnp.zeros(acc_ref.shape, F32)
        xb_ref[...] = x_ref[...].astype(BF16)

    @pl.when(i < nv_ref[0])
    def _():
        x = xb_ref[...]
        act = _silu(_dot(x, wg_ref[...])) * _dot(x, wu_ref[...])
        acc_ref[...] += _dot(act.astype(BF16), wd_ref[...])

    @pl.when(f == pl.num_programs(1) - 1)
    def _():
        o_ref[...] = acc_ref[...]


def moe_experts(xs, tile_expert, n_valid, w_gu_b, w_down_b, *, tm, tf):
    p, d = xs.shape
    edim = w_down_b.shape[1]
    nf = edim // tf
    n_tiles = p // tm

    def wmap(off):
        def index(i, f, te, nv):
            ok = i < nv[0]
            return (te[i], 0, off + jnp.where(ok, f, nf - 1))
        return index

    def dmap(i, f, te, nv):
        return (te[i], jnp.where(i < nv[0], f, nf - 1), 0)

    grid_spec = pltpu.PrefetchScalarGridSpec(
        num_scalar_prefetch=2,
        grid=(n_tiles, nf),
        in_specs=[pl.BlockSpec((tm, d), lambda i, f, te, nv: (i, 0)),
                  pl.BlockSpec((None, d, tf), wmap(0)),
                  pl.BlockSpec((None, d, tf), wmap(nf)),
                  pl.BlockSpec((None, tf, d), dmap)],
        out_specs=pl.BlockSpec((tm, d), lambda i, f, te, nv: (i, 0)),
        scratch_shapes=[pltpu.VMEM((tm, d), BF16), pltpu.VMEM((tm, d), F32)],
    )
    return pl.pallas_call(
        _moe_kernel,
        grid_spec=grid_spec,
        out_shape=jax.ShapeDtypeStruct((p, d), F32),
        compiler_params=_cparams("parallel", "arbitrary"),
        name="moe_experts",
    )(tile_expert, n_valid, xs, w_gu_b, w_gu_b, w_down_b)


def _moe_tables(route, tm):
    m = route.shape[0]
    na = TOP_K * m
    e = route[:, :TOP_K].astype(I32).reshape(na)
    onehot = (e[:, None] == jnp.arange(N_EXPERTS, dtype=I32)[None, :]).astype(I32)
    within = jnp.sum((jnp.cumsum(onehot, axis=0) - onehot) * onehot, axis=1)
    counts = jnp.sum(onehot, axis=0)
    padded = ((counts + tm - 1) // tm) * tm
    ends = jnp.cumsum(padded)
    starts = ends - padded
    dest = starts[e] + within
    n_slots = (-(-na // tm) + N_EXPERTS) * tm
    src_tok = jnp.zeros((n_slots,), I32).at[dest].set(jnp.arange(na, dtype=I32) // TOP_K,
                                                       mode='promise_in_bounds', unique_indices=True)
    n_tiles = n_slots // tm
    n_valid = (ends[-1] // tm).astype(I32)
    tile_start = jnp.arange(n_tiles, dtype=I32) * tm
    tile_expert = jnp.sum((tile_start[:, None] >= ends[None, :]).astype(I32), axis=1)
    last = jnp.take(tile_expert, jnp.maximum(n_valid - 1, 0))
    tile_expert = jnp.where(jnp.arange(n_tiles) < n_valid, tile_expert, last).astype(I32)
    return src_tok, tile_expert, n_valid.reshape(1), dest


def _split_mod(mod):
    return [mod[:, i * D_MODEL:(i + 1) * D_MODEL] for i in range(mod.shape[1] // D_MODEL)]


def kernel(x_prompt, x_sample, c_prompt, c_sample, state_ret, cache_kv, cache_win, page_table, w_mod, b_mod, norm_g, ret_w_in, ret_gn_g, ret_w_out, kv_w_mod, kv_b_mod, kv_norm_g, kv_w, cmp_pos, cmp_w1, cmp_b1, cmp_w2, nsa_w_in, nsa_w_out, ffn_w_gu, ffn_w_down, moe_router_w, moe_router_b, moe_w_gu, moe_w_down):
    bp, t, d = x_prompt.shape
    bs, ts, _ = x_sample.shape
    mp, ms = bp * t, bs * ts
    n_pool, page = cache_kv.shape[:2]
    past = page_table.shape[1] * page
    g, r, hd = NSA_KV_HEADS, NSA_GROUP, NSA_HEAD_DIM

    ret_w_in_b = ret_w_in[0].astype(BF16)
    ret_w_out_b = ret_w_out[0].astype(BF16)
    kv_w_b = kv_w.astype(BF16)
    kv_wt_b = kv_w.T.astype(BF16)
    nq = NSA_HEADS * hd
    gate_w = nsa_w_in[0][:, nq:].reshape(d, g, r * N_BRANCH)
    gate_w = jnp.pad(gate_w, ((0, 0), (0, 0), (0, LANES - r * N_BRANCH))).reshape(d, g * LANES)
    nsa_w_in_b = jnp.concatenate([nsa_w_in[0][:, :nq], gate_w], axis=1).astype(BF16)
    gate_wt = nsa_w_in[0][:, nq:].T.reshape(g, r * N_BRANCH, d)
    gate_wt = jnp.pad(gate_wt, ((0, 0), (0, GATE_ROWS - r * N_BRANCH), (0, 0))).reshape(g * GATE_ROWS, d)
    nsa_wt_b = jnp.concatenate([nsa_w_in[0][:, :nq].T, gate_wt], axis=0).astype(BF16)
    def pad_heads(w):
        return jnp.pad(w.reshape(d, g, hd), ((0, 0), (0, 0), (0, LANES - hd))).reshape(d, g * LANES)
    kpad_w_b = jnp.concatenate([pad_heads(kv_w[:, 2 * KVW:3 * KVW]), pad_heads(kv_w[:, 4 * KVW:5 * KVW])],
                               axis=1).astype(BF16)
    nsa_w_out_b = nsa_w_out[0].astype(BF16)
    ffn_w_gu_b = ffn_w_gu[0].astype(BF16)
    ffn_w_down_b = ffn_w_down[0].astype(BF16)
    moe_w_gu_b = moe_w_gu[0].astype(BF16)
    moe_w_down_b = moe_w_down[0].astype(BF16)
    rw_pad = jnp.pad(moe_router_w[0], ((0, 0), (0, 128 - N_EXPERTS)))
    rb_pad = jnp.pad(moe_router_b[0], (0, 128 - N_EXPERTS)).reshape(1, 128)
    cw = _compress_weights(cmp_pos, cmp_w1, cmp_b1, cmp_w2)

    c_all = jnp.concatenate([c_prompt, c_sample], axis=0)
    mods = [cond_matmul(c_all, w_mod, b_mod, layer) for layer in range(w_mod.shape[0])]
    kv_mod = cond_matmul(c_all, kv_w_mod[None], kv_b_mod[None], 0)

    tm_p = min(512, t)
    tm_f = min(1024, t)
    groups = {
        'p': dict(x=x_prompt.reshape(mp, d), b=bp, t=t, per_token=False, tm=tm_p, tpb=t // tm_p,
                  tm_f=tm_f, tpb_f=t // tm_f,
                  mod=lambda a: a[:bp]),
        's': dict(x=x_sample.reshape(ms, d), b=bs, t=ts, per_token=True, tm=ms, tpb=1, tm_f=ms, tpb_f=1,
                  mod=lambda a: jnp.repeat(a[bp:], ts, axis=0)),
    }
    out = {}

    for name, gr in groups.items():
        kw = dict(per_token=gr['per_token'], tm=gr['tm'], tpb=gr['tpb'])
        m0 = [gr['mod'](a) for a in _split_mod(mods[0])]
        x = gr['x']
        proj = norm_mod_matmul(x, norm_g[0, 0], m0[0], m0[1], ret_w_in_b, tn=1024, **kw)
        proj = proj.reshape(gr['b'], gr['t'], -1)
        if name == 'p':
            pos = jnp.arange(t)
            s0 = jnp.zeros((bp, RET_HEADS, RET_DK, RET_DV), F32)
            o, s_new = retention(proj, pos, s0, ret_gn_g[0], math.gcd(t, RET_CHUNK))
        else:
            pos = past + jnp.arange(ts)
            o, s_new = retention(proj, pos, state_ret[0], ret_gn_g[0], math.gcd(ts, RET_CHUNK))
        out['ret_' + name] = s_new[None]
        x = matmul_norm_residual(o.reshape(-1, d), ret_w_out_b, x, m0[2], norm_g[0, 1], **kw)
        x = ffn_sublayer(x, norm_g[0, 2], m0[3], m0[4], ffn_w_gu_b, ffn_w_down_b, m0[5], norm_g[0, 3],
                         per_token=gr['per_token'], tm=gr['tm_f'], tpb=gr['tpb_f'],
                         tf=_largest_tile(ffn_w_down.shape[1], FFN_HIDDEN_TILE_CAP))
        gr['x1'] = x

    def attention(name):
        gr = groups[name]
        kw = dict(per_token=gr['per_token'], tm=gr['tm'], tpb=gr['tpb'])
        m1 = [gr['mod'](a) for a in _split_mod(mods[1])]
        kvm = [gr['mod'](a) for a in _split_mod(kv_mod)]
        gr['m1'] = m1
        x = gr['x1']
        b_, t_ = gr['b'], gr['t']
        if name == 'p':
            kvt, wint = norm_mod_matmul_t(x, kv_norm_g, kvm[0], kvm[1], kv_wt_b,
                                          (N_PAGED_ROWS * KVW, (N_KV_ROWS - N_PAGED_ROWS) * KVW), b=b_, tm=gr['tm'])
            out['kv_p'] = kvt.reshape(b_, N_PAGED_ROWS, g, hd, t_).transpose(0, 4, 1, 2, 3)
            n_keep = min(WINDOW, t_)
            out['win_p'] = wint[:, :, t_ - n_keep:].reshape(b_, 2, g, hd, n_keep).transpose(0, 4, 1, 2, 3)
            kpad = norm_mod_matmul(x, kv_norm_g, kvm[0], kvm[1], kpad_w_b, out_dtype=BF16, **kw)
            qt, glt = norm_mod_matmul_t(x, norm_g[1, 0], m1[0], m1[1], nsa_wt_b, (nq, g * GATE_ROWS),
                                        b=b_, tm=gr['tm'])
            cmp = compress_prompt(kvt, cw).reshape(2, b_, -1, g, hd)
            kc = jnp.pad(cmp[0].transpose(0, 2, 1, 3), ((0, 0), (0, 0), (0, 0), (0, LANES - hd)))
            o = nsa_prompt(qt, glt, kc, cmp[1].transpose(0, 2, 3, 1), kpad, kvt, wint)
        else:
            proj = norm_mod_matmul(x, norm_g[1, 0], m1[0], m1[1], nsa_w_in_b, **kw)
            rows = norm_mod_matmul(x, kv_norm_g, kvm[0], kvm[1], kv_w_b, **kw)
            q = (proj[:, :NSA_HEADS * hd] * (hd ** -0.5)).reshape(b_, t_, g, r, hd)
            gl = proj[:, NSA_HEADS * hd:].reshape(b_, t_, g, LANES)[..., :r * N_BRANCH]
            gl = gl.reshape(b_, t_, g, r, N_BRANCH)
            rows3 = rows.reshape(b_, t_, N_KV_ROWS * KVW)
            rows6 = rows.reshape(b_, t_, N_KV_ROWS, g, hd)
            out['kv_s'] = rows6[:, :, :N_PAGED_ROWS]
            cache_t = cache_kv.transpose(0, 2, 3, 4, 1).reshape(n_pool, N_PAGED_ROWS, KVW, page)
            win_t5 = cache_win.transpose(0, 2, 3, 4, 1)
            n_win = cache_win.shape[1]
            new_win_t = rows6[:, :, N_PAGED_ROWS:].transpose(0, 2, 3, 4, 1)
            out['win_s'] = jnp.concatenate([win_t5, new_win_t], axis=-1)[..., t_:].transpose(0, 4, 1, 2, 3)
            new_sub = jnp.pad(rows3[:, :, :2 * KVW], ((0, 0), (0, CMP_STRIDE - t_), (0, 0)))
            new_sub = new_sub.reshape(b_, CMP_STRIDE, 2, KVW).transpose(2, 0, 1, 3)
            cmp = compress_sample(cache_t, page_table, new_sub, cw)
            eye = jnp.eye(g, dtype=F32)
            qrows = q.transpose(0, 2, 3, 1, 4).reshape(b_, g, r * t_, hd)
            q4 = jnp.einsum('bgxd,gj->bgxjd', qrows, eye).reshape(b_, g * r * t_, KVW)
            glr = gl.transpose(0, 2, 3, 1, 4).reshape(b_, g * r * t_, N_BRANCH)
            ocmp, sel = nsa_sample_select(q4, cmp, t=t_, past=past)
            o4 = nsa_sample_attend(q4, sel, ocmp, glr, cache_t, page_table, rows3,
                                   win_t5.reshape(b_, 2, KVW, n_win), t=t_, past=past)
            o4 = o4.reshape(b_, g, r, t_, g, hd)
            o = jnp.einsum('bgrtjd,gj->btgrd', o4, eye).reshape(ms, NSA_HEADS * hd)
        gr['x2'] = matmul_norm_residual(o, nsa_w_out_b, x, m1[2], norm_g[1, 1], **kw)

    take = lambda a, idx: a.at[idx].get(mode='promise_in_bounds')

    def moe_dispatch(name):
        gr = groups[name]
        kw = dict(per_token=gr['per_token'], tm=gr['tm'], tpb=gr['tpb'])
        h, route = moe_route(gr['x2'], norm_g[1, 2], gr['m1'][3], gr['m1'][4], rw_pad, rb_pad, **kw)
        tm_e = 512 if h.shape[0] >= 4096 else 128
        src_tok, tile_expert, n_valid, dest = _moe_tables(route, tm_e)
        return dict(xs=take(h, src_tok), route=route, tile_expert=tile_expert, n_valid=n_valid,
                    dest=dest.reshape(-1, TOP_K), tm_e=tm_e)

    def moe_finish(name, dp):
        gr = groups[name]
        kw = dict(per_token=gr['per_token'], tm=gr['tm'], tpb=gr['tpb'])
        ys = moe_experts(dp['xs'], dp['tile_expert'], dp['n_valid'], moe_w_gu_b, moe_w_down_b, tm=dp['tm_e'],
                         tf=_largest_tile(moe_w_down.shape[2], MOE_HIDDEN_TILE_CAP))
        out['y_' + name] = moe_combine(take(ys, dp['dest'][:, 0]), take(ys, dp['dest'][:, 1]), dp['route'],
                                       gr['x2'], gr['m1'][5], norm_g[1, 3], **kw)

    attention('p')
    dispatch_p = moe_dispatch('p')
    attention('s')
    moe_finish('p', dispatch_p)
    moe_finish('s', moe_dispatch('s'))

    return (out['y_p'].reshape(bp, t, d), out['y_s'].reshape(bs, ts, d),
            out['ret_p'], out['ret_s'], out['kv_p'], out['kv_s'], out['win_p'], out['win_s'])
```

```python
import functools
import math

import numpy as np
import jax
import jax.numpy as jnp
from jax import lax
from jax.experimental import pallas as pl
from jax.experimental.pallas import tpu as pltpu

F32 = jnp.float32
BF16 = jnp.bfloat16
I32 = jnp.int32

D_MODEL = 1024
N_MOD = 6
RET_HEADS = 4
RET_DK = 256
RET_DV = 256
RET_CHUNK = 128
ROPE_BASE = 10000.0
NSA_HEADS = 16
NSA_KV_HEADS = 4
NSA_GROUP = 4
NSA_HEAD_DIM = 64
N_BRANCH = 3
N_KV_ROWS = 6
N_PAGED_ROWS = 4
CMP_BLOCK = 32
CMP_STRIDE = 16
CMP_HIDDEN = 128
SEL_BLOCK = 64
N_SEL = 16
FORCED_SCORE = 1.0e4
WINDOW = 512
Q_BLOCK = 128
N_EXPERTS = 8
TOP_K = 2
EPS = 1e-6

NEG = -1.0e30
KVW = NSA_KV_HEADS * NSA_HEAD_DIM
VMEM_LIMIT_BYTES = 56 * 1024 * 1024
HIGHEST = lax.Precision.HIGHEST


LANES = 128
FFN_HIDDEN_TILE_CAP = 1408
MOE_HIDDEN_TILE_CAP = 896


def _largest_tile(n, cap):
    best = LANES
    for k in range(LANES, cap + 1, LANES):
        if n % k == 0:
            best = k
    return best


def _cparams(*sem):
    return pltpu.CompilerParams(dimension_semantics=sem, vmem_limit_bytes=VMEM_LIMIT_BYTES)


def _sigmoid(x):
    return 1.0 / (1.0 + jnp.exp(-x))


def _silu(x):
    return x * _sigmoid(x)


def _gelu_tanh(x):
    return x * (0.5 * (1.0 + jnp.tanh(math.sqrt(2.0 / math.pi) * (x + 0.044715 * (x * x * x)))))


def _norm_mod(x, g, shift, scale):
    ms = jnp.mean(x * x, axis=-1, keepdims=True)
    return (x * lax.rsqrt(ms + EPS) * g) * (1.0 + scale) + shift


def _rms_residual(x, gate, o, g):
    ms = jnp.mean(o * o, axis=-1, keepdims=True)
    return x + gate * (o * lax.rsqrt(ms + EPS) * g)


def _dot(a, b):
    return jnp.dot(a, b, preferred_element_type=F32)


def _dot_nt(a, b):
    return lax.dot_general(a, b, (((1,), (1,)), ((), ())), preferred_element_type=F32)


def _masked_softmax(s, mask):
    sm = jnp.where(mask, s, NEG)
    m = jnp.max(sm, axis=-1, keepdims=True)
    e = jnp.where(mask, jnp.exp(sm - m), 0.0)
    return e / jnp.maximum(jnp.sum(e, axis=-1, keepdims=True), 1e-30)


def _flash_init(m_ref, l_ref, acc_ref):
    m_ref[...] = jnp.full(m_ref.shape, NEG, F32)
    l_ref[...] = jnp.zeros(l_ref.shape, F32)
    acc_ref[...] = jnp.zeros(acc_ref.shape, F32)


def _flash_update(s, mask, v_b, m_ref, l_ref, acc_ref):
    sm = jnp.where(mask, s, NEG)
    m_old = m_ref[...]
    m_new = jnp.maximum(m_old, jnp.max(sm, axis=-1, keepdims=True))
    alpha = jnp.exp(m_old - m_new)
    p = jnp.where(mask, jnp.exp(sm - m_new), 0.0)
    l_ref[...] = alpha * l_ref[...] + jnp.sum(p, axis=-1, keepdims=True)
    acc_ref[...] = alpha * acc_ref[...] + _dot(p.astype(BF16), v_b)
    m_ref[...] = m_new


def _flash_result(l_ref, acc_ref):
    return acc_ref[...] / jnp.maximum(l_ref[...], 1e-30)


def _select_blocks(imp, cur, n_sel):
    rows, lanes = imp.shape
    blk = lax.broadcasted_iota(I32, (rows, lanes), 1)
    valid = (blk <= cur) & (blk < n_sel)
    forced = (blk == 0) | (blk == cur) | (blk == cur - 1)
    score = jnp.where(valid, jnp.where(forced, FORCED_SCORE, imp), -1.0)
    rank = jnp.zeros((rows, lanes), F32)
    for i in range(n_sel):
        ci = score[:, i:i + 1]
        beats = (ci > score) | ((ci == score) & (blk > i))
        rank = rank + jnp.where(beats, 1.0, 0.0)
    return jnp.where(valid & (rank < float(min(N_SEL, n_sel))), 1.0, 0.0)


def _cond_kernel(c_ref, w_ref, b_ref, o_ref):
    sc = _silu(c_ref[...])
    o_ref[...] = _dot(sc.astype(BF16), w_ref[...].astype(BF16)) + b_ref[...]


def cond_matmul(c, w, b, layer):
    bc, d = c.shape
    n_layers, _, n = w.shape
    tn = 1024
    return pl.pallas_call(
        _cond_kernel,
        grid=(n // tn,),
        in_specs=[pl.BlockSpec((bc, d), lambda j: (0, 0)),
                  pl.BlockSpec((None, d, tn), lambda j: (layer, 0, j)),
                  pl.BlockSpec((None, 1, tn), lambda j: (layer, 0, j))],
        out_specs=pl.BlockSpec((bc, tn), lambda j: (0, j)),
        out_shape=jax.ShapeDtypeStruct((bc, n), F32),
        compiler_params=_cparams("parallel"),
        name="cond_matmul",
    )(c, w, b.reshape(n_layers, 1, n))


def _mod_arg(m, per_token, tm, tiles_per_batch):
    d = m.shape[-1]
    if per_token:
        return m, pl.BlockSpec((tm, d), lambda i, *_: (i, 0))
    return m[:, None, :], pl.BlockSpec((None, 1, d), lambda i, *_: (i // tiles_per_batch, 0, 0))


def _vec_spec(d):
    return pl.BlockSpec((1, d), lambda i, *_: (0, 0))


def _nmm_kernel(x_ref, g_ref, sh_ref, sc_ref, w_ref, o_ref, h_ref):
    @pl.when(pl.program_id(1) == 0)
    def _():
        h_ref[...] = _norm_mod(x_ref[...], g_ref[...], sh_ref[...], sc_ref[...]).astype(BF16)

    o_ref[...] = _dot(h_ref[...], w_ref[...]).astype(o_ref.dtype)


def norm_mod_matmul(x, g, shift, scale, w_b, *, per_token, tm, tpb, tn=None, out_dtype=F32):
    m, d = x.shape
    n = w_b.shape[1]
    tn = n if tn is None else tn
    sh, sh_spec = _mod_arg(shift, per_token, tm, tpb)
    sc, sc_spec = _mod_arg(scale, per_token, tm, tpb)
    return pl.pallas_call(
        _nmm_kernel,
        grid=(m // tm, n // tn),
        in_specs=[pl.BlockSpec((tm, d), lambda i, j: (i, 0)), _vec_spec(d), sh_spec, sc_spec,
                  pl.BlockSpec((d, tn), lambda i, j: (0, j))],
        out_specs=pl.BlockSpec((tm, tn), lambda i, j: (i, j)),
        out_shape=jax.ShapeDtypeStruct((m, n), out_dtype),
        scratch_shapes=[pltpu.VMEM((tm, d), BF16)],
        compiler_params=_cparams("parallel", "arbitrary"),
        name="norm_mod_matmul",
    )(x, g.reshape(1, d), sh, sc, w_b)


def _nmm_t_kernel(x_ref, g_ref, sh_ref, sc_ref, wt_ref, *o_refs):
    h = _norm_mod(x_ref[...], g_ref[...], sh_ref[...], sc_ref[...]).astype(BF16)
    o = _dot_nt(wt_ref[...], h)
    row = 0
    for o_ref in o_refs:
        o_ref[...] = o[row:row + o_ref.shape[0]]
        row += o_ref.shape[0]


def norm_mod_matmul_t(x, g, shift, scale, wt_b, splits, *, b, tm):
    m, d = x.shape
    t = m // b
    tpb = t // tm
    n = wt_b.shape[0]
    sh, sh_spec = _mod_arg(shift, False, tm, tpb)
    sc, sc_spec = _mod_arg(scale, False, tm, tpb)
    return pl.pallas_call(
        _nmm_t_kernel,
        grid=(m // tm,),
        in_specs=[pl.BlockSpec((tm, d), lambda i: (i, 0)), _vec_spec(d), sh_spec, sc_spec,
                  pl.BlockSpec((n, d), lambda i: (0, 0))],
        out_specs=[pl.BlockSpec((None, ni, tm), lambda i: (i // tpb, 0, i % tpb)) for ni in splits],
        out_shape=[jax.ShapeDtypeStruct((b, ni, t), F32) for ni in splits],
        compiler_params=_cparams("parallel"),
        name="norm_mod_matmul_t",
    )(x, g.reshape(1, d), sh, sc, wt_b)


def _mnr_kernel(a_ref, w_ref, x_ref, gate_ref, g_ref, o_ref):
    o = _dot(a_ref[...].astype(BF16), w_ref[...])
    o_ref[...] = _rms_residual(x_ref[...], gate_ref[...], o, g_ref[...])


def matmul_norm_residual(a, w_b, x, gate, g, *, per_token, tm, tpb):
    m, k = a.shape
    d = w_b.shape[1]
    gt, gt_spec = _mod_arg(gate, per_token, tm, tpb)
    return pl.pallas_call(
        _mnr_kernel,
        grid=(m // tm,),
        in_specs=[pl.BlockSpec((tm, k), lambda i: (i, 0)),
                  pl.BlockSpec((k, d), lambda i: (0, 0)),
                  pl.BlockSpec((tm, d), lambda i: (i, 0)), gt_spec, _vec_spec(d)],
        out_specs=pl.BlockSpec((tm, d), lambda i: (i, 0)),
        out_shape=jax.ShapeDtypeStruct((m, d), F32),
        compiler_params=_cparams("parallel"),
        name="matmul_norm_residual",
    )(a, w_b, x, gt, g.reshape(1, d))


def _combine_kernel(y1_ref, y2_ref, r_ref, x_ref, gate_ref, g_ref, o_ref):
    route = r_ref[...]
    y = route[:, TOP_K:TOP_K + 1] * y1_ref[...] + route[:, TOP_K + 1:TOP_K + 2] * y2_ref[...]
    o_ref[...] = _rms_residual(x_ref[...], gate_ref[...], y, g_ref[...])


def moe_combine(y1, y2, route, x, gate, g, *, per_token, tm, tpb):
    m, d = x.shape
    gt, gt_spec = _mod_arg(gate, per_token, tm, tpb)
    row = lambda n: pl.BlockSpec((tm, n), lambda i: (i, 0))
    return pl.pallas_call(
        _combine_kernel,
        grid=(m // tm,),
        in_specs=[row(d), row(d), row(route.shape[1]), row(d), gt_spec, _vec_spec(d)],
        out_specs=row(d),
        out_shape=jax.ShapeDtypeStruct((m, d), F32),
        compiler_params=_cparams("parallel"),
        name="moe_combine",
    )(y1, y2, route, x, gt, g.reshape(1, d))


def _ffn_kernel(x_ref, g2_ref, sh_ref, sc_ref, wg_ref, wu_ref, wd_ref, gate_ref, g3_ref, o_ref,
                h_ref, acc_ref):
    f = pl.program_id(1)

    @pl.when(f == 0)
    def _():
        h_ref[...] = _norm_mod(x_ref[...], g2_ref[...], sh_ref[...], sc_ref[...]).astype(BF16)
        acc_ref[...] = jnp.zeros(acc_ref.shape, F32)

    h = h_ref[...]
    act = _silu(_dot(h, wg_ref[...])) * _dot(h, wu_ref[...])
    acc_ref[...] += _dot(act.astype(BF16), wd_ref[...])

    @pl.when(f == pl.num_programs(1) - 1)
    def _():
        o_ref[...] = _rms_residual(x_ref[...], gate_ref[...], acc_ref[...], g3_ref[...])


def ffn_sublayer(x, g2, shift, scale, w_gu_b, w_down_b, gate, g3, *, per_token, tm, tpb, tf):
    m, d = x.shape
    fdim = w_down_b.shape[0]
    nf = fdim // tf
    sh, sh_spec = _mod_arg(shift, per_token, tm, tpb)
    sc, sc_spec = _mod_arg(scale, per_token, tm, tpb)
    gt, gt_spec = _mod_arg(gate, per_token, tm, tpb)
    return pl.pallas_call(
        _ffn_kernel,
        grid=(m // tm, nf),
        in_specs=[pl.BlockSpec((tm, d), lambda i, f: (i, 0)), _vec_spec(d), sh_spec, sc_spec,
                  pl.BlockSpec((d, tf), lambda i, f: (0, f)),
                  pl.BlockSpec((d, tf), lambda i, f: (0, nf + f)),
                  pl.BlockSpec((tf, d), lambda i, f: (f, 0)),
                  gt_spec, _vec_spec(d)],
        out_specs=pl.BlockSpec((tm, d), lambda i, f: (i, 0)),
        out_shape=jax.ShapeDtypeStruct((m, d), F32),
        scratch_shapes=[pltpu.VMEM((tm, d), BF16), pltpu.VMEM((tm, d), F32)],
        compiler_params=_cparams("parallel", "arbitrary"),
        name="ffn_sublayer",
    )(x, g2.reshape(1, d), sh, sc, w_gu_b, w_gu_b, w_down_b, gt, g3.reshape(1, d))


def _ret_kernel(q_ref, k_ref, v_ref, gt_ref, cos_ref, sin_ref, dm_ref, xi_ref, zt_ref, gc_ref,
                gn_ref, s0_ref, o_ref, s_ref):
    @pl.when(pl.program_id(1) == 0)
    def _():
        s_ref[...] = s0_ref[...]

    rows = q_ref.shape[0]

    def pad(x):
        if rows == RET_CHUNK:
            return x
        return jnp.concatenate([x, jnp.zeros((RET_CHUNK - rows, x.shape[1]), x.dtype)], axis=0)

    cos = pad(cos_ref[...])
    sin = pad(sin_ref[...])
    half = RET_DK // 2

    def rot(x):
        x1 = x[:, :half]
        x2 = x[:, half:]
        return jnp.concatenate([x1 * cos - x2 * sin, x2 * cos + x1 * sin], axis=-1)

    for h in range(RET_HEADS):
        kcols = slice(h * RET_DK, (h + 1) * RET_DK)
        vcols = slice(h * RET_DV, (h + 1) * RET_DV)
        q = rot(pad(q_ref[:, kcols]))
        k = rot(pad(k_ref[:, kcols])) * (RET_DK ** -0.5)
        qb = q.astype(BF16)
        kb = k.astype(BF16)
        vb = pad(v_ref[:, vcols]).astype(BF16)
        state = s_ref[h]
        inner = _dot_nt(qb, kb) * dm_ref[h]
        o = _dot(inner.astype(BF16), vb) + _dot(qb, state.astype(BF16)) * xi_ref[h]
        kz = (k * zt_ref[h]).astype(BF16)
        upd = lax.dot_general(kz, vb, (((0,), (0,)), ((), ())), preferred_element_type=F32)
        s_ref[h] = gc_ref[h] * state + upd
        mu = jnp.mean(o, axis=-1, keepdims=True)
        dev = o - mu
        var = jnp.mean(dev * dev, axis=-1, keepdims=True)
        on = dev * lax.rsqrt(var + EPS) * gn_ref[:, vcols]
        o_ref[:, vcols] = _silu(gt_ref[:, vcols]) * on[:rows]


def _ret_tables(chunk, rows):
    h = RET_HEADS
    log_g = jnp.log(1.0 - jnp.exp(jnp.linspace(math.log(1.0 / 32), math.log(1.0 / 512), h, dtype=F32)))
    i = jnp.arange(chunk, dtype=F32)
    diff = i[:, None] - i[None, :]
    dmat = jnp.where(diff >= 0, jnp.exp(log_g[:, None, None] * jnp.maximum(diff, 0.0)), 0.0)
    xi = jnp.exp(log_g[:, None] * (i + 1.0))
    zeta = jnp.exp(log_g[:, None] * (chunk - 1.0 - i))
    gch = jnp.exp(log_g * chunk)
    pad = rows - chunk
    dmat = jnp.pad(dmat, ((0, 0), (0, pad), (0, pad)))
    xi = jnp.pad(xi, ((0, 0), (0, pad)))[..., None]
    zeta = jnp.pad(zeta, ((0, 0), (0, pad)))[..., None]
    gch = jnp.broadcast_to(gch[:, None, None], (h, 1, RET_DV))
    return dmat, xi, zeta, gch


def _rope_tables(pos, rows):
    half = RET_DK // 2
    inv = ROPE_BASE ** (-jnp.arange(half, dtype=F32) / half)
    ang = pos.astype(F32)[:, None] * inv[None, :]
    pad = rows - pos.shape[0]
    return jnp.pad(jnp.cos(ang), ((0, pad), (0, 0))), jnp.pad(jnp.sin(ang), ((0, pad), (0, 0)))


def retention(proj, pos, s0, gn_g, chunk):
    b, t, _ = proj.shape
    c = min(t, RET_CHUNK)
    n = t // c
    h = RET_HEADS
    dmat, xi, zeta, gch = _ret_tables(chunk, RET_CHUNK)
    cos, sin = _rope_tables(pos, t)
    col = lambda j: pl.BlockSpec((None, c, h * RET_DK), lambda bi, ni: (bi, ni, j))
    const = lambda a: pl.BlockSpec(a.shape, lambda bi, ni: (0,) * a.ndim)
    state_spec = pl.BlockSpec((None, h, RET_DK, RET_DV), lambda bi, ni: (bi, 0, 0, 0))
    gn = gn_g.reshape(1, -1)
    o, s = pl.pallas_call(
        _ret_kernel,
        grid=(b, n),
        in_specs=[col(0), col(1), col(2), col(3),
                  pl.BlockSpec((c, RET_DK // 2), lambda bi, ni: (ni, 0)),
                  pl.BlockSpec((c, RET_DK // 2), lambda bi, ni: (ni, 0)),
                  const(dmat), const(xi), const(zeta), const(gch), const(gn), state_spec],
        out_specs=[pl.BlockSpec((None, c, h * RET_DV), lambda bi, ni: (bi, ni, 0)), state_spec],
        out_shape=[jax.ShapeDtypeStruct((b, t, h * RET_DV), F32),
                   jax.ShapeDtypeStruct((b, h, RET_DK, RET_DV), F32)],
        compiler_params=_cparams("parallel", "arbitrary"),
        name="retention",
    )(proj, proj, proj, proj, cos, sin, dmat, xi, zeta, gch, gn, s0)
    return o, s


def _compress_weights(cmp_pos, cmp_w1, cmp_b1, cmp_w2):
    g, d, hd = NSA_KV_HEADS, NSA_HEAD_DIM, CMP_HIDDEN
    eye = jnp.eye(g, dtype=F32)
    w1 = cmp_w1.reshape(2, CMP_BLOCK, d, hd)
    bd1 = jnp.einsum('ksdh,gj->ksgdjh', w1, eye).reshape(2, CMP_BLOCK, g * d, g * hd).astype(BF16)
    bd2 = jnp.einsum('khd,gj->kghjd', cmp_w2, eye).reshape(2, g * hd, g * d).astype(BF16)
    pos = jnp.tile(cmp_pos[:, :, None, :], (1, 1, g, 1)).reshape(2, CMP_BLOCK, 1, g * d)
    b1 = jnp.tile(cmp_b1[:, None, :], (1, g, 1)).reshape(2, 1, g * hd)
    return bd1, bd2, pos, b1


def _compress_finish(p0, p1, b1, bd2):
    rows = p0.shape[0]
    hid = b1 + p0 + pltpu.roll(p1, rows - 1, 0)
    return _dot(_gelu_tanh(hid).astype(BF16), bd2)


CMP_PAGES_PER_STEP = 8
ATT_PAGES_PER_STEP = 16


def _tokens_to_sublanes(xt_ref, tr_ref):
    for half in range(KVW // LANES):
        tr_ref[half] = xt_ref[half * LANES:(half + 1) * LANES, :].T


def _cmp_prompt_kernel(xt_ref, bd1_ref, bd2_ref, pos_ref, b1_ref, o_ref, tr_ref):
    n_sub = xt_ref.shape[1] // CMP_STRIDE
    hid = bd1_ref.shape[-1]
    _tokens_to_sublanes(xt_ref, tr_ref)
    p0 = jnp.zeros((n_sub, hid), F32)
    p1 = jnp.zeros((n_sub, hid), F32)
    for s in range(CMP_STRIDE):
        xs = jnp.concatenate([tr_ref[half, pl.ds(s, n_sub, stride=CMP_STRIDE), :]
                              for half in range(KVW // LANES)], axis=-1)
        p0 = p0 + _dot((xs + pos_ref[s]).astype(BF16), bd1_ref[s])
        p1 = p1 + _dot((xs + pos_ref[CMP_STRIDE + s]).astype(BF16), bd1_ref[CMP_STRIDE + s])
    o_ref[...] = _compress_finish(p0, p1, b1_ref[...], bd2_ref[...])


def compress_prompt(kvt, cw):
    bd1, bd2, pos, b1 = cw
    b, _, t = kvt.shape
    n_sub = t // CMP_STRIDE
    return pl.pallas_call(
        _cmp_prompt_kernel,
        grid=(b, 2),
        in_specs=[pl.BlockSpec((None, KVW, t), lambda bi, ki: (bi, ki, 0)),
                  pl.BlockSpec((None,) + bd1.shape[1:], lambda bi, ki: (ki, 0, 0, 0)),
                  pl.BlockSpec((None,) + bd2.shape[1:], lambda bi, ki: (ki, 0, 0)),
                  pl.BlockSpec((None,) + pos.shape[1:], lambda bi, ki: (ki, 0, 0, 0)),
                  pl.BlockSpec((None,) + b1.shape[1:], lambda bi, ki: (ki, 0, 0))],
        out_specs=pl.BlockSpec((None, None, n_sub, KVW), lambda bi, ki: (ki, bi, 0, 0)),
        out_shape=jax.ShapeDtypeStruct((2, b, n_sub, KVW), F32),
        scratch_shapes=[pltpu.VMEM((KVW // LANES, t, LANES), F32)],
        compiler_params=_cparams("parallel", "parallel"),
        name="compress_prompt",
    )(kvt, bd1, bd2, pos, b1)


def _cmp_sample_kernel(pt_ref, *refs):
    page_refs = refs[:CMP_PAGES_PER_STEP]
    new_ref, bd1_ref, bd2_ref, pos_ref, b1_ref, o_ref, sub_ref, tr_ref = refs[CMP_PAGES_PER_STEP:]
    p = pl.program_id(1)
    page = page_refs[0].shape[2]
    per_page = page // CMP_STRIDE
    n_cmp = o_ref.shape[1]
    for j, page_ref in enumerate(page_refs):
        row0 = pl.multiple_of((p * CMP_PAGES_PER_STEP + j) * per_page, per_page)
        for kind in range(2):
            _tokens_to_sublanes(page_ref.at[kind], tr_ref.at[kind])
            for s in range(CMP_STRIDE):
                for half in range(KVW // LANES):
                    piece = tr_ref[kind, half, pl.ds(s, per_page, stride=CMP_STRIDE), :]
                    sub_ref[kind, s, pl.ds(row0, per_page), half * LANES:(half + 1) * LANES] = piece

    @pl.when(p == pl.num_programs(1) - 1)
    def _():
        n_rows = sub_ref.shape[2]
        tail = n_rows - n_cmp
        hid = bd1_ref.shape[-1]
        for kind in range(2):
            p0 = jnp.zeros((n_rows, hid), F32)
            p1 = jnp.zeros((n_rows, hid), F32)
            for s in range(CMP_STRIDE):
                sub_ref[kind, s, pl.ds(n_cmp, tail), :] = jnp.broadcast_to(new_ref[kind, s:s + 1, :], (tail, KVW))
                xs = sub_ref[kind, s]
                p0 = p0 + _dot((xs + pos_ref[kind, s]).astype(BF16), bd1_ref[kind, s])
                p1 = p1 + _dot((xs + pos_ref[kind, CMP_STRIDE + s]).astype(BF16), bd1_ref[kind, CMP_STRIDE + s])
            o_ref[kind] = _compress_finish(p0, p1, b1_ref[kind], bd2_ref[kind])[:n_cmp]


def _page_specs(n_per_step, kind_block, page):
    def spec(j):
        return pl.BlockSpec((None, 2, KVW, page),
                            lambda bi, pi, pt: (pt[bi, pi * n_per_step + j], kind_block, 0, 0))
    return [spec(j) for j in range(n_per_step)]


def compress_sample(cache_t, page_table, new_sub, cw):
    bd1, bd2, pos, b1 = cw
    b, n_pages = page_table.shape
    page = cache_t.shape[3]
    n_cmp = n_pages * page // CMP_STRIDE
    nps = CMP_PAGES_PER_STEP
    const = lambda a: pl.BlockSpec(a.shape, lambda bi, pi, pt: (0,) * a.ndim, pipeline_mode=pl.Buffered(1))
    grid_spec = pltpu.PrefetchScalarGridSpec(
        num_scalar_prefetch=1,
        grid=(b, n_pages // nps),
        in_specs=_page_specs(nps, 0, page) + [
            pl.BlockSpec((2, None, CMP_STRIDE, KVW), lambda bi, pi, pt: (0, bi, 0, 0)),
            const(bd1), const(bd2), const(pos), const(b1)],
        out_specs=pl.BlockSpec((2, None, n_cmp, KVW), lambda bi, pi, pt: (0, bi, 0, 0)),
        scratch_shapes=[pltpu.VMEM((2, CMP_STRIDE, n_cmp + 8, KVW), F32),
                        pltpu.VMEM((2, KVW // LANES, page, LANES), F32)],
    )
    return pl.pallas_call(
        _cmp_sample_kernel,
        grid_spec=grid_spec,
        out_shape=jax.ShapeDtypeStruct((2, b, n_cmp, KVW), F32),
        compiler_params=_cparams("parallel", "arbitrary"),
        name="compress_sample",
    )(page_table, *([cache_t] * nps), new_sub, bd1, bd2, pos, b1)


def _cover_matrix(n_cmp, n_sel, rows, cols):
    i = np.arange(n_cmp)[:, None]
    j = np.arange(n_sel)[None, :]
    cover = (i * CMP_STRIDE < (j + 1) * SEL_BLOCK) & (i * CMP_STRIDE + CMP_BLOCK > j * SEL_BLOCK)
    out = np.zeros((rows, cols), np.float32)
    out[:n_cmp, :n_sel] = cover
    return jnp.asarray(out)


SEL_CHUNK = 256
SEL_CLASS = 512


def _select_blocks_t(imp_t, cur, n_sel):
    nb, nq = imp_t.shape
    blk = lax.broadcasted_iota(I32, (nb, nq), 0)
    valid = (blk <= cur) & (blk < n_sel)
    forced = (blk == 0) | (blk == cur) | (blk == cur - 1)
    score = jnp.where(valid, jnp.where(forced, FORCED_SCORE, imp_t), -1.0)
    rank = jnp.zeros((nb, nq), F32)
    for i in range(n_sel):
        ci = score[i:i + 1, :]
        beats = (ci > score) | ((ci == score) & (blk > i))
        rank = rank + jnp.where(beats, 1.0, 0.0)
    return jnp.where(valid & (rank < float(min(N_SEL, n_sel))), 1.0, 0.0)


def _nsa_prompt_kernel(qt_ref, glt_ref, kc_ref, vct_ref, covt_ref, kblk_ref, wb_ref, ks_ref, vst_ref, kw_ref, vwt_ref,
                       o_ref, s_ref, osel_ref, *, n_cmp, n_sel):
    tq = qt_ref.shape[1]
    r, d = NSA_GROUP, NSA_HEAD_DIM
    cols = r * tq
    t = ks_ref.shape[0]
    q0 = pl.program_id(2) * tq
    qt = qt_ref[...] * (d ** -0.5)
    qcat = jnp.concatenate([qt[h * d:(h + 1) * d, :] for h in range(r)], axis=-1)
    qb = jnp.concatenate([qcat, jnp.zeros_like(qcat)], axis=0).astype(BF16)
    gl = _sigmoid(glt_ref[...])

    def gate(branch):
        return jnp.concatenate([jnp.broadcast_to(gl[h * N_BRANCH + branch:h * N_BRANCH + branch + 1, :], (d, tq))
                                for h in range(r)], axis=0)

    def stack(o_t):
        return jnp.concatenate([o_t[:, h * tq:(h + 1) * tq] for h in range(r)], axis=0)

    def tile(x):
        return jnp.concatenate([x] * r, axis=-1)

    ones_rows = 16

    def with_ones(vt):
        return jnp.concatenate([vt.astype(BF16), jnp.ones((ones_rows, vt.shape[1]), BF16)], axis=0)

    qpos = q0 + lax.broadcasted_iota(I32, (1, tq), 1)

    span = min(WINDOW + tq, t)
    w0 = pl.multiple_of(jnp.maximum(q0 + tq - span, 0), tq)
    sm = _dot(kw_ref[pl.ds(w0, span), :], qb) + tile(wb_ref[(q0 - w0) // tq])
    p = jnp.exp(sm - jnp.max(sm, axis=0, keepdims=True))
    o_win_t = _dot(with_ones(vwt_ref[:, pl.ds(w0, span)]), p.astype(BF16))
    out = gate(2) * stack(o_win_t[:d] / jnp.maximum(o_win_t[d:d + 1], 1e-30))

    ncp = kc_ref.shape[0]
    nn = lax.broadcasted_iota(I32, (ncp, tq), 0)
    ok = tile(jnp.where((nn * CMP_STRIDE + (CMP_BLOCK - 1) <= qpos) & (nn < n_cmp), 1.0, 0.0))
    sm = _dot(kc_ref[...].astype(BF16), qb) + (ok - 1.0) * (-NEG)
    e = jnp.exp(sm - jnp.max(sm, axis=0, keepdims=True)) * ok
    p_cmp = e / jnp.maximum(jnp.sum(e, axis=0, keepdims=True), 1e-30)
    out = out + gate(0) * stack(_dot(vct_ref[...].astype(BF16), p_cmp.astype(BF16)))

    psum_t = p_cmp[:, :tq]
    for h in range(1, r):
        psum_t = psum_t + p_cmp[:, h * tq:(h + 1) * tq]
    imp_t = jnp.dot(covt_ref[...], psum_t, precision=HIGHEST, preferred_element_type=F32)
    sel_t = _select_blocks_t(imp_t, qpos >> 6, n_sel)
    nb = sel_t.shape[0]
    q_sel = jnp.concatenate([qcat, tile((sel_t - 1.0) * (-NEG)), jnp.zeros((LANES - d - nb, cols), F32)],
                            axis=0).astype(BF16)

    groups = SEL_CHUNK // 8
    sel_class = SEL_CLASS if t % SEL_CLASS == 0 else t

    def sel_branch(n_keys):
        m_run = jnp.full((8, cols), NEG, F32)
        for k0 in range(0, n_keys, SEL_CHUNK):
            k_aug = ks_ref[k0:k0 + SEL_CHUNK, :] + kblk_ref[k0:k0 + SEL_CHUNK, :]
            sm = _dot(k_aug, q_sel)
            if k0 + SEL_CHUNK > n_keys - sel_class:
                keypos = k0 + lax.broadcasted_iota(I32, (SEL_CHUNK, tq), 0)
                sm = sm + tile(jnp.where(keypos <= qpos, 0.0, NEG))
            s_ref[k0:k0 + SEL_CHUNK, :] = sm
            m_run = jnp.maximum(m_run, jnp.max(sm.reshape(groups, 8, cols), axis=0))
        m_sel = jnp.max(m_run, axis=0, keepdims=True)
        acc_t = jnp.zeros((d + ones_rows, cols), F32)
        for k0 in range(0, n_keys, SEL_CHUNK):
            p = jnp.exp(s_ref[k0:k0 + SEL_CHUNK, :] - m_sel)
            acc_t = acc_t + _dot(with_ones(vst_ref[:, k0:k0 + SEL_CHUNK]), p.astype(BF16))
        osel_ref[...] = acc_t[:d] / jnp.maximum(acc_t[d:d + 1], 1e-30)

    cls_id = pl.program_id(2) // (sel_class // tq)
    for cls in range(t // sel_class):
        pl.when(cls_id == cls)(functools.partial(sel_branch, (cls + 1) * sel_class))
    o_ref[...] = (out + gate(1) * stack(osel_ref[...])).T


GATE_ROWS = 16


def nsa_prompt(qt, glt, kc, vct, kpad, kvt, wint):
    b, _, t = qt.shape
    g, r, d = NSA_KV_HEADS, NSA_GROUP, NSA_HEAD_DIM
    tq = Q_BLOCK
    nt = t // tq
    ncp = kc.shape[2]
    n_cmp = t // CMP_STRIDE - 1
    n_sel = -(-t // SEL_BLOCK)
    nb = -(-n_sel // 8) * 8
    covt = _cover_matrix(n_cmp, n_sel, ncp, nb).T
    assert d + nb <= LANES, "block one-hot must fit in the keys' padding lanes"
    kblk = np.zeros((t, LANES), np.float32)
    kblk[np.arange(t), d + np.arange(t) // SEL_BLOCK] = 1.0
    kblk = jnp.asarray(kblk, BF16)
    span = min(WINDOW + tq, t)
    i = np.arange(span)[None, :, None]
    j = np.arange(tq)[None, None, :]
    rel = i - j - np.arange(0, span - tq + 1, tq)[:, None, None]
    wbias = jnp.asarray(np.where((rel <= 0) & (rel > -WINDOW), 0.0, NEG), F32)
    per_bg = lambda shape: pl.BlockSpec((None, None) + shape, lambda bi, gi, qi: (bi, gi, 0, 0))
    const = lambda a: pl.BlockSpec(a.shape, lambda bi, gi, qi: (0,) * a.ndim)
    return pl.pallas_call(
        functools.partial(_nsa_prompt_kernel, n_cmp=n_cmp, n_sel=n_sel),
        grid=(b, g, nt),
        in_specs=[pl.BlockSpec((None, r * d, tq), lambda bi, gi, qi: (bi, gi, qi)),
                  pl.BlockSpec((None, GATE_ROWS, tq), lambda bi, gi, qi: (bi, gi, qi)),
                  per_bg((ncp, LANES)), per_bg((d, ncp)), const(covt), const(kblk), const(wbias),
                  pl.BlockSpec((t, LANES), lambda bi, gi, qi: (bi, gi)),
                  pl.BlockSpec((None, d, t), lambda bi, gi, qi: (bi, 3 * g + gi, 0)),
                  pl.BlockSpec((t, LANES), lambda bi, gi, qi: (bi, g + gi)),
                  pl.BlockSpec((None, d, t), lambda bi, gi, qi: (bi, g + gi, 0))],
        out_specs=pl.BlockSpec((tq, r * d), lambda bi, gi, qi: (bi * nt + qi, gi)),
        out_shape=jax.ShapeDtypeStruct((b * t, NSA_HEADS * d), F32),
        scratch_shapes=[pltpu.VMEM((t, r * tq), F32), pltpu.VMEM((d, r * tq), F32)],
        compiler_params=_cparams("parallel", "parallel", "arbitrary"),
        name="nsa_prompt",
    )(qt, glt, kc, vct, covt, kblk, wbias, kpad, kvt, kpad, wint)


def _nsa_sample_select_kernel(q_ref, kc_ref, vc_ref, cov_ref, oc_ref, sel_ref, *, t, past, n_cmp, n_sel):
    rows = q_ref.shape[0]
    g, r = NSA_KV_HEADS, NSA_GROUP
    qb = q_ref[...].astype(BF16)
    qpos_r = past + (lax.broadcasted_iota(I32, (rows, 1), 0) & (t - 1))
    s = _dot_nt(qb, kc_ref[...].astype(BF16))
    nn = lax.broadcasted_iota(I32, s.shape, 1)
    p_cmp = _masked_softmax(s, (nn * CMP_STRIDE + (CMP_BLOCK - 1) <= qpos_r) & (nn < n_cmp))
    oc_ref[...] = _dot(p_cmp.astype(BF16), vc_ref[...].astype(BF16))
    psum = jnp.sum(p_cmp.reshape(g, r, t, s.shape[1]), axis=1).reshape(g * t, s.shape[1])
    imp = jnp.dot(psum, cov_ref[...], precision=HIGHEST, preferred_element_type=F32)
    qpos_gt = past + (lax.broadcasted_iota(I32, (g * t, 1), 0) & (t - 1))
    sel = _select_blocks(imp, qpos_gt >> 6, n_sel)
    lanes = sel.shape[1]
    sel_ref[...] = jnp.broadcast_to(sel.reshape(g, 1, t, lanes), (g, r, t, lanes)).reshape(rows, lanes)


def nsa_sample_select(q4, cmp, *, t, past):
    b, rows, _ = q4.shape
    n_cmp = cmp.shape[2]
    n_sel = -(-(past + t) // SEL_BLOCK)
    lanes = -(-n_sel // 128) * 128
    cover = _cover_matrix(n_cmp, n_sel, n_cmp, lanes)
    return pl.pallas_call(
        functools.partial(_nsa_sample_select_kernel, t=t, past=past, n_cmp=n_cmp, n_sel=n_sel),
        grid=(b,),
        in_specs=[pl.BlockSpec((None, rows, KVW), lambda bi: (bi, 0, 0)),
                  pl.BlockSpec((None, None, n_cmp, KVW), lambda bi: (0, bi, 0, 0)),
                  pl.BlockSpec((None, None, n_cmp, KVW), lambda bi: (1, bi, 0, 0)),
                  pl.BlockSpec(cover.shape, lambda bi: (0, 0))],
        out_specs=[pl.BlockSpec((None, rows, KVW), lambda bi: (bi, 0, 0)),
                   pl.BlockSpec((None, rows, lanes), lambda bi: (bi, 0, 0))],
        out_shape=[jax.ShapeDtypeStruct((b, rows, KVW), F32),
                   jax.ShapeDtypeStruct((b, rows, lanes), F32)],
        compiler_params=_cparams("parallel"),
        name="nsa_sample_select",
    )(q4, cmp, cmp, cover)


def _nsa_sample_attend_kernel(pt_ref, q_ref, sel_ref, oc_ref, gl_ref, *refs, t, past):
    page_refs = refs[:ATT_PAGES_PER_STEP]
    new_ref, win_ref, eb_ref, o_ref, m_ref, l_ref, acc_ref = refs[ATT_PAGES_PER_STEP:]
    rows = q_ref.shape[0]
    p = pl.program_id(1)
    page = page_refs[0].shape[2]
    width = ATT_PAGES_PER_STEP * page
    qb = q_ref[...].astype(BF16)
    sel = sel_ref[...]
    nblk = sel.shape[1]

    @pl.when(p == 0)
    def _():
        _flash_init(m_ref, l_ref, acc_ref)

    sc = jnp.concatenate([_dot(qb, pr[0].astype(BF16)) for pr in page_refs], axis=-1)
    blocks_per_step = width // SEL_BLOCK
    e0 = pl.multiple_of((pl.num_programs(1) - 1 - p) * blocks_per_step, blocks_per_step)
    mask = _dot(sel.astype(BF16), eb_ref[pl.ds(e0, nblk), :]) > 0.5
    sm = jnp.where(mask, sc, NEG)
    m_old = m_ref[...]
    m_new = jnp.maximum(m_old, jnp.max(sm, axis=-1, keepdims=True))
    alpha = jnp.exp(m_old - m_new)
    pr_b = jnp.where(mask, jnp.exp(sm - m_new), 0.0)
    l_ref[...] = alpha * l_ref[...] + jnp.sum(pr_b, axis=-1, keepdims=True)
    pr_b = pr_b.astype(BF16)
    pv = _dot_nt(pr_b[:, :page], page_refs[0][1].astype(BF16))
    for j in range(1, ATT_PAGES_PER_STEP):
        pv = pv + _dot_nt(pr_b[:, j * page:(j + 1) * page], page_refs[j][1].astype(BF16))
    acc_ref[...] = alpha * acc_ref[...] + pv
    m_ref[...] = m_new

    @pl.when(p == pl.num_programs(1) - 1)
    def _():
        qpos_r = past + (lax.broadcasted_iota(I32, (rows, 1), 0) & (t - 1))
        tp = LANES
        newr = jnp.concatenate([new_ref[...], jnp.zeros((tp - t, new_ref.shape[1]), F32)], axis=0)
        jn = lax.broadcasted_iota(I32, (rows, tp), 1)
        new_ok = (jn < t) & (past + jn <= qpos_r)
        blk_new = past // SEL_BLOCK
        sel_new = sel[:, blk_new:blk_new + 1] > 0.5
        sc_n = _dot_nt(qb, newr[:, 2 * KVW:3 * KVW].astype(BF16))
        _flash_update(sc_n, new_ok & sel_new, newr[:, 3 * KVW:4 * KVW].astype(BF16), m_ref, l_ref, acc_ref)
        o_sel = _flash_result(l_ref, acc_ref)

        n_win = win_ref.shape[2]
        s_a = _dot(qb, win_ref[0].astype(BF16))
        s_b = _dot_nt(qb, newr[:, 4 * KVW:5 * KVW].astype(BF16))
        kp_a = (past - n_win) + lax.broadcasted_iota(I32, (rows, n_win), 1)
        ok_a = (kp_a <= qpos_r) & (kp_a > qpos_r - WINDOW) & (kp_a >= 0)
        ok_b = new_ok & (past + jn > qpos_r - WINDOW)
        sm_a = jnp.where(ok_a, s_a, NEG)
        sm_b = jnp.where(ok_b, s_b, NEG)
        mx = jnp.maximum(jnp.max(sm_a, axis=-1, keepdims=True), jnp.max(sm_b, axis=-1, keepdims=True))
        e_a = jnp.where(ok_a, jnp.exp(sm_a - mx), 0.0)
        e_b = jnp.where(ok_b, jnp.exp(sm_b - mx), 0.0)
        den = jnp.maximum(jnp.sum(e_a, axis=-1, keepdims=True) + jnp.sum(e_b, axis=-1, keepdims=True), 1e-30)
        o_win = (_dot_nt((e_a / den).astype(BF16), win_ref[1].astype(BF16))
                 + _dot((e_b / den).astype(BF16), newr[:, 5 * KVW:6 * KVW].astype(BF16)))

        gates = _sigmoid(gl_ref[...])
        o_ref[...] = gates[:, 0:1] * oc_ref[...] + gates[:, 1:2] * o_sel + gates[:, 2:3] * o_win


def nsa_sample_attend(q4, sel, ocmp, gl, cache_t, page_table, new_rows, win_t, *, t, past):
    b, rows, _ = q4.shape
    n_pages = page_table.shape[1]
    page = cache_t.shape[3]
    lanes = sel.shape[2]
    tp = new_rows.shape[1]
    n_win = win_t.shape[3]
    nps = ATT_PAGES_PER_STEP
    n_steps = n_pages // nps
    blocks_per_step = nps * page // SEL_BLOCK
    shift = (n_steps - 1) * blocks_per_step
    expand = jnp.asarray((np.arange(lanes + shift)[:, None] - shift) == (np.arange(nps * page)[None, :] // SEL_BLOCK), BF16)
    per_b = lambda shape: pl.BlockSpec((None,) + shape, lambda bi, pi, pt: (bi,) + (0,) * len(shape))
    grid_spec = pltpu.PrefetchScalarGridSpec(
        num_scalar_prefetch=1,
        grid=(b, n_pages // nps),
        in_specs=[per_b((rows, KVW)), per_b((rows, lanes)), per_b((rows, KVW)), per_b((rows, N_BRANCH))]
        + _page_specs(nps, 1, page)
        + [per_b((tp, N_KV_ROWS * KVW)), per_b((2, KVW, n_win)),
           pl.BlockSpec(expand.shape, lambda bi, pi, pt: (0, 0))],
        out_specs=per_b((rows, KVW)),
        scratch_shapes=[pltpu.VMEM((rows, 1), F32), pltpu.VMEM((rows, 1), F32), pltpu.VMEM((rows, KVW), F32)],
    )
    return pl.pallas_call(
        functools.partial(_nsa_sample_attend_kernel, t=t, past=past),
        grid_spec=grid_spec,
        out_shape=jax.ShapeDtypeStruct((b, rows, KVW), F32),
        compiler_params=_cparams("parallel", "arbitrary"),
        name="nsa_sample_attend",
    )(page_table, q4, sel, ocmp, gl, *([cache_t] * nps), new_rows, win_t, expand)


def _route_kernel(x_ref, g_ref, sh_ref, sc_ref, rw_ref, rb_ref, h_ref, r_ref):
    h = _norm_mod(x_ref[...], g_ref[...], sh_ref[...], sc_ref[...])
    h_ref[...] = h
    logits = jnp.dot(h, rw_ref[...], precision=HIGHEST, preferred_element_type=F32) + rb_ref[...]
    lane = lax.broadcasted_iota(I32, logits.shape, 1)
    lane_f = lane.astype(F32)
    lg = jnp.where(lane < N_EXPERTS, logits, NEG)
    v1 = jnp.max(lg, axis=-1, keepdims=True)
    i1 = jnp.min(jnp.where(lg == v1, lane_f, 128.0), axis=-1, keepdims=True)
    lg2 = jnp.where(lane_f == i1, NEG, lg)
    v2 = jnp.max(lg2, axis=-1, keepdims=True)
    i2 = jnp.min(jnp.where(lg2 == v2, lane_f, 128.0), axis=-1, keepdims=True)
    e = jnp.exp(v2 - v1)
    w1 = 1.0 / (1.0 + e)
    w2 = e / (1.0 + e)
    r_ref[...] = jnp.where(lane == 0, i1, jnp.where(lane == 1, i2, jnp.where(lane == 2, w1, jnp.where(lane == 3, w2, 0.0))))


def moe_route(x, g, shift, scale, rw_pad, rb_pad, *, per_token, tm, tpb):
    m, d = x.shape
    sh, sh_spec = _mod_arg(shift, per_token, tm, tpb)
    sc, sc_spec = _mod_arg(scale, per_token, tm, tpb)
    return pl.pallas_call(
        _route_kernel,
        grid=(m // tm,),
        in_specs=[pl.BlockSpec((tm, d), lambda i: (i, 0)), _vec_spec(d), sh_spec, sc_spec,
                  pl.BlockSpec((d, 128), lambda i: (0, 0)), _vec_spec(128)],
        out_specs=[pl.BlockSpec((tm, d), lambda i: (i, 0)), pl.BlockSpec((tm, 128), lambda i: (i, 0))],
        out_shape=[jax.ShapeDtypeStruct((m, d), F32), jax.ShapeDtypeStruct((m, 128), F32)],
        compiler_params=_cparams("parallel"),
        name="moe_route",
    )(x, g.reshape(1, d), sh, sc, rw_pad, rb_pad)


def _moe_kernel(te_ref, nv_ref, x_ref, wg_ref, wu_ref, wd_ref, o_ref, xb_ref, acc_ref):
    i = pl.program_id(0)
    f = pl.program_id(1)

    @pl.when(f == 0)
    def _():
        acc_ref[...] = jnp.zeros(acc_ref.shape, F32)
        xb_ref[...] = x_ref[...].astype(BF16)

    @pl.when(i < nv_ref[0])
    def _():
        x = xb_ref[...]
        act = _silu(_dot(x, wg_ref[...])) * _dot(x, wu_ref[...])
        acc_ref[...] += _dot(act.astype(BF16), wd_ref[...])

    @pl.when(f == pl.num_programs(1) - 1)
    def _():
        o_ref[...] = acc_ref[...]


def moe_experts(xs, tile_expert, n_valid, w_gu_b, w_down_b, *, tm, tf):
    p, d = xs.shape
    edim = w_down_b.shape[1]
    nf = edim // tf
    n_tiles = p // tm

    def wmap(off):
        def index(i, f, te, nv):
            ok = i < nv[0]
            return (te[i], 0, off + jnp.where(ok, f, nf - 1))
        return index

    def dmap(i, f, te, nv):
        return (te[i], jnp.where(i < nv[0], f, nf - 1), 0)

    grid_spec = pltpu.PrefetchScalarGridSpec(
        num_scalar_prefetch=2,
        grid=(n_tiles, nf),
        in_specs=[pl.BlockSpec((tm, d), lambda i, f, te, nv: (i, 0)),
                  pl.BlockSpec((None, d, tf), wmap(0)),
                  pl.BlockSpec((None, d, tf), wmap(nf)),
                  pl.BlockSpec((None, tf, d), dmap)],
        out_specs=pl.BlockSpec((tm, d), lambda i, f, te, nv: (i, 0)),
        scratch_shapes=[pltpu.VMEM((tm, d), BF16), pltpu.VMEM((tm, d), F32)],
    )
    return pl.pallas_call(
        _moe_kernel,
        grid_spec=grid_spec,
        out_shape=jax.ShapeDtypeStruct((p, d), F32),
        compiler_params=_cparams("parallel", "arbitrary"),
        name="moe_experts",
    )(tile_expert, n_valid, xs, w_gu_b, w_gu_b, w_down_b)


def _moe_tables(route, tm):
    m = route.shape[0]
    na = TOP_K * m
    e = route[:, :TOP_K].astype(I32).reshape(na)
    onehot = (e[:, None] == jnp.arange(N_EXPERTS, dtype=I32)[None, :]).astype(I32)
    within = jnp.sum((jnp.cumsum(onehot, axis=0) - onehot) * onehot, axis=1)
    counts = jnp.sum(onehot, axis=0)
    padded = ((counts + tm - 1) // tm) * tm
    ends = jnp.cumsum(padded)
    starts = ends - padded
    dest = starts[e] + within
    n_slots = (-(-na // tm) + N_EXPERTS) * tm
    src_tok = jnp.zeros((n_slots,), I32).at[dest].set(jnp.arange(na, dtype=I32) // TOP_K,
                                                       mode='promise_in_bounds', unique_indices=True)
    n_tiles = n_slots // tm
    n_valid = (ends[-1] // tm).astype(I32)
    tile_start = jnp.arange(n_tiles, dtype=I32) * tm
    tile_expert = jnp.sum((tile_start[:, None] >= ends[None, :]).astype(I32), axis=1)
    last = jnp.take(tile_expert, jnp.maximum(n_valid - 1, 0))
    tile_expert = jnp.where(jnp.arange(n_tiles) < n_valid, tile_expert, last).astype(I32)
    return src_tok, tile_expert, n_valid.reshape(1), dest


def _split_mod(mod):
    return [mod[:, i * D_MODEL:(i + 1) * D_MODEL] for i in range(mod.shape[1] // D_MODEL)]


def kernel(x_prompt, x_sample, c_prompt, c_sample, state_ret, cache_kv, cache_win, page_table, w_mod, b_mod, norm_g, ret_w_in, ret_gn_g, ret_w_out, kv_w_mod, kv_b_mod, kv_norm_g, kv_w, cmp_pos, cmp_w1, cmp_b1, cmp_w2, nsa_w_in, nsa_w_out, ffn_w_gu, ffn_w_down, moe_router_w, moe_router_b, moe_w_gu, moe_w_down):
    bp, t, d = x_prompt.shape
    bs, ts, _ = x_sample.shape
    mp, ms = bp * t, bs * ts
    n_pool, page = cache_kv.shape[:2]
    past = page_table.shape[1] * page
    g, r, hd = NSA_KV_HEADS, NSA_GROUP, NSA_HEAD_DIM

    ret_w_in_b = ret_w_in[0].astype(BF16)
    ret_w_out_b = ret_w_out[0].astype(BF16)
    kv_w_b = kv_w.astype(BF16)
    kv_wt_b = kv_w.T.astype(BF16)
    nq = NSA_HEADS * hd
    gate_w = nsa_w_in[0][:, nq:].reshape(d, g, r * N_BRANCH)
    gate_w = jnp.pad(gate_w, ((0, 0), (0, 0), (0, LANES - r * N_BRANCH))).reshape(d, g * LANES)
    nsa_w_in_b = jnp.concatenate([nsa_w_in[0][:, :nq], gate_w], axis=1).astype(BF16)
    gate_wt = nsa_w_in[0][:, nq:].T.reshape(g, r * N_BRANCH, d)
    gate_wt = jnp.pad(gate_wt, ((0, 0), (0, GATE_ROWS - r * N_BRANCH), (0, 0))).reshape(g * GATE_ROWS, d)
    nsa_wt_b = jnp.concatenate([nsa_w_in[0][:, :nq].T, gate_wt], axis=0).astype(BF16)
    def pad_heads(w):
        return jnp.pad(w.reshape(d, g, hd), ((0, 0), (0, 0), (0, LANES - hd))).reshape(d, g * LANES)
    kpad_w_b = jnp.concatenate([pad_heads(kv_w[:, 2 * KVW:3 * KVW]), pad_heads(kv_w[:, 4 * KVW:5 * KVW])],
                               axis=1).astype(BF16)
    nsa_w_out_b = nsa_w_out[0].astype(BF16)
    ffn_w_gu_b = ffn_w_gu[0].astype(BF16)
    ffn_w_down_b = ffn_w_down[0].astype(BF16)
    moe_w_gu_b = moe_w_gu[0].astype(BF16)
    moe_w_down_b = moe_w_down[0].astype(BF16)
    x_prompt, moe_w_gu_b, moe_w_down_b = lax.optimization_barrier((x_prompt, moe_w_gu_b, moe_w_down_b))
    rw_pad = jnp.pad(moe_router_w[0], ((0, 0), (0, 128 - N_EXPERTS)))
    rb_pad = jnp.pad(moe_router_b[0], (0, 128 - N_EXPERTS)).reshape(1, 128)
    cw = _compress_weights(cmp_pos, cmp_w1, cmp_b1, cmp_w2)

    c_all = jnp.concatenate([c_prompt, c_sample], axis=0)
    mods = [cond_matmul(c_all, w_mod, b_mod, layer) for layer in range(w_mod.shape[0])]
    kv_mod = cond_matmul(c_all, kv_w_mod[None], kv_b_mod[None], 0)

    tm_p = min(512, t)
    tm_f = min(1024, t)
    groups = {
        'p': dict(x=x_prompt.reshape(mp, d), b=bp, t=t, per_token=False, tm=tm_p, tpb=t // tm_p,
                  tm_f=tm_f, tpb_f=t // tm_f,
                  mod=lambda a: a[:bp]),
        's': dict(x=x_sample.reshape(ms, d), b=bs, t=ts, per_token=True, tm=ms, tpb=1, tm_f=ms, tpb_f=1,
                  mod=lambda a: jnp.repeat(a[bp:], ts, axis=0)),
    }
    out = {}

    for name, gr in groups.items():
        kw = dict(per_token=gr['per_token'], tm=gr['tm'], tpb=gr['tpb'])
        m0 = [gr['mod'](a) for a in _split_mod(mods[0])]
        x = gr['x']
        proj = norm_mod_matmul(x, norm_g[0, 0], m0[0], m0[1], ret_w_in_b, tn=1024, **kw)
        proj = proj.reshape(gr['b'], gr['t'], -1)
        if name == 'p':
            pos = jnp.arange(t)
            s0 = jnp.zeros((bp, RET_HEADS, RET_DK, RET_DV), F32)
            o, s_new = retention(proj, pos, s0, ret_gn_g[0], math.gcd(t, RET_CHUNK))
        else:
            pos = past + jnp.arange(ts)
            o, s_new = retention(proj, pos, state_ret[0], ret_gn_g[0], math.gcd(ts, RET_CHUNK))
        out['ret_' + name] = s_new[None]
        x = matmul_norm_residual(o.reshape(-1, d), ret_w_out_b, x, m0[2], norm_g[0, 1], **kw)
        x = ffn_sublayer(x, norm_g[0, 2], m0[3], m0[4], ffn_w_gu_b, ffn_w_down_b, m0[5], norm_g[0, 3],
                         per_token=gr['per_token'], tm=gr['tm_f'], tpb=gr['tpb_f'],
                         tf=_largest_tile(ffn_w_down.shape[1], FFN_HIDDEN_TILE_CAP))
        gr['x1'] = x

    def attention(name):
        gr = groups[name]
        kw = dict(per_token=gr['per_token'], tm=gr['tm'], tpb=gr['tpb'])
        m1 = [gr['mod'](a) for a in _split_mod(mods[1])]
        kvm = [gr['mod'](a) for a in _split_mod(kv_mod)]
        gr['m1'] = m1
        x = gr['x1']
        b_, t_ = gr['b'], gr['t']
        if name == 'p':
            kvt, wint = norm_mod_matmul_t(x, kv_norm_g, kvm[0], kvm[1], kv_wt_b,
                                          (N_PAGED_ROWS * KVW, (N_KV_ROWS - N_PAGED_ROWS) * KVW), b=b_, tm=gr['tm'])
            out['kv_p'] = kvt.reshape(b_, N_PAGED_ROWS, g, hd, t_).transpose(0, 4, 1, 2, 3)
            n_keep = min(WINDOW, t_)
            out['win_p'] = wint[:, :, t_ - n_keep:].reshape(b_, 2, g, hd, n_keep).transpose(0, 4, 1, 2, 3)
            kpad = norm_mod_matmul(x, kv_norm_g, kvm[0], kvm[1], kpad_w_b, out_dtype=BF16, **kw)
            qt, glt = norm_mod_matmul_t(x, norm_g[1, 0], m1[0], m1[1], nsa_wt_b, (nq, g * GATE_ROWS),
                                        b=b_, tm=gr['tm'])
            cmp = compress_prompt(kvt, cw).reshape(2, b_, -1, g, hd)
            kc = jnp.pad(cmp[0].transpose(0, 2, 1, 3), ((0, 0), (0, 0), (0, 0), (0, LANES - hd)))
            o = nsa_prompt(qt, glt, kc, cmp[1].transpose(0, 2, 3, 1), kpad, kvt, wint)
        else:
            proj = norm_mod_matmul(x, norm_g[1, 0], m1[0], m1[1], nsa_w_in_b, **kw)
            rows = norm_mod_matmul(x, kv_norm_g, kvm[0], kvm[1], kv_w_b, **kw)
            q = (proj[:, :NSA_HEADS * hd] * (hd ** -0.5)).reshape(b_, t_, g, r, hd)
            gl = proj[:, NSA_HEADS * hd:].reshape(b_, t_, g, LANES)[..., :r * N_BRANCH]
            gl = gl.reshape(b_, t_, g, r, N_BRANCH)
            rows3 = rows.reshape(b_, t_, N_KV_ROWS * KVW)
            rows6 = rows.reshape(b_, t_, N_KV_ROWS, g, hd)
            out['kv_s'] = rows6[:, :, :N_PAGED_ROWS]
            cache_t = cache_kv.transpose(0, 2, 3, 4, 1).reshape(n_pool, N_PAGED_ROWS, KVW, page)
            win_t5 = cache_win.transpose(0, 2, 3, 4, 1)
            n_win = cache_win.shape[1]
            new_win_t = rows6[:, :, N_PAGED_ROWS:].transpose(0, 2, 3, 4, 1)
            out['win_s'] = jnp.concatenate([win_t5, new_win_t], axis=-1)[..., t_:].transpose(0, 4, 1, 2, 3)
            new_sub = jnp.pad(rows3[:, :, :2 * KVW], ((0, 0), (0, CMP_STRIDE - t_), (0, 0)))
            new_sub = new_sub.reshape(b_, CMP_STRIDE, 2, KVW).transpose(2, 0, 1, 3)
            cmp = compress_sample(cache_t, page_table, new_sub, cw)
            eye = jnp.eye(g, dtype=F32)
            qrows = q.transpose(0, 2, 3, 1, 4).reshape(b_, g, r * t_, hd)
            q4 = jnp.einsum('bgxd,gj->bgxjd', qrows, eye).reshape(b_, g * r * t_, KVW)
            glr = gl.transpose(0, 2, 3, 1, 4).reshape(b_, g * r * t_, N_BRANCH)
            ocmp, sel = nsa_sample_select(q4, cmp, t=t_, past=past)
            o4 = nsa_sample_attend(q4, sel, ocmp, glr, cache_t, page_table, rows3,
                                   win_t5.reshape(b_, 2, KVW, n_win), t=t_, past=past)
            o4 = o4.reshape(b_, g, r, t_, g, hd)
            o = jnp.einsum('bgrtjd,gj->btgrd', o4, eye).reshape(ms, NSA_HEADS * hd)
        gr['x2'] = matmul_norm_residual(o, nsa_w_out_b, x, m1[2], norm_g[1, 1], **kw)

    take = lambda a, idx: a.at[idx].get(mode='promise_in_bounds')

    def moe_dispatch(name):
        gr = groups[name]
        kw = dict(per_token=gr['per_token'], tm=gr['tm'], tpb=gr['tpb'])
        h, route = moe_route(gr['x2'], norm_g[1, 2], gr['m1'][3], gr['m1'][4], rw_pad, rb_pad, **kw)
        tm_e = 512 if h.shape[0] >= 4096 else 128
        src_tok, tile_expert, n_valid, dest = _moe_tables(route, tm_e)
        return dict(xs=take(h, src_tok), route=route, tile_expert=tile_expert, n_valid=n_valid,
                    dest=dest.reshape(-1, TOP_K), tm_e=tm_e)

    def moe_finish(name, dp):
        gr = groups[name]
        kw = dict(per_token=gr['per_token'], tm=gr['tm'], tpb=gr['tpb'])
        ys = moe_experts(dp['xs'], dp['tile_expert'], dp['n_valid'], moe_w_gu_b, moe_w_down_b, tm=dp['tm_e'],
                         tf=_largest_tile(moe_w_down.shape[2], MOE_HIDDEN_TILE_CAP))
        out['y_' + name] = moe_combine(take(ys, dp['dest'][:, 0]), take(ys, dp['dest'][:, 1]), dp['route'],
                                       gr['x2'], gr['m1'][5], norm_g[1, 3], **kw)

    attention('p')
    dispatch_p = moe_dispatch('p')
    attention('s')
    moe_finish('p', dispatch_p)
    moe_finish('s', moe_dispatch('s'))

    return (out['y_p'].reshape(bp, t, d), out['y_s'].reshape(bs, ts, d),
            out['ret_p'], out['ret_s'], out['kv_p'], out['kv_s'], out['win_p'], out['win_s'])
```

```python
import functools
import math

import numpy as np
import jax
import jax.numpy as jnp
from jax import lax
from jax.experimental import pallas as pl
from jax.experimental.pallas import tpu as pltpu

F32 = jnp.float32
BF16 = jnp.bfloat16
I32 = jnp.int32

D_MODEL = 1024
N_MOD = 6
RET_HEADS = 4
RET_DK = 256
RET_DV = 256
RET_CHUNK = 128
ROPE_BASE = 10000.0
NSA_HEADS = 16
NSA_KV_HEADS = 4
NSA_GROUP = 4
NSA_HEAD_DIM = 64
N_BRANCH = 3
N_KV_ROWS = 6
N_PAGED_ROWS = 4
CMP_BLOCK = 32
CMP_STRIDE = 16
CMP_HIDDEN = 128
SEL_BLOCK = 64
N_SEL = 16
FORCED_SCORE = 1.0e4
WINDOW = 512
Q_BLOCK = 128
N_EXPERTS = 8
TOP_K = 2
EPS = 1e-6

NEG = -1.0e30
KVW = NSA_KV_HEADS * NSA_HEAD_DIM
VMEM_LIMIT_BYTES = 56 * 1024 * 1024
HIGHEST = lax.Precision.HIGHEST


LANES = 128
FFN_HIDDEN_TILE_CAP = 1408
MOE_HIDDEN_TILE_CAP = 896


def _largest_tile(n, cap):
    best = LANES
    for k in range(LANES, cap + 1, LANES):
        if n % k == 0:
            best = k
    return best


def _cparams(*sem):
    return pltpu.CompilerParams(dimension_semantics=sem, vmem_limit_bytes=VMEM_LIMIT_BYTES)


def _sigmoid(x):
    return 1.0 / (1.0 + jnp.exp(-x))


def _silu(x):
    return x * _sigmoid(x)


def _gelu_tanh(x):
    return x * (0.5 * (1.0 + jnp.tanh(math.sqrt(2.0 / math.pi) * (x + 0.044715 * (x * x * x)))))


def _norm_mod(x, g, shift, scale):
    ms = jnp.mean(x * x, axis=-1, keepdims=True)
    return (x * lax.rsqrt(ms + EPS) * g) * (1.0 + scale) + shift


def _rms_residual(x, gate, o, g):
    ms = jnp.mean(o * o, axis=-1, keepdims=True)
    return x + gate * (o * lax.rsqrt(ms + EPS) * g)


def _dot(a, b):
    return jnp.dot(a, b, preferred_element_type=F32)


def _dot_nt(a, b):
    return lax.dot_general(a, b, (((1,), (1,)), ((), ())), preferred_element_type=F32)


def _masked_softmax(s, mask):
    sm = jnp.where(mask, s, NEG)
    m = jnp.max(sm, axis=-1, keepdims=True)
    e = jnp.where(mask, jnp.exp(sm - m), 0.0)
    return e / jnp.maximum(jnp.sum(e, axis=-1, keepdims=True), 1e-30)


def _flash_init(m_ref, l_ref, acc_ref):
    m_ref[...] = jnp.full(m_ref.shape, NEG, F32)
    l_ref[...] = jnp.zeros(l_ref.shape, F32)
    acc_ref[...] = jnp.zeros(acc_ref.shape, F32)


def _flash_update(s, mask, v_b, m_ref, l_ref, acc_ref):
    sm = jnp.where(mask, s, NEG)
    m_old = m_ref[...]
    m_new = jnp.maximum(m_old, jnp.max(sm, axis=-1, keepdims=True))
    alpha = jnp.exp(m_old - m_new)
    p = jnp.where(mask, jnp.exp(sm - m_new), 0.0)
    l_ref[...] = alpha * l_ref[...] + jnp.sum(p, axis=-1, keepdims=True)
    acc_ref[...] = alpha * acc_ref[...] + _dot(p.astype(BF16), v_b)
    m_ref[...] = m_new


def _flash_result(l_ref, acc_ref):
    return acc_ref[...] / jnp.maximum(l_ref[...], 1e-30)


def _select_blocks(imp, cur, n_sel):
    rows, lanes = imp.shape
    blk = lax.broadcasted_iota(I32, (rows, lanes), 1)
    valid = (blk <= cur) & (blk < n_sel)
    forced = (blk == 0) | (blk == cur) | (blk == cur - 1)
    score = jnp.where(valid, jnp.where(forced, FORCED_SCORE, imp), -1.0)
    rank = jnp.zeros((rows, lanes), F32)
    for i in range(n_sel):
        ci = score[:, i:i + 1]
        beats = (ci > score) | ((ci == score) & (blk > i))
        rank = rank + jnp.where(beats, 1.0, 0.0)
    return jnp.where(valid & (rank < float(min(N_SEL, n_sel))), 1.0, 0.0)


def _cond_kernel(c_ref, w_ref, b_ref, o_ref):
    sc = _silu(c_ref[...])
    o_ref[...] = _dot(sc.astype(BF16), w_ref[...].astype(BF16)) + b_ref[...]


def cond_matmul(c, w, b, layer):
    bc, d = c.shape
    n_layers, _, n = w.shape
    tn = 1024
    return pl.pallas_call(
        _cond_kernel,
        grid=(n // tn,),
        in_specs=[pl.BlockSpec((bc, d), lambda j: (0, 0)),
                  pl.BlockSpec((None, d, tn), lambda j: (layer, 0, j)),
                  pl.BlockSpec((None, 1, tn), lambda j: (layer, 0, j))],
        out_specs=pl.BlockSpec((bc, tn), lambda j: (0, j)),
        out_shape=jax.ShapeDtypeStruct((bc, n), F32),
        compiler_params=_cparams("parallel"),
        name="cond_matmul",
    )(c, w, b.reshape(n_layers, 1, n))


def _mod_arg(m, per_token, tm, tiles_per_batch):
    d = m.shape[-1]
    if per_token:
        return m, pl.BlockSpec((tm, d), lambda i, *_: (i, 0))
    return m[:, None, :], pl.BlockSpec((None, 1, d), lambda i, *_: (i // tiles_per_batch, 0, 0))


def _vec_spec(d):
    return pl.BlockSpec((1, d), lambda i, *_: (0, 0))


def _nmm_kernel(x_ref, g_ref, sh_ref, sc_ref, w_ref, o_ref, h_ref):
    @pl.when(pl.program_id(1) == 0)
    def _():
        h_ref[...] = _norm_mod(x_ref[...], g_ref[...], sh_ref[...], sc_ref[...]).astype(BF16)

    o_ref[...] = _dot(h_ref[...], w_ref[...]).astype(o_ref.dtype)


def norm_mod_matmul(x, g, shift, scale, w_b, *, per_token, tm, tpb, tn=None, out_dtype=F32):
    m, d = x.shape
    n = w_b.shape[1]
    tn = n if tn is None else tn
    sh, sh_spec = _mod_arg(shift, per_token, tm, tpb)
    sc, sc_spec = _mod_arg(scale, per_token, tm, tpb)
    return pl.pallas_call(
        _nmm_kernel,
        grid=(m // tm, n // tn),
        in_specs=[pl.BlockSpec((tm, d), lambda i, j: (i, 0)), _vec_spec(d), sh_spec, sc_spec,
                  pl.BlockSpec((d, tn), lambda i, j: (0, j))],
        out_specs=pl.BlockSpec((tm, tn), lambda i, j: (i, j)),
        out_shape=jax.ShapeDtypeStruct((m, n), out_dtype),
        scratch_shapes=[pltpu.VMEM((tm, d), BF16)],
        compiler_params=_cparams("parallel", "arbitrary"),
        name="norm_mod_matmul",
    )(x, g.reshape(1, d), sh, sc, w_b)


def _nmm_t_kernel(x_ref, g_ref, sh_ref, sc_ref, wt_ref, *o_refs):
    h = _norm_mod(x_ref[...], g_ref[...], sh_ref[...], sc_ref[...]).astype(BF16)
    o = _dot_nt(wt_ref[...], h)
    row = 0
    for o_ref in o_refs:
        o_ref[...] = o[row:row + o_ref.shape[0]]
        row += o_ref.shape[0]


def norm_mod_matmul_t(x, g, shift, scale, wt_b, splits, *, b, tm):
    m, d = x.shape
    t = m // b
    tpb = t // tm
    n = wt_b.shape[0]
    sh, sh_spec = _mod_arg(shift, False, tm, tpb)
    sc, sc_spec = _mod_arg(scale, False, tm, tpb)
    return pl.pallas_call(
        _nmm_t_kernel,
        grid=(m // tm,),
        in_specs=[pl.BlockSpec((tm, d), lambda i: (i, 0)), _vec_spec(d), sh_spec, sc_spec,
                  pl.BlockSpec((n, d), lambda i: (0, 0))],
        out_specs=[pl.BlockSpec((None, ni, tm), lambda i: (i // tpb, 0, i % tpb)) for ni in splits],
        out_shape=[jax.ShapeDtypeStruct((b, ni, t), F32) for ni in splits],
        compiler_params=_cparams("parallel"),
        name="norm_mod_matmul_t",
    )(x, g.reshape(1, d), sh, sc, wt_b)


def _mnr_kernel(a_ref, w_ref, x_ref, gate_ref, g_ref, o_ref):
    o = _dot(a_ref[...].astype(BF16), w_ref[...])
    o_ref[...] = _rms_residual(x_ref[...], gate_ref[...], o, g_ref[...])


def matmul_norm_residual(a, w_b, x, gate, g, *, per_token, tm, tpb):
    m, k = a.shape
    d = w_b.shape[1]
    gt, gt_spec = _mod_arg(gate, per_token, tm, tpb)
    return pl.pallas_call(
        _mnr_kernel,
        grid=(m // tm,),
        in_specs=[pl.BlockSpec((tm, k), lambda i: (i, 0)),
                  pl.BlockSpec((k, d), lambda i: (0, 0)),
                  pl.BlockSpec((tm, d), lambda i: (i, 0)), gt_spec, _vec_spec(d)],
        out_specs=pl.BlockSpec((tm, d), lambda i: (i, 0)),
        out_shape=jax.ShapeDtypeStruct((m, d), F32),
        compiler_params=_cparams("parallel"),
        name="matmul_norm_residual",
    )(a, w_b, x, gt, g.reshape(1, d))


def _combine_kernel(y1_ref, y2_ref, r_ref, x_ref, gate_ref, g_ref, o_ref):
    route = r_ref[...]
    y = route[:, TOP_K:TOP_K + 1] * y1_ref[...] + route[:, TOP_K + 1:TOP_K + 2] * y2_ref[...]
    o_ref[...] = _rms_residual(x_ref[...], gate_ref[...], y, g_ref[...])


def moe_combine(y1, y2, route, x, gate, g, *, per_token, tm, tpb):
    m, d = x.shape
    gt, gt_spec = _mod_arg(gate, per_token, tm, tpb)
    row = lambda n: pl.BlockSpec((tm, n), lambda i: (i, 0))
    return pl.pallas_call(
        _combine_kernel,
        grid=(m // tm,),
        in_specs=[row(d), row(d), row(route.shape[1]), row(d), gt_spec, _vec_spec(d)],
        out_specs=row(d),
        out_shape=jax.ShapeDtypeStruct((m, d), F32),
        compiler_params=_cparams("parallel"),
        name="moe_combine",
    )(y1, y2, route, x, gt, g.reshape(1, d))


def _ffn_kernel(x_ref, g2_ref, sh_ref, sc_ref, wg_ref, wu_ref, wd_ref, gate_ref, g3_ref, o_ref,
                h_ref, acc_ref):
    f = pl.program_id(1)

    @pl.when(f == 0)
    def _():
        h_ref[...] = _norm_mod(x_ref[...], g2_ref[...], sh_ref[...], sc_ref[...]).astype(BF16)
        acc_ref[...] = jnp.zeros(acc_ref.shape, F32)

    h = h_ref[...]
    act = _silu(_dot(h, wg_ref[...])) * _dot(h, wu_ref[...])
    acc_ref[...] += _dot(act.astype(BF16), wd_ref[...])

    @pl.when(f == pl.num_programs(1) - 1)
    def _():
        o_ref[...] = _rms_residual(x_ref[...], gate_ref[...], acc_ref[...], g3_ref[...])


def ffn_sublayer(x, g2, shift, scale, w_gu_b, w_down_b, gate, g3, *, per_token, tm, tpb, tf):
    m, d = x.shape
    fdim = w_down_b.shape[0]
    nf = fdim // tf
    sh, sh_spec = _mod_arg(shift, per_token, tm, tpb)
    sc, sc_spec = _mod_arg(scale, per_token, tm, tpb)
    gt, gt_spec = _mod_arg(gate, per_token, tm, tpb)
    return pl.pallas_call(
        _ffn_kernel,
        grid=(m // tm, nf),
        in_specs=[pl.BlockSpec((tm, d), lambda i, f: (i, 0)), _vec_spec(d), sh_spec, sc_spec,
                  pl.BlockSpec((d, tf), lambda i, f: (0, f)),
                  pl.BlockSpec((d, tf), lambda i, f: (0, nf + f)),
                  pl.BlockSpec((tf, d), lambda i, f: (f, 0)),
                  gt_spec, _vec_spec(d)],
        out_specs=pl.BlockSpec((tm, d), lambda i, f: (i, 0)),
        out_shape=jax.ShapeDtypeStruct((m, d), F32),
        scratch_shapes=[pltpu.VMEM((tm, d), BF16), pltpu.VMEM((tm, d), F32)],
        compiler_params=_cparams("parallel", "arbitrary"),
        name="ffn_sublayer",
    )(x, g2.reshape(1, d), sh, sc, w_gu_b, w_gu_b, w_down_b, gt, g3.reshape(1, d))


def _ret_kernel(q_ref, k_ref, v_ref, gt_ref, cos_ref, sin_ref, dm_ref, xi_ref, zt_ref, gc_ref,
                gn_ref, s0_ref, o_ref, s_ref):
    @pl.when(pl.program_id(1) == 0)
    def _():
        s_ref[...] = s0_ref[...]

    rows = q_ref.shape[0]

    def pad(x):
        if rows == RET_CHUNK:
            return x
        return jnp.concatenate([x, jnp.zeros((RET_CHUNK - rows, x.shape[1]), x.dtype)], axis=0)

    cos = pad(cos_ref[...])
    sin = pad(sin_ref[...])
    half = RET_DK // 2

    def rot(x):
        x1 = x[:, :half]
        x2 = x[:, half:]
        return jnp.concatenate([x1 * cos - x2 * sin, x2 * cos + x1 * sin], axis=-1)

    for h in range(RET_HEADS):
        kcols = slice(h * RET_DK, (h + 1) * RET_DK)
        vcols = slice(h * RET_DV, (h + 1) * RET_DV)
        q = rot(pad(q_ref[:, kcols]))
        k = rot(pad(k_ref[:, kcols])) * (RET_DK ** -0.5)
        qb = q.astype(BF16)
        kb = k.astype(BF16)
        vb = pad(v_ref[:, vcols]).astype(BF16)
        state = s_ref[h]
        inner = _dot_nt(qb, kb) * dm_ref[h]
        o = _dot(inner.astype(BF16), vb) + _dot(qb, state.astype(BF16)) * xi_ref[h]
        kz = (k * zt_ref[h]).astype(BF16)
        upd = lax.dot_general(kz, vb, (((0,), (0,)), ((), ())), preferred_element_type=F32)
        s_ref[h] = gc_ref[h] * state + upd
        mu = jnp.mean(o, axis=-1, keepdims=True)
        dev = o - mu
        var = jnp.mean(dev * dev, axis=-1, keepdims=True)
        on = dev * lax.rsqrt(var + EPS) * gn_ref[:, vcols]
        o_ref[:, vcols] = _silu(gt_ref[:, vcols]) * on[:rows]


def _ret_tables(chunk, rows):
    h = RET_HEADS
    log_g = jnp.log(1.0 - jnp.exp(jnp.linspace(math.log(1.0 / 32), math.log(1.0 / 512), h, dtype=F32)))
    i = jnp.arange(chunk, dtype=F32)
    diff = i[:, None] - i[None, :]
    dmat = jnp.where(diff >= 0, jnp.exp(log_g[:, None, None] * jnp.maximum(diff, 0.0)), 0.0)
    xi = jnp.exp(log_g[:, None] * (i + 1.0))
    zeta = jnp.exp(log_g[:, None] * (chunk - 1.0 - i))
    gch = jnp.exp(log_g * chunk)
    pad = rows - chunk
    dmat = jnp.pad(dmat, ((0, 0), (0, pad), (0, pad)))
    xi = jnp.pad(xi, ((0, 0), (0, pad)))[..., None]
    zeta = jnp.pad(zeta, ((0, 0), (0, pad)))[..., None]
    gch = jnp.broadcast_to(gch[:, None, None], (h, 1, RET_DV))
    return dmat, xi, zeta, gch


def _rope_tables(pos, rows):
    half = RET_DK // 2
    inv = ROPE_BASE ** (-jnp.arange(half, dtype=F32) / half)
    ang = pos.astype(F32)[:, None] * inv[None, :]
    pad = rows - pos.shape[0]
    return jnp.pad(jnp.cos(ang), ((0, pad), (0, 0))), jnp.pad(jnp.sin(ang), ((0, pad), (0, 0)))


def retention(proj, pos, s0, gn_g, chunk):
    b, t, _ = proj.shape
    c = min(t, RET_CHUNK)
    n = t // c
    h = RET_HEADS
    dmat, xi, zeta, gch = _ret_tables(chunk, RET_CHUNK)
    cos, sin = _rope_tables(pos, t)
    col = lambda j: pl.BlockSpec((None, c, h * RET_DK), lambda bi, ni: (bi, ni, j))
    const = lambda a: pl.BlockSpec(a.shape, lambda bi, ni: (0,) * a.ndim)
    state_spec = pl.BlockSpec((None, h, RET_DK, RET_DV), lambda bi, ni: (bi, 0, 0, 0))
    gn = gn_g.reshape(1, -1)
    o, s = pl.pallas_call(
        _ret_kernel,
        grid=(b, n),
        in_specs=[col(0), col(1), col(2), col(3),
                  pl.BlockSpec((c, RET_DK // 2), lambda bi, ni: (ni, 0)),
                  pl.BlockSpec((c, RET_DK // 2), lambda bi, ni: (ni, 0)),
                  const(dmat), const(xi), const(zeta), const(gch), const(gn), state_spec],
        out_specs=[pl.BlockSpec((None, c, h * RET_DV), lambda bi, ni: (bi, ni, 0)), state_spec],
        out_shape=[jax.ShapeDtypeStruct((b, t, h * RET_DV), F32),
                   jax.ShapeDtypeStruct((b, h, RET_DK, RET_DV), F32)],
        compiler_params=_cparams("parallel", "arbitrary"),
        name="retention",
    )(proj, proj, proj, proj, cos, sin, dmat, xi, zeta, gch, gn, s0)
    return o, s


def _compress_weights(cmp_pos, cmp_w1, cmp_b1, cmp_w2):
    g, d, hd = NSA_KV_HEADS, NSA_HEAD_DIM, CMP_HIDDEN
    eye = jnp.eye(g, dtype=F32)
    w1 = cmp_w1.reshape(2, CMP_BLOCK, d, hd)
    bd1 = jnp.einsum('ksdh,gj->ksgdjh', w1, eye).reshape(2, CMP_BLOCK, g * d, g * hd).astype(BF16)
    bd2 = jnp.einsum('khd,gj->kghjd', cmp_w2, eye).reshape(2, g * hd, g * d).astype(BF16)
    pos = jnp.tile(cmp_pos[:, :, None, :], (1, 1, g, 1)).reshape(2, CMP_BLOCK, 1, g * d)
    b1 = jnp.tile(cmp_b1[:, None, :], (1, g, 1)).reshape(2, 1, g * hd)
    return bd1, bd2, pos, b1


def _compress_finish(p0, p1, b1, bd2):
    rows = p0.shape[0]
    hid = b1 + p0 + pltpu.roll(p1, rows - 1, 0)
    return _dot(_gelu_tanh(hid).astype(BF16), bd2)


CMP_PAGES_PER_STEP = 8
ATT_PAGES_PER_STEP = 16


def _tokens_to_sublanes(xt_ref, tr_ref):
    for half in range(KVW // LANES):
        tr_ref[half] = xt_ref[half * LANES:(half + 1) * LANES, :].T


def _cmp_prompt_kernel(xt_ref, bd1_ref, bd2_ref, pos_ref, b1_ref, o_ref, tr_ref):
    n_sub = xt_ref.shape[1] // CMP_STRIDE
    hid = bd1_ref.shape[-1]
    _tokens_to_sublanes(xt_ref, tr_ref)
    p0 = jnp.zeros((n_sub, hid), F32)
    p1 = jnp.zeros((n_sub, hid), F32)
    for s in range(CMP_STRIDE):
        xs = jnp.concatenate([tr_ref[half, pl.ds(s, n_sub, stride=CMP_STRIDE), :]
                              for half in range(KVW // LANES)], axis=-1)
        p0 = p0 + _dot((xs + pos_ref[s]).astype(BF16), bd1_ref[s])
        p1 = p1 + _dot((xs + pos_ref[CMP_STRIDE + s]).astype(BF16), bd1_ref[CMP_STRIDE + s])
    o_ref[...] = _compress_finish(p0, p1, b1_ref[...], bd2_ref[...])


def compress_prompt(kvt, cw):
    bd1, bd2, pos, b1 = cw
    b, _, t = kvt.shape
    n_sub = t // CMP_STRIDE
    return pl.pallas_call(
        _cmp_prompt_kernel,
        grid=(b, 2),
        in_specs=[pl.BlockSpec((None, KVW, t), lambda bi, ki: (bi, ki, 0)),
                  pl.BlockSpec((None,) + bd1.shape[1:], lambda bi, ki: (ki, 0, 0, 0)),
                  pl.BlockSpec((None,) + bd2.shape[1:], lambda bi, ki: (ki, 0, 0)),
                  pl.BlockSpec((None,) + pos.shape[1:], lambda bi, ki: (ki, 0, 0, 0)),
                  pl.BlockSpec((None,) + b1.shape[1:], lambda bi, ki: (ki, 0, 0))],
        out_specs=pl.BlockSpec((None, None, n_sub, KVW), lambda bi, ki: (ki, bi, 0, 0)),
        out_shape=jax.ShapeDtypeStruct((2, b, n_sub, KVW), F32),
        scratch_shapes=[pltpu.VMEM((KVW // LANES, t, LANES), F32)],
        compiler_params=_cparams("parallel", "parallel"),
        name="compress_prompt",
    )(kvt, bd1, bd2, pos, b1)


def _cmp_sample_kernel(pt_ref, *refs):
    page_refs = refs[:CMP_PAGES_PER_STEP]
    new_ref, bd1_ref, bd2_ref, pos_ref, b1_ref, o_ref, sub_ref, tr_ref = refs[CMP_PAGES_PER_STEP:]
    p = pl.program_id(1)
    page = page_refs[0].shape[2]
    per_page = page // CMP_STRIDE
    n_cmp = o_ref.shape[1]
    for j, page_ref in enumerate(page_refs):
        row0 = pl.multiple_of((p * CMP_PAGES_PER_STEP + j) * per_page, per_page)
        for kind in range(2):
            _tokens_to_sublanes(page_ref.at[kind], tr_ref.at[kind])
            for s in range(CMP_STRIDE):
                for half in range(KVW // LANES):
                    piece = tr_ref[kind, half, pl.ds(s, per_page, stride=CMP_STRIDE), :]
                    sub_ref[kind, s, pl.ds(row0, per_page), half * LANES:(half + 1) * LANES] = piece

    @pl.when(p == pl.num_programs(1) - 1)
    def _():
        n_rows = sub_ref.shape[2]
        tail = n_rows - n_cmp
        hid = bd1_ref.shape[-1]
        for kind in range(2):
            p0 = jnp.zeros((n_rows, hid), F32)
            p1 = jnp.zeros((n_rows, hid), F32)
            for s in range(CMP_STRIDE):
                sub_ref[kind, s, pl.ds(n_cmp, tail), :] = jnp.broadcast_to(new_ref[kind, s:s + 1, :], (tail, KVW))
                xs = sub_ref[kind, s]
                p0 = p0 + _dot((xs + pos_ref[kind, s]).astype(BF16), bd1_ref[kind, s])
                p1 = p1 + _dot((xs + pos_ref[kind, CMP_STRIDE + s]).astype(BF16), bd1_ref[kind, CMP_STRIDE + s])
            o_ref[kind] = _compress_finish(p0, p1, b1_ref[kind], bd2_ref[kind])[:n_cmp]


def _page_specs(n_per_step, kind_block, page):
    def spec(j):
        return pl.BlockSpec((None, 2, KVW, page),
                            lambda bi, pi, pt: (pt[bi, pi * n_per_step + j], kind_block, 0, 0))
    return [spec(j) for j in range(n_per_step)]


def compress_sample(cache_t, page_table, new_sub, cw):
    bd1, bd2, pos, b1 = cw
    b, n_pages = page_table.shape
    page = cache_t.shape[3]
    n_cmp = n_pages * page // CMP_STRIDE
    nps = CMP_PAGES_PER_STEP
    const = lambda a: pl.BlockSpec(a.shape, lambda bi, pi, pt: (0,) * a.ndim, pipeline_mode=pl.Buffered(1))
    grid_spec = pltpu.PrefetchScalarGridSpec(
        num_scalar_prefetch=1,
        grid=(b, n_pages // nps),
        in_specs=_page_specs(nps, 0, page) + [
            pl.BlockSpec((2, None, CMP_STRIDE, KVW), lambda bi, pi, pt: (0, bi, 0, 0)),
            const(bd1), const(bd2), const(pos), const(b1)],
        out_specs=pl.BlockSpec((2, None, n_cmp, KVW), lambda bi, pi, pt: (0, bi, 0, 0)),
        scratch_shapes=[pltpu.VMEM((2, CMP_STRIDE, n_cmp + 8, KVW), F32),
                        pltpu.VMEM((2, KVW // LANES, page, LANES), F32)],
    )
    return pl.pallas_call(
        _cmp_sample_kernel,
        grid_spec=grid_spec,
        out_shape=jax.ShapeDtypeStruct((2, b, n_cmp, KVW), F32),
        compiler_params=_cparams("parallel", "arbitrary"),
        name="compress_sample",
    )(page_table, *([cache_t] * nps), new_sub, bd1, bd2, pos, b1)


def _cover_matrix(n_cmp, n_sel, rows, cols):
    i = np.arange(n_cmp)[:, None]
    j = np.arange(n_sel)[None, :]
    cover = (i * CMP_STRIDE < (j + 1) * SEL_BLOCK) & (i * CMP_STRIDE + CMP_BLOCK > j * SEL_BLOCK)
    out = np.zeros((rows, cols), np.float32)
    out[:n_cmp, :n_sel] = cover
    return jnp.asarray(out)


SEL_CHUNK = 256
SEL_CLASS = 512


def _select_blocks_t(imp_t, cur, n_sel):
    nb, nq = imp_t.shape
    blk = lax.broadcasted_iota(I32, (nb, nq), 0)
    valid = (blk <= cur) & (blk < n_sel)
    forced = (blk == 0) | (blk == cur) | (blk == cur - 1)
    score = jnp.where(valid, jnp.where(forced, FORCED_SCORE, imp_t), -1.0)
    rank = jnp.zeros((nb, nq), F32)
    for i in range(n_sel):
        ci = score[i:i + 1, :]
        beats = (ci > score) | ((ci == score) & (blk > i))
        rank = rank + jnp.where(beats, 1.0, 0.0)
    return jnp.where(valid & (rank < float(min(N_SEL, n_sel))), 1.0, 0.0)


def _nsa_prompt_kernel(qt_ref, glt_ref, kc_ref, vct_ref, covt_ref, kblk_ref, wb_ref, ks_ref, vst_ref, kw_ref, vwt_ref,
                       o_ref, s_ref, osel_ref, *, n_cmp, n_sel):
    tq = qt_ref.shape[1]
    r, d = NSA_GROUP, NSA_HEAD_DIM
    cols = r * tq
    t = ks_ref.shape[0]
    q0 = pl.program_id(2) * tq
    qt = qt_ref[...] * (d ** -0.5)
    qcat = jnp.concatenate([qt[h * d:(h + 1) * d, :] for h in range(r)], axis=-1)
    qb = jnp.concatenate([qcat, jnp.zeros_like(qcat)], axis=0).astype(BF16)
    gl = _sigmoid(glt_ref[...])

    def gate(branch):
        return jnp.concatenate([jnp.broadcast_to(gl[h * N_BRANCH + branch:h * N_BRANCH + branch + 1, :], (d, tq))
                                for h in range(r)], axis=0)

    def stack(o_t):
        return jnp.concatenate([o_t[:, h * tq:(h + 1) * tq] for h in range(r)], axis=0)

    def tile(x):
        return jnp.concatenate([x] * r, axis=-1)

    ones_rows = 16

    def with_ones(vt):
        return jnp.concatenate([vt.astype(BF16), jnp.ones((ones_rows, vt.shape[1]), BF16)], axis=0)

    qpos = q0 + lax.broadcasted_iota(I32, (1, tq), 1)

    span = min(WINDOW + tq, t)
    w0 = pl.multiple_of(jnp.maximum(q0 + tq - span, 0), tq)
    sm = _dot(kw_ref[pl.ds(w0, span), :], qb) + tile(wb_ref[(q0 - w0) // tq])
    p = jnp.exp(sm - jnp.max(sm, axis=0, keepdims=True))
    o_win_t = _dot(with_ones(vwt_ref[:, pl.ds(w0, span)]), p.astype(BF16))
    out = gate(2) * stack(o_win_t[:d] / jnp.maximum(o_win_t[d:d + 1], 1e-30))

    ncp = kc_ref.shape[0]
    nn = lax.broadcasted_iota(I32, (ncp, tq), 0)
    ok = tile(jnp.where((nn * CMP_STRIDE + (CMP_BLOCK - 1) <= qpos) & (nn < n_cmp), 1.0, 0.0))
    sm = _dot(kc_ref[...].astype(BF16), qb) + (ok - 1.0) * (-NEG)
    e = jnp.exp(sm - jnp.max(sm, axis=0, keepdims=True)) * ok
    p_cmp = e / jnp.maximum(jnp.sum(e, axis=0, keepdims=True), 1e-30)
    out = out + gate(0) * stack(_dot(vct_ref[...].astype(BF16), p_cmp.astype(BF16)))

    psum_t = p_cmp[:, :tq]
    for h in range(1, r):
        psum_t = psum_t + p_cmp[:, h * tq:(h + 1) * tq]
    imp_t = jnp.dot(covt_ref[...], psum_t, precision=HIGHEST, preferred_element_type=F32)
    sel_t = _select_blocks_t(imp_t, qpos >> 6, n_sel)
    nb = sel_t.shape[0]
    q_sel = jnp.concatenate([qcat, tile((sel_t - 1.0) * (-NEG)), jnp.zeros((LANES - d - nb, cols), F32)],
                            axis=0).astype(BF16)

    groups = SEL_CHUNK // 8
    sel_class = SEL_CLASS if t % SEL_CLASS == 0 else t

    def sel_branch(n_keys):
        m_run = jnp.full((8, cols), NEG, F32)
        for k0 in range(0, n_keys, SEL_CHUNK):
            k_aug = ks_ref[k0:k0 + SEL_CHUNK, :] + kblk_ref[k0:k0 + SEL_CHUNK, :]
            sm = _dot(k_aug, q_sel)
            if k0 + SEL_CHUNK > n_keys - sel_class:
                keypos = k0 + lax.broadcasted_iota(I32, (SEL_CHUNK, tq), 0)
                sm = sm + tile(jnp.where(keypos <= qpos, 0.0, NEG))
            s_ref[k0:k0 + SEL_CHUNK, :] = sm
            m_run = jnp.maximum(m_run, jnp.max(sm.reshape(groups, 8, cols), axis=0))
        m_sel = jnp.max(m_run, axis=0, keepdims=True)
        acc_t = jnp.zeros((d + ones_rows, cols), F32)
        for k0 in range(0, n_keys, SEL_CHUNK):
            p = jnp.exp(s_ref[k0:k0 + SEL_CHUNK, :] - m_sel)
            acc_t = acc_t + _dot(with_ones(vst_ref[:, k0:k0 + SEL_CHUNK]), p.astype(BF16))
        osel_ref[...] = acc_t[:d] / jnp.maximum(acc_t[d:d + 1], 1e-30)

    cls_id = pl.program_id(2) // (sel_class // tq)
    for cls in range(t // sel_class):
        pl.when(cls_id == cls)(functools.partial(sel_branch, (cls + 1) * sel_class))
    o_ref[...] = (out + gate(1) * stack(osel_ref[...])).T


GATE_ROWS = 16


def nsa_prompt(qt, glt, kc, vct, kpad, kvt, wint):
    b, _, t = qt.shape
    g, r, d = NSA_KV_HEADS, NSA_GROUP, NSA_HEAD_DIM
    tq = Q_BLOCK
    nt = t // tq
    ncp = kc.shape[2]
    n_cmp = t // CMP_STRIDE - 1
    n_sel = -(-t // SEL_BLOCK)
    nb = -(-n_sel // 8) * 8
    covt = _cover_matrix(n_cmp, n_sel, ncp, nb).T
    assert d + nb <= LANES, "block one-hot must fit in the keys' padding lanes"
    kblk = np.zeros((t, LANES), np.float32)
    kblk[np.arange(t), d + np.arange(t) // SEL_BLOCK] = 1.0
    kblk = jnp.asarray(kblk, BF16)
    span = min(WINDOW + tq, t)
    i = np.arange(span)[None, :, None]
    j = np.arange(tq)[None, None, :]
    rel = i - j - np.arange(0, span - tq + 1, tq)[:, None, None]
    wbias = jnp.asarray(np.where((rel <= 0) & (rel > -WINDOW), 0.0, NEG), F32)
    per_bg = lambda shape: pl.BlockSpec((None, None) + shape, lambda bi, gi, qi: (bi, gi, 0, 0))
    const = lambda a: pl.BlockSpec(a.shape, lambda bi, gi, qi: (0,) * a.ndim)
    return pl.pallas_call(
        functools.partial(_nsa_prompt_kernel, n_cmp=n_cmp, n_sel=n_sel),
        grid=(b, g, nt),
        in_specs=[pl.BlockSpec((None, r * d, tq), lambda bi, gi, qi: (bi, gi, qi)),
                  pl.BlockSpec((None, GATE_ROWS, tq), lambda bi, gi, qi: (bi, gi, qi)),
                  per_bg((ncp, LANES)), per_bg((d, ncp)), const(covt), const(kblk), const(wbias),
                  pl.BlockSpec((t, LANES), lambda bi, gi, qi: (bi, gi)),
                  pl.BlockSpec((None, d, t), lambda bi, gi, qi: (bi, 3 * g + gi, 0)),
                  pl.BlockSpec((t, LANES), lambda bi, gi, qi: (bi, g + gi)),
                  pl.BlockSpec((None, d, t), lambda bi, gi, qi: (bi, g + gi, 0))],
        out_specs=pl.BlockSpec((tq, r * d), lambda bi, gi, qi: (bi * nt + qi, gi)),
        out_shape=jax.ShapeDtypeStruct((b * t, NSA_HEADS * d), F32),
        scratch_shapes=[pltpu.VMEM((t, r * tq), F32), pltpu.VMEM((d, r * tq), F32)],
        compiler_params=_cparams("parallel", "parallel", "arbitrary"),
        name="nsa_prompt",
    )(qt, glt, kc, vct, covt, kblk, wbias, kpad, kvt, kpad, wint)


def _nsa_sample_select_kernel(q_ref, kc_ref, vc_ref, cov_ref, oc_ref, sel_ref, *, t, past, n_cmp, n_sel):
    rows = q_ref.shape[0]
    g, r = NSA_KV_HEADS, NSA_GROUP
    qb = q_ref[...].astype(BF16)
    qpos_r = past + (lax.broadcasted_iota(I32, (rows, 1), 0) & (t - 1))
    s = _dot_nt(qb, kc_ref[...].astype(BF16))
    nn = lax.broadcasted_iota(I32, s.shape, 1)
    p_cmp = _masked_softmax(s, (nn * CMP_STRIDE + (CMP_BLOCK - 1) <= qpos_r) & (nn < n_cmp))
    oc_ref[...] = _dot(p_cmp.astype(BF16), vc_ref[...].astype(BF16))
    psum = jnp.sum(p_cmp.reshape(g, r, t, s.shape[1]), axis=1).reshape(g * t, s.shape[1])
    imp = jnp.dot(psum, cov_ref[...], precision=HIGHEST, preferred_element_type=F32)
    qpos_gt = past + (lax.broadcasted_iota(I32, (g * t, 1), 0) & (t - 1))
    sel = _select_blocks(imp, qpos_gt >> 6, n_sel)
    lanes = sel.shape[1]
    sel_ref[...] = jnp.broadcast_to(sel.reshape(g, 1, t, lanes), (g, r, t, lanes)).reshape(rows, lanes)


def nsa_sample_select(q4, cmp, *, t, past):
    b, rows, _ = q4.shape
    n_cmp = cmp.shape[2]
    n_sel = -(-(past + t) // SEL_BLOCK)
    lanes = -(-n_sel // 128) * 128
    cover = _cover_matrix(n_cmp, n_sel, n_cmp, lanes)
    return pl.pallas_call(
        functools.partial(_nsa_sample_select_kernel, t=t, past=past, n_cmp=n_cmp, n_sel=n_sel),
        grid=(b,),
        in_specs=[pl.BlockSpec((None, rows, KVW), lambda bi: (bi, 0, 0)),
                  pl.BlockSpec((None, None, n_cmp, KVW), lambda bi: (0, bi, 0, 0)),
                  pl.BlockSpec((None, None, n_cmp, KVW), lambda bi: (1, bi, 0, 0)),
                  pl.BlockSpec(cover.shape, lambda bi: (0, 0))],
        out_specs=[pl.BlockSpec((None, rows, KVW), lambda bi: (bi, 0, 0)),
                   pl.BlockSpec((None, rows, lanes), lambda bi: (bi, 0, 0))],
        out_shape=[jax.ShapeDtypeStruct((b, rows, KVW), F32),
                   jax.ShapeDtypeStruct((b, rows, lanes), F32)],
        compiler_params=_cparams("parallel"),
        name="nsa_sample_select",
    )(q4, cmp, cmp, cover)


def _nsa_sample_attend_kernel(pt_ref, q_ref, sel_ref, oc_ref, gl_ref, *refs, t, past):
    page_refs = refs[:ATT_PAGES_PER_STEP]
    new_ref, win_ref, eb_ref, o_ref, m_ref, l_ref, acc_ref = refs[ATT_PAGES_PER_STEP:]
    rows = q_ref.shape[0]
    p = pl.program_id(1)
    page = page_refs[0].shape[2]
    width = ATT_PAGES_PER_STEP * page
    qb = q_ref[...].astype(BF16)
    sel = sel_ref[...]
    nblk = sel.shape[1]

    @pl.when(p == 0)
    def _():
        _flash_init(m_ref, l_ref, acc_ref)

    sc = jnp.concatenate([_dot(qb, pr[0].astype(BF16)) for pr in page_refs], axis=-1)
    blocks_per_step = width // SEL_BLOCK
    e0 = pl.multiple_of((pl.num_programs(1) - 1 - p) * blocks_per_step, blocks_per_step)
    mask = _dot(sel.astype(BF16), eb_ref[pl.ds(e0, nblk), :]) > 0.5
    sm = jnp.where(mask, sc, NEG)
    m_old = m_ref[...]
    m_new = jnp.maximum(m_old, jnp.max(sm, axis=-1, keepdims=True))
    alpha = jnp.exp(m_old - m_new)
    pr_b = jnp.where(mask, jnp.exp(sm - m_new), 0.0)
    l_ref[...] = alpha * l_ref[...] + jnp.sum(pr_b, axis=-1, keepdims=True)
    pr_b = pr_b.astype(BF16)
    pv = _dot_nt(pr_b[:, :page], page_refs[0][1].astype(BF16))
    for j in range(1, ATT_PAGES_PER_STEP):
        pv = pv + _dot_nt(pr_b[:, j * page:(j + 1) * page], page_refs[j][1].astype(BF16))
    acc_ref[...] = alpha * acc_ref[...] + pv
    m_ref[...] = m_new

    @pl.when(p == pl.num_programs(1) - 1)
    def _():
        qpos_r = past + (lax.broadcasted_iota(I32, (rows, 1), 0) & (t - 1))
        tp = LANES
        newr = jnp.concatenate([new_ref[...], jnp.zeros((tp - t, new_ref.shape[1]), F32)], axis=0)
        jn = lax.broadcasted_iota(I32, (rows, tp), 1)
        new_ok = (jn < t) & (past + jn <= qpos_r)
        blk_new = past // SEL_BLOCK
        sel_new = sel[:, blk_new:blk_new + 1] > 0.5
        sc_n = _dot_nt(qb, newr[:, 2 * KVW:3 * KVW].astype(BF16))
        _flash_update(sc_n, new_ok & sel_new, newr[:, 3 * KVW:4 * KVW].astype(BF16), m_ref, l_ref, acc_ref)
        o_sel = _flash_result(l_ref, acc_ref)

        n_win = win_ref.shape[2]
        s_a = _dot(qb, win_ref[0].astype(BF16))
        s_b = _dot_nt(qb, newr[:, 4 * KVW:5 * KVW].astype(BF16))
        kp_a = (past - n_win) + lax.broadcasted_iota(I32, (rows, n_win), 1)
        ok_a = (kp_a <= qpos_r) & (kp_a > qpos_r - WINDOW) & (kp_a >= 0)
        ok_b = new_ok & (past + jn > qpos_r - WINDOW)
        sm_a = jnp.where(ok_a, s_a, NEG)
        sm_b = jnp.where(ok_b, s_b, NEG)
        mx = jnp.maximum(jnp.max(sm_a, axis=-1, keepdims=True), jnp.max(sm_b, axis=-1, keepdims=True))
        e_a = jnp.where(ok_a, jnp.exp(sm_a - mx), 0.0)
        e_b = jnp.where(ok_b, jnp.exp(sm_b - mx), 0.0)
        den = jnp.maximum(jnp.sum(e_a, axis=-1, keepdims=True) + jnp.sum(e_b, axis=-1, keepdims=True), 1e-30)
        o_win = (_dot_nt((e_a / den).astype(BF16), win_ref[1].astype(BF16))
                 + _dot((e_b / den).astype(BF16), newr[:, 5 * KVW:6 * KVW].astype(BF16)))

        gates = _sigmoid(gl_ref[...])
        o_ref[...] = gates[:, 0:1] * oc_ref[...] + gates[:, 1:2] * o_sel + gates[:, 2:3] * o_win


def nsa_sample_attend(q4, sel, ocmp, gl, cache_t, page_table, new_rows, win_t, *, t, past):
    b, rows, _ = q4.shape
    n_pages = page_table.shape[1]
    page = cache_t.shape[3]
    lanes = sel.shape[2]
    tp = new_rows.shape[1]
    n_win = win_t.shape[3]
    nps = ATT_PAGES_PER_STEP
    n_steps = n_pages // nps
    blocks_per_step = nps * page // SEL_BLOCK
    shift = (n_steps - 1) * blocks_per_step
    expand = jnp.asarray((np.arange(lanes + shift)[:, None] - shift) == (np.arange(nps * page)[None, :] // SEL_BLOCK), BF16)
    per_b = lambda shape: pl.BlockSpec((None,) + shape, lambda bi, pi, pt: (bi,) + (0,) * len(shape))
    grid_spec = pltpu.PrefetchScalarGridSpec(
        num_scalar_prefetch=1,
        grid=(b, n_pages // nps),
        in_specs=[per_b((rows, KVW)), per_b((rows, lanes)), per_b((rows, KVW)), per_b((rows, N_BRANCH))]
        + _page_specs(nps, 1, page)
        + [per_b((tp, N_KV_ROWS * KVW)), per_b((2, KVW, n_win)),
           pl.BlockSpec(expand.shape, lambda bi, pi, pt: (0, 0))],
        out_specs=per_b((rows, KVW)),
        scratch_shapes=[pltpu.VMEM((rows, 1), F32), pltpu.VMEM((rows, 1), F32), pltpu.VMEM((rows, KVW), F32)],
    )
    return pl.pallas_call(
        functools.partial(_nsa_sample_attend_kernel, t=t, past=past),
        grid_spec=grid_spec,
        out_shape=jax.ShapeDtypeStruct((b, rows, KVW), F32),
        compiler_params=_cparams("parallel", "arbitrary"),
        name="nsa_sample_attend",
    )(page_table, q4, sel, ocmp, gl, *([cache_t] * nps), new_rows, win_t, expand)


def _route_kernel(x_ref, g_ref, sh_ref, sc_ref, rw_ref, rb_ref, h_ref, r_ref):
    h = _norm_mod(x_ref[...], g_ref[...], sh_ref[...], sc_ref[...])
    h_ref[...] = h
    logits = jnp.dot(h, rw_ref[...], precision=HIGHEST, preferred_element_type=F32) + rb_ref[...]
    lane = lax.broadcasted_iota(I32, logits.shape, 1)
    lane_f = lane.astype(F32)
    lg = jnp.where(lane < N_EXPERTS, logits, NEG)
    v1 = jnp.max(lg, axis=-1, keepdims=True)
    i1 = jnp.min(jnp.where(lg == v1, lane_f, 128.0), axis=-1, keepdims=True)
    lg2 = jnp.where(lane_f == i1, NEG, lg)
    v2 = jnp.max(lg2, axis=-1, keepdims=True)
    i2 = jnp.min(jnp.where(lg2 == v2, lane_f, 128.0), axis=-1, keepdims=True)
    e = jnp.exp(v2 - v1)
    w1 = 1.0 / (1.0 + e)
    w2 = e / (1.0 + e)
    r_ref[...] = jnp.where(lane == 0, i1, jnp.where(lane == 1, i2, jnp.where(lane == 2, w1, jnp.where(lane == 3, w2, 0.0))))


def moe_route(x, g, shift, scale, rw_pad, rb_pad, *, per_token, tm, tpb):
    m, d = x.shape
    sh, sh_spec = _mod_arg(shift, per_token, tm, tpb)
    sc, sc_spec = _mod_arg(scale, per_token, tm, tpb)
    return pl.pallas_call(
        _route_kernel,
        grid=(m // tm,),
        in_specs=[pl.BlockSpec((tm, d), lambda i: (i, 0)), _vec_spec(d), sh_spec, sc_spec,
                  pl.BlockSpec((d, 128), lambda i: (0, 0)), _vec_spec(128)],
        out_specs=[pl.BlockSpec((tm, d), lambda i: (i, 0)), pl.BlockSpec((tm, 128), lambda i: (i, 0))],
        out_shape=[jax.ShapeDtypeStruct((m, d), F32), jax.ShapeDtypeStruct((m, 128), F32)],
        compiler_params=_cparams("parallel"),
        name="moe_route",
    )(x, g.reshape(1, d), sh, sc, rw_pad, rb_pad)


def _moe_kernel(te_ref, nv_ref, x_ref, wg_ref, wu_ref, wd_ref, o_ref, xb_ref, acc_ref):
    i = pl.program_id(0)
    f = pl.program_id(1)

    @pl.when(f == 0)
    def _():
        acc_ref[...] = jnp.zeros(acc_ref.shape, F32)
        xb_ref[...] = x_ref[...].astype(BF16)

    @pl.when(i < nv_ref[0])
    def _():
        x = xb_ref[...]
        act = _silu(_dot(x, wg_ref[...])) * _dot(x, wu_ref[...])
        acc_ref[...] += _dot(act.astype(BF16), wd_ref[...])

    @pl.when(f == pl.num_programs(1) - 1)
    def _():
        o_ref[...] = acc_ref[...]


def moe_experts(xs, tile_expert, n_valid, w_gu_b, w_down_b, *, tm, tf):
    p, d = xs.shape
    edim = w_down_b.shape[1]
    nf = edim // tf
    n_tiles = p // tm

    def wmap(off):
        def index(i, f, te, nv):
            ok = i < nv[0]
            return (te[i], 0, off + jnp.where(ok, f, nf - 1))
        return index

    def dmap(i, f, te, nv):
        return (te[i], jnp.where(i < nv[0], f, nf - 1), 0)

    grid_spec = pltpu.PrefetchScalarGridSpec(
        num_scalar_prefetch=2,
        grid=(n_tiles, nf),
        in_specs=[pl.BlockSpec((tm, d), lambda i, f, te, nv: (i, 0)),
                  pl.BlockSpec((None, d, tf), wmap(0)),
                  pl.BlockSpec((None, d, tf), wmap(nf)),
                  pl.BlockSpec((None, tf, d), dmap)],
        out_specs=pl.BlockSpec((tm, d), lambda i, f, te, nv: (i, 0)),
        scratch_shapes=[pltpu.VMEM((tm, d), BF16), pltpu.VMEM((tm, d), F32)],
    )
    return pl.pallas_call(
        _moe_kernel,
        grid_spec=grid_spec,
        out_shape=jax.ShapeDtypeStruct((p, d), F32),
        compiler_params=_cparams("parallel", "arbitrary"),
        name="moe_experts",
    )(tile_expert, n_valid, xs, w_gu_b, w_gu_b, w_down_b)


def _moe_tables(route, tm):
    m = route.shape[0]
    na = TOP_K * m
    e = route[:, :TOP_K].astype(I32).reshape(na)
    onehot = (e[:, None] == jnp.arange(N_EXPERTS, dtype=I32)[None, :]).astype(I32)
    within = jnp.sum((jnp.cumsum(onehot, axis=0) - onehot) * onehot, axis=1)
    counts = jnp.sum(onehot, axis=0)
    padded = ((counts + tm - 1) // tm) * tm
    ends = jnp.cumsum(padded)
    starts = ends - padded
    dest = starts[e] + within
    n_slots = (-(-na // tm) + N_EXPERTS) * tm
    src_tok = jnp.zeros((n_slots,), I32).at[dest].set(jnp.arange(na, dtype=I32) // TOP_K,
                                                       mode='promise_in_bounds', unique_indices=True)
    n_tiles = n_slots // tm
    n_valid = (ends[-1] // tm).astype(I32)
    tile_start = jnp.arange(n_tiles, dtype=I32) * tm
    tile_expert = jnp.sum((tile_start[:, None] >= ends[None, :]).astype(I32), axis=1)
    last = jnp.take(tile_expert, jnp.maximum(n_valid - 1, 0))
    tile_expert = jnp.where(jnp.arange(n_tiles) < n_valid, tile_expert, last).astype(I32)
    return src_tok, tile_expert, n_valid.reshape(1), dest


def _split_mod(mod):
    return [mod[:, i * D_MODEL:(i + 1) * D_MODEL] for i in range(mod.shape[1] // D_MODEL)]


def kernel(x_prompt, x_sample, c_prompt, c_sample, state_ret, cache_kv, cache_win, page_table, w_mod, b_mod, norm_g, ret_w_in, ret_gn_g, ret_w_out, kv_w_mod, kv_b_mod, kv_norm_g, kv_w, cmp_pos, cmp_w1, cmp_b1, cmp_w2, nsa_w_in, nsa_w_out, ffn_w_gu, ffn_w_down, moe_router_w, moe_router_b, moe_w_gu, moe_w_down):
    bp, t, d = x_prompt.shape
    bs, ts, _ = x_sample.shape
    mp, ms = bp * t, bs * ts
    n_pool, page = cache_kv.shape[:2]
    past = page_table.shape[1] * page
    g, r, hd = NSA_KV_HEADS, NSA_GROUP, NSA_HEAD_DIM

    ret_w_in_b = ret_w_in[0].astype(BF16)
    ret_w_out_b = ret_w_out[0].astype(BF16)
    kv_w_b = kv_w.astype(BF16)
    kv_wt_b = kv_w.T.astype(BF16)
    nq = NSA_HEADS * hd
    gate_w = nsa_w_in[0][:, nq:].reshape(d, g, r * N_BRANCH)
    gate_w = jnp.pad(gate_w, ((0, 0), (0, 0), (0, LANES - r * N_BRANCH))).reshape(d, g * LANES)
    nsa_w_in_b = jnp.concatenate([nsa_w_in[0][:, :nq], gate_w], axis=1).astype(BF16)
    gate_wt = nsa_w_in[0][:, nq:].T.reshape(g, r * N_BRANCH, d)
    gate_wt = jnp.pad(gate_wt, ((0, 0), (0, GATE_ROWS - r * N_BRANCH), (0, 0))).reshape(g * GATE_ROWS, d)
    nsa_wt_b = jnp.concatenate([nsa_w_in[0][:, :nq].T, gate_wt], axis=0).astype(BF16)
    def pad_heads(w):
        return jnp.pad(w.reshape(d, g, hd), ((0, 0), (0, 0), (0, LANES - hd))).reshape(d, g * LANES)
    kpad_w_b = jnp.concatenate([pad_heads(kv_w[:, 2 * KVW:3 * KVW]), pad_heads(kv_w[:, 4 * KVW:5 * KVW])],
                               axis=1).astype(BF16)
    nsa_w_out_b = nsa_w_out[0].astype(BF16)
    ffn_w_gu_b = ffn_w_gu[0].astype(BF16)
    ffn_w_down_b = ffn_w_down[0].astype(BF16)
    moe_w_gu_b = moe_w_gu[0].astype(BF16)
    moe_w_down_b = moe_w_down[0].astype(BF16)
    x_prompt, moe_w_gu_b, moe_w_down_b = lax.optimization_barrier((x_prompt, moe_w_gu_b, moe_w_down_b))
    rw_pad = jnp.pad(moe_router_w[0], ((0, 0), (0, 128 - N_EXPERTS)))
    rb_pad = jnp.pad(moe_router_b[0], (0, 128 - N_EXPERTS)).reshape(1, 128)
    cw = _compress_weights(cmp_pos, cmp_w1, cmp_b1, cmp_w2)

    c_all = jnp.concatenate([c_prompt, c_sample], axis=0)
    mods = [cond_matmul(c_all, w_mod, b_mod, layer) for layer in range(w_mod.shape[0])]
    kv_mod = cond_matmul(c_all, kv_w_mod[None], kv_b_mod[None], 0)

    tm_p = min(512, t)
    tm_f = min(1024, t)
    groups = {
        'p': dict(x=x_prompt.reshape(mp, d), b=bp, t=t, per_token=False, tm=tm_p, tpb=t // tm_p,
                  tm_f=tm_f, tpb_f=t // tm_f,
                  mod=lambda a: a[:bp]),
        's': dict(x=x_sample.reshape(ms, d), b=bs, t=ts, per_token=True, tm=ms, tpb=1, tm_f=ms, tpb_f=1,
                  mod=lambda a: jnp.repeat(a[bp:], ts, axis=0)),
    }
    out = {}

    for name, gr in groups.items():
        kw = dict(per_token=gr['per_token'], tm=gr['tm'], tpb=gr['tpb'])
        m0 = [gr['mod'](a) for a in _split_mod(mods[0])]
        x = gr['x']
        proj = norm_mod_matmul(x, norm_g[0, 0], m0[0], m0[1], ret_w_in_b, tn=1024,
                               per_token=gr['per_token'], tm=gr['tm_f'], tpb=gr['tpb_f'])
        proj = proj.reshape(gr['b'], gr['t'], -1)
        if name == 'p':
            pos = jnp.arange(t)
            s0 = jnp.zeros((bp, RET_HEADS, RET_DK, RET_DV), F32)
            o, s_new = retention(proj, pos, s0, ret_gn_g[0], math.gcd(t, RET_CHUNK))
        else:
            pos = past + jnp.arange(ts)
            o, s_new = retention(proj, pos, state_ret[0], ret_gn_g[0], math.gcd(ts, RET_CHUNK))
        out['ret_' + name] = s_new[None]
        x = matmul_norm_residual(o.reshape(-1, d), ret_w_out_b, x, m0[2], norm_g[0, 1], **kw)
        x = ffn_sublayer(x, norm_g[0, 2], m0[3], m0[4], ffn_w_gu_b, ffn_w_down_b, m0[5], norm_g[0, 3],
                         per_token=gr['per_token'], tm=gr['tm_f'], tpb=gr['tpb_f'],
                         tf=_largest_tile(ffn_w_down.shape[1], FFN_HIDDEN_TILE_CAP))
        gr['x1'] = x

    def attention(name):
        gr = groups[name]
        kw = dict(per_token=gr['per_token'], tm=gr['tm'], tpb=gr['tpb'])
        m1 = [gr['mod'](a) for a in _split_mod(mods[1])]
        kvm = [gr['mod'](a) for a in _split_mod(kv_mod)]
        gr['m1'] = m1
        x = gr['x1']
        b_, t_ = gr['b'], gr['t']
        if name == 'p':
            kvt, wint = norm_mod_matmul_t(x, kv_norm_g, kvm[0], kvm[1], kv_wt_b,
                                          (N_PAGED_ROWS * KVW, (N_KV_ROWS - N_PAGED_ROWS) * KVW), b=b_, tm=gr['tm'])
            out['kv_p'] = kvt.reshape(b_, N_PAGED_ROWS, g, hd, t_).transpose(0, 4, 1, 2, 3)
            n_keep = min(WINDOW, t_)
            out['win_p'] = wint[:, :, t_ - n_keep:].reshape(b_, 2, g, hd, n_keep).transpose(0, 4, 1, 2, 3)
            kpad = norm_mod_matmul(x, kv_norm_g, kvm[0], kvm[1], kpad_w_b, out_dtype=BF16, **kw)
            qt, glt = norm_mod_matmul_t(x, norm_g[1, 0], m1[0], m1[1], nsa_wt_b, (nq, g * GATE_ROWS),
                                        b=b_, tm=gr['tm'])
            cmp = compress_prompt(kvt, cw).reshape(2, b_, -1, g, hd)
            kc = jnp.pad(cmp[0].transpose(0, 2, 1, 3), ((0, 0), (0, 0), (0, 0), (0, LANES - hd)))
            o = nsa_prompt(qt, glt, kc, cmp[1].transpose(0, 2, 3, 1), kpad, kvt, wint)
        else:
            proj = norm_mod_matmul(x, norm_g[1, 0], m1[0], m1[1], nsa_w_in_b, **kw)
            rows = norm_mod_matmul(x, kv_norm_g, kvm[0], kvm[1], kv_w_b, **kw)
            q = (proj[:, :NSA_HEADS * hd] * (hd ** -0.5)).reshape(b_, t_, g, r, hd)
            gl = proj[:, NSA_HEADS * hd:].reshape(b_, t_, g, LANES)[..., :r * N_BRANCH]
            gl = gl.reshape(b_, t_, g, r, N_BRANCH)
            rows3 = rows.reshape(b_, t_, N_KV_ROWS * KVW)
            rows6 = rows.reshape(b_, t_, N_KV_ROWS, g, hd)
            out['kv_s'] = rows6[:, :, :N_PAGED_ROWS]
            cache_t = cache_kv.transpose(0, 2, 3, 4, 1).reshape(n_pool, N_PAGED_ROWS, KVW, page)
            win_t5 = cache_win.transpose(0, 2, 3, 4, 1)
            n_win = cache_win.shape[1]
            new_win_t = rows6[:, :, N_PAGED_ROWS:].transpose(0, 2, 3, 4, 1)
            out['win_s'] = jnp.concatenate([win_t5, new_win_t], axis=-1)[..., t_:].transpose(0, 4, 1, 2, 3)
            new_sub = jnp.pad(rows3[:, :, :2 * KVW], ((0, 0), (0, CMP_STRIDE - t_), (0, 0)))
            new_sub = new_sub.reshape(b_, CMP_STRIDE, 2, KVW).transpose(2, 0, 1, 3)
            cmp = compress_sample(cache_t, page_table, new_sub, cw)
            eye = jnp.eye(g, dtype=F32)
            qrows = q.transpose(0, 2, 3, 1, 4).reshape(b_, g, r * t_, hd)
            q4 = jnp.einsum('bgxd,gj->bgxjd', qrows, eye).reshape(b_, g * r * t_, KVW)
            glr = gl.transpose(0, 2, 3, 1, 4).reshape(b_, g * r * t_, N_BRANCH)
            ocmp, sel = nsa_sample_select(q4, cmp, t=t_, past=past)
            o4 = nsa_sample_attend(q4, sel, ocmp, glr, cache_t, page_table, rows3,
                                   win_t5.reshape(b_, 2, KVW, n_win), t=t_, past=past)
            o4 = o4.reshape(b_, g, r, t_, g, hd)
            o = jnp.einsum('bgrtjd,gj->btgrd', o4, eye).reshape(ms, NSA_HEADS * hd)
        gr['x2'] = matmul_norm_residual(o, nsa_w_out_b, x, m1[2], norm_g[1, 1], **kw)

    take = lambda a, idx: a.at[idx].get(mode='promise_in_bounds')

    def moe_plan(name):
        gr = groups[name]
        kw = dict(per_token=gr['per_token'], tm=gr['tm'], tpb=gr['tpb'])
        h, route = moe_route(gr['x2'], norm_g[1, 2], gr['m1'][3], gr['m1'][4], rw_pad, rb_pad, **kw)
        tm_e = 512 if h.shape[0] >= 4096 else 128
        src_tok, tile_expert, n_valid, dest = _moe_tables(route, tm_e)
        return dict(h=h, src_tok=src_tok, route=route, tile_expert=tile_expert, n_valid=n_valid,
                    dest=dest.reshape(-1, TOP_K), tm_e=tm_e)

    def moe_finish(name, dp, xs):
        gr = groups[name]
        kw = dict(per_token=gr['per_token'], tm=gr['tm'], tpb=gr['tpb'])
        ys = moe_experts(xs, dp['tile_expert'], dp['n_valid'], moe_w_gu_b, moe_w_down_b, tm=dp['tm_e'],
                         tf=_largest_tile(moe_w_down.shape[2], MOE_HIDDEN_TILE_CAP))
        out['y_' + name] = moe_combine(take(ys, dp['dest'][:, 0]), take(ys, dp['dest'][:, 1]), dp['route'],
                                       gr['x2'], gr['m1'][5], norm_g[1, 3], **kw)

    attention('p')
    plan_p = moe_plan('p')
    plan_p['src_tok'], groups['s']['x1'] = lax.optimization_barrier((plan_p['src_tok'], groups['s']['x1']))
    xs_p = take(plan_p['h'], plan_p['src_tok'])
    attention('s')
    xs_p, groups['s']['x2'] = lax.optimization_barrier((xs_p, groups['s']['x2']))
    moe_finish('p', plan_p, xs_p)
    plan_s = moe_plan('s')
    moe_finish('s', plan_s, take(plan_s['h'], plan_s['src_tok']))

    return (out['y_p'].reshape(bp, t, d), out['y_s'].reshape(bs, ts, d),
            out['ret_p'], out['ret_s'], out['kv_p'], out['kv_s'], out['win_p'], out['win_s'])
```

```python
import functools
import math

import numpy as np
import jax
import jax.numpy as jnp
from jax import lax
from jax.experimental import pallas as pl
from jax.experimental.pallas import tpu as pltpu

F32 = jnp.float32
BF16 = jnp.bfloat16
I32 = jnp.int32

D_MODEL = 1024
N_MOD = 6
RET_HEADS = 4
RET_DK = 256
RET_DV = 256
RET_CHUNK = 128
ROPE_BASE = 10000.0
NSA_HEADS = 16
NSA_KV_HEADS = 4
NSA_GROUP = 4
NSA_HEAD_DIM = 64
N_BRANCH = 3
N_KV_ROWS = 6
N_PAGED_ROWS = 4
CMP_BLOCK = 32
CMP_STRIDE = 16
CMP_HIDDEN = 128
SEL_BLOCK = 64
N_SEL = 16
FORCED_SCORE = 1.0e4
WINDOW = 512
Q_BLOCK = 128
N_EXPERTS = 8
TOP_K = 2
EPS = 1e-6

NEG = -1.0e30
KVW = NSA_KV_HEADS * NSA_HEAD_DIM
VMEM_LIMIT_BYTES = 56 * 1024 * 1024
HIGHEST = lax.Precision.HIGHEST


LANES = 128
FFN_HIDDEN_TILE_CAP = 1408
MOE_HIDDEN_TILE_CAP = 896


def _largest_tile(n, cap):
    best = LANES
    for k in range(LANES, cap + 1, LANES):
        if n % k == 0:
            best = k
    return best


def _cparams(*sem):
    return pltpu.CompilerParams(dimension_semantics=sem, vmem_limit_bytes=VMEM_LIMIT_BYTES)


def _sigmoid(x):
    return 1.0 / (1.0 + jnp.exp(-x))


def _silu(x):
    return x * _sigmoid(x)


def _gelu_tanh(x):
    return x * (0.5 * (1.0 + jnp.tanh(math.sqrt(2.0 / math.pi) * (x + 0.044715 * (x * x * x)))))


def _norm_mod(x, g, shift, scale):
    ms = jnp.mean(x * x, axis=-1, keepdims=True)
    return (x * lax.rsqrt(ms + EPS) * g) * (1.0 + scale) + shift


def _rms_residual(x, gate, o, g):
    ms = jnp.mean(o * o, axis=-1, keepdims=True)
    return x + gate * (o * lax.rsqrt(ms + EPS) * g)


def _dot(a, b):
    return jnp.dot(a, b, preferred_element_type=F32)


def _dot_nt(a, b):
    return lax.dot_general(a, b, (((1,), (1,)), ((), ())), preferred_element_type=F32)


def _masked_softmax(s, mask):
    sm = jnp.where(mask, s, NEG)
    m = jnp.max(sm, axis=-1, keepdims=True)
    e = jnp.where(mask, jnp.exp(sm - m), 0.0)
    return e / jnp.maximum(jnp.sum(e, axis=-1, keepdims=True), 1e-30)


def _flash_init(m_ref, l_ref, acc_ref):
    m_ref[...] = jnp.full(m_ref.shape, NEG, F32)
    l_ref[...] = jnp.zeros(l_ref.shape, F32)
    acc_ref[...] = jnp.zeros(acc_ref.shape, F32)


def _flash_update(s, mask, v_b, m_ref, l_ref, acc_ref):
    sm = jnp.where(mask, s, NEG)
    m_old = m_ref[...]
    m_new = jnp.maximum(m_old, jnp.max(sm, axis=-1, keepdims=True))
    alpha = jnp.exp(m_old - m_new)
    p = jnp.where(mask, jnp.exp(sm - m_new), 0.0)
    l_ref[...] = alpha * l_ref[...] + jnp.sum(p, axis=-1, keepdims=True)
    acc_ref[...] = alpha * acc_ref[...] + _dot(p.astype(BF16), v_b)
    m_ref[...] = m_new


def _flash_result(l_ref, acc_ref):
    return acc_ref[...] / jnp.maximum(l_ref[...], 1e-30)


def _select_blocks(imp, cur, n_sel):
    rows, lanes = imp.shape
    blk = lax.broadcasted_iota(I32, (rows, lanes), 1)
    valid = (blk <= cur) & (blk < n_sel)
    forced = (blk == 0) | (blk == cur) | (blk == cur - 1)
    score = jnp.where(valid, jnp.where(forced, FORCED_SCORE, imp), -1.0)
    rank = jnp.zeros((rows, lanes), F32)
    for i in range(n_sel):
        ci = score[:, i:i + 1]
        beats = (ci > score) | ((ci == score) & (blk > i))
        rank = rank + jnp.where(beats, 1.0, 0.0)
    return jnp.where(valid & (rank < float(min(N_SEL, n_sel))), 1.0, 0.0)


def _cond_kernel(c_ref, w_ref, b_ref, o_ref):
    sc = _silu(c_ref[...])
    o_ref[...] = _dot(sc.astype(BF16), w_ref[...].astype(BF16)) + b_ref[...]


def cond_matmul(c, w, b, layer):
    bc, d = c.shape
    n_layers, _, n = w.shape
    tn = 1024
    return pl.pallas_call(
        _cond_kernel,
        grid=(n // tn,),
        in_specs=[pl.BlockSpec((bc, d), lambda j: (0, 0)),
                  pl.BlockSpec((None, d, tn), lambda j: (layer, 0, j)),
                  pl.BlockSpec((None, 1, tn), lambda j: (layer, 0, j))],
        out_specs=pl.BlockSpec((bc, tn), lambda j: (0, j)),
        out_shape=jax.ShapeDtypeStruct((bc, n), F32),
        compiler_params=_cparams("parallel"),
        name="cond_matmul",
    )(c, w, b.reshape(n_layers, 1, n))


def _mod_arg(m, per_token, tm, tiles_per_batch):
    d = m.shape[-1]
    if per_token:
        return m, pl.BlockSpec((tm, d), lambda i, *_: (i, 0))
    return m[:, None, :], pl.BlockSpec((None, 1, d), lambda i, *_: (i // tiles_per_batch, 0, 0))


def _vec_spec(d):
    return pl.BlockSpec((1, d), lambda i, *_: (0, 0))


def _nmm_kernel(x_ref, g_ref, sh_ref, sc_ref, w_ref, o_ref, h_ref):
    @pl.when(pl.program_id(1) == 0)
    def _():
        h_ref[...] = _norm_mod(x_ref[...], g_ref[...], sh_ref[...], sc_ref[...]).astype(BF16)

    o_ref[...] = _dot(h_ref[...], w_ref[...]).astype(o_ref.dtype)


def norm_mod_matmul(x, g, shift, scale, w_b, *, per_token, tm, tpb, tn=None, out_dtype=F32):
    m, d = x.shape
    n = w_b.shape[1]
    tn = n if tn is None else tn
    sh, sh_spec = _mod_arg(shift, per_token, tm, tpb)
    sc, sc_spec = _mod_arg(scale, per_token, tm, tpb)
    return pl.pallas_call(
        _nmm_kernel,
        grid=(m // tm, n // tn),
        in_specs=[pl.BlockSpec((tm, d), lambda i, j: (i, 0)), _vec_spec(d), sh_spec, sc_spec,
                  pl.BlockSpec((d, tn), lambda i, j: (0, j))],
        out_specs=pl.BlockSpec((tm, tn), lambda i, j: (i, j)),
        out_shape=jax.ShapeDtypeStruct((m, n), out_dtype),
        scratch_shapes=[pltpu.VMEM((tm, d), BF16)],
        compiler_params=_cparams("parallel", "arbitrary"),
        name="norm_mod_matmul",
    )(x, g.reshape(1, d), sh, sc, w_b)


def _nmm_t_kernel(x_ref, g_ref, sh_ref, sc_ref, wt_ref, *o_refs):
    h = _norm_mod(x_ref[...], g_ref[...], sh_ref[...], sc_ref[...]).astype(BF16)
    o = _dot_nt(wt_ref[...], h)
    row = 0
    for o_ref in o_refs:
        o_ref[...] = o[row:row + o_ref.shape[0]]
        row += o_ref.shape[0]


def norm_mod_matmul_t(x, g, shift, scale, wt_b, splits, *, b, tm):
    m, d = x.shape
    t = m // b
    tpb = t // tm
    n = wt_b.shape[0]
    sh, sh_spec = _mod_arg(shift, False, tm, tpb)
    sc, sc_spec = _mod_arg(scale, False, tm, tpb)
    return pl.pallas_call(
        _nmm_t_kernel,
        grid=(m // tm,),
        in_specs=[pl.BlockSpec((tm, d), lambda i: (i, 0)), _vec_spec(d), sh_spec, sc_spec,
                  pl.BlockSpec((n, d), lambda i: (0, 0))],
        out_specs=[pl.BlockSpec((None, ni, tm), lambda i: (i // tpb, 0, i % tpb)) for ni in splits],
        out_shape=[jax.ShapeDtypeStruct((b, ni, t), F32) for ni in splits],
        compiler_params=_cparams("parallel"),
        name="norm_mod_matmul_t",
    )(x, g.reshape(1, d), sh, sc, wt_b)


def _mnr_kernel(a_ref, w_ref, x_ref, gate_ref, g_ref, o_ref):
    o = _dot(a_ref[...].astype(BF16), w_ref[...])
    o_ref[...] = _rms_residual(x_ref[...], gate_ref[...], o, g_ref[...])


def matmul_norm_residual(a, w_b, x, gate, g, *, per_token, tm, tpb):
    m, k = a.shape
    d = w_b.shape[1]
    gt, gt_spec = _mod_arg(gate, per_token, tm, tpb)
    return pl.pallas_call(
        _mnr_kernel,
        grid=(m // tm,),
        in_specs=[pl.BlockSpec((tm, k), lambda i: (i, 0)),
                  pl.BlockSpec((k, d), lambda i: (0, 0)),
                  pl.BlockSpec((tm, d), lambda i: (i, 0)), gt_spec, _vec_spec(d)],
        out_specs=pl.BlockSpec((tm, d), lambda i: (i, 0)),
        out_shape=jax.ShapeDtypeStruct((m, d), F32),
        compiler_params=_cparams("parallel"),
        name="matmul_norm_residual",
    )(a, w_b, x, gt, g.reshape(1, d))


def _combine_kernel(y1_ref, y2_ref, r_ref, x_ref, gate_ref, g_ref, o_ref):
    route = r_ref[...]
    y = route[:, TOP_K:TOP_K + 1] * y1_ref[...] + route[:, TOP_K + 1:TOP_K + 2] * y2_ref[...]
    o_ref[...] = _rms_residual(x_ref[...], gate_ref[...], y, g_ref[...])


def moe_combine(y1, y2, route, x, gate, g, *, per_token, tm, tpb):
    m, d = x.shape
    gt, gt_spec = _mod_arg(gate, per_token, tm, tpb)
    row = lambda n: pl.BlockSpec((tm, n), lambda i: (i, 0))
    return pl.pallas_call(
        _combine_kernel,
        grid=(m // tm,),
        in_specs=[row(d), row(d), row(route.shape[1]), row(d), gt_spec, _vec_spec(d)],
        out_specs=row(d),
        out_shape=jax.ShapeDtypeStruct((m, d), F32),
        compiler_params=_cparams("parallel"),
        name="moe_combine",
    )(y1, y2, route, x, gt, g.reshape(1, d))


def _ffn_kernel(x_ref, g2_ref, sh_ref, sc_ref, wg_ref, wu_ref, wd_ref, gate_ref, g3_ref, o_ref,
                h_ref, acc_ref):
    f = pl.program_id(1)

    @pl.when(f == 0)
    def _():
        h_ref[...] = _norm_mod(x_ref[...], g2_ref[...], sh_ref[...], sc_ref[...]).astype(BF16)
        acc_ref[...] = jnp.zeros(acc_ref.shape, F32)

    h = h_ref[...]
    act = _silu(_dot(h, wg_ref[...])) * _dot(h, wu_ref[...])
    acc_ref[...] += _dot(act.astype(BF16), wd_ref[...])

    @pl.when(f == pl.num_programs(1) - 1)
    def _():
        o_ref[...] = _rms_residual(x_ref[...], gate_ref[...], acc_ref[...], g3_ref[...])


def ffn_sublayer(x, g2, shift, scale, w_gu_b, w_down_b, gate, g3, *, per_token, tm, tpb, tf):
    m, d = x.shape
    fdim = w_down_b.shape[0]
    nf = fdim // tf
    sh, sh_spec = _mod_arg(shift, per_token, tm, tpb)
    sc, sc_spec = _mod_arg(scale, per_token, tm, tpb)
    gt, gt_spec = _mod_arg(gate, per_token, tm, tpb)
    return pl.pallas_call(
        _ffn_kernel,
        grid=(m // tm, nf),
        in_specs=[pl.BlockSpec((tm, d), lambda i, f: (i, 0)), _vec_spec(d), sh_spec, sc_spec,
                  pl.BlockSpec((d, tf), lambda i, f: (0, f)),
                  pl.BlockSpec((d, tf), lambda i, f: (0, nf + f)),
                  pl.BlockSpec((tf, d), lambda i, f: (f, 0)),
                  gt_spec, _vec_spec(d)],
        out_specs=pl.BlockSpec((tm, d), lambda i, f: (i, 0)),
        out_shape=jax.ShapeDtypeStruct((m, d), F32),
        scratch_shapes=[pltpu.VMEM((tm, d), BF16), pltpu.VMEM((tm, d), F32)],
        compiler_params=_cparams("parallel", "arbitrary"),
        name="ffn_sublayer",
    )(x, g2.reshape(1, d), sh, sc, w_gu_b, w_gu_b, w_down_b, gt, g3.reshape(1, d))


def _ret_layer_kernel(x_ref, g0_ref, sh_ref, sc_ref, win_ref, cos_ref, sin_ref, dm_ref, xi_ref, zt_ref, gc_ref,
                      gn_ref, s0_ref, wout_ref, rgate_ref, g1_ref, o_ref, s_ref, pr_ref, oc_ref):
    @pl.when(pl.program_id(1) == 0)
    def _():
        s_ref[...] = s0_ref[...]

    x = x_ref[...]
    pr_ref[...] = _dot(_norm_mod(x, g0_ref[...], sh_ref[...], sc_ref[...]).astype(BF16), win_ref[...])
    tm = x.shape[0]
    rows = min(tm, RET_CHUNK)
    nq = RET_HEADS * RET_DK
    nv = RET_HEADS * RET_DV
    half = RET_DK // 2

    def pad(v):
        if rows == RET_CHUNK:
            return v
        return jnp.concatenate([v, jnp.zeros((RET_CHUNK - rows, v.shape[1]), v.dtype)], axis=0)

    for r0 in range(0, tm, rows):
        rs = slice(r0, r0 + rows)
        cos = pad(cos_ref[rs, :])
        sin = pad(sin_ref[rs, :])

        def rot(v):
            v1 = v[:, :half]
            v2 = v[:, half:]
            return jnp.concatenate([v1 * cos - v2 * sin, v2 * cos + v1 * sin], axis=-1)

        for h in range(RET_HEADS):
            vcols = slice(h * RET_DV, (h + 1) * RET_DV)
            q = rot(pad(pr_ref[rs, h * RET_DK:(h + 1) * RET_DK]))
            k = rot(pad(pr_ref[rs, nq + h * RET_DK:nq + (h + 1) * RET_DK])) * (RET_DK ** -0.5)
            qb = q.astype(BF16)
            kb = k.astype(BF16)
            vb = pad(pr_ref[rs, 2 * nq + h * RET_DV:2 * nq + (h + 1) * RET_DV]).astype(BF16)
            state = s_ref[h]
            inner = _dot_nt(qb, kb) * dm_ref[h]
            o = _dot(inner.astype(BF16), vb) + _dot(qb, state.astype(BF16)) * xi_ref[h]
            kz = (k * zt_ref[h]).astype(BF16)
            upd = lax.dot_general(kz, vb, (((0,), (0,)), ((), ())), preferred_element_type=F32)
            s_ref[h] = gc_ref[h] * state + upd
            mu = jnp.mean(o, axis=-1, keepdims=True)
            dev = o - mu
            var = jnp.mean(dev * dev, axis=-1, keepdims=True)
            on = dev * lax.rsqrt(var + EPS) * gn_ref[:, vcols]
            sgate = pr_ref[rs, 2 * nq + nv + h * RET_DV:2 * nq + nv + (h + 1) * RET_DV]
            oc_ref[rs, vcols] = _silu(sgate) * on[:rows]

    o = _dot(oc_ref[...].astype(BF16), wout_ref[...])
    o_ref[...] = _rms_residual(x, rgate_ref[...], o, g1_ref[...])


def _ret_tables(chunk, rows):
    h = RET_HEADS
    log_g = jnp.log(1.0 - jnp.exp(jnp.linspace(math.log(1.0 / 32), math.log(1.0 / 512), h, dtype=F32)))
    i = jnp.arange(chunk, dtype=F32)
    diff = i[:, None] - i[None, :]
    dmat = jnp.where(diff >= 0, jnp.exp(log_g[:, None, None] * jnp.maximum(diff, 0.0)), 0.0)
    xi = jnp.exp(log_g[:, None] * (i + 1.0))
    zeta = jnp.exp(log_g[:, None] * (chunk - 1.0 - i))
    gch = jnp.exp(log_g * chunk)
    pad = rows - chunk
    dmat = jnp.pad(dmat, ((0, 0), (0, pad), (0, pad)))
    xi = jnp.pad(xi, ((0, 0), (0, pad)))[..., None]
    zeta = jnp.pad(zeta, ((0, 0), (0, pad)))[..., None]
    gch = jnp.broadcast_to(gch[:, None, None], (h, 1, RET_DV))
    return dmat, xi, zeta, gch


def _rope_tables(pos, rows):
    half = RET_DK // 2
    inv = ROPE_BASE ** (-jnp.arange(half, dtype=F32) / half)
    ang = pos.astype(F32)[:, None] * inv[None, :]
    pad = rows - pos.shape[0]
    return jnp.pad(jnp.cos(ang), ((0, pad), (0, 0))), jnp.pad(jnp.sin(ang), ((0, pad), (0, 0)))


RET_ROW_TILE = 512


def retention_sublayer(x, g0, shift, scale, w_in_b, pos, s0, gn_g, w_out_b, rgate, g1, *, b, per_token, chunk):
    m, d = x.shape
    t = m // b
    tm = min(t, RET_ROW_TILE)
    nt = t // tm
    h = RET_HEADS
    dmat, xi, zeta, gch = _ret_tables(chunk, RET_CHUNK)
    cos, sin = _rope_tables(pos, t)
    gn = gn_g.reshape(1, -1)
    rows = lambda n: pl.BlockSpec((tm, n), lambda bi, ti: (bi * nt + ti, 0))
    const = lambda a: pl.BlockSpec(a.shape, lambda bi, ti: (0,) * a.ndim, pipeline_mode=pl.Buffered(1))
    state_spec = pl.BlockSpec((None, h, RET_DK, RET_DV), lambda bi, ti: (bi, 0, 0, 0))

    def mod_arg(a):
        if per_token:
            return a, rows(d)
        return a[:, None, :], pl.BlockSpec((None, 1, d), lambda bi, ti: (bi, 0, 0))

    sh, sh_spec = mod_arg(shift)
    sc, sc_spec = mod_arg(scale)
    rg, rg_spec = mod_arg(rgate)
    return pl.pallas_call(
        _ret_layer_kernel,
        grid=(b, nt),
        in_specs=[rows(d), const(g0.reshape(1, d)), sh_spec, sc_spec, const(w_in_b),
                  pl.BlockSpec((tm, RET_DK // 2), lambda bi, ti: (ti, 0)),
                  pl.BlockSpec((tm, RET_DK // 2), lambda bi, ti: (ti, 0)),
                  const(dmat), const(xi), const(zeta), const(gch), const(gn), state_spec,
                  const(w_out_b), rg_spec, const(g1.reshape(1, d))],
        out_specs=[rows(d), state_spec],
        out_shape=[jax.ShapeDtypeStruct((m, d), F32), jax.ShapeDtypeStruct((b, h, RET_DK, RET_DV), F32)],
        scratch_shapes=[pltpu.VMEM((tm, w_in_b.shape[1]), F32), pltpu.VMEM((tm, h * RET_DV), F32)],
        compiler_params=_cparams("parallel", "arbitrary"),
        name="retention_sublayer",
    )(x, g0.reshape(1, d), sh, sc, w_in_b, cos, sin, dmat, xi, zeta, gch, gn, s0, w_out_b, rg, g1.reshape(1, d))


def _compress_weights(cmp_pos, cmp_w1, cmp_b1, cmp_w2):
    g, d, hd = NSA_KV_HEADS, NSA_HEAD_DIM, CMP_HIDDEN
    eye = jnp.eye(g, dtype=F32)
    w1 = cmp_w1.reshape(2, CMP_BLOCK, d, hd)
    bd1 = jnp.einsum('ksdh,gj->ksgdjh', w1, eye).reshape(2, CMP_BLOCK, g * d, g * hd).astype(BF16)
    bd2 = jnp.einsum('khd,gj->kghjd', cmp_w2, eye).reshape(2, g * hd, g * d).astype(BF16)
    pos = jnp.tile(cmp_pos[:, :, None, :], (1, 1, g, 1)).reshape(2, CMP_BLOCK, 1, g * d)
    b1 = jnp.tile(cmp_b1[:, None, :], (1, g, 1)).reshape(2, 1, g * hd)
    return bd1, bd2, pos, b1


def _compress_finish(p0, p1, b1, bd2):
    rows = p0.shape[0]
    hid = b1 + p0 + pltpu.roll(p1, rows - 1, 0)
    return _dot(_gelu_tanh(hid).astype(BF16), bd2)


CMP_PAGES_PER_STEP = 8
ATT_PAGES_PER_STEP = 16


def _tokens_to_sublanes(xt_ref, tr_ref):
    for half in range(KVW // LANES):
        tr_ref[half] = xt_ref[half * LANES:(half + 1) * LANES, :].T


def _cmp_prompt_kernel(xt_ref, bd1_ref, bd2_ref, pos_ref, b1_ref, o_ref, tr_ref):
    n_sub = xt_ref.shape[1] // CMP_STRIDE
    hid = bd1_ref.shape[-1]
    _tokens_to_sublanes(xt_ref, tr_ref)
    p0 = jnp.zeros((n_sub, hid), F32)
    p1 = jnp.zeros((n_sub, hid), F32)
    for s in range(CMP_STRIDE):
        xs = jnp.concatenate([tr_ref[half, pl.ds(s, n_sub, stride=CMP_STRIDE), :]
                              for half in range(KVW // LANES)], axis=-1)
        p0 = p0 + _dot((xs + pos_ref[s]).astype(BF16), bd1_ref[s])
        p1 = p1 + _dot((xs + pos_ref[CMP_STRIDE + s]).astype(BF16), bd1_ref[CMP_STRIDE + s])
    o_ref[...] = _compress_finish(p0, p1, b1_ref[...], bd2_ref[...])


def compress_prompt(kvt, cw):
    bd1, bd2, pos, b1 = cw
    b, _, t = kvt.shape
    n_sub = t // CMP_STRIDE
    return pl.pallas_call(
        _cmp_prompt_kernel,
        grid=(b, 2),
        in_specs=[pl.BlockSpec((None, KVW, t), lambda bi, ki: (bi, ki, 0)),
                  pl.BlockSpec((None,) + bd1.shape[1:], lambda bi, ki: (ki, 0, 0, 0)),
                  pl.BlockSpec((None,) + bd2.shape[1:], lambda bi, ki: (ki, 0, 0)),
                  pl.BlockSpec((None,) + pos.shape[1:], lambda bi, ki: (ki, 0, 0, 0)),
                  pl.BlockSpec((None,) + b1.shape[1:], lambda bi, ki: (ki, 0, 0))],
        out_specs=pl.BlockSpec((None, None, n_sub, KVW), lambda bi, ki: (ki, bi, 0, 0)),
        out_shape=jax.ShapeDtypeStruct((2, b, n_sub, KVW), F32),
        scratch_shapes=[pltpu.VMEM((KVW // LANES, t, LANES), F32)],
        compiler_params=_cparams("parallel", "parallel"),
        name="compress_prompt",
    )(kvt, bd1, bd2, pos, b1)


def _cmp_sample_kernel(pt_ref, *refs):
    page_refs = refs[:CMP_PAGES_PER_STEP]
    new_ref, bd1_ref, bd2_ref, pos_ref, b1_ref, o_ref, sub_ref, tr_ref = refs[CMP_PAGES_PER_STEP:]
    p = pl.program_id(1)
    page = page_refs[0].shape[2]
    per_page = page // CMP_STRIDE
    n_cmp = o_ref.shape[1]
    for j, page_ref in enumerate(page_refs):
        row0 = pl.multiple_of((p * CMP_PAGES_PER_STEP + j) * per_page, per_page)
        for kind in range(2):
            _tokens_to_sublanes(page_ref.at[kind], tr_ref.at[kind])
            for s in range(CMP_STRIDE):
                for half in range(KVW // LANES):
                    piece = tr_ref[kind, half, pl.ds(s, per_page, stride=CMP_STRIDE), :]
                    sub_ref[kind, s, pl.ds(row0, per_page), half * LANES:(half + 1) * LANES] = piece

    @pl.when(p == pl.num_programs(1) - 1)
    def _():
        n_rows = sub_ref.shape[2]
        tail = n_rows - n_cmp
        hid = bd1_ref.shape[-1]
        for kind in range(2):
            p0 = jnp.zeros((n_rows, hid), F32)
            p1 = jnp.zeros((n_rows, hid), F32)
            for s in range(CMP_STRIDE):
                sub_ref[kind, s, pl.ds(n_cmp, tail), :] = jnp.broadcast_to(new_ref[kind, s:s + 1, :], (tail, KVW))
                xs = sub_ref[kind, s]
                p0 = p0 + _dot((xs + pos_ref[kind, s]).astype(BF16), bd1_ref[kind, s])
                p1 = p1 + _dot((xs + pos_ref[kind, CMP_STRIDE + s]).astype(BF16), bd1_ref[kind, CMP_STRIDE + s])
            o_ref[kind] = _compress_finish(p0, p1, b1_ref[kind], bd2_ref[kind])[:n_cmp]


def _page_specs(n_per_step, kind_block, page):
    def spec(j):
        return pl.BlockSpec((None, 2, KVW, page),
                            lambda bi, pi, pt: (pt[bi, pi * n_per_step + j], kind_block, 0, 0))
    return [spec(j) for j in range(n_per_step)]


def compress_sample(cache_t, page_table, new_sub, cw):
    bd1, bd2, pos, b1 = cw
    b, n_pages = page_table.shape
    page = cache_t.shape[3]
    n_cmp = n_pages * page // CMP_STRIDE
    nps = CMP_PAGES_PER_STEP
    const = lambda a: pl.BlockSpec(a.shape, lambda bi, pi, pt: (0,) * a.ndim, pipeline_mode=pl.Buffered(1))
    grid_spec = pltpu.PrefetchScalarGridSpec(
        num_scalar_prefetch=1,
        grid=(b, n_pages // nps),
        in_specs=_page_specs(nps, 0, page) + [
            pl.BlockSpec((2, None, CMP_STRIDE, KVW), lambda bi, pi, pt: (0, bi, 0, 0)),
            const(bd1), const(bd2), const(pos), const(b1)],
        out_specs=pl.BlockSpec((2, None, n_cmp, KVW), lambda bi, pi, pt: (0, bi, 0, 0)),
        scratch_shapes=[pltpu.VMEM((2, CMP_STRIDE, n_cmp + 8, KVW), F32),
                        pltpu.VMEM((2, KVW // LANES, page, LANES), F32)],
    )
    return pl.pallas_call(
        _cmp_sample_kernel,
        grid_spec=grid_spec,
        out_shape=jax.ShapeDtypeStruct((2, b, n_cmp, KVW), F32),
        compiler_params=_cparams("parallel", "arbitrary"),
        name="compress_sample",
    )(page_table, *([cache_t] * nps), new_sub, bd1, bd2, pos, b1)


def _cover_matrix(n_cmp, n_sel, rows, cols):
    i = np.arange(n_cmp)[:, None]
    j = np.arange(n_sel)[None, :]
    cover = (i * CMP_STRIDE < (j + 1) * SEL_BLOCK) & (i * CMP_STRIDE + CMP_BLOCK > j * SEL_BLOCK)
    out = np.zeros((rows, cols), np.float32)
    out[:n_cmp, :n_sel] = cover
    return jnp.asarray(out)


SEL_CHUNK = 256
SEL_CLASS = 512


def _select_blocks_t(imp_t, cur, n_sel):
    nb, nq = imp_t.shape
    blk = lax.broadcasted_iota(I32, (nb, nq), 0)
    valid = (blk <= cur) & (blk < n_sel)
    forced = (blk == 0) | (blk == cur) | (blk == cur - 1)
    score = jnp.where(valid, jnp.where(forced, FORCED_SCORE, imp_t), -1.0)
    rank = jnp.zeros((nb, nq), F32)
    for i in range(n_sel):
        ci = score[i:i + 1, :]
        beats = (ci > score) | ((ci == score) & (blk > i))
        rank = rank + jnp.where(beats, 1.0, 0.0)
    return jnp.where(valid & (rank < float(min(N_SEL, n_sel))), 1.0, 0.0)


def _nsa_prompt_kernel(qt_ref, glt_ref, kc_ref, vct_ref, covt_ref, kblk_ref, wb_ref, ks_ref, vst_ref, kw_ref, vwt_ref,
                       o_ref, s_ref, osel_ref, *, n_cmp, n_sel):
    tq = qt_ref.shape[1]
    r, d = NSA_GROUP, NSA_HEAD_DIM
    cols = r * tq
    t = ks_ref.shape[0]
    q0 = pl.program_id(2) * tq
    qt = qt_ref[...] * (d ** -0.5)
    qcat = jnp.concatenate([qt[h * d:(h + 1) * d, :] for h in range(r)], axis=-1)
    qb = jnp.concatenate([qcat, jnp.zeros_like(qcat)], axis=0).astype(BF16)
    gl = _sigmoid(glt_ref[...])

    def gate(branch):
        return jnp.concatenate([jnp.broadcast_to(gl[h * N_BRANCH + branch:h * N_BRANCH + branch + 1, :], (d, tq))
                                for h in range(r)], axis=0)

    def stack(o_t):
        return jnp.concatenate([o_t[:, h * tq:(h + 1) * tq] for h in range(r)], axis=0)

    def tile(x):
        return jnp.concatenate([x] * r, axis=-1)

    ones_rows = 16

    def with_ones(vt):
        return jnp.concatenate([vt.astype(BF16), jnp.ones((ones_rows, vt.shape[1]), BF16)], axis=0)

    qpos = q0 + lax.broadcasted_iota(I32, (1, tq), 1)

    span = min(WINDOW + tq, t)
    w0 = pl.multiple_of(jnp.maximum(q0 + tq - span, 0), tq)
    sm = _dot(kw_ref[pl.ds(w0, span), :], qb) + tile(wb_ref[(q0 - w0) // tq])
    p = jnp.exp(sm - jnp.max(sm, axis=0, keepdims=True))
    o_win_t = _dot(with_ones(vwt_ref[:, pl.ds(w0, span)]), p.astype(BF16))
    out = gate(2) * stack(o_win_t[:d] / jnp.maximum(o_win_t[d:d + 1], 1e-30))

    ncp = kc_ref.shape[0]
    nn = lax.broadcasted_iota(I32, (ncp, tq), 0)
    ok = tile(jnp.where((nn * CMP_STRIDE + (CMP_BLOCK - 1) <= qpos) & (nn < n_cmp), 1.0, 0.0))
    sm = _dot(kc_ref[...].astype(BF16), qb) + (ok - 1.0) * (-NEG)
    e = jnp.exp(sm - jnp.max(sm, axis=0, keepdims=True)) * ok
    p_cmp = e / jnp.maximum(jnp.sum(e, axis=0, keepdims=True), 1e-30)
    out = out + gate(0) * stack(_dot(vct_ref[...].astype(BF16), p_cmp.astype(BF16)))

    psum_t = p_cmp[:, :tq]
    for h in range(1, r):
        psum_t = psum_t + p_cmp[:, h * tq:(h + 1) * tq]
    imp_t = jnp.dot(covt_ref[...], psum_t, precision=HIGHEST, preferred_element_type=F32)
    sel_t = _select_blocks_t(imp_t, qpos >> 6, n_sel)
    nb = sel_t.shape[0]
    q_sel = jnp.concatenate([qcat, tile((sel_t - 1.0) * (-NEG)), jnp.zeros((LANES - d - nb, cols), F32)],
                            axis=0).astype(BF16)

    groups = SEL_CHUNK // 8
    sel_class = SEL_CLASS if t % SEL_CLASS == 0 else t

    def sel_branch(n_keys):
        m_run = jnp.full((8, cols), NEG, F32)
        for k0 in range(0, n_keys, SEL_CHUNK):
            k_aug = ks_ref[k0:k0 + SEL_CHUNK, :] + kblk_ref[k0:k0 + SEL_CHUNK, :]
            sm = _dot(k_aug, q_sel)
            if k0 + SEL_CHUNK > n_keys - sel_class:
                keypos = k0 + lax.broadcasted_iota(I32, (SEL_CHUNK, tq), 0)
                sm = sm + tile(jnp.where(keypos <= qpos, 0.0, NEG))
            s_ref[k0:k0 + SEL_CHUNK, :] = sm
            m_run = jnp.maximum(m_run, jnp.max(sm.reshape(groups, 8, cols), axis=0))
        m_sel = jnp.max(m_run, axis=0, keepdims=True)
        acc_t = jnp.zeros((d + ones_rows, cols), F32)
        for k0 in range(0, n_keys, SEL_CHUNK):
            p = jnp.exp(s_ref[k0:k0 + SEL_CHUNK, :] - m_sel)
            acc_t = acc_t + _dot(with_ones(vst_ref[:, k0:k0 + SEL_CHUNK]), p.astype(BF16))
        osel_ref[...] = acc_t[:d] / jnp.maximum(acc_t[d:d + 1], 1e-30)

    cls_id = pl.program_id(2) // (sel_class // tq)
    for cls in range(t // sel_class):
        pl.when(cls_id == cls)(functools.partial(sel_branch, (cls + 1) * sel_class))
    o_ref[...] = (out + gate(1) * stack(osel_ref[...])).T


GATE_ROWS = 16


def nsa_prompt(qt, glt, kc, vct, kpad, kvt, wint):
    b, _, t = qt.shape
    g, r, d = NSA_KV_HEADS, NSA_GROUP, NSA_HEAD_DIM
    tq = Q_BLOCK
    nt = t // tq
    ncp = kc.shape[2]
    n_cmp = t // CMP_STRIDE - 1
    n_sel = -(-t // SEL_BLOCK)
    nb = -(-n_sel // 8) * 8
    covt = _cover_matrix(n_cmp, n_sel, ncp, nb).T
    assert d + nb <= LANES, "block one-hot must fit in the keys' padding lanes"
    kblk = np.zeros((t, LANES), np.float32)
    kblk[np.arange(t), d + np.arange(t) // SEL_BLOCK] = 1.0
    kblk = jnp.asarray(kblk, BF16)
    span = min(WINDOW + tq, t)
    i = np.arange(span)[None, :, None]
    j = np.arange(tq)[None, None, :]
    rel = i - j - np.arange(0, span - tq + 1, tq)[:, None, None]
    wbias = jnp.asarray(np.where((rel <= 0) & (rel > -WINDOW), 0.0, NEG), F32)
    per_bg = lambda shape: pl.BlockSpec((None, None) + shape, lambda bi, gi, qi: (bi, gi, 0, 0))
    const = lambda a: pl.BlockSpec(a.shape, lambda bi, gi, qi: (0,) * a.ndim)
    return pl.pallas_call(
        functools.partial(_nsa_prompt_kernel, n_cmp=n_cmp, n_sel=n_sel),
        grid=(b, g, nt),
        in_specs=[pl.BlockSpec((None, r * d, tq), lambda bi, gi, qi: (bi, gi, qi)),
                  pl.BlockSpec((None, GATE_ROWS, tq), lambda bi, gi, qi: (bi, gi, qi)),
                  per_bg((ncp, LANES)), per_bg((d, ncp)), const(covt), const(kblk), const(wbias),
                  pl.BlockSpec((t, LANES), lambda bi, gi, qi: (bi, gi)),
                  pl.BlockSpec((None, d, t), lambda bi, gi, qi: (bi, 3 * g + gi, 0)),
                  pl.BlockSpec((t, LANES), lambda bi, gi, qi: (bi, g + gi)),
                  pl.BlockSpec((None, d, t), lambda bi, gi, qi: (bi, g + gi, 0))],
        out_specs=pl.BlockSpec((tq, r * d), lambda bi, gi, qi: (bi * nt + qi, gi)),
        out_shape=jax.ShapeDtypeStruct((b * t, NSA_HEADS * d), F32),
        scratch_shapes=[pltpu.VMEM((t, r * tq), F32), pltpu.VMEM((d, r * tq), F32)],
        compiler_params=_cparams("parallel", "parallel", "arbitrary"),
        name="nsa_prompt",
    )(qt, glt, kc, vct, covt, kblk, wbias, kpad, kvt, kpad, wint)


def _nsa_sample_select_kernel(q_ref, kc_ref, vc_ref, cov_ref, oc_ref, sel_ref, *, t, past, n_cmp, n_sel):
    rows = q_ref.shape[0]
    g, r = NSA_KV_HEADS, NSA_GROUP
    qb = q_ref[...].astype(BF16)
    qpos_r = past + (lax.broadcasted_iota(I32, (rows, 1), 0) & (t - 1))
    s = _dot_nt(qb, kc_ref[...].astype(BF16))
    nn = lax.broadcasted_iota(I32, s.shape, 1)
    p_cmp = _masked_softmax(s, (nn * CMP_STRIDE + (CMP_BLOCK - 1) <= qpos_r) & (nn < n_cmp))
    oc_ref[...] = _dot(p_cmp.astype(BF16), vc_ref[...].astype(BF16))
    psum = jnp.sum(p_cmp.reshape(g, r, t, s.shape[1]), axis=1).reshape(g * t, s.shape[1])
    imp = jnp.dot(psum, cov_ref[...], precision=HIGHEST, preferred_element_type=F32)
    qpos_gt = past + (lax.broadcasted_iota(I32, (g * t, 1), 0) & (t - 1))
    sel = _select_blocks(imp, qpos_gt >> 6, n_sel)
    lanes = sel.shape[1]
    sel_ref[...] = jnp.broadcast_to(sel.reshape(g, 1, t, lanes), (g, r, t, lanes)).reshape(rows, lanes)


def nsa_sample_select(q4, cmp, *, t, past):
    b, rows, _ = q4.shape
    n_cmp = cmp.shape[2]
    n_sel = -(-(past + t) // SEL_BLOCK)
    lanes = -(-n_sel // 128) * 128
    cover = _cover_matrix(n_cmp, n_sel, n_cmp, lanes)
    return pl.pallas_call(
        functools.partial(_nsa_sample_select_kernel, t=t, past=past, n_cmp=n_cmp, n_sel=n_sel),
        grid=(b,),
        in_specs=[pl.BlockSpec((None, rows, KVW), lambda bi: (bi, 0, 0)),
                  pl.BlockSpec((None, None, n_cmp, KVW), lambda bi: (0, bi, 0, 0)),
                  pl.BlockSpec((None, None, n_cmp, KVW), lambda bi: (1, bi, 0, 0)),
                  pl.BlockSpec(cover.shape, lambda bi: (0, 0))],
        out_specs=[pl.BlockSpec((None, rows, KVW), lambda bi: (bi, 0, 0)),
                   pl.BlockSpec((None, rows, lanes), lambda bi: (bi, 0, 0))],
        out_shape=[jax.ShapeDtypeStruct((b, rows, KVW), F32),
                   jax.ShapeDtypeStruct((b, rows, lanes), F32)],
        compiler_params=_cparams("parallel"),
        name="nsa_sample_select",
    )(q4, cmp, cmp, cover)


def _nsa_sample_attend_kernel(pt_ref, q_ref, sel_ref, oc_ref, gl_ref, *refs, t, past):
    page_refs = refs[:ATT_PAGES_PER_STEP]
    new_ref, win_ref, eb_ref, o_ref, m_ref, l_ref, acc_ref = refs[ATT_PAGES_PER_STEP:]
    rows = q_ref.shape[0]
    p = pl.program_id(1)
    page = page_refs[0].shape[2]
    width = ATT_PAGES_PER_STEP * page
    qb = q_ref[...].astype(BF16)
    sel = sel_ref[...]
    nblk = sel.shape[1]

    @pl.when(p == 0)
    def _():
        _flash_init(m_ref, l_ref, acc_ref)

    sc = jnp.concatenate([_dot(qb, pr[0].astype(BF16)) for pr in page_refs], axis=-1)
    blocks_per_step = width // SEL_BLOCK
    e0 = pl.multiple_of((pl.num_programs(1) - 1 - p) * blocks_per_step, blocks_per_step)
    mask = _dot(sel.astype(BF16), eb_ref[pl.ds(e0, nblk), :]) > 0.5
    sm = jnp.where(mask, sc, NEG)
    m_old = m_ref[...]
    m_new = jnp.maximum(m_old, jnp.max(sm, axis=-1, keepdims=True))
    alpha = jnp.exp(m_old - m_new)
    pr_b = jnp.where(mask, jnp.exp(sm - m_new), 0.0)
    l_ref[...] = alpha * l_ref[...] + jnp.sum(pr_b, axis=-1, keepdims=True)
    pr_b = pr_b.astype(BF16)
    pv = _dot_nt(pr_b[:, :page], page_refs[0][1].astype(BF16))
    for j in range(1, ATT_PAGES_PER_STEP):
        pv = pv + _dot_nt(pr_b[:, j * page:(j + 1) * page], page_refs[j][1].astype(BF16))
    acc_ref[...] = alpha * acc_ref[...] + pv
    m_ref[...] = m_new

    @pl.when(p == pl.num_programs(1) - 1)
    def _():
        qpos_r = past + (lax.broadcasted_iota(I32, (rows, 1), 0) & (t - 1))
        tp = LANES
        newr = jnp.concatenate([new_ref[...], jnp.zeros((tp - t, new_ref.shape[1]), F32)], axis=0)
        jn = lax.broadcasted_iota(I32, (rows, tp), 1)
        new_ok = (jn < t) & (past + jn <= qpos_r)
        blk_new = past // SEL_BLOCK
        sel_new = sel[:, blk_new:blk_new + 1] > 0.5
        sc_n = _dot_nt(qb, newr[:, 2 * KVW:3 * KVW].astype(BF16))
        _flash_update(sc_n, new_ok & sel_new, newr[:, 3 * KVW:4 * KVW].astype(BF16), m_ref, l_ref, acc_ref)
        o_sel = _flash_result(l_ref, acc_ref)

        n_win = win_ref.shape[2]
        s_a = _dot(qb, win_ref[0].astype(BF16))
        s_b = _dot_nt(qb, newr[:, 4 * KVW:5 * KVW].astype(BF16))
        kp_a = (past - n_win) + lax.broadcasted_iota(I32, (rows, n_win), 1)
        ok_a = (kp_a <= qpos_r) & (kp_a > qpos_r - WINDOW) & (kp_a >= 0)
        ok_b = new_ok & (past + jn > qpos_r - WINDOW)
        sm_a = jnp.where(ok_a, s_a, NEG)
        sm_b = jnp.where(ok_b, s_b, NEG)
        mx = jnp.maximum(jnp.max(sm_a, axis=-1, keepdims=True), jnp.max(sm_b, axis=-1, keepdims=True))
        e_a = jnp.where(ok_a, jnp.exp(sm_a - mx), 0.0)
        e_b = jnp.where(ok_b, jnp.exp(sm_b - mx), 0.0)
        den = jnp.maximum(jnp.sum(e_a, axis=-1, keepdims=True) + jnp.sum(e_b, axis=-1, keepdims=True), 1e-30)
        o_win = (_dot_nt((e_a / den).astype(BF16), win_ref[1].astype(BF16))
                 + _dot((e_b / den).astype(BF16), newr[:, 5 * KVW:6 * KVW].astype(BF16)))

        gates = _sigmoid(gl_ref[...])
        o_ref[...] = gates[:, 0:1] * oc_ref[...] + gates[:, 1:2] * o_sel + gates[:, 2:3] * o_win


def nsa_sample_attend(q4, sel, ocmp, gl, cache_t, page_table, new_rows, win_t, *, t, past):
    b, rows, _ = q4.shape
    n_pages = page_table.shape[1]
    page = cache_t.shape[3]
    lanes = sel.shape[2]
    tp = new_rows.shape[1]
    n_win = win_t.shape[3]
    nps = ATT_PAGES_PER_STEP
    n_steps = n_pages // nps
    blocks_per_step = nps * page // SEL_BLOCK
    shift = (n_steps - 1) * blocks_per_step
    expand = jnp.asarray((np.arange(lanes + shift)[:, None] - shift) == (np.arange(nps * page)[None, :] // SEL_BLOCK), BF16)
    per_b = lambda shape: pl.BlockSpec((None,) + shape, lambda bi, pi, pt: (bi,) + (0,) * len(shape))
    grid_spec = pltpu.PrefetchScalarGridSpec(
        num_scalar_prefetch=1,
        grid=(b, n_pages // nps),
        in_specs=[per_b((rows, KVW)), per_b((rows, lanes)), per_b((rows, KVW)), per_b((rows, N_BRANCH))]
        + _page_specs(nps, 1, page)
        + [per_b((tp, N_KV_ROWS * KVW)), per_b((2, KVW, n_win)),
           pl.BlockSpec(expand.shape, lambda bi, pi, pt: (0, 0))],
        out_specs=per_b((rows, KVW)),
        scratch_shapes=[pltpu.VMEM((rows, 1), F32), pltpu.VMEM((rows, 1), F32), pltpu.VMEM((rows, KVW), F32)],
    )
    return pl.pallas_call(
        functools.partial(_nsa_sample_attend_kernel, t=t, past=past),
        grid_spec=grid_spec,
        out_shape=jax.ShapeDtypeStruct((b, rows, KVW), F32),
        compiler_params=_cparams("parallel", "arbitrary"),
        name="nsa_sample_attend",
    )(page_table, q4, sel, ocmp, gl, *([cache_t] * nps), new_rows, win_t, expand)


def _route_kernel(x_ref, g_ref, sh_ref, sc_ref, rw_ref, rb_ref, h_ref, r_ref):
    h = _norm_mod(x_ref[...], g_ref[...], sh_ref[...], sc_ref[...])
    h_ref[...] = h
    logits = jnp.dot(h, rw_ref[...], precision=HIGHEST, preferred_element_type=F32) + rb_ref[...]
    lane = lax.broadcasted_iota(I32, logits.shape, 1)
    lane_f = lane.astype(F32)
    lg = jnp.where(lane < N_EXPERTS, logits, NEG)
    v1 = jnp.max(lg, axis=-1, keepdims=True)
    i1 = jnp.min(jnp.where(lg == v1, lane_f, 128.0), axis=-1, keepdims=True)
    lg2 = jnp.where(lane_f == i1, NEG, lg)
    v2 = jnp.max(lg2, axis=-1, keepdims=True)
    i2 = jnp.min(jnp.where(lg2 == v2, lane_f, 128.0), axis=-1, keepdims=True)
    e = jnp.exp(v2 - v1)
    w1 = 1.0 / (1.0 + e)
    w2 = e / (1.0 + e)
    r_ref[...] = jnp.where(lane == 0, i1, jnp.where(lane == 1, i2, jnp.where(lane == 2, w1, jnp.where(lane == 3, w2, 0.0))))


def moe_route(x, g, shift, scale, rw_pad, rb_pad, *, per_token, tm, tpb):
    m, d = x.shape
    sh, sh_spec = _mod_arg(shift, per_token, tm, tpb)
    sc, sc_spec = _mod_arg(scale, per_token, tm, tpb)
    return pl.pallas_call(
        _route_kernel,
        grid=(m // tm,),
        in_specs=[pl.BlockSpec((tm, d), lambda i: (i, 0)), _vec_spec(d), sh_spec, sc_spec,
                  pl.BlockSpec((d, 128), lambda i: (0, 0)), _vec_spec(128)],
        out_specs=[pl.BlockSpec((tm, d), lambda i: (i, 0)), pl.BlockSpec((tm, 128), lambda i: (i, 0))],
        out_shape=[jax.ShapeDtypeStruct((m, d), F32), jax.ShapeDtypeStruct((m, 128), F32)],
        compiler_params=_cparams("parallel"),
        name="moe_route",
    )(x, g.reshape(1, d), sh, sc, rw_pad, rb_pad)


def _moe_kernel(te_ref, nv_ref, x_ref, wg_ref, wu_ref, wd_ref, o_ref, xb_ref, acc_ref):
    i = pl.program_id(0)
    f = pl.program_id(1)

    @pl.when(f == 0)
    def _():
        acc_ref[...] = jnp.zeros(acc_ref.shape, F32)
        xb_ref[...] = x_ref[...].astype(BF16)

    @pl.when(i < nv_ref[0])
    def _():
        x = xb_ref[...]
        act = _silu(_dot(x, wg_ref[...])) * _dot(x, wu_ref[...])
        acc_ref[...] += _dot(act.astype(BF16), wd_ref[...])

    @pl.when(f == pl.num_programs(1) - 1)
    def _():
        o_ref[...] = acc_ref[...]


def moe_experts(xs, tile_expert, n_valid, w_gu_b, w_down_b, *, tm, tf):
    p, d = xs.shape
    edim = w_down_b.shape[1]
    nf = edim // tf
    n_tiles = p // tm

    def wmap(off):
        def index(i, f, te, nv):
            ok = i < nv[0]
            return (te[i], 0, off + jnp.where(ok, f, nf - 1))
        return index

    def dmap(i, f, te, nv):
        return (te[i], jnp.where(i < nv[0], f, nf - 1), 0)

    grid_spec = pltpu.PrefetchScalarGridSpec(
        num_scalar_prefetch=2,
        grid=(n_tiles, nf),
        in_specs=[pl.BlockSpec((tm, d), lambda i, f, te, nv: (i, 0)),
                  pl.BlockSpec((None, d, tf), wmap(0)),
                  pl.BlockSpec((None, d, tf), wmap(nf)),
                  pl.BlockSpec((None, tf, d), dmap)],
        out_specs=pl.BlockSpec((tm, d), lambda i, f, te, nv: (i, 0)),
        scratch_shapes=[pltpu.VMEM((tm, d), BF16), pltpu.VMEM((tm, d), F32)],
    )
    return pl.pallas_call(
        _moe_kernel,
        grid_spec=grid_spec,
        out_shape=jax.ShapeDtypeStruct((p, d), F32),
        compiler_params=_cparams("parallel", "arbitrary"),
        name="moe_experts",
    )(tile_expert, n_valid, xs, w_gu_b, w_gu_b, w_down_b)


def _moe_tables(route, tm):
    m = route.shape[0]
    na = TOP_K * m
    e = route[:, :TOP_K].astype(I32).reshape(na)
    onehot = (e[:, None] == jnp.arange(N_EXPERTS, dtype=I32)[None, :]).astype(I32)
    within = jnp.sum((jnp.cumsum(onehot, axis=0) - onehot) * onehot, axis=1)
    counts = jnp.sum(onehot, axis=0)
    padded = ((counts + tm - 1) // tm) * tm
    ends = jnp.cumsum(padded)
    starts = ends - padded
    dest = starts[e] + within
    n_slots = (-(-na // tm) + N_EXPERTS) * tm
    src_tok = jnp.zeros((n_slots,), I32).at[dest].set(jnp.arange(na, dtype=I32) // TOP_K,
                                                       mode='promise_in_bounds', unique_indices=True)
    n_tiles = n_slots // tm
    n_valid = (ends[-1] // tm).astype(I32)
    tile_start = jnp.arange(n_tiles, dtype=I32) * tm
    tile_expert = jnp.sum((tile_start[:, None] >= ends[None, :]).astype(I32), axis=1)
    last = jnp.take(tile_expert, jnp.maximum(n_valid - 1, 0))
    tile_expert = jnp.where(jnp.arange(n_tiles) < n_valid, tile_expert, last).astype(I32)
    return src_tok, tile_expert, n_valid.reshape(1), dest


def _split_mod(mod):
    return [mod[:, i * D_MODEL:(i + 1) * D_MODEL] for i in range(mod.shape[1] // D_MODEL)]


def kernel(x_prompt, x_sample, c_prompt, c_sample, state_ret, cache_kv, cache_win, page_table, w_mod, b_mod, norm_g, ret_w_in, ret_gn_g, ret_w_out, kv_w_mod, kv_b_mod, kv_norm_g, kv_w, cmp_pos, cmp_w1, cmp_b1, cmp_w2, nsa_w_in, nsa_w_out, ffn_w_gu, ffn_w_down, moe_router_w, moe_router_b, moe_w_gu, moe_w_down):
    bp, t, d = x_prompt.shape
    bs, ts, _ = x_sample.shape
    mp, ms = bp * t, bs * ts
    n_pool, page = cache_kv.shape[:2]
    past = page_table.shape[1] * page
    g, r, hd = NSA_KV_HEADS, NSA_GROUP, NSA_HEAD_DIM

    ret_w_in_b = ret_w_in[0].astype(BF16)
    ret_w_out_b = ret_w_out[0].astype(BF16)
    kv_w_b = kv_w.astype(BF16)
    kv_wt_b = kv_w.T.astype(BF16)
    nq = NSA_HEADS * hd
    gate_w = nsa_w_in[0][:, nq:].reshape(d, g, r * N_BRANCH)
    gate_w = jnp.pad(gate_w, ((0, 0), (0, 0), (0, LANES - r * N_BRANCH))).reshape(d, g * LANES)
    nsa_w_in_b = jnp.concatenate([nsa_w_in[0][:, :nq], gate_w], axis=1).astype(BF16)
    gate_wt = nsa_w_in[0][:, nq:].T.reshape(g, r * N_BRANCH, d)
    gate_wt = jnp.pad(gate_wt, ((0, 0), (0, GATE_ROWS - r * N_BRANCH), (0, 0))).reshape(g * GATE_ROWS, d)
    nsa_wt_b = jnp.concatenate([nsa_w_in[0][:, :nq].T, gate_wt], axis=0).astype(BF16)
    def pad_heads(w):
        return jnp.pad(w.reshape(d, g, hd), ((0, 0), (0, 0), (0, LANES - hd))).reshape(d, g * LANES)
    kpad_w_b = jnp.concatenate([pad_heads(kv_w[:, 2 * KVW:3 * KVW]), pad_heads(kv_w[:, 4 * KVW:5 * KVW])],
                               axis=1).astype(BF16)
    nsa_w_out_b = nsa_w_out[0].astype(BF16)
    ffn_w_gu_b = ffn_w_gu[0].astype(BF16)
    ffn_w_down_b = ffn_w_down[0].astype(BF16)
    moe_w_gu_b = moe_w_gu[0].astype(BF16)
    moe_w_down_b = moe_w_down[0].astype(BF16)
    x_prompt, moe_w_gu_b, moe_w_down_b = lax.optimization_barrier((x_prompt, moe_w_gu_b, moe_w_down_b))
    rw_pad = jnp.pad(moe_router_w[0], ((0, 0), (0, 128 - N_EXPERTS)))
    rb_pad = jnp.pad(moe_router_b[0], (0, 128 - N_EXPERTS)).reshape(1, 128)
    cw = _compress_weights(cmp_pos, cmp_w1, cmp_b1, cmp_w2)

    c_all = jnp.concatenate([c_prompt, c_sample], axis=0)
    mods = [cond_matmul(c_all, w_mod, b_mod, layer) for layer in range(w_mod.shape[0])]
    kv_mod = cond_matmul(c_all, kv_w_mod[None], kv_b_mod[None], 0)

    tm_p = min(512, t)
    tm_f = min(1024, t)
    groups = {
        'p': dict(x=x_prompt.reshape(mp, d), b=bp, t=t, per_token=False, tm=tm_p, tpb=t // tm_p,
                  tm_f=tm_f, tpb_f=t // tm_f,
                  mod=lambda a: a[:bp]),
        's': dict(x=x_sample.reshape(ms, d), b=bs, t=ts, per_token=True, tm=ms, tpb=1, tm_f=ms, tpb_f=1,
                  mod=lambda a: jnp.repeat(a[bp:], ts, axis=0)),
    }
    out = {}

    for name, gr in groups.items():
        m0 = [gr['mod'](a) for a in _split_mod(mods[0])]
        if name == 'p':
            pos = jnp.arange(t)
            s0 = jnp.zeros((bp, RET_HEADS, RET_DK, RET_DV), F32)
        else:
            pos = past + jnp.arange(ts)
            s0 = state_ret[0]
        x, s_new = retention_sublayer(gr['x'], norm_g[0, 0], m0[0], m0[1], ret_w_in_b, pos, s0, ret_gn_g[0],
                                      ret_w_out_b, m0[2], norm_g[0, 1], b=gr['b'], per_token=gr['per_token'],
                                      chunk=math.gcd(gr['t'], RET_CHUNK))
        out['ret_' + name] = s_new[None]
        x = ffn_sublayer(x, norm_g[0, 2], m0[3], m0[4], ffn_w_gu_b, ffn_w_down_b, m0[5], norm_g[0, 3],
                         per_token=gr['per_token'], tm=gr['tm_f'], tpb=gr['tpb_f'],
                         tf=_largest_tile(ffn_w_down.shape[1], FFN_HIDDEN_TILE_CAP))
        gr['x1'] = x

    def attention(name):
        gr = groups[name]
        kw = dict(per_token=gr['per_token'], tm=gr['tm'], tpb=gr['tpb'])
        m1 = [gr['mod'](a) for a in _split_mod(mods[1])]
        kvm = [gr['mod'](a) for a in _split_mod(kv_mod)]
        gr['m1'] = m1
        x = gr['x1']
        b_, t_ = gr['b'], gr['t']
        if name == 'p':
            kvt, wint = norm_mod_matmul_t(x, kv_norm_g, kvm[0], kvm[1], kv_wt_b,
                                          (N_PAGED_ROWS * KVW, (N_KV_ROWS - N_PAGED_ROWS) * KVW), b=b_, tm=gr['tm'])
            out['kv_p'] = kvt.reshape(b_, N_PAGED_ROWS, g, hd, t_).transpose(0, 4, 1, 2, 3)
            n_keep = min(WINDOW, t_)
            out['win_p'] = wint[:, :, t_ - n_keep:].reshape(b_, 2, g, hd, n_keep).transpose(0, 4, 1, 2, 3)
            kpad = norm_mod_matmul(x, kv_norm_g, kvm[0], kvm[1], kpad_w_b, out_dtype=BF16, **kw)
            qt, glt = norm_mod_matmul_t(x, norm_g[1, 0], m1[0], m1[1], nsa_wt_b, (nq, g * GATE_ROWS),
                                        b=b_, tm=gr['tm'])
            cmp = compress_prompt(kvt, cw).reshape(2, b_, -1, g, hd)
            kc = jnp.pad(cmp[0].transpose(0, 2, 1, 3), ((0, 0), (0, 0), (0, 0), (0, LANES - hd)))
            o = nsa_prompt(qt, glt, kc, cmp[1].transpose(0, 2, 3, 1), kpad, kvt, wint)
        else:
            proj = norm_mod_matmul(x, norm_g[1, 0], m1[0], m1[1], nsa_w_in_b, **kw)
            rows = norm_mod_matmul(x, kv_norm_g, kvm[0], kvm[1], kv_w_b, **kw)
            q = (proj[:, :NSA_HEADS * hd] * (hd ** -0.5)).reshape(b_, t_, g, r, hd)
            gl = proj[:, NSA_HEADS * hd:].reshape(b_, t_, g, LANES)[..., :r * N_BRANCH]
            gl = gl.reshape(b_, t_, g, r, N_BRANCH)
            rows3 = rows.reshape(b_, t_, N_KV_ROWS * KVW)
            rows6 = rows.reshape(b_, t_, N_KV_ROWS, g, hd)
            out['kv_s'] = rows6[:, :, :N_PAGED_ROWS]
            cache_t = cache_kv.transpose(0, 2, 3, 4, 1).reshape(n_pool, N_PAGED_ROWS, KVW, page)
            win_t5 = cache_win.transpose(0, 2, 3, 4, 1)
            n_win = cache_win.shape[1]
            new_win_t = rows6[:, :, N_PAGED_ROWS:].transpose(0, 2, 3, 4, 1)
            out['win_s'] = jnp.concatenate([win_t5, new_win_t], axis=-1)[..., t_:].transpose(0, 4, 1, 2, 3)
            new_sub = jnp.pad(rows3[:, :, :2 * KVW], ((0, 0), (0, CMP_STRIDE - t_), (0, 0)))
            new_sub = new_sub.reshape(b_, CMP_STRIDE, 2, KVW).transpose(2, 0, 1, 3)
            cmp = compress_sample(cache_t, page_table, new_sub, cw)
            eye = jnp.eye(g, dtype=F32)
            qrows = q.transpose(0, 2, 3, 1, 4).reshape(b_, g, r * t_, hd)
            q4 = jnp.einsum('bgxd,gj->bgxjd', qrows, eye).reshape(b_, g * r * t_, KVW)
            glr = gl.transpose(0, 2, 3, 1, 4).reshape(b_, g * r * t_, N_BRANCH)
            ocmp, sel = nsa_sample_select(q4, cmp, t=t_, past=past)
            o4 = nsa_sample_attend(q4, sel, ocmp, glr, cache_t, page_table, rows3,
                                   win_t5.reshape(b_, 2, KVW, n_win), t=t_, past=past)
            o4 = o4.reshape(b_, g, r, t_, g, hd)
            o = jnp.einsum('bgrtjd,gj->btgrd', o4, eye).reshape(ms, NSA_HEADS * hd)
        gr['x2'] = matmul_norm_residual(o, nsa_w_out_b, x, m1[2], norm_g[1, 1], **kw)

    take = lambda a, idx: a.at[idx].get(mode='promise_in_bounds')

    def moe_plan(name):
        gr = groups[name]
        kw = dict(per_token=gr['per_token'], tm=gr['tm'], tpb=gr['tpb'])
        h, route = moe_route(gr['x2'], norm_g[1, 2], gr['m1'][3], gr['m1'][4], rw_pad, rb_pad, **kw)
        tm_e = 512 if h.shape[0] >= 4096 else 128
        src_tok, tile_expert, n_valid, dest = _moe_tables(route, tm_e)
        return dict(h=h, src_tok=src_tok, route=route, tile_expert=tile_expert, n_valid=n_valid,
                    dest=dest.reshape(-1, TOP_K), tm_e=tm_e)

    def moe_finish(name, dp, xs):
        gr = groups[name]
        kw = dict(per_token=gr['per_token'], tm=gr['tm'], tpb=gr['tpb'])
        ys = moe_experts(xs, dp['tile_expert'], dp['n_valid'], moe_w_gu_b, moe_w_down_b, tm=dp['tm_e'],
                         tf=_largest_tile(moe_w_down.shape[2], MOE_HIDDEN_TILE_CAP))
        out['y_' + name] = moe_combine(take(ys, dp['dest'][:, 0]), take(ys, dp['dest'][:, 1]), dp['route'],
                                       gr['x2'], gr['m1'][5], norm_g[1, 3], **kw)

    attention('p')
    plan_p = moe_plan('p')
    plan_p['src_tok'], groups['s']['x1'] = lax.optimization_barrier((plan_p['src_tok'], groups['s']['x1']))
    xs_p = take(plan_p['h'], plan_p['src_tok'])
    attention('s')
    xs_p, groups['s']['x2'] = lax.optimization_barrier((xs_p, groups['s']['x2']))
    moe_finish('p', plan_p, xs_p)
    plan_s = moe_plan('s')
    moe_finish('s', plan_s, take(plan_s['h'], plan_s['src_tok']))

    return (out['y_p'].reshape(bp, t, d), out['y_s'].reshape(bs, ts, d),
            out['ret_p'], out['ret_s'], out['kv_p'], out['kv_s'], out['win_p'], out['win_s'])
```

```python
import functools
import math

import numpy as np
import jax
import jax.numpy as jnp
from jax import lax
from jax.experimental import pallas as pl
from jax.experimental.pallas import tpu as pltpu

F32 = jnp.float32
BF16 = jnp.bfloat16
I32 = jnp.int32

D_MODEL = 1024
N_MOD = 6
RET_HEADS = 4
RET_DK = 256
RET_DV = 256
RET_CHUNK = 128
ROPE_BASE = 10000.0
NSA_HEADS = 16
NSA_KV_HEADS = 4
NSA_GROUP = 4
NSA_HEAD_DIM = 64
N_BRANCH = 3
N_KV_ROWS = 6
N_PAGED_ROWS = 4
CMP_BLOCK = 32
CMP_STRIDE = 16
CMP_HIDDEN = 128
SEL_BLOCK = 64
N_SEL = 16
FORCED_SCORE = 1.0e4
WINDOW = 512
Q_BLOCK = 128
N_EXPERTS = 8
TOP_K = 2
EPS = 1e-6

NEG = -1.0e30
KVW = NSA_KV_HEADS * NSA_HEAD_DIM
VMEM_LIMIT_BYTES = 56 * 1024 * 1024
HIGHEST = lax.Precision.HIGHEST


LANES = 128
FFN_HIDDEN_TILE_CAP = 1408
MOE_HIDDEN_TILE_CAP = 896


def _largest_tile(n, cap):
    best = LANES
    for k in range(LANES, cap + 1, LANES):
        if n % k == 0:
            best = k
    return best


def _cparams(*sem):
    return pltpu.CompilerParams(dimension_semantics=sem, vmem_limit_bytes=VMEM_LIMIT_BYTES)


def _sigmoid(x):
    return 1.0 / (1.0 + jnp.exp(-x))


def _silu(x):
    return x * _sigmoid(x)


def _gelu_tanh(x):
    return x * (0.5 * (1.0 + jnp.tanh(math.sqrt(2.0 / math.pi) * (x + 0.044715 * (x * x * x)))))


def _norm_mod(x, g, shift, scale):
    ms = jnp.mean(x * x, axis=-1, keepdims=True)
    return (x * lax.rsqrt(ms + EPS) * g) * (1.0 + scale) + shift


def _rms_residual(x, gate, o, g):
    ms = jnp.mean(o * o, axis=-1, keepdims=True)
    return x + gate * (o * lax.rsqrt(ms + EPS) * g)


def _dot(a, b):
    return jnp.dot(a, b, preferred_element_type=F32)


def _dot_nt(a, b):
    return lax.dot_general(a, b, (((1,), (1,)), ((), ())), preferred_element_type=F32)


def _masked_softmax(s, mask):
    sm = jnp.where(mask, s, NEG)
    m = jnp.max(sm, axis=-1, keepdims=True)
    e = jnp.where(mask, jnp.exp(sm - m), 0.0)
    return e / jnp.maximum(jnp.sum(e, axis=-1, keepdims=True), 1e-30)


def _flash_init(m_ref, l_ref, acc_ref):
    m_ref[...] = jnp.full(m_ref.shape, NEG, F32)
    l_ref[...] = jnp.zeros(l_ref.shape, F32)
    acc_ref[...] = jnp.zeros(acc_ref.shape, F32)


def _flash_update(s, mask, v_b, m_ref, l_ref, acc_ref):
    sm = jnp.where(mask, s, NEG)
    m_old = m_ref[...]
    m_new = jnp.maximum(m_old, jnp.max(sm, axis=-1, keepdims=True))
    alpha = jnp.exp(m_old - m_new)
    p = jnp.where(mask, jnp.exp(sm - m_new), 0.0)
    l_ref[...] = alpha * l_ref[...] + jnp.sum(p, axis=-1, keepdims=True)
    acc_ref[...] = alpha * acc_ref[...] + _dot(p.astype(BF16), v_b)
    m_ref[...] = m_new


def _flash_result(l_ref, acc_ref):
    return acc_ref[...] / jnp.maximum(l_ref[...], 1e-30)


def _select_blocks(imp, cur, n_sel):
    rows, lanes = imp.shape
    blk = lax.broadcasted_iota(I32, (rows, lanes), 1)
    valid = (blk <= cur) & (blk < n_sel)
    forced = (blk == 0) | (blk == cur) | (blk == cur - 1)
    score = jnp.where(valid, jnp.where(forced, FORCED_SCORE, imp), -1.0)
    rank = jnp.zeros((rows, lanes), F32)
    for i in range(n_sel):
        ci = score[:, i:i + 1]
        beats = (ci > score) | ((ci == score) & (blk > i))
        rank = rank + jnp.where(beats, 1.0, 0.0)
    return jnp.where(valid & (rank < float(min(N_SEL, n_sel))), 1.0, 0.0)


def _cond_kernel(c_ref, w_ref, b_ref, o_ref):
    sc = _silu(c_ref[...])
    o_ref[...] = _dot(sc.astype(BF16), w_ref[...].astype(BF16)) + b_ref[...]


def cond_matmul(c, w, b, layer):
    bc, d = c.shape
    n_layers, _, n = w.shape
    tn = 1024
    return pl.pallas_call(
        _cond_kernel,
        grid=(n // tn,),
        in_specs=[pl.BlockSpec((bc, d), lambda j: (0, 0)),
                  pl.BlockSpec((None, d, tn), lambda j: (layer, 0, j)),
                  pl.BlockSpec((None, 1, tn), lambda j: (layer, 0, j))],
        out_specs=pl.BlockSpec((bc, tn), lambda j: (0, j)),
        out_shape=jax.ShapeDtypeStruct((bc, n), F32),
        compiler_params=_cparams("parallel"),
        name="cond_matmul",
    )(c, w, b.reshape(n_layers, 1, n))


def _mod_arg(m, per_token, tm, tiles_per_batch):
    d = m.shape[-1]
    if per_token:
        return m, pl.BlockSpec((tm, d), lambda i, *_: (i, 0))
    return m[:, None, :], pl.BlockSpec((None, 1, d), lambda i, *_: (i // tiles_per_batch, 0, 0))


def _vec_spec(d):
    return pl.BlockSpec((1, d), lambda i, *_: (0, 0))


def _nmm_kernel(x_ref, g_ref, sh_ref, sc_ref, w_ref, o_ref, h_ref):
    @pl.when(pl.program_id(1) == 0)
    def _():
        h_ref[...] = _norm_mod(x_ref[...], g_ref[...], sh_ref[...], sc_ref[...]).astype(BF16)

    o_ref[...] = _dot(h_ref[...], w_ref[...]).astype(o_ref.dtype)


def norm_mod_matmul(x, g, shift, scale, w_b, *, per_token, tm, tpb, tn=None, out_dtype=F32):
    m, d = x.shape
    n = w_b.shape[1]
    tn = n if tn is None else tn
    sh, sh_spec = _mod_arg(shift, per_token, tm, tpb)
    sc, sc_spec = _mod_arg(scale, per_token, tm, tpb)
    return pl.pallas_call(
        _nmm_kernel,
        grid=(m // tm, n // tn),
        in_specs=[pl.BlockSpec((tm, d), lambda i, j: (i, 0)), _vec_spec(d), sh_spec, sc_spec,
                  pl.BlockSpec((d, tn), lambda i, j: (0, j))],
        out_specs=pl.BlockSpec((tm, tn), lambda i, j: (i, j)),
        out_shape=jax.ShapeDtypeStruct((m, n), out_dtype),
        scratch_shapes=[pltpu.VMEM((tm, d), BF16)],
        compiler_params=_cparams("parallel", "arbitrary"),
        name="norm_mod_matmul",
    )(x, g.reshape(1, d), sh, sc, w_b)


def _nmm_t_kernel(x_ref, g_ref, sh_ref, sc_ref, wt_ref, *o_refs):
    h = _norm_mod(x_ref[...], g_ref[...], sh_ref[...], sc_ref[...]).astype(BF16)
    o = _dot_nt(wt_ref[...], h)
    row = 0
    for o_ref in o_refs:
        o_ref[...] = o[row:row + o_ref.shape[0]]
        row += o_ref.shape[0]


def norm_mod_matmul_t(x, g, shift, scale, wt_b, splits, *, b, tm):
    m, d = x.shape
    t = m // b
    tpb = t // tm
    n = wt_b.shape[0]
    sh, sh_spec = _mod_arg(shift, False, tm, tpb)
    sc, sc_spec = _mod_arg(scale, False, tm, tpb)
    return pl.pallas_call(
        _nmm_t_kernel,
        grid=(m // tm,),
        in_specs=[pl.BlockSpec((tm, d), lambda i: (i, 0)), _vec_spec(d), sh_spec, sc_spec,
                  pl.BlockSpec((n, d), lambda i: (0, 0))],
        out_specs=[pl.BlockSpec((None, ni, tm), lambda i: (i // tpb, 0, i % tpb)) for ni in splits],
        out_shape=[jax.ShapeDtypeStruct((b, ni, t), F32) for ni in splits],
        compiler_params=_cparams("parallel"),
        name="norm_mod_matmul_t",
    )(x, g.reshape(1, d), sh, sc, wt_b)


def _mnr_kernel(a_ref, w_ref, x_ref, gate_ref, g_ref, o_ref):
    o = _dot(a_ref[...].astype(BF16), w_ref[...])
    o_ref[...] = _rms_residual(x_ref[...], gate_ref[...], o, g_ref[...])


def matmul_norm_residual(a, w_b, x, gate, g, *, per_token, tm, tpb):
    m, k = a.shape
    d = w_b.shape[1]
    gt, gt_spec = _mod_arg(gate, per_token, tm, tpb)
    return pl.pallas_call(
        _mnr_kernel,
        grid=(m // tm,),
        in_specs=[pl.BlockSpec((tm, k), lambda i: (i, 0)),
                  pl.BlockSpec((k, d), lambda i: (0, 0)),
                  pl.BlockSpec((tm, d), lambda i: (i, 0)), gt_spec, _vec_spec(d)],
        out_specs=pl.BlockSpec((tm, d), lambda i: (i, 0)),
        out_shape=jax.ShapeDtypeStruct((m, d), F32),
        compiler_params=_cparams("parallel"),
        name="matmul_norm_residual",
    )(a, w_b, x, gt, g.reshape(1, d))


def _combine_kernel(y1_ref, y2_ref, r_ref, x_ref, gate_ref, g_ref, o_ref):
    route = r_ref[...]
    y = route[:, TOP_K:TOP_K + 1] * y1_ref[...] + route[:, TOP_K + 1:TOP_K + 2] * y2_ref[...]
    o_ref[...] = _rms_residual(x_ref[...], gate_ref[...], y, g_ref[...])


def moe_combine(y1, y2, route, x, gate, g, *, per_token, tm, tpb):
    m, d = x.shape
    gt, gt_spec = _mod_arg(gate, per_token, tm, tpb)
    row = lambda n: pl.BlockSpec((tm, n), lambda i: (i, 0))
    return pl.pallas_call(
        _combine_kernel,
        grid=(m // tm,),
        in_specs=[row(d), row(d), row(route.shape[1]), row(d), gt_spec, _vec_spec(d)],
        out_specs=row(d),
        out_shape=jax.ShapeDtypeStruct((m, d), F32),
        compiler_params=_cparams("parallel"),
        name="moe_combine",
    )(y1, y2, route, x, gt, g.reshape(1, d))


def _ffn_kernel(x_ref, g2_ref, sh_ref, sc_ref, wg_ref, wu_ref, wd_ref, gate_ref, g3_ref, o_ref,
                h_ref, acc_ref):
    f = pl.program_id(1)

    @pl.when(f == 0)
    def _():
        h_ref[...] = _norm_mod(x_ref[...], g2_ref[...], sh_ref[...], sc_ref[...]).astype(BF16)
        acc_ref[...] = jnp.zeros(acc_ref.shape, F32)

    h = h_ref[...]
    act = _silu(_dot(h, wg_ref[...])) * _dot(h, wu_ref[...])
    acc_ref[...] += _dot(act.astype(BF16), wd_ref[...])

    @pl.when(f == pl.num_programs(1) - 1)
    def _():
        o_ref[...] = _rms_residual(x_ref[...], gate_ref[...], acc_ref[...], g3_ref[...])


def ffn_sublayer(x, g2, shift, scale, w_gu_b, w_down_b, gate, g3, *, per_token, tm, tpb, tf):
    m, d = x.shape
    fdim = w_down_b.shape[0]
    nf = fdim // tf
    sh, sh_spec = _mod_arg(shift, per_token, tm, tpb)
    sc, sc_spec = _mod_arg(scale, per_token, tm, tpb)
    gt, gt_spec = _mod_arg(gate, per_token, tm, tpb)
    return pl.pallas_call(
        _ffn_kernel,
        grid=(m // tm, nf),
        in_specs=[pl.BlockSpec((tm, d), lambda i, f: (i, 0)), _vec_spec(d), sh_spec, sc_spec,
                  pl.BlockSpec((d, tf), lambda i, f: (0, f)),
                  pl.BlockSpec((d, tf), lambda i, f: (0, nf + f)),
                  pl.BlockSpec((tf, d), lambda i, f: (f, 0)),
                  gt_spec, _vec_spec(d)],
        out_specs=pl.BlockSpec((tm, d), lambda i, f: (i, 0)),
        out_shape=jax.ShapeDtypeStruct((m, d), F32),
        scratch_shapes=[pltpu.VMEM((tm, d), BF16), pltpu.VMEM((tm, d), F32)],
        compiler_params=_cparams("parallel", "arbitrary"),
        name="ffn_sublayer",
    )(x, g2.reshape(1, d), sh, sc, w_gu_b, w_gu_b, w_down_b, gt, g3.reshape(1, d))


def _ret_layer_kernel(x_ref, g0_ref, sh_ref, sc_ref, win_ref, cos_ref, sin_ref, dm_ref, xi_ref, zt_ref, gc_ref,
                      gn_ref, s0_ref, wout_ref, rgate_ref, g1_ref, o_ref, s_ref, pr_ref, oc_ref):
    @pl.when(pl.program_id(1) == 0)
    def _():
        s_ref[...] = s0_ref[...]

    x = x_ref[...]
    pr_ref[...] = _dot(_norm_mod(x, g0_ref[...], sh_ref[...], sc_ref[...]).astype(BF16), win_ref[...])
    tm = x.shape[0]
    rows = min(tm, RET_CHUNK)
    nq = RET_HEADS * RET_DK
    nv = RET_HEADS * RET_DV
    half = RET_DK // 2

    def pad(v):
        if rows == RET_CHUNK:
            return v
        return jnp.concatenate([v, jnp.zeros((RET_CHUNK - rows, v.shape[1]), v.dtype)], axis=0)

    for r0 in range(0, tm, rows):
        rs = slice(r0, r0 + rows)
        cos = pad(cos_ref[rs, :])
        sin = pad(sin_ref[rs, :])

        def rot(v):
            v1 = v[:, :half]
            v2 = v[:, half:]
            return jnp.concatenate([v1 * cos - v2 * sin, v2 * cos + v1 * sin], axis=-1)

        for h in range(RET_HEADS):
            vcols = slice(h * RET_DV, (h + 1) * RET_DV)
            q = rot(pad(pr_ref[rs, h * RET_DK:(h + 1) * RET_DK]))
            k = rot(pad(pr_ref[rs, nq + h * RET_DK:nq + (h + 1) * RET_DK])) * (RET_DK ** -0.5)
            qb = q.astype(BF16)
            kb = k.astype(BF16)
            vb = pad(pr_ref[rs, 2 * nq + h * RET_DV:2 * nq + (h + 1) * RET_DV]).astype(BF16)
            state = s_ref[h]
            inner = _dot_nt(qb, kb) * dm_ref[h]
            o = _dot(inner.astype(BF16), vb) + _dot(qb, state.astype(BF16)) * xi_ref[h]
            kz = (k * zt_ref[h]).astype(BF16)
            upd = lax.dot_general(kz, vb, (((0,), (0,)), ((), ())), preferred_element_type=F32)
            s_ref[h] = gc_ref[h] * state + upd
            mu = jnp.mean(o, axis=-1, keepdims=True)
            dev = o - mu
            var = jnp.mean(dev * dev, axis=-1, keepdims=True)
            on = dev * lax.rsqrt(var + EPS) * gn_ref[:, vcols]
            sgate = pr_ref[rs, 2 * nq + nv + h * RET_DV:2 * nq + nv + (h + 1) * RET_DV]
            oc_ref[rs, vcols] = _silu(sgate) * on[:rows]

    o = _dot(oc_ref[...].astype(BF16), wout_ref[...])
    o_ref[...] = _rms_residual(x, rgate_ref[...], o, g1_ref[...])


def _ret_tables(chunk, rows):
    h = RET_HEADS
    log_g = jnp.log(1.0 - jnp.exp(jnp.linspace(math.log(1.0 / 32), math.log(1.0 / 512), h, dtype=F32)))
    i = jnp.arange(chunk, dtype=F32)
    diff = i[:, None] - i[None, :]
    dmat = jnp.where(diff >= 0, jnp.exp(log_g[:, None, None] * jnp.maximum(diff, 0.0)), 0.0)
    xi = jnp.exp(log_g[:, None] * (i + 1.0))
    zeta = jnp.exp(log_g[:, None] * (chunk - 1.0 - i))
    gch = jnp.exp(log_g * chunk)
    pad = rows - chunk
    dmat = jnp.pad(dmat, ((0, 0), (0, pad), (0, pad)))
    xi = jnp.pad(xi, ((0, 0), (0, pad)))[..., None]
    zeta = jnp.pad(zeta, ((0, 0), (0, pad)))[..., None]
    gch = jnp.broadcast_to(gch[:, None, None], (h, 1, RET_DV))
    return dmat, xi, zeta, gch


def _rope_tables(pos, rows):
    half = RET_DK // 2
    inv = ROPE_BASE ** (-jnp.arange(half, dtype=F32) / half)
    ang = pos.astype(F32)[:, None] * inv[None, :]
    pad = rows - pos.shape[0]
    return jnp.pad(jnp.cos(ang), ((0, pad), (0, 0))), jnp.pad(jnp.sin(ang), ((0, pad), (0, 0)))


RET_ROW_TILE = 512


def retention_sublayer(x, g0, shift, scale, w_in_b, pos, s0, gn_g, w_out_b, rgate, g1, *, b, per_token, chunk):
    m, d = x.shape
    t = m // b
    tm = min(t, RET_ROW_TILE)
    nt = t // tm
    h = RET_HEADS
    dmat, xi, zeta, gch = _ret_tables(chunk, RET_CHUNK)
    cos, sin = _rope_tables(pos, t)
    gn = gn_g.reshape(1, -1)
    rows = lambda n: pl.BlockSpec((tm, n), lambda bi, ti: (bi * nt + ti, 0))
    const = lambda a: pl.BlockSpec(a.shape, lambda bi, ti: (0,) * a.ndim, pipeline_mode=pl.Buffered(1))
    state_spec = pl.BlockSpec((None, h, RET_DK, RET_DV), lambda bi, ti: (bi, 0, 0, 0))

    def mod_arg(a):
        if per_token:
            return a, rows(d)
        return a[:, None, :], pl.BlockSpec((None, 1, d), lambda bi, ti: (bi, 0, 0))

    sh, sh_spec = mod_arg(shift)
    sc, sc_spec = mod_arg(scale)
    rg, rg_spec = mod_arg(rgate)
    return pl.pallas_call(
        _ret_layer_kernel,
        grid=(b, nt),
        in_specs=[rows(d), const(g0.reshape(1, d)), sh_spec, sc_spec, const(w_in_b),
                  pl.BlockSpec((tm, RET_DK // 2), lambda bi, ti: (ti, 0)),
                  pl.BlockSpec((tm, RET_DK // 2), lambda bi, ti: (ti, 0)),
                  const(dmat), const(xi), const(zeta), const(gch), const(gn), state_spec,
                  const(w_out_b), rg_spec, const(g1.reshape(1, d))],
        out_specs=[rows(d), state_spec],
        out_shape=[jax.ShapeDtypeStruct((m, d), F32), jax.ShapeDtypeStruct((b, h, RET_DK, RET_DV), F32)],
        scratch_shapes=[pltpu.VMEM((tm, w_in_b.shape[1]), F32), pltpu.VMEM((tm, h * RET_DV), F32)],
        compiler_params=_cparams("parallel", "arbitrary"),
        name="retention_sublayer",
    )(x, g0.reshape(1, d), sh, sc, w_in_b, cos, sin, dmat, xi, zeta, gch, gn, s0, w_out_b, rg, g1.reshape(1, d))


def _compress_weights(cmp_pos, cmp_w1, cmp_b1, cmp_w2):
    g, d, hd = NSA_KV_HEADS, NSA_HEAD_DIM, CMP_HIDDEN
    eye = jnp.eye(g, dtype=F32)
    w1 = cmp_w1.reshape(2, CMP_BLOCK, d, hd)
    bd1 = jnp.einsum('ksdh,gj->ksgdjh', w1, eye).reshape(2, CMP_BLOCK, g * d, g * hd).astype(BF16)
    bd2 = jnp.einsum('khd,gj->kghjd', cmp_w2, eye).reshape(2, g * hd, g * d).astype(BF16)
    pos = jnp.tile(cmp_pos[:, :, None, :], (1, 1, g, 1)).reshape(2, CMP_BLOCK, 1, g * d)
    b1 = jnp.tile(cmp_b1[:, None, :], (1, g, 1)).reshape(2, 1, g * hd)
    return bd1, bd2, pos, b1


def _compress_finish(p0, p1, b1, bd2):
    rows = p0.shape[0]
    hid = b1 + p0 + pltpu.roll(p1, rows - 1, 0)
    return _dot(_gelu_tanh(hid).astype(BF16), bd2)


CMP_PAGES_PER_STEP = 16
ATT_PAGES_PER_STEP = 16


def _tokens_to_sublanes(xt_ref, tr_ref):
    for half in range(KVW // LANES):
        tr_ref[half] = xt_ref[half * LANES:(half + 1) * LANES, :].T


def _cmp_prompt_kernel(xt_ref, bd1_ref, bd2_ref, pos_ref, b1_ref, o_ref, tr_ref):
    n_sub = xt_ref.shape[1] // CMP_STRIDE
    hid = bd1_ref.shape[-1]
    _tokens_to_sublanes(xt_ref, tr_ref)
    p0 = jnp.zeros((n_sub, hid), F32)
    p1 = jnp.zeros((n_sub, hid), F32)
    for s in range(CMP_STRIDE):
        xs = jnp.concatenate([tr_ref[half, pl.ds(s, n_sub, stride=CMP_STRIDE), :]
                              for half in range(KVW // LANES)], axis=-1)
        p0 = p0 + _dot((xs + pos_ref[s]).astype(BF16), bd1_ref[s])
        p1 = p1 + _dot((xs + pos_ref[CMP_STRIDE + s]).astype(BF16), bd1_ref[CMP_STRIDE + s])
    o_ref[...] = _compress_finish(p0, p1, b1_ref[...], bd2_ref[...])


def compress_prompt(kvt, cw):
    bd1, bd2, pos, b1 = cw
    b, _, t = kvt.shape
    n_sub = t // CMP_STRIDE
    return pl.pallas_call(
        _cmp_prompt_kernel,
        grid=(b, 2),
        in_specs=[pl.BlockSpec((None, KVW, t), lambda bi, ki: (bi, ki, 0)),
                  pl.BlockSpec((None,) + bd1.shape[1:], lambda bi, ki: (ki, 0, 0, 0)),
                  pl.BlockSpec((None,) + bd2.shape[1:], lambda bi, ki: (ki, 0, 0)),
                  pl.BlockSpec((None,) + pos.shape[1:], lambda bi, ki: (ki, 0, 0, 0)),
                  pl.BlockSpec((None,) + b1.shape[1:], lambda bi, ki: (ki, 0, 0))],
        out_specs=pl.BlockSpec((None, None, n_sub, KVW), lambda bi, ki: (ki, bi, 0, 0)),
        out_shape=jax.ShapeDtypeStruct((2, b, n_sub, KVW), F32),
        scratch_shapes=[pltpu.VMEM((KVW // LANES, t, LANES), F32)],
        compiler_params=_cparams("parallel", "parallel"),
        name="compress_prompt",
    )(kvt, bd1, bd2, pos, b1)


def _cmp_sample_kernel(pt_ref, *refs):
    page_refs = refs[:CMP_PAGES_PER_STEP]
    new_ref, bd1_ref, bd2_ref, pos_ref, b1_ref, o_ref, sub_ref, tr_ref = refs[CMP_PAGES_PER_STEP:]
    p = pl.program_id(1)
    page = page_refs[0].shape[2]
    per_page = page // CMP_STRIDE
    n_cmp = o_ref.shape[1]
    for j, page_ref in enumerate(page_refs):
        row0 = pl.multiple_of((p * CMP_PAGES_PER_STEP + j) * per_page, per_page)
        for kind in range(2):
            _tokens_to_sublanes(page_ref.at[kind], tr_ref.at[kind])
            for s in range(CMP_STRIDE):
                for half in range(KVW // LANES):
                    piece = tr_ref[kind, half, pl.ds(s, per_page, stride=CMP_STRIDE), :]
                    sub_ref[kind, s, pl.ds(row0, per_page), half * LANES:(half + 1) * LANES] = piece

    @pl.when(p == pl.num_programs(1) - 1)
    def _():
        n_rows = sub_ref.shape[2]
        tail = n_rows - n_cmp
        hid = bd1_ref.shape[-1]
        for kind in range(2):
            p0 = jnp.zeros((n_rows, hid), F32)
            p1 = jnp.zeros((n_rows, hid), F32)
            for s in range(CMP_STRIDE):
                sub_ref[kind, s, pl.ds(n_cmp, tail), :] = jnp.broadcast_to(new_ref[kind, s:s + 1, :], (tail, KVW))
                xs = sub_ref[kind, s]
                p0 = p0 + _dot((xs + pos_ref[kind, s]).astype(BF16), bd1_ref[kind, s])
                p1 = p1 + _dot((xs + pos_ref[kind, CMP_STRIDE + s]).astype(BF16), bd1_ref[kind, CMP_STRIDE + s])
            o_ref[kind] = _compress_finish(p0, p1, b1_ref[kind], bd2_ref[kind])[:n_cmp]


def _page_specs(n_per_step, kind_block, page):
    def spec(j):
        return pl.BlockSpec((None, 2, KVW, page),
                            lambda bi, pi, pt: (pt[bi, pi * n_per_step + j], kind_block, 0, 0))
    return [spec(j) for j in range(n_per_step)]


def compress_sample(cache_t, page_table, new_sub, cw):
    bd1, bd2, pos, b1 = cw
    b, n_pages = page_table.shape
    page = cache_t.shape[3]
    n_cmp = n_pages * page // CMP_STRIDE
    nps = CMP_PAGES_PER_STEP
    const = lambda a: pl.BlockSpec(a.shape, lambda bi, pi, pt: (0,) * a.ndim, pipeline_mode=pl.Buffered(1))
    grid_spec = pltpu.PrefetchScalarGridSpec(
        num_scalar_prefetch=1,
        grid=(b, n_pages // nps),
        in_specs=_page_specs(nps, 0, page) + [
            pl.BlockSpec((2, None, CMP_STRIDE, KVW), lambda bi, pi, pt: (0, bi, 0, 0)),
            const(bd1), const(bd2), const(pos), const(b1)],
        out_specs=pl.BlockSpec((2, None, n_cmp, KVW), lambda bi, pi, pt: (0, bi, 0, 0)),
        scratch_shapes=[pltpu.VMEM((2, CMP_STRIDE, n_cmp + 8, KVW), F32),
                        pltpu.VMEM((2, KVW // LANES, page, LANES), F32)],
    )
    return pl.pallas_call(
        _cmp_sample_kernel,
        grid_spec=grid_spec,
        out_shape=jax.ShapeDtypeStruct((2, b, n_cmp, KVW), F32),
        compiler_params=_cparams("parallel", "arbitrary"),
        name="compress_sample",
    )(page_table, *([cache_t] * nps), new_sub, bd1, bd2, pos, b1)


def _cover_matrix(n_cmp, n_sel, rows, cols):
    i = np.arange(n_cmp)[:, None]
    j = np.arange(n_sel)[None, :]
    cover = (i * CMP_STRIDE < (j + 1) * SEL_BLOCK) & (i * CMP_STRIDE + CMP_BLOCK > j * SEL_BLOCK)
    out = np.zeros((rows, cols), np.float32)
    out[:n_cmp, :n_sel] = cover
    return jnp.asarray(out)


SEL_CHUNK = 256
SEL_CLASS = 512


def _select_blocks_t(imp_t, cur, n_sel):
    nb, nq = imp_t.shape
    blk = lax.broadcasted_iota(I32, (nb, nq), 0)
    valid = (blk <= cur) & (blk < n_sel)
    forced = (blk == 0) | (blk == cur) | (blk == cur - 1)
    score = jnp.where(valid, jnp.where(forced, FORCED_SCORE, imp_t), -1.0)
    rank = jnp.zeros((nb, nq), F32)
    for i in range(n_sel):
        ci = score[i:i + 1, :]
        beats = (ci > score) | ((ci == score) & (blk > i))
        rank = rank + jnp.where(beats, 1.0, 0.0)
    return jnp.where(valid & (rank < float(min(N_SEL, n_sel))), 1.0, 0.0)


def _nsa_prompt_kernel(qt_ref, glt_ref, kc_ref, vct_ref, covt_ref, kblk_ref, wb_ref, ks_ref, vst_ref, kw_ref, vwt_ref,
                       o_ref, s_ref, osel_ref, *, n_cmp, n_sel):
    tq = qt_ref.shape[1]
    r, d = NSA_GROUP, NSA_HEAD_DIM
    cols = r * tq
    t = ks_ref.shape[0]
    q0 = pl.program_id(2) * tq
    qt = qt_ref[...] * (d ** -0.5)
    qcat = jnp.concatenate([qt[h * d:(h + 1) * d, :] for h in range(r)], axis=-1)
    qb = jnp.concatenate([qcat, jnp.zeros_like(qcat)], axis=0).astype(BF16)
    gl = _sigmoid(glt_ref[...])

    def gate(branch):
        return jnp.concatenate([jnp.broadcast_to(gl[h * N_BRANCH + branch:h * N_BRANCH + branch + 1, :], (d, tq))
                                for h in range(r)], axis=0)

    def stack(o_t):
        return jnp.concatenate([o_t[:, h * tq:(h + 1) * tq] for h in range(r)], axis=0)

    def tile(x):
        return jnp.concatenate([x] * r, axis=-1)

    ones_rows = 16

    def with_ones(vt):
        return jnp.concatenate([vt.astype(BF16), jnp.ones((ones_rows, vt.shape[1]), BF16)], axis=0)

    qpos = q0 + lax.broadcasted_iota(I32, (1, tq), 1)

    span = min(WINDOW + tq, t)
    w0 = pl.multiple_of(jnp.maximum(q0 + tq - span, 0), tq)
    sm = _dot(kw_ref[pl.ds(w0, span), :], qb) + tile(wb_ref[(q0 - w0) // tq])
    p = jnp.exp(sm - jnp.max(sm, axis=0, keepdims=True))
    o_win_t = _dot(with_ones(vwt_ref[:, pl.ds(w0, span)]), p.astype(BF16))
    out = gate(2) * stack(o_win_t[:d] / jnp.maximum(o_win_t[d:d + 1], 1e-30))

    ncp = kc_ref.shape[0]
    nn = lax.broadcasted_iota(I32, (ncp, tq), 0)
    ok = tile(jnp.where((nn * CMP_STRIDE + (CMP_BLOCK - 1) <= qpos) & (nn < n_cmp), 1.0, 0.0))
    sm = _dot(kc_ref[...].astype(BF16), qb) + (ok - 1.0) * (-NEG)
    e = jnp.exp(sm - jnp.max(sm, axis=0, keepdims=True)) * ok
    p_cmp = e / jnp.maximum(jnp.sum(e, axis=0, keepdims=True), 1e-30)
    out = out + gate(0) * stack(_dot(vct_ref[...].astype(BF16), p_cmp.astype(BF16)))

    psum_t = p_cmp[:, :tq]
    for h in range(1, r):
        psum_t = psum_t + p_cmp[:, h * tq:(h + 1) * tq]
    imp_t = jnp.dot(covt_ref[...], psum_t, precision=HIGHEST, preferred_element_type=F32)
    sel_t = _select_blocks_t(imp_t, qpos >> 6, n_sel)
    nb = sel_t.shape[0]
    q_sel = jnp.concatenate([qcat, tile((sel_t - 1.0) * (-NEG)), jnp.zeros((LANES - d - nb, cols), F32)],
                            axis=0).astype(BF16)

    groups = SEL_CHUNK // 8
    sel_class = SEL_CLASS if t % SEL_CLASS == 0 else t

    def sel_branch(n_keys):
        m_run = jnp.full((8, cols), NEG, F32)
        for k0 in range(0, n_keys, SEL_CHUNK):
            k_aug = ks_ref[k0:k0 + SEL_CHUNK, :] + kblk_ref[k0:k0 + SEL_CHUNK, :]
            sm = _dot(k_aug, q_sel)
            if k0 + SEL_CHUNK > n_keys - sel_class:
                keypos = k0 + lax.broadcasted_iota(I32, (SEL_CHUNK, tq), 0)
                sm = sm + tile(jnp.where(keypos <= qpos, 0.0, NEG))
            s_ref[k0:k0 + SEL_CHUNK, :] = sm
            m_run = jnp.maximum(m_run, jnp.max(sm.reshape(groups, 8, cols), axis=0))
        m_sel = jnp.max(m_run, axis=0, keepdims=True)
        acc_t = jnp.zeros((d + ones_rows, cols), F32)
        for k0 in range(0, n_keys, SEL_CHUNK):
            p = jnp.exp(s_ref[k0:k0 + SEL_CHUNK, :] - m_sel)
            acc_t = acc_t + _dot(with_ones(vst_ref[:, k0:k0 + SEL_CHUNK]), p.astype(BF16))
        osel_ref[...] = acc_t[:d] / jnp.maximum(acc_t[d:d + 1], 1e-30)

    cls_id = pl.program_id(2) // (sel_class // tq)
    for cls in range(t // sel_class):
        pl.when(cls_id == cls)(functools.partial(sel_branch, (cls + 1) * sel_class))
    o_ref[...] = (out + gate(1) * stack(osel_ref[...])).T


GATE_ROWS = 16


def nsa_prompt(qt, glt, kc, vct, kpad, kvt, wint):
    b, _, t = qt.shape
    g, r, d = NSA_KV_HEADS, NSA_GROUP, NSA_HEAD_DIM
    tq = Q_BLOCK
    nt = t // tq
    ncp = kc.shape[2]
    n_cmp = t // CMP_STRIDE - 1
    n_sel = -(-t // SEL_BLOCK)
    nb = -(-n_sel // 8) * 8
    covt = _cover_matrix(n_cmp, n_sel, ncp, nb).T
    assert d + nb <= LANES, "block one-hot must fit in the keys' padding lanes"
    kblk = np.zeros((t, LANES), np.float32)
    kblk[np.arange(t), d + np.arange(t) // SEL_BLOCK] = 1.0
    kblk = jnp.asarray(kblk, BF16)
    span = min(WINDOW + tq, t)
    i = np.arange(span)[None, :, None]
    j = np.arange(tq)[None, None, :]
    rel = i - j - np.arange(0, span - tq + 1, tq)[:, None, None]
    wbias = jnp.asarray(np.where((rel <= 0) & (rel > -WINDOW), 0.0, NEG), F32)
    per_bg = lambda shape: pl.BlockSpec((None, None) + shape, lambda bi, gi, qi: (bi, gi, 0, 0))
    const = lambda a: pl.BlockSpec(a.shape, lambda bi, gi, qi: (0,) * a.ndim)
    return pl.pallas_call(
        functools.partial(_nsa_prompt_kernel, n_cmp=n_cmp, n_sel=n_sel),
        grid=(b, g, nt),
        in_specs=[pl.BlockSpec((None, r * d, tq), lambda bi, gi, qi: (bi, gi, qi)),
                  pl.BlockSpec((None, GATE_ROWS, tq), lambda bi, gi, qi: (bi, gi, qi)),
                  per_bg((ncp, LANES)), per_bg((d, ncp)), const(covt), const(kblk), const(wbias),
                  pl.BlockSpec((t, LANES), lambda bi, gi, qi: (bi, gi)),
                  pl.BlockSpec((None, d, t), lambda bi, gi, qi: (bi, 3 * g + gi, 0)),
                  pl.BlockSpec((t, LANES), lambda bi, gi, qi: (bi, g + gi)),
                  pl.BlockSpec((None, d, t), lambda bi, gi, qi: (bi, g + gi, 0))],
        out_specs=pl.BlockSpec((tq, r * d), lambda bi, gi, qi: (bi * nt + qi, gi)),
        out_shape=jax.ShapeDtypeStruct((b * t, NSA_HEADS * d), F32),
        scratch_shapes=[pltpu.VMEM((t, r * tq), F32), pltpu.VMEM((d, r * tq), F32)],
        compiler_params=_cparams("parallel", "parallel", "arbitrary"),
        name="nsa_prompt",
    )(qt, glt, kc, vct, covt, kblk, wbias, kpad, kvt, kpad, wint)


def _nsa_sample_select_kernel(q_ref, kc_ref, vc_ref, cov_ref, oc_ref, sel_ref, *, t, past, n_cmp, n_sel):
    rows = q_ref.shape[0]
    g, r = NSA_KV_HEADS, NSA_GROUP
    qb = q_ref[...].astype(BF16)
    qpos_r = past + (lax.broadcasted_iota(I32, (rows, 1), 0) & (t - 1))
    s = _dot_nt(qb, kc_ref[...].astype(BF16))
    nn = lax.broadcasted_iota(I32, s.shape, 1)
    p_cmp = _masked_softmax(s, (nn * CMP_STRIDE + (CMP_BLOCK - 1) <= qpos_r) & (nn < n_cmp))
    oc_ref[...] = _dot(p_cmp.astype(BF16), vc_ref[...].astype(BF16))
    psum = jnp.sum(p_cmp.reshape(g, r, t, s.shape[1]), axis=1).reshape(g * t, s.shape[1])
    imp = jnp.dot(psum, cov_ref[...], precision=HIGHEST, preferred_element_type=F32)
    qpos_gt = past + (lax.broadcasted_iota(I32, (g * t, 1), 0) & (t - 1))
    sel = _select_blocks(imp, qpos_gt >> 6, n_sel)
    lanes = sel.shape[1]
    sel_ref[...] = jnp.broadcast_to(sel.reshape(g, 1, t, lanes), (g, r, t, lanes)).reshape(rows, lanes)


def nsa_sample_select(q4, cmp, *, t, past):
    b, rows, _ = q4.shape
    n_cmp = cmp.shape[2]
    n_sel = -(-(past + t) // SEL_BLOCK)
    lanes = -(-n_sel // 128) * 128
    cover = _cover_matrix(n_cmp, n_sel, n_cmp, lanes)
    return pl.pallas_call(
        functools.partial(_nsa_sample_select_kernel, t=t, past=past, n_cmp=n_cmp, n_sel=n_sel),
        grid=(b,),
        in_specs=[pl.BlockSpec((None, rows, KVW), lambda bi: (bi, 0, 0)),
                  pl.BlockSpec((None, None, n_cmp, KVW), lambda bi: (0, bi, 0, 0)),
                  pl.BlockSpec((None, None, n_cmp, KVW), lambda bi: (1, bi, 0, 0)),
                  pl.BlockSpec(cover.shape, lambda bi: (0, 0))],
        out_specs=[pl.BlockSpec((None, rows, KVW), lambda bi: (bi, 0, 0)),
                   pl.BlockSpec((None, rows, lanes), lambda bi: (bi, 0, 0))],
        out_shape=[jax.ShapeDtypeStruct((b, rows, KVW), F32),
                   jax.ShapeDtypeStruct((b, rows, lanes), F32)],
        compiler_params=_cparams("parallel"),
        name="nsa_sample_select",
    )(q4, cmp, cmp, cover)


def _nsa_sample_attend_kernel(pt_ref, q_ref, sel_ref, oc_ref, gl_ref, *refs, t, past):
    page_refs = refs[:ATT_PAGES_PER_STEP]
    new_ref, win_ref, eb_ref, o_ref, m_ref, l_ref, acc_ref = refs[ATT_PAGES_PER_STEP:]
    rows = q_ref.shape[0]
    p = pl.program_id(1)
    page = page_refs[0].shape[2]
    width = ATT_PAGES_PER_STEP * page
    qb = q_ref[...].astype(BF16)
    sel = sel_ref[...]
    nblk = sel.shape[1]

    @pl.when(p == 0)
    def _():
        _flash_init(m_ref, l_ref, acc_ref)

    blocks_per_step = width // SEL_BLOCK
    e0 = pl.multiple_of((pl.num_programs(1) - 1 - p) * blocks_per_step, blocks_per_step)
    sel_b = sel.astype(BF16)

    def partial_softmax(prs, lane0):
        n = len(prs) * page
        sc = jnp.concatenate([_dot(qb, pr[0].astype(BF16)) for pr in prs], axis=-1)
        mask = _dot(sel_b, eb_ref[pl.ds(e0, nblk), lane0:lane0 + n]) > 0.5
        sm = jnp.where(mask, sc, NEG)
        m_g = jnp.max(sm, axis=-1, keepdims=True)
        p_g = jnp.where(mask, jnp.exp(sm - m_g), 0.0)
        l_g = jnp.sum(p_g, axis=-1, keepdims=True)
        p_b = p_g.astype(BF16)
        pv = _dot_nt(p_b[:, :page], prs[0][1].astype(BF16))
        for j in range(1, len(prs)):
            pv = pv + _dot_nt(p_b[:, j * page:(j + 1) * page], prs[j][1].astype(BF16))
        return m_g, l_g, pv

    n_half = ATT_PAGES_PER_STEP // 2
    m_a, l_a, pv_a = partial_softmax(page_refs[:n_half], 0)
    m_b, l_b, pv_b = partial_softmax(page_refs[n_half:], n_half * page)
    m_old = m_ref[...]
    m_new = jnp.maximum(m_old, jnp.maximum(m_a, m_b))
    s_old = jnp.exp(m_old - m_new)
    s_a = jnp.exp(m_a - m_new)
    s_b = jnp.exp(m_b - m_new)
    l_ref[...] = s_old * l_ref[...] + s_a * l_a + s_b * l_b
    acc_ref[...] = s_old * acc_ref[...] + s_a * pv_a + s_b * pv_b
    m_ref[...] = m_new

    @pl.when(p == pl.num_programs(1) - 1)
    def _():
        qpos_r = past + (lax.broadcasted_iota(I32, (rows, 1), 0) & (t - 1))
        tp = LANES
        newr = jnp.concatenate([new_ref[...], jnp.zeros((tp - t, new_ref.shape[1]), F32)], axis=0)
        jn = lax.broadcasted_iota(I32, (rows, tp), 1)
        new_ok = (jn < t) & (past + jn <= qpos_r)
        blk_new = past // SEL_BLOCK
        sel_new = sel[:, blk_new:blk_new + 1] > 0.5
        sc_n = _dot_nt(qb, newr[:, 2 * KVW:3 * KVW].astype(BF16))
        _flash_update(sc_n, new_ok & sel_new, newr[:, 3 * KVW:4 * KVW].astype(BF16), m_ref, l_ref, acc_ref)
        o_sel = _flash_result(l_ref, acc_ref)

        n_win = win_ref.shape[2]
        s_a = _dot(qb, win_ref[0].astype(BF16))
        s_b = _dot_nt(qb, newr[:, 4 * KVW:5 * KVW].astype(BF16))
        kp_a = (past - n_win) + lax.broadcasted_iota(I32, (rows, n_win), 1)
        ok_a = (kp_a <= qpos_r) & (kp_a > qpos_r - WINDOW) & (kp_a >= 0)
        ok_b = new_ok & (past + jn > qpos_r - WINDOW)
        sm_a = jnp.where(ok_a, s_a, NEG)
        sm_b = jnp.where(ok_b, s_b, NEG)
        mx = jnp.maximum(jnp.max(sm_a, axis=-1, keepdims=True), jnp.max(sm_b, axis=-1, keepdims=True))
        e_a = jnp.where(ok_a, jnp.exp(sm_a - mx), 0.0)
        e_b = jnp.where(ok_b, jnp.exp(sm_b - mx), 0.0)
        den = jnp.maximum(jnp.sum(e_a, axis=-1, keepdims=True) + jnp.sum(e_b, axis=-1, keepdims=True), 1e-30)
        o_win = (_dot_nt((e_a / den).astype(BF16), win_ref[1].astype(BF16))
                 + _dot((e_b / den).astype(BF16), newr[:, 5 * KVW:6 * KVW].astype(BF16)))

        gates = _sigmoid(gl_ref[...])
        o_ref[...] = gates[:, 0:1] * oc_ref[...] + gates[:, 1:2] * o_sel + gates[:, 2:3] * o_win


def nsa_sample_attend(q4, sel, ocmp, gl, cache_t, page_table, new_rows, win_t, *, t, past):
    b, rows, _ = q4.shape
    n_pages = page_table.shape[1]
    page = cache_t.shape[3]
    lanes = sel.shape[2]
    tp = new_rows.shape[1]
    n_win = win_t.shape[3]
    nps = ATT_PAGES_PER_STEP
    n_steps = n_pages // nps
    blocks_per_step = nps * page // SEL_BLOCK
    shift = (n_steps - 1) * blocks_per_step
    expand = jnp.asarray((np.arange(lanes + shift)[:, None] - shift) == (np.arange(nps * page)[None, :] // SEL_BLOCK), BF16)
    per_b = lambda shape: pl.BlockSpec((None,) + shape, lambda bi, pi, pt: (bi,) + (0,) * len(shape))
    grid_spec = pltpu.PrefetchScalarGridSpec(
        num_scalar_prefetch=1,
        grid=(b, n_pages // nps),
        in_specs=[per_b((rows, KVW)), per_b((rows, lanes)), per_b((rows, KVW)), per_b((rows, N_BRANCH))]
        + _page_specs(nps, 1, page)
        + [per_b((tp, N_KV_ROWS * KVW)), per_b((2, KVW, n_win)),
           pl.BlockSpec(expand.shape, lambda bi, pi, pt: (0, 0))],
        out_specs=per_b((rows, KVW)),
        scratch_shapes=[pltpu.VMEM((rows, 1), F32), pltpu.VMEM((rows, 1), F32), pltpu.VMEM((rows, KVW), F32)],
    )
    return pl.pallas_call(
        functools.partial(_nsa_sample_attend_kernel, t=t, past=past),
        grid_spec=grid_spec,
        out_shape=jax.ShapeDtypeStruct((b, rows, KVW), F32),
        compiler_params=_cparams("parallel", "arbitrary"),
        name="nsa_sample_attend",
    )(page_table, q4, sel, ocmp, gl, *([cache_t] * nps), new_rows, win_t, expand)


def _route_kernel(x_ref, g_ref, sh_ref, sc_ref, rw_ref, rb_ref, h_ref, r_ref):
    h = _norm_mod(x_ref[...], g_ref[...], sh_ref[...], sc_ref[...])
    h_ref[...] = h
    logits = jnp.dot(h, rw_ref[...], precision=HIGHEST, preferred_element_type=F32) + rb_ref[...]
    lane = lax.broadcasted_iota(I32, logits.shape, 1)
    lane_f = lane.astype(F32)
    lg = jnp.where(lane < N_EXPERTS, logits, NEG)
    v1 = jnp.max(lg, axis=-1, keepdims=True)
    i1 = jnp.min(jnp.where(lg == v1, lane_f, 128.0), axis=-1, keepdims=True)
    lg2 = jnp.where(lane_f == i1, NEG, lg)
    v2 = jnp.max(lg2, axis=-1, keepdims=True)
    i2 = jnp.min(jnp.where(lg2 == v2, lane_f, 128.0), axis=-1, keepdims=True)
    e = jnp.exp(v2 - v1)
    w1 = 1.0 / (1.0 + e)
    w2 = e / (1.0 + e)
    r_ref[...] = jnp.where(lane == 0, i1, jnp.where(lane == 1, i2, jnp.where(lane == 2, w1, jnp.where(lane == 3, w2, 0.0))))


def moe_route(x, g, shift, scale, rw_pad, rb_pad, *, per_token, tm, tpb):
    m, d = x.shape
    sh, sh_spec = _mod_arg(shift, per_token, tm, tpb)
    sc, sc_spec = _mod_arg(scale, per_token, tm, tpb)
    return pl.pallas_call(
        _route_kernel,
        grid=(m // tm,),
        in_specs=[pl.BlockSpec((tm, d), lambda i: (i, 0)), _vec_spec(d), sh_spec, sc_spec,
                  pl.BlockSpec((d, 128), lambda i: (0, 0)), _vec_spec(128)],
        out_specs=[pl.BlockSpec((tm, d), lambda i: (i, 0)), pl.BlockSpec((tm, 128), lambda i: (i, 0))],
        out_shape=[jax.ShapeDtypeStruct((m, d), F32), jax.ShapeDtypeStruct((m, 128), F32)],
        compiler_params=_cparams("parallel"),
        name="moe_route",
    )(x, g.reshape(1, d), sh, sc, rw_pad, rb_pad)


def _moe_kernel(te_ref, nv_ref, x_ref, wg_ref, wu_ref, wd_ref, o_ref, xb_ref, acc_ref):
    i = pl.program_id(0)
    f = pl.program_id(1)

    @pl.when(f == 0)
    def _():
        acc_ref[...] = jnp.zeros(acc_ref.shape, F32)
        xb_ref[...] = x_ref[...].astype(BF16)

    @pl.when(i < nv_ref[0])
    def _():
        x = xb_ref[...]
        act = _silu(_dot(x, wg_ref[...])) * _dot(x, wu_ref[...])
        acc_ref[...] += _dot(act.astype(BF16), wd_ref[...])

    @pl.when(f == pl.num_programs(1) - 1)
    def _():
        o_ref[...] = acc_ref[...]


def moe_experts(xs, tile_expert, n_valid, w_gu_b, w_down_b, *, tm, tf):
    p, d = xs.shape
    edim = w_down_b.shape[1]
    nf = edim // tf
    n_tiles = p // tm

    def wmap(off):
        def index(i, f, te, nv):
            ok = i < nv[0]
            return (te[i], 0, off + jnp.where(ok, f, nf - 1))
        return index

    def dmap(i, f, te, nv):
        return (te[i], jnp.where(i < nv[0], f, nf - 1), 0)

    grid_spec = pltpu.PrefetchScalarGridSpec(
        num_scalar_prefetch=2,
        grid=(n_tiles, nf),
        in_specs=[pl.BlockSpec((tm, d), lambda i, f, te, nv: (i, 0)),
                  pl.BlockSpec((None, d, tf), wmap(0)),
                  pl.BlockSpec((None, d, tf), wmap(nf)),
                  pl.BlockSpec((None, tf, d), dmap)],
        out_specs=pl.BlockSpec((tm, d), lambda i, f, te, nv: (i, 0)),
        scratch_shapes=[pltpu.VMEM((tm, d), BF16), pltpu.VMEM((tm, d), F32)],
    )
    return pl.pallas_call(
        _moe_kernel,
        grid_spec=grid_spec,
        out_shape=jax.ShapeDtypeStruct((p, d), F32),
        compiler_params=_cparams("parallel", "arbitrary"),
        name="moe_experts",
    )(tile_expert, n_valid, xs, w_gu_b, w_gu_b, w_down_b)


def _moe_tables(route, tm):
    m = route.shape[0]
    na = TOP_K * m
    e = route[:, :TOP_K].astype(I32).reshape(na)
    onehot = (e[:, None] == jnp.arange(N_EXPERTS, dtype=I32)[None, :]).astype(I32)
    within = jnp.sum((jnp.cumsum(onehot, axis=0) - onehot) * onehot, axis=1)
    counts = jnp.sum(onehot, axis=0)
    padded = ((counts + tm - 1) // tm) * tm
    ends = jnp.cumsum(padded)
    starts = ends - padded
    dest = starts[e] + within
    n_slots = (-(-na // tm) + N_EXPERTS) * tm
    src_tok = jnp.zeros((n_slots,), I32).at[dest].set(jnp.arange(na, dtype=I32) // TOP_K,
                                                       mode='promise_in_bounds', unique_indices=True)
    n_tiles = n_slots // tm
    n_valid = (ends[-1] // tm).astype(I32)
    tile_start = jnp.arange(n_tiles, dtype=I32) * tm
    tile_expert = jnp.sum((tile_start[:, None] >= ends[None, :]).astype(I32), axis=1)
    last = jnp.take(tile_expert, jnp.maximum(n_valid - 1, 0))
    tile_expert = jnp.where(jnp.arange(n_tiles) < n_valid, tile_expert, last).astype(I32)
    return src_tok, tile_expert, n_valid.reshape(1), dest


def _split_mod(mod):
    return [mod[:, i * D_MODEL:(i + 1) * D_MODEL] for i in range(mod.shape[1] // D_MODEL)]


def kernel(x_prompt, x_sample, c_prompt, c_sample, state_ret, cache_kv, cache_win, page_table, w_mod, b_mod, norm_g, ret_w_in, ret_gn_g, ret_w_out, kv_w_mod, kv_b_mod, kv_norm_g, kv_w, cmp_pos, cmp_w1, cmp_b1, cmp_w2, nsa_w_in, nsa_w_out, ffn_w_gu, ffn_w_down, moe_router_w, moe_router_b, moe_w_gu, moe_w_down):
    bp, t, d = x_prompt.shape
    bs, ts, _ = x_sample.shape
    mp, ms = bp * t, bs * ts
    n_pool, page = cache_kv.shape[:2]
    past = page_table.shape[1] * page
    g, r, hd = NSA_KV_HEADS, NSA_GROUP, NSA_HEAD_DIM

    ret_w_in_b = ret_w_in[0].astype(BF16)
    ret_w_out_b = ret_w_out[0].astype(BF16)
    kv_w_b = kv_w.astype(BF16)
    kv_wt_b = kv_w.T.astype(BF16)
    nq = NSA_HEADS * hd
    gate_w = nsa_w_in[0][:, nq:].reshape(d, g, r * N_BRANCH)
    gate_w = jnp.pad(gate_w, ((0, 0), (0, 0), (0, LANES - r * N_BRANCH))).reshape(d, g * LANES)
    nsa_w_in_b = jnp.concatenate([nsa_w_in[0][:, :nq], gate_w], axis=1).astype(BF16)
    gate_wt = nsa_w_in[0][:, nq:].T.reshape(g, r * N_BRANCH, d)
    gate_wt = jnp.pad(gate_wt, ((0, 0), (0, GATE_ROWS - r * N_BRANCH), (0, 0))).reshape(g * GATE_ROWS, d)
    nsa_wt_b = jnp.concatenate([nsa_w_in[0][:, :nq].T, gate_wt], axis=0).astype(BF16)
    def pad_heads(w):
        return jnp.pad(w.reshape(d, g, hd), ((0, 0), (0, 0), (0, LANES - hd))).reshape(d, g * LANES)
    kpad_w_b = jnp.concatenate([pad_heads(kv_w[:, 2 * KVW:3 * KVW]), pad_heads(kv_w[:, 4 * KVW:5 * KVW])],
                               axis=1).astype(BF16)
    nsa_w_out_b = nsa_w_out[0].astype(BF16)
    ffn_w_gu_b = ffn_w_gu[0].astype(BF16)
    ffn_w_down_b = ffn_w_down[0].astype(BF16)
    moe_w_gu_b = moe_w_gu[0].astype(BF16)
    moe_w_down_b = moe_w_down[0].astype(BF16)
    x_prompt, moe_w_gu_b, moe_w_down_b = lax.optimization_barrier((x_prompt, moe_w_gu_b, moe_w_down_b))
    rw_pad = jnp.pad(moe_router_w[0], ((0, 0), (0, 128 - N_EXPERTS)))
    rb_pad = jnp.pad(moe_router_b[0], (0, 128 - N_EXPERTS)).reshape(1, 128)
    cw = _compress_weights(cmp_pos, cmp_w1, cmp_b1, cmp_w2)

    c_all = jnp.concatenate([c_prompt, c_sample], axis=0)
    mods = [cond_matmul(c_all, w_mod, b_mod, layer) for layer in range(w_mod.shape[0])]
    kv_mod = cond_matmul(c_all, kv_w_mod[None], kv_b_mod[None], 0)

    tm_p = min(512, t)
    tm_f = min(1024, t)
    groups = {
        'p': dict(x=x_prompt.reshape(mp, d), b=bp, t=t, per_token=False, tm=tm_p, tpb=t // tm_p,
                  tm_f=tm_f, tpb_f=t // tm_f,
                  mod=lambda a: a[:bp]),
        's': dict(x=x_sample.reshape(ms, d), b=bs, t=ts, per_token=True, tm=ms, tpb=1, tm_f=ms, tpb_f=1,
                  mod=lambda a: jnp.repeat(a[bp:], ts, axis=0)),
    }
    out = {}

    for name, gr in groups.items():
        m0 = [gr['mod'](a) for a in _split_mod(mods[0])]
        if name == 'p':
            pos = jnp.arange(t)
            s0 = jnp.zeros((bp, RET_HEADS, RET_DK, RET_DV), F32)
        else:
            pos = past + jnp.arange(ts)
            s0 = state_ret[0]
        x, s_new = retention_sublayer(gr['x'], norm_g[0, 0], m0[0], m0[1], ret_w_in_b, pos, s0, ret_gn_g[0],
                                      ret_w_out_b, m0[2], norm_g[0, 1], b=gr['b'], per_token=gr['per_token'],
                                      chunk=math.gcd(gr['t'], RET_CHUNK))
        out['ret_' + name] = s_new[None]
        x = ffn_sublayer(x, norm_g[0, 2], m0[3], m0[4], ffn_w_gu_b, ffn_w_down_b, m0[5], norm_g[0, 3],
                         per_token=gr['per_token'], tm=gr['tm_f'], tpb=gr['tpb_f'],
                         tf=_largest_tile(ffn_w_down.shape[1], FFN_HIDDEN_TILE_CAP))
        gr['x1'] = x

    def attention(name):
        gr = groups[name]
        kw = dict(per_token=gr['per_token'], tm=gr['tm'], tpb=gr['tpb'])
        m1 = [gr['mod'](a) for a in _split_mod(mods[1])]
        kvm = [gr['mod'](a) for a in _split_mod(kv_mod)]
        gr['m1'] = m1
        x = gr['x1']
        b_, t_ = gr['b'], gr['t']
        if name == 'p':
            kvt, wint = norm_mod_matmul_t(x, kv_norm_g, kvm[0], kvm[1], kv_wt_b,
                                          (N_PAGED_ROWS * KVW, (N_KV_ROWS - N_PAGED_ROWS) * KVW), b=b_, tm=gr['tm'])
            out['kv_p'] = kvt.reshape(b_, N_PAGED_ROWS, g, hd, t_).transpose(0, 4, 1, 2, 3)
            n_keep = min(WINDOW, t_)
            out['win_p'] = wint[:, :, t_ - n_keep:].reshape(b_, 2, g, hd, n_keep).transpose(0, 4, 1, 2, 3)
            kpad = norm_mod_matmul(x, kv_norm_g, kvm[0], kvm[1], kpad_w_b, out_dtype=BF16, **kw)
            qt, glt = norm_mod_matmul_t(x, norm_g[1, 0], m1[0], m1[1], nsa_wt_b, (nq, g * GATE_ROWS),
                                        b=b_, tm=gr['tm'])
            cmp = compress_prompt(kvt, cw).reshape(2, b_, -1, g, hd)
            kc = jnp.pad(cmp[0].transpose(0, 2, 1, 3), ((0, 0), (0, 0), (0, 0), (0, LANES - hd)))
            o = nsa_prompt(qt, glt, kc, cmp[1].transpose(0, 2, 3, 1), kpad, kvt, wint)
        else:
            proj = norm_mod_matmul(x, norm_g[1, 0], m1[0], m1[1], nsa_w_in_b, **kw)
            rows = norm_mod_matmul(x, kv_norm_g, kvm[0], kvm[1], kv_w_b, **kw)
            q = (proj[:, :NSA_HEADS * hd] * (hd ** -0.5)).reshape(b_, t_, g, r, hd)
            gl = proj[:, NSA_HEADS * hd:].reshape(b_, t_, g, LANES)[..., :r * N_BRANCH]
            gl = gl.reshape(b_, t_, g, r, N_BRANCH)
            rows3 = rows.reshape(b_, t_, N_KV_ROWS * KVW)
            rows6 = rows.reshape(b_, t_, N_KV_ROWS, g, hd)
            out['kv_s'] = rows6[:, :, :N_PAGED_ROWS]
            cache_t = cache_kv.transpose(0, 2, 3, 4, 1).reshape(n_pool, N_PAGED_ROWS, KVW, page)
            win_t5 = cache_win.transpose(0, 2, 3, 4, 1)
            n_win = cache_win.shape[1]
            new_win_t = rows6[:, :, N_PAGED_ROWS:].transpose(0, 2, 3, 4, 1)
            out['win_s'] = jnp.concatenate([win_t5, new_win_t], axis=-1)[..., t_:].transpose(0, 4, 1, 2, 3)
            new_sub = jnp.pad(rows3[:, :, :2 * KVW], ((0, 0), (0, CMP_STRIDE - t_), (0, 0)))
            new_sub = new_sub.reshape(b_, CMP_STRIDE, 2, KVW).transpose(2, 0, 1, 3)
            cmp = compress_sample(cache_t, page_table, new_sub, cw)
            eye = jnp.eye(g, dtype=F32)
            qrows = q.transpose(0, 2, 3, 1, 4).reshape(b_, g, r * t_, hd)
            q4 = jnp.einsum('bgxd,gj->bgxjd', qrows, eye).reshape(b_, g * r * t_, KVW)
            glr = gl.transpose(0, 2, 3, 1, 4).reshape(b_, g * r * t_, N_BRANCH)
            ocmp, sel = nsa_sample_select(q4, cmp, t=t_, past=past)
            o4 = nsa_sample_attend(q4, sel, ocmp, glr, cache_t, page_table, rows3,
                                   win_t5.reshape(b_, 2, KVW, n_win), t=t_, past=past)
            o4 = o4.reshape(b_, g, r, t_, g, hd)
            o = jnp.einsum('bgrtjd,gj->btgrd', o4, eye).reshape(ms, NSA_HEADS * hd)
        gr['x2'] = matmul_norm_residual(o, nsa_w_out_b, x, m1[2], norm_g[1, 1], **kw)

    take = lambda a, idx: a.at[idx].get(mode='promise_in_bounds')

    def moe_plan(name):
        gr = groups[name]
        kw = dict(per_token=gr['per_token'], tm=gr['tm'], tpb=gr['tpb'])
        h, route = moe_route(gr['x2'], norm_g[1, 2], gr['m1'][3], gr['m1'][4], rw_pad, rb_pad, **kw)
        tm_e = 512 if h.shape[0] >= 4096 else 128
        src_tok, tile_expert, n_valid, dest = _moe_tables(route, tm_e)
        return dict(h=h, src_tok=src_tok, route=route, tile_expert=tile_expert, n_valid=n_valid,
                    dest=dest.reshape(-1, TOP_K), tm_e=tm_e)

    def moe_finish(name, dp, xs):
        gr = groups[name]
        kw = dict(per_token=gr['per_token'], tm=gr['tm'], tpb=gr['tpb'])
        ys = moe_experts(xs, dp['tile_expert'], dp['n_valid'], moe_w_gu_b, moe_w_down_b, tm=dp['tm_e'],
                         tf=_largest_tile(moe_w_down.shape[2], MOE_HIDDEN_TILE_CAP))
        out['y_' + name] = moe_combine(take(ys, dp['dest'][:, 0]), take(ys, dp['dest'][:, 1]), dp['route'],
                                       gr['x2'], gr['m1'][5], norm_g[1, 3], **kw)

    attention('p')
    plan_p = moe_plan('p')
    plan_p['src_tok'], groups['s']['x1'] = lax.optimization_barrier((plan_p['src_tok'], groups['s']['x1']))
    xs_p = take(plan_p['h'], plan_p['src_tok'])
    attention('s')
    xs_p, groups['s']['x2'] = lax.optimization_barrier((xs_p, groups['s']['x2']))
    moe_finish('p', plan_p, xs_p)
    plan_s = moe_plan('s')
    moe_finish('s', plan_s, take(plan_s['h'], plan_s['src_tok']))

    return (out['y_p'].reshape(bp, t, d), out['y_s'].reshape(bs, ts, d),
            out['ret_p'], out['ret_s'], out['kv_p'], out['kv_s'], out['win_p'], out['win_s'])
```

```python
import functools
import math

import numpy as np
import jax
import jax.numpy as jnp
from jax import lax
from jax.experimental import pallas as pl
from jax.experimental.pallas import tpu as pltpu

F32 = jnp.float32
BF16 = jnp.bfloat16
I32 = jnp.int32

D_MODEL = 1024
N_MOD = 6
RET_HEADS = 4
RET_DK = 256
RET_DV = 256
RET_CHUNK = 128
ROPE_BASE = 10000.0
NSA_HEADS = 16
NSA_KV_HEADS = 4
NSA_GROUP = 4
NSA_HEAD_DIM = 64
N_BRANCH = 3
N_KV_ROWS = 6
N_PAGED_ROWS = 4
CMP_BLOCK = 32
CMP_STRIDE = 16
CMP_HIDDEN = 128
SEL_BLOCK = 64
N_SEL = 16
FORCED_SCORE = 1.0e4
WINDOW = 512
Q_BLOCK = 128
N_EXPERTS = 8
TOP_K = 2
EPS = 1e-6

NEG = -1.0e30
KVW = NSA_KV_HEADS * NSA_HEAD_DIM
VMEM_LIMIT_BYTES = 56 * 1024 * 1024
HIGHEST = lax.Precision.HIGHEST


LANES = 128
FFN_HIDDEN_TILE_CAP = 1408
MOE_HIDDEN_TILE_CAP = 896


def _largest_tile(n, cap):
    best = LANES
    for k in range(LANES, cap + 1, LANES):
        if n % k == 0:
            best = k
    return best


def _cparams(*sem):
    return pltpu.CompilerParams(dimension_semantics=sem, vmem_limit_bytes=VMEM_LIMIT_BYTES)


def _sigmoid(x):
    return 1.0 / (1.0 + jnp.exp(-x))


def _silu(x):
    return x * _sigmoid(x)


def _gelu_tanh(x):
    return x * (0.5 * (1.0 + jnp.tanh(math.sqrt(2.0 / math.pi) * (x + 0.044715 * (x * x * x)))))


def _norm_mod(x, g, shift, scale):
    ms = jnp.mean(x * x, axis=-1, keepdims=True)
    return (x * lax.rsqrt(ms + EPS) * g) * (1.0 + scale) + shift


def _rms_residual(x, gate, o, g):
    ms = jnp.mean(o * o, axis=-1, keepdims=True)
    return x + gate * (o * lax.rsqrt(ms + EPS) * g)


def _dot(a, b):
    return jnp.dot(a, b, preferred_element_type=F32)


def _dot_nt(a, b):
    return lax.dot_general(a, b, (((1,), (1,)), ((), ())), preferred_element_type=F32)


def _masked_softmax(s, mask):
    sm = jnp.where(mask, s, NEG)
    m = jnp.max(sm, axis=-1, keepdims=True)
    e = jnp.where(mask, jnp.exp(sm - m), 0.0)
    return e / jnp.maximum(jnp.sum(e, axis=-1, keepdims=True), 1e-30)


def _flash_init(m_ref, l_ref, acc_ref):
    m_ref[...] = jnp.full(m_ref.shape, NEG, F32)
    l_ref[...] = jnp.zeros(l_ref.shape, F32)
    acc_ref[...] = jnp.zeros(acc_ref.shape, F32)


def _flash_update(s, mask, v_b, m_ref, l_ref, acc_ref):
    sm = jnp.where(mask, s, NEG)
    m_old = m_ref[...]
    m_new = jnp.maximum(m_old, jnp.max(sm, axis=-1, keepdims=True))
    alpha = jnp.exp(m_old - m_new)
    p = jnp.where(mask, jnp.exp(sm - m_new), 0.0)
    l_ref[...] = alpha * l_ref[...] + jnp.sum(p, axis=-1, keepdims=True)
    acc_ref[...] = alpha * acc_ref[...] + _dot(p.astype(BF16), v_b)
    m_ref[...] = m_new


def _flash_result(l_ref, acc_ref):
    return acc_ref[...] / jnp.maximum(l_ref[...], 1e-30)


def _select_blocks(imp, cur, n_sel):
    rows, lanes = imp.shape
    blk = lax.broadcasted_iota(I32, (rows, lanes), 1)
    valid = (blk <= cur) & (blk < n_sel)
    forced = (blk == 0) | (blk == cur) | (blk == cur - 1)
    score = jnp.where(valid, jnp.where(forced, FORCED_SCORE, imp), -1.0)
    rank = jnp.zeros((rows, lanes), F32)
    for i in range(n_sel):
        ci = score[:, i:i + 1]
        beats = (ci > score) | ((ci == score) & (blk > i))
        rank = rank + jnp.where(beats, 1.0, 0.0)
    return jnp.where(valid & (rank < float(min(N_SEL, n_sel))), 1.0, 0.0)


def _cond_kernel(c_ref, w_ref, b_ref, o_ref):
    sc = _silu(c_ref[...])
    o_ref[...] = _dot(sc.astype(BF16), w_ref[...].astype(BF16)) + b_ref[...]


def cond_matmul(c, w, b, layer):
    bc, d = c.shape
    n_layers, _, n = w.shape
    tn = 1024
    return pl.pallas_call(
        _cond_kernel,
        grid=(n // tn,),
        in_specs=[pl.BlockSpec((bc, d), lambda j: (0, 0)),
                  pl.BlockSpec((None, d, tn), lambda j: (layer, 0, j)),
                  pl.BlockSpec((None, 1, tn), lambda j: (layer, 0, j))],
        out_specs=pl.BlockSpec((bc, tn), lambda j: (0, j)),
        out_shape=jax.ShapeDtypeStruct((bc, n), F32),
        compiler_params=_cparams("parallel"),
        name="cond_matmul",
    )(c, w, b.reshape(n_layers, 1, n))


def _mod_arg(m, per_token, tm, tiles_per_batch):
    d = m.shape[-1]
    if per_token:
        return m, pl.BlockSpec((tm, d), lambda i, *_: (i, 0))
    return m[:, None, :], pl.BlockSpec((None, 1, d), lambda i, *_: (i // tiles_per_batch, 0, 0))


def _vec_spec(d):
    return pl.BlockSpec((1, d), lambda i, *_: (0, 0))


def _nmm_kernel(x_ref, g_ref, sh_ref, sc_ref, w_ref, o_ref, h_ref):
    @pl.when(pl.program_id(1) == 0)
    def _():
        h_ref[...] = _norm_mod(x_ref[...], g_ref[...], sh_ref[...], sc_ref[...]).astype(BF16)

    o_ref[...] = _dot(h_ref[...], w_ref[...]).astype(o_ref.dtype)


def norm_mod_matmul(x, g, shift, scale, w_b, *, per_token, tm, tpb, tn=None, out_dtype=F32):
    m, d = x.shape
    n = w_b.shape[1]
    tn = n if tn is None else tn
    sh, sh_spec = _mod_arg(shift, per_token, tm, tpb)
    sc, sc_spec = _mod_arg(scale, per_token, tm, tpb)
    return pl.pallas_call(
        _nmm_kernel,
        grid=(m // tm, n // tn),
        in_specs=[pl.BlockSpec((tm, d), lambda i, j: (i, 0)), _vec_spec(d), sh_spec, sc_spec,
                  pl.BlockSpec((d, tn), lambda i, j: (0, j))],
        out_specs=pl.BlockSpec((tm, tn), lambda i, j: (i, j)),
        out_shape=jax.ShapeDtypeStruct((m, n), out_dtype),
        scratch_shapes=[pltpu.VMEM((tm, d), BF16)],
        compiler_params=_cparams("parallel", "arbitrary"),
        name="norm_mod_matmul",
    )(x, g.reshape(1, d), sh, sc, w_b)


def _nmm_t_kernel(x_ref, g_ref, sh_ref, sc_ref, wt_ref, *o_refs):
    h = _norm_mod(x_ref[...], g_ref[...], sh_ref[...], sc_ref[...]).astype(BF16)
    o = _dot_nt(wt_ref[...], h)
    row = 0
    for o_ref in o_refs:
        o_ref[...] = o[row:row + o_ref.shape[0]]
        row += o_ref.shape[0]


def norm_mod_matmul_t(x, g, shift, scale, wt_b, splits, *, b, tm):
    m, d = x.shape
    t = m // b
    tpb = t // tm
    n = wt_b.shape[0]
    sh, sh_spec = _mod_arg(shift, False, tm, tpb)
    sc, sc_spec = _mod_arg(scale, False, tm, tpb)
    return pl.pallas_call(
        _nmm_t_kernel,
        grid=(m // tm,),
        in_specs=[pl.BlockSpec((tm, d), lambda i: (i, 0)), _vec_spec(d), sh_spec, sc_spec,
                  pl.BlockSpec((n, d), lambda i: (0, 0))],
        out_specs=[pl.BlockSpec((None, ni, tm), lambda i: (i // tpb, 0, i % tpb)) for ni in splits],
        out_shape=[jax.ShapeDtypeStruct((b, ni, t), F32) for ni in splits],
        compiler_params=_cparams("parallel"),
        name="norm_mod_matmul_t",
    )(x, g.reshape(1, d), sh, sc, wt_b)


def _mnr_kernel(a_ref, w_ref, x_ref, gate_ref, g_ref, o_ref):
    o = _dot(a_ref[...].astype(BF16), w_ref[...])
    o_ref[...] = _rms_residual(x_ref[...], gate_ref[...], o, g_ref[...])


def matmul_norm_residual(a, w_b, x, gate, g, *, per_token, tm, tpb):
    m, k = a.shape
    d = w_b.shape[1]
    gt, gt_spec = _mod_arg(gate, per_token, tm, tpb)
    return pl.pallas_call(
        _mnr_kernel,
        grid=(m // tm,),
        in_specs=[pl.BlockSpec((tm, k), lambda i: (i, 0)),
                  pl.BlockSpec((k, d), lambda i: (0, 0)),
                  pl.BlockSpec((tm, d), lambda i: (i, 0)), gt_spec, _vec_spec(d)],
        out_specs=pl.BlockSpec((tm, d), lambda i: (i, 0)),
        out_shape=jax.ShapeDtypeStruct((m, d), F32),
        compiler_params=_cparams("parallel"),
        name="matmul_norm_residual",
    )(a, w_b, x, gt, g.reshape(1, d))


def _combine_kernel(y1_ref, y2_ref, r_ref, x_ref, gate_ref, g_ref, o_ref):
    route = r_ref[...]
    y = route[:, TOP_K:TOP_K + 1] * y1_ref[...] + route[:, TOP_K + 1:TOP_K + 2] * y2_ref[...]
    o_ref[...] = _rms_residual(x_ref[...], gate_ref[...], y, g_ref[...])


def moe_combine(y1, y2, route, x, gate, g, *, per_token, tm, tpb):
    m, d = x.shape
    gt, gt_spec = _mod_arg(gate, per_token, tm, tpb)
    row = lambda n: pl.BlockSpec((tm, n), lambda i: (i, 0))
    return pl.pallas_call(
        _combine_kernel,
        grid=(m // tm,),
        in_specs=[row(d), row(d), row(route.shape[1]), row(d), gt_spec, _vec_spec(d)],
        out_specs=row(d),
        out_shape=jax.ShapeDtypeStruct((m, d), F32),
        compiler_params=_cparams("parallel"),
        name="moe_combine",
    )(y1, y2, route, x, gt, g.reshape(1, d))


def _ffn_kernel(x_ref, g2_ref, sh_ref, sc_ref, wg_ref, wu_ref, wd_ref, gate_ref, g3_ref, o_ref,
                h_ref, acc_ref):
    f = pl.program_id(1)

    @pl.when(f == 0)
    def _():
        h_ref[...] = _norm_mod(x_ref[...], g2_ref[...], sh_ref[...], sc_ref[...]).astype(BF16)
        acc_ref[...] = jnp.zeros(acc_ref.shape, F32)

    h = h_ref[...]
    act = _silu(_dot(h, wg_ref[...])) * _dot(h, wu_ref[...])
    acc_ref[...] += _dot(act.astype(BF16), wd_ref[...])

    @pl.when(f == pl.num_programs(1) - 1)
    def _():
        o_ref[...] = _rms_residual(x_ref[...], gate_ref[...], acc_ref[...], g3_ref[...])


def ffn_sublayer(x, g2, shift, scale, w_gu_b, w_down_b, gate, g3, *, per_token, tm, tpb, tf):
    m, d = x.shape
    fdim = w_down_b.shape[0]
    nf = fdim // tf
    sh, sh_spec = _mod_arg(shift, per_token, tm, tpb)
    sc, sc_spec = _mod_arg(scale, per_token, tm, tpb)
    gt, gt_spec = _mod_arg(gate, per_token, tm, tpb)
    return pl.pallas_call(
        _ffn_kernel,
        grid=(m // tm, nf),
        in_specs=[pl.BlockSpec((tm, d), lambda i, f: (i, 0)), _vec_spec(d), sh_spec, sc_spec,
                  pl.BlockSpec((d, tf), lambda i, f: (0, f)),
                  pl.BlockSpec((d, tf), lambda i, f: (0, nf + f)),
                  pl.BlockSpec((tf, d), lambda i, f: (f, 0)),
                  gt_spec, _vec_spec(d)],
        out_specs=pl.BlockSpec((tm, d), lambda i, f: (i, 0)),
        out_shape=jax.ShapeDtypeStruct((m, d), F32),
        scratch_shapes=[pltpu.VMEM((tm, d), BF16), pltpu.VMEM((tm, d), F32)],
        compiler_params=_cparams("parallel", "arbitrary"),
        name="ffn_sublayer",
    )(x, g2.reshape(1, d), sh, sc, w_gu_b, w_gu_b, w_down_b, gt, g3.reshape(1, d))


def _ret_layer_kernel(x_ref, g0_ref, sh_ref, sc_ref, win_ref, cos_ref, sin_ref, dm_ref, xi_ref, zt_ref, gc_ref,
                      gn_ref, s0_ref, wout_ref, rgate_ref, g1_ref, o_ref, s_ref, pr_ref, oc_ref):
    @pl.when(pl.program_id(1) == 0)
    def _():
        s_ref[...] = s0_ref[...]

    x = x_ref[...]
    pr_ref[...] = _dot(_norm_mod(x, g0_ref[...], sh_ref[...], sc_ref[...]).astype(BF16), win_ref[...])
    tm = x.shape[0]
    rows = min(tm, RET_CHUNK)
    nq = RET_HEADS * RET_DK
    nv = RET_HEADS * RET_DV
    half = RET_DK // 2

    def pad(v):
        if rows == RET_CHUNK:
            return v
        return jnp.concatenate([v, jnp.zeros((RET_CHUNK - rows, v.shape[1]), v.dtype)], axis=0)

    for r0 in range(0, tm, rows):
        rs = slice(r0, r0 + rows)
        cos = pad(cos_ref[rs, :])
        sin = pad(sin_ref[rs, :])

        def rot(v):
            v1 = v[:, :half]
            v2 = v[:, half:]
            return jnp.concatenate([v1 * cos - v2 * sin, v2 * cos + v1 * sin], axis=-1)

        for h in range(RET_HEADS):
            vcols = slice(h * RET_DV, (h + 1) * RET_DV)
            q = rot(pad(pr_ref[rs, h * RET_DK:(h + 1) * RET_DK]))
            k = rot(pad(pr_ref[rs, nq + h * RET_DK:nq + (h + 1) * RET_DK])) * (RET_DK ** -0.5)
            qb = q.astype(BF16)
            kb = k.astype(BF16)
            vb = pad(pr_ref[rs, 2 * nq + h * RET_DV:2 * nq + (h + 1) * RET_DV]).astype(BF16)
            state = s_ref[h]
            inner = _dot_nt(qb, kb) * dm_ref[h]
            o = _dot(inner.astype(BF16), vb) + _dot(qb, state.astype(BF16)) * xi_ref[h]
            kz = (k * zt_ref[h]).astype(BF16)
            upd = lax.dot_general(kz, vb, (((0,), (0,)), ((), ())), preferred_element_type=F32)
            s_ref[h] = gc_ref[h] * state + upd
            mu = jnp.mean(o, axis=-1, keepdims=True)
            dev = o - mu
            var = jnp.mean(dev * dev, axis=-1, keepdims=True)
            on = dev * lax.rsqrt(var + EPS) * gn_ref[:, vcols]
            sgate = pr_ref[rs, 2 * nq + nv + h * RET_DV:2 * nq + nv + (h + 1) * RET_DV]
            oc_ref[rs, vcols] = _silu(sgate) * on[:rows]

    o = _dot(oc_ref[...].astype(BF16), wout_ref[...])
    o_ref[...] = _rms_residual(x, rgate_ref[...], o, g1_ref[...])


def _ret_tables(chunk, rows):
    h = RET_HEADS
    log_g = jnp.log(1.0 - jnp.exp(jnp.linspace(math.log(1.0 / 32), math.log(1.0 / 512), h, dtype=F32)))
    i = jnp.arange(chunk, dtype=F32)
    diff = i[:, None] - i[None, :]
    dmat = jnp.where(diff >= 0, jnp.exp(log_g[:, None, None] * jnp.maximum(diff, 0.0)), 0.0)
    xi = jnp.exp(log_g[:, None] * (i + 1.0))
    zeta = jnp.exp(log_g[:, None] * (chunk - 1.0 - i))
    gch = jnp.exp(log_g * chunk)
    pad = rows - chunk
    dmat = jnp.pad(dmat, ((0, 0), (0, pad), (0, pad)))
    xi = jnp.pad(xi, ((0, 0), (0, pad)))[..., None]
    zeta = jnp.pad(zeta, ((0, 0), (0, pad)))[..., None]
    gch = jnp.broadcast_to(gch[:, None, None], (h, 1, RET_DV))
    return dmat, xi, zeta, gch


def _rope_tables(pos, rows):
    half = RET_DK // 2
    inv = ROPE_BASE ** (-jnp.arange(half, dtype=F32) / half)
    ang = pos.astype(F32)[:, None] * inv[None, :]
    pad = rows - pos.shape[0]
    return jnp.pad(jnp.cos(ang), ((0, pad), (0, 0))), jnp.pad(jnp.sin(ang), ((0, pad), (0, 0)))


RET_ROW_TILE = 512


def retention_sublayer(x, g0, shift, scale, w_in_b, pos, s0, gn_g, w_out_b, rgate, g1, *, b, per_token, chunk):
    m, d = x.shape
    t = m // b
    tm = min(t, RET_ROW_TILE)
    nt = t // tm
    h = RET_HEADS
    dmat, xi, zeta, gch = _ret_tables(chunk, RET_CHUNK)
    cos, sin = _rope_tables(pos, t)
    gn = gn_g.reshape(1, -1)
    rows = lambda n: pl.BlockSpec((tm, n), lambda bi, ti: (bi * nt + ti, 0))
    const = lambda a: pl.BlockSpec(a.shape, lambda bi, ti: (0,) * a.ndim, pipeline_mode=pl.Buffered(1))
    state_spec = pl.BlockSpec((None, h, RET_DK, RET_DV), lambda bi, ti: (bi, 0, 0, 0))

    def mod_arg(a):
        if per_token:
            return a, rows(d)
        return a[:, None, :], pl.BlockSpec((None, 1, d), lambda bi, ti: (bi, 0, 0))

    sh, sh_spec = mod_arg(shift)
    sc, sc_spec = mod_arg(scale)
    rg, rg_spec = mod_arg(rgate)
    return pl.pallas_call(
        _ret_layer_kernel,
        grid=(b, nt),
        in_specs=[rows(d), const(g0.reshape(1, d)), sh_spec, sc_spec, const(w_in_b),
                  pl.BlockSpec((tm, RET_DK // 2), lambda bi, ti: (ti, 0)),
                  pl.BlockSpec((tm, RET_DK // 2), lambda bi, ti: (ti, 0)),
                  const(dmat), const(xi), const(zeta), const(gch), const(gn), state_spec,
                  const(w_out_b), rg_spec, const(g1.reshape(1, d))],
        out_specs=[rows(d), state_spec],
        out_shape=[jax.ShapeDtypeStruct((m, d), F32), jax.ShapeDtypeStruct((b, h, RET_DK, RET_DV), F32)],
        scratch_shapes=[pltpu.VMEM((tm, w_in_b.shape[1]), F32), pltpu.VMEM((tm, h * RET_DV), F32)],
        compiler_params=_cparams("parallel", "arbitrary"),
        name="retention_sublayer",
    )(x, g0.reshape(1, d), sh, sc, w_in_b, cos, sin, dmat, xi, zeta, gch, gn, s0, w_out_b, rg, g1.reshape(1, d))


def _compress_weights(cmp_pos, cmp_w1, cmp_b1, cmp_w2):
    g, d, hd = NSA_KV_HEADS, NSA_HEAD_DIM, CMP_HIDDEN
    eye = jnp.eye(g, dtype=F32)
    w1 = cmp_w1.reshape(2, CMP_BLOCK, d, hd)
    bd1 = jnp.einsum('ksdh,gj->ksgdjh', w1, eye).reshape(2, CMP_BLOCK, g * d, g * hd).astype(BF16)
    bd2 = jnp.einsum('khd,gj->kghjd', cmp_w2, eye).reshape(2, g * hd, g * d).astype(BF16)
    pos = jnp.tile(cmp_pos[:, :, None, :], (1, 1, g, 1)).reshape(2, CMP_BLOCK, 1, g * d)
    b1 = jnp.tile(cmp_b1[:, None, :], (1, g, 1)).reshape(2, 1, g * hd)
    return bd1, bd2, pos, b1


def _compress_finish(p0, p1, b1, bd2):
    rows = p0.shape[0]
    hid = b1 + p0 + pltpu.roll(p1, rows - 1, 0)
    return _dot(_gelu_tanh(hid).astype(BF16), bd2)


CMP_PAGES_PER_STEP = 16
ATT_PAGES_PER_STEP = 16


def _tokens_to_sublanes(xt_ref, tr_ref):
    for half in range(KVW // LANES):
        tr_ref[half] = xt_ref[half * LANES:(half + 1) * LANES, :].T


def _cmp_prompt_kernel(xt_ref, bd1_ref, bd2_ref, pos_ref, b1_ref, o_ref, tr_ref):
    n_sub = xt_ref.shape[1] // CMP_STRIDE
    hid = bd1_ref.shape[-1]
    _tokens_to_sublanes(xt_ref, tr_ref)
    p0 = jnp.zeros((n_sub, hid), F32)
    p1 = jnp.zeros((n_sub, hid), F32)
    for s in range(CMP_STRIDE):
        xs = jnp.concatenate([tr_ref[half, pl.ds(s, n_sub, stride=CMP_STRIDE), :]
                              for half in range(KVW // LANES)], axis=-1)
        p0 = p0 + _dot((xs + pos_ref[s]).astype(BF16), bd1_ref[s])
        p1 = p1 + _dot((xs + pos_ref[CMP_STRIDE + s]).astype(BF16), bd1_ref[CMP_STRIDE + s])
    o_ref[...] = _compress_finish(p0, p1, b1_ref[...], bd2_ref[...])


def compress_prompt(kvt, cw):
    bd1, bd2, pos, b1 = cw
    b, _, t = kvt.shape
    n_sub = t // CMP_STRIDE
    return pl.pallas_call(
        _cmp_prompt_kernel,
        grid=(b, 2),
        in_specs=[pl.BlockSpec((None, KVW, t), lambda bi, ki: (bi, ki, 0)),
                  pl.BlockSpec((None,) + bd1.shape[1:], lambda bi, ki: (ki, 0, 0, 0)),
                  pl.BlockSpec((None,) + bd2.shape[1:], lambda bi, ki: (ki, 0, 0)),
                  pl.BlockSpec((None,) + pos.shape[1:], lambda bi, ki: (ki, 0, 0, 0)),
                  pl.BlockSpec((None,) + b1.shape[1:], lambda bi, ki: (ki, 0, 0))],
        out_specs=pl.BlockSpec((None, None, n_sub, KVW), lambda bi, ki: (ki, bi, 0, 0)),
        out_shape=jax.ShapeDtypeStruct((2, b, n_sub, KVW), F32),
        scratch_shapes=[pltpu.VMEM((KVW // LANES, t, LANES), F32)],
        compiler_params=_cparams("parallel", "parallel"),
        name="compress_prompt",
    )(kvt, bd1, bd2, pos, b1)


def _cmp_sample_kernel(pt_ref, *refs):
    page_refs = refs[:CMP_PAGES_PER_STEP]
    new_ref, bd1_ref, bd2_ref, pos_ref, b1_ref, o_ref, sub_ref, tr_ref = refs[CMP_PAGES_PER_STEP:]
    p = pl.program_id(1)
    page = page_refs[0].shape[2]
    per_page = page // CMP_STRIDE
    n_cmp = o_ref.shape[1]
    for j, page_ref in enumerate(page_refs):
        row0 = pl.multiple_of((p * CMP_PAGES_PER_STEP + j) * per_page, per_page)
        for kind in range(2):
            _tokens_to_sublanes(page_ref.at[kind], tr_ref.at[kind])
            for s in range(CMP_STRIDE):
                for half in range(KVW // LANES):
                    piece = tr_ref[kind, half, pl.ds(s, per_page, stride=CMP_STRIDE), :]
                    sub_ref[kind, s, pl.ds(row0, per_page), half * LANES:(half + 1) * LANES] = piece

    @pl.when(p == pl.num_programs(1) - 1)
    def _():
        n_rows = sub_ref.shape[2]
        tail = n_rows - n_cmp
        hid = bd1_ref.shape[-1]
        for kind in range(2):
            p0 = jnp.zeros((n_rows, hid), F32)
            p1 = jnp.zeros((n_rows, hid), F32)
            for s in range(CMP_STRIDE):
                sub_ref[kind, s, pl.ds(n_cmp, tail), :] = jnp.broadcast_to(new_ref[kind, s:s + 1, :], (tail, KVW))
                xs = sub_ref[kind, s]
                p0 = p0 + _dot((xs + pos_ref[kind, s]).astype(BF16), bd1_ref[kind, s])
                p1 = p1 + _dot((xs + pos_ref[kind, CMP_STRIDE + s]).astype(BF16), bd1_ref[kind, CMP_STRIDE + s])
            o_ref[kind] = _compress_finish(p0, p1, b1_ref[kind], bd2_ref[kind])[:n_cmp]


def _page_specs(n_per_step, kind_block, page):
    def spec(j):
        return pl.BlockSpec((None, 2, KVW, page),
                            lambda bi, pi, pt: (pt[bi, pi * n_per_step + j], kind_block, 0, 0))
    return [spec(j) for j in range(n_per_step)]


def compress_sample(cache_t, page_table, new_sub, cw):
    bd1, bd2, pos, b1 = cw
    b, n_pages = page_table.shape
    page = cache_t.shape[3]
    n_cmp = n_pages * page // CMP_STRIDE
    nps = CMP_PAGES_PER_STEP
    const = lambda a: pl.BlockSpec(a.shape, lambda bi, pi, pt: (0,) * a.ndim, pipeline_mode=pl.Buffered(1))
    grid_spec = pltpu.PrefetchScalarGridSpec(
        num_scalar_prefetch=1,
        grid=(b, n_pages // nps),
        in_specs=_page_specs(nps, 0, page) + [
            pl.BlockSpec((2, None, CMP_STRIDE, KVW), lambda bi, pi, pt: (0, bi, 0, 0)),
            const(bd1), const(bd2), const(pos), const(b1)],
        out_specs=pl.BlockSpec((2, None, n_cmp, KVW), lambda bi, pi, pt: (0, bi, 0, 0)),
        scratch_shapes=[pltpu.VMEM((2, CMP_STRIDE, n_cmp + 8, KVW), F32),
                        pltpu.VMEM((2, KVW // LANES, page, LANES), F32)],
    )
    return pl.pallas_call(
        _cmp_sample_kernel,
        grid_spec=grid_spec,
        out_shape=jax.ShapeDtypeStruct((2, b, n_cmp, KVW), F32),
        compiler_params=_cparams("parallel", "arbitrary"),
        name="compress_sample",
    )(page_table, *([cache_t] * nps), new_sub, bd1, bd2, pos, b1)


def _cover_matrix(n_cmp, n_sel, rows, cols):
    i = np.arange(n_cmp)[:, None]
    j = np.arange(n_sel)[None, :]
    cover = (i * CMP_STRIDE < (j + 1) * SEL_BLOCK) & (i * CMP_STRIDE + CMP_BLOCK > j * SEL_BLOCK)
    out = np.zeros((rows, cols), np.float32)
    out[:n_cmp, :n_sel] = cover
    return jnp.asarray(out)


SEL_CHUNK = 256
SEL_CLASS = 512


def _select_blocks_t(imp_t, cur, n_sel):
    nb, nq = imp_t.shape
    blk = lax.broadcasted_iota(I32, (nb, nq), 0)
    valid = (blk <= cur) & (blk < n_sel)
    forced = (blk == 0) | (blk == cur) | (blk == cur - 1)
    score = jnp.where(valid, jnp.where(forced, FORCED_SCORE, imp_t), -1.0)
    rank = jnp.zeros((nb, nq), F32)
    for i in range(n_sel):
        ci = score[i:i + 1, :]
        beats = (ci > score) | ((ci == score) & (blk > i))
        rank = rank + jnp.where(beats, 1.0, 0.0)
    return jnp.where(valid & (rank < float(min(N_SEL, n_sel))), 1.0, 0.0)


def _nsa_prompt_kernel(qt_ref, glt_ref, kc_ref, vct_ref, covt_ref, kblk_ref, wb_ref, ks_ref, vst_ref, kw_ref, vwt_ref,
                       o_ref, s_ref, osel_ref, *, n_cmp, n_sel):
    tq = qt_ref.shape[1]
    r, d = NSA_GROUP, NSA_HEAD_DIM
    cols = r * tq
    t = ks_ref.shape[0]
    q0 = pl.program_id(2) * tq
    qt = qt_ref[...] * (d ** -0.5)
    qcat = jnp.concatenate([qt[h * d:(h + 1) * d, :] for h in range(r)], axis=-1)
    qb = jnp.concatenate([qcat, jnp.zeros_like(qcat)], axis=0).astype(BF16)
    gl = _sigmoid(glt_ref[...])

    def gate(branch):
        return jnp.concatenate([jnp.broadcast_to(gl[h * N_BRANCH + branch:h * N_BRANCH + branch + 1, :], (d, tq))
                                for h in range(r)], axis=0)

    def stack(o_t):
        return jnp.concatenate([o_t[:, h * tq:(h + 1) * tq] for h in range(r)], axis=0)

    def tile(x):
        return jnp.concatenate([x] * r, axis=-1)

    ones_rows = 16

    def with_ones(vt):
        return jnp.concatenate([vt.astype(BF16), jnp.ones((ones_rows, vt.shape[1]), BF16)], axis=0)

    qpos = q0 + lax.broadcasted_iota(I32, (1, tq), 1)

    span = min(WINDOW + tq, t)
    w0 = pl.multiple_of(jnp.maximum(q0 + tq - span, 0), tq)
    sm = _dot(kw_ref[pl.ds(w0, span), :], qb) + tile(wb_ref[(q0 - w0) // tq])
    p = jnp.exp(sm - jnp.max(sm, axis=0, keepdims=True))
    o_win_t = _dot(with_ones(vwt_ref[:, pl.ds(w0, span)]), p.astype(BF16))
    out = gate(2) * stack(o_win_t[:d] / jnp.maximum(o_win_t[d:d + 1], 1e-30))

    ncp = kc_ref.shape[0]
    nn = lax.broadcasted_iota(I32, (ncp, tq), 0)
    ok = tile(jnp.where((nn * CMP_STRIDE + (CMP_BLOCK - 1) <= qpos) & (nn < n_cmp), 1.0, 0.0))
    sm = _dot(kc_ref[...].astype(BF16), qb) + (ok - 1.0) * (-NEG)
    e = jnp.exp(sm - jnp.max(sm, axis=0, keepdims=True)) * ok
    p_cmp = e / jnp.maximum(jnp.sum(e, axis=0, keepdims=True), 1e-30)
    out = out + gate(0) * stack(_dot(vct_ref[...].astype(BF16), p_cmp.astype(BF16)))

    psum_t = p_cmp[:, :tq]
    for h in range(1, r):
        psum_t = psum_t + p_cmp[:, h * tq:(h + 1) * tq]
    imp_t = jnp.dot(covt_ref[...], psum_t, precision=HIGHEST, preferred_element_type=F32)
    sel_t = _select_blocks_t(imp_t, qpos >> 6, n_sel)
    nb = sel_t.shape[0]
    q_sel = jnp.concatenate([qcat, tile((sel_t - 1.0) * (-NEG)), jnp.zeros((LANES - d - nb, cols), F32)],
                            axis=0).astype(BF16)

    groups = SEL_CHUNK // 8
    sel_class = SEL_CLASS if t % SEL_CLASS == 0 else t

    def sel_branch(n_keys):
        m_run = jnp.full((8, cols), NEG, F32)
        for k0 in range(0, n_keys, SEL_CHUNK):
            k_aug = ks_ref[k0:k0 + SEL_CHUNK, :] + kblk_ref[k0:k0 + SEL_CHUNK, :]
            sm = _dot(k_aug, q_sel)
            if k0 + SEL_CHUNK > n_keys - sel_class:
                keypos = k0 + lax.broadcasted_iota(I32, (SEL_CHUNK, tq), 0)
                sm = sm + tile(jnp.where(keypos <= qpos, 0.0, NEG))
            s_ref[k0:k0 + SEL_CHUNK, :] = sm
            m_run = jnp.maximum(m_run, jnp.max(sm.reshape(groups, 8, cols), axis=0))
        m_sel = jnp.max(m_run, axis=0, keepdims=True)
        acc_t = jnp.zeros((d + ones_rows, cols), F32)
        for k0 in range(0, n_keys, SEL_CHUNK):
            p = jnp.exp(s_ref[k0:k0 + SEL_CHUNK, :] - m_sel)
            acc_t = acc_t + _dot(with_ones(vst_ref[:, k0:k0 + SEL_CHUNK]), p.astype(BF16))
        osel_ref[...] = acc_t[:d] / jnp.maximum(acc_t[d:d + 1], 1e-30)

    cls_id = pl.program_id(2) // (sel_class // tq)
    for cls in range(t // sel_class):
        pl.when(cls_id == cls)(functools.partial(sel_branch, (cls + 1) * sel_class))
    o_ref[...] = (out + gate(1) * stack(osel_ref[...])).T


GATE_ROWS = 16


def nsa_prompt(qt, glt, kc, vct, kpad, kvt, wint):
    b, _, t = qt.shape
    g, r, d = NSA_KV_HEADS, NSA_GROUP, NSA_HEAD_DIM
    tq = Q_BLOCK
    nt = t // tq
    ncp = kc.shape[2]
    n_cmp = t // CMP_STRIDE - 1
    n_sel = -(-t // SEL_BLOCK)
    nb = -(-n_sel // 8) * 8
    covt = _cover_matrix(n_cmp, n_sel, ncp, nb).T
    assert d + nb <= LANES, "block one-hot must fit in the keys' padding lanes"
    kblk = np.zeros((t, LANES), np.float32)
    kblk[np.arange(t), d + np.arange(t) // SEL_BLOCK] = 1.0
    kblk = jnp.asarray(kblk, BF16)
    span = min(WINDOW + tq, t)
    i = np.arange(span)[None, :, None]
    j = np.arange(tq)[None, None, :]
    rel = i - j - np.arange(0, span - tq + 1, tq)[:, None, None]
    wbias = jnp.asarray(np.where((rel <= 0) & (rel > -WINDOW), 0.0, NEG), F32)
    per_bg = lambda shape: pl.BlockSpec((None, None) + shape, lambda bi, gi, qi: (bi, gi, 0, 0))
    const = lambda a: pl.BlockSpec(a.shape, lambda bi, gi, qi: (0,) * a.ndim)
    return pl.pallas_call(
        functools.partial(_nsa_prompt_kernel, n_cmp=n_cmp, n_sel=n_sel),
        grid=(b, g, nt),
        in_specs=[pl.BlockSpec((None, r * d, tq), lambda bi, gi, qi: (bi, gi, qi)),
                  pl.BlockSpec((None, GATE_ROWS, tq), lambda bi, gi, qi: (bi, gi, qi)),
                  per_bg((ncp, LANES)), per_bg((d, ncp)), const(covt), const(kblk), const(wbias),
                  pl.BlockSpec((t, LANES), lambda bi, gi, qi: (bi, gi)),
                  pl.BlockSpec((None, d, t), lambda bi, gi, qi: (bi, 3 * g + gi, 0)),
                  pl.BlockSpec((t, LANES), lambda bi, gi, qi: (bi, g + gi)),
                  pl.BlockSpec((None, d, t), lambda bi, gi, qi: (bi, g + gi, 0))],
        out_specs=pl.BlockSpec((tq, r * d), lambda bi, gi, qi: (bi * nt + qi, gi)),
        out_shape=jax.ShapeDtypeStruct((b * t, NSA_HEADS * d), F32),
        scratch_shapes=[pltpu.VMEM((t, r * tq), F32), pltpu.VMEM((d, r * tq), F32)],
        compiler_params=_cparams("parallel", "parallel", "arbitrary"),
        name="nsa_prompt",
    )(qt, glt, kc, vct, covt, kblk, wbias, kpad, kvt, kpad, wint)


def _nsa_sample_select_kernel(q_ref, kc_ref, vc_ref, cov_ref, oc_ref, sel_ref, *, t, past, n_cmp, n_sel):
    rows = q_ref.shape[0]
    g, r = NSA_KV_HEADS, NSA_GROUP
    qb = q_ref[...].astype(BF16)
    qpos_r = past + (lax.broadcasted_iota(I32, (rows, 1), 0) & (t - 1))
    s = _dot_nt(qb, kc_ref[...].astype(BF16))
    nn = lax.broadcasted_iota(I32, s.shape, 1)
    p_cmp = _masked_softmax(s, (nn * CMP_STRIDE + (CMP_BLOCK - 1) <= qpos_r) & (nn < n_cmp))
    oc_ref[...] = _dot(p_cmp.astype(BF16), vc_ref[...].astype(BF16))
    psum = jnp.sum(p_cmp.reshape(g, r, t, s.shape[1]), axis=1).reshape(g * t, s.shape[1])
    imp = jnp.dot(psum, cov_ref[...], precision=HIGHEST, preferred_element_type=F32)
    qpos_gt = past + (lax.broadcasted_iota(I32, (g * t, 1), 0) & (t - 1))
    sel = _select_blocks(imp, qpos_gt >> 6, n_sel)
    lanes = sel.shape[1]
    sel_ref[...] = jnp.broadcast_to(sel.reshape(g, 1, t, lanes), (g, r, t, lanes)).reshape(rows, lanes)


def nsa_sample_select(q4, cmp, *, t, past):
    b, rows, _ = q4.shape
    n_cmp = cmp.shape[2]
    n_sel = -(-(past + t) // SEL_BLOCK)
    lanes = -(-n_sel // 128) * 128
    cover = _cover_matrix(n_cmp, n_sel, n_cmp, lanes)
    return pl.pallas_call(
        functools.partial(_nsa_sample_select_kernel, t=t, past=past, n_cmp=n_cmp, n_sel=n_sel),
        grid=(b,),
        in_specs=[pl.BlockSpec((None, rows, KVW), lambda bi: (bi, 0, 0)),
                  pl.BlockSpec((None, None, n_cmp, KVW), lambda bi: (0, bi, 0, 0)),
                  pl.BlockSpec((None, None, n_cmp, KVW), lambda bi: (1, bi, 0, 0)),
                  pl.BlockSpec(cover.shape, lambda bi: (0, 0))],
        out_specs=[pl.BlockSpec((None, rows, KVW), lambda bi: (bi, 0, 0)),
                   pl.BlockSpec((None, rows, lanes), lambda bi: (bi, 0, 0))],
        out_shape=[jax.ShapeDtypeStruct((b, rows, KVW), F32),
                   jax.ShapeDtypeStruct((b, rows, lanes), F32)],
        compiler_params=_cparams("parallel"),
        name="nsa_sample_select",
    )(q4, cmp, cmp, cover)


def _nsa_sample_attend_kernel(pt_ref, q_ref, sel_ref, oc_ref, gl_ref, *refs, t, past):
    page_refs = refs[:ATT_PAGES_PER_STEP]
    new_ref, win_ref, eb_ref, o_ref, m_ref, l_ref, acc_ref = refs[ATT_PAGES_PER_STEP:]
    rows = q_ref.shape[0]
    p = pl.program_id(1)
    page = page_refs[0].shape[2]
    width = ATT_PAGES_PER_STEP * page
    qb = q_ref[...].astype(BF16)
    sel = sel_ref[...]
    nblk = sel.shape[1]

    @pl.when(p == 0)
    def _():
        _flash_init(m_ref, l_ref, acc_ref)

    blocks_per_step = width // SEL_BLOCK
    e0 = pl.multiple_of((pl.num_programs(1) - 1 - p) * blocks_per_step, blocks_per_step)
    sel_b = sel.astype(BF16)

    def partial_softmax(prs, lane0):
        n = len(prs) * page
        sc = jnp.concatenate([_dot(qb, pr[0].astype(BF16)) for pr in prs], axis=-1)
        mask = _dot(sel_b, eb_ref[pl.ds(e0, nblk), lane0:lane0 + n]) > 0.5
        sm = jnp.where(mask, sc, NEG)
        m_g = jnp.max(sm, axis=-1, keepdims=True)
        p_g = jnp.where(mask, jnp.exp(sm - m_g), 0.0)
        l_g = jnp.sum(p_g, axis=-1, keepdims=True)
        p_b = p_g.astype(BF16)
        pv = _dot_nt(p_b[:, :page], prs[0][1].astype(BF16))
        for j in range(1, len(prs)):
            pv = pv + _dot_nt(p_b[:, j * page:(j + 1) * page], prs[j][1].astype(BF16))
        return m_g, l_g, pv

    n_half = ATT_PAGES_PER_STEP // 2
    m_a, l_a, pv_a = partial_softmax(page_refs[:n_half], 0)
    m_b, l_b, pv_b = partial_softmax(page_refs[n_half:], n_half * page)
    m_old = m_ref[...]
    m_new = jnp.maximum(m_old, jnp.maximum(m_a, m_b))
    s_old = jnp.exp(m_old - m_new)
    s_a = jnp.exp(m_a - m_new)
    s_b = jnp.exp(m_b - m_new)
    l_ref[...] = s_old * l_ref[...] + s_a * l_a + s_b * l_b
    acc_ref[...] = s_old * acc_ref[...] + s_a * pv_a + s_b * pv_b
    m_ref[...] = m_new

    @pl.when(p == pl.num_programs(1) - 1)
    def _():
        qpos_r = past + (lax.broadcasted_iota(I32, (rows, 1), 0) & (t - 1))
        tp = LANES
        newr = jnp.concatenate([new_ref[...], jnp.zeros((tp - t, new_ref.shape[1]), F32)], axis=0)
        jn = lax.broadcasted_iota(I32, (rows, tp), 1)
        new_ok = (jn < t) & (past + jn <= qpos_r)
        blk_new = past // SEL_BLOCK
        sel_new = sel[:, blk_new:blk_new + 1] > 0.5
        sc_n = _dot_nt(qb, newr[:, 2 * KVW:3 * KVW].astype(BF16))
        _flash_update(sc_n, new_ok & sel_new, newr[:, 3 * KVW:4 * KVW].astype(BF16), m_ref, l_ref, acc_ref)
        o_sel = _flash_result(l_ref, acc_ref)

        n_win = win_ref.shape[2]
        s_a = _dot(qb, win_ref[0].astype(BF16))
        s_b = _dot_nt(qb, newr[:, 4 * KVW:5 * KVW].astype(BF16))
        kp_a = (past - n_win) + lax.broadcasted_iota(I32, (rows, n_win), 1)
        ok_a = (kp_a <= qpos_r) & (kp_a > qpos_r - WINDOW) & (kp_a >= 0)
        ok_b = new_ok & (past + jn > qpos_r - WINDOW)
        sm_a = jnp.where(ok_a, s_a, NEG)
        sm_b = jnp.where(ok_b, s_b, NEG)
        mx = jnp.maximum(jnp.max(sm_a, axis=-1, keepdims=True), jnp.max(sm_b, axis=-1, keepdims=True))
        e_a = jnp.where(ok_a, jnp.exp(sm_a - mx), 0.0)
        e_b = jnp.where(ok_b, jnp.exp(sm_b - mx), 0.0)
        den = jnp.maximum(jnp.sum(e_a, axis=-1, keepdims=True) + jnp.sum(e_b, axis=-1, keepdims=True), 1e-30)
        o_win = (_dot_nt((e_a / den).astype(BF16), win_ref[1].astype(BF16))
                 + _dot((e_b / den).astype(BF16), newr[:, 5 * KVW:6 * KVW].astype(BF16)))

        gates = _sigmoid(gl_ref[...])
        o_ref[...] = gates[:, 0:1] * oc_ref[...] + gates[:, 1:2] * o_sel + gates[:, 2:3] * o_win


def nsa_sample_attend(q4, sel, ocmp, gl, cache_t, page_table, new_rows, win_t, *, t, past):
    b, rows, _ = q4.shape
    n_pages = page_table.shape[1]
    page = cache_t.shape[3]
    lanes = sel.shape[2]
    tp = new_rows.shape[1]
    n_win = win_t.shape[3]
    nps = ATT_PAGES_PER_STEP
    n_steps = n_pages // nps
    blocks_per_step = nps * page // SEL_BLOCK
    shift = (n_steps - 1) * blocks_per_step
    expand = jnp.asarray((np.arange(lanes + shift)[:, None] - shift) == (np.arange(nps * page)[None, :] // SEL_BLOCK), BF16)
    per_b = lambda shape: pl.BlockSpec((None,) + shape, lambda bi, pi, pt: (bi,) + (0,) * len(shape))
    grid_spec = pltpu.PrefetchScalarGridSpec(
        num_scalar_prefetch=1,
        grid=(b, n_pages // nps),
        in_specs=[per_b((rows, KVW)), per_b((rows, lanes)), per_b((rows, KVW)), per_b((rows, N_BRANCH))]
        + _page_specs(nps, 1, page)
        + [per_b((tp, N_KV_ROWS * KVW)), per_b((2, KVW, n_win)),
           pl.BlockSpec(expand.shape, lambda bi, pi, pt: (0, 0))],
        out_specs=per_b((rows, KVW)),
        scratch_shapes=[pltpu.VMEM((rows, 1), F32), pltpu.VMEM((rows, 1), F32), pltpu.VMEM((rows, KVW), F32)],
    )
    return pl.pallas_call(
        functools.partial(_nsa_sample_attend_kernel, t=t, past=past),
        grid_spec=grid_spec,
        out_shape=jax.ShapeDtypeStruct((b, rows, KVW), F32),
        compiler_params=_cparams("parallel", "arbitrary"),
        name="nsa_sample_attend",
    )(page_table, q4, sel, ocmp, gl, *([cache_t] * nps), new_rows, win_t, expand)


def _route_kernel(x_ref, g_ref, sh_ref, sc_ref, rw_ref, rb_ref, h_ref, r_ref):
    h = _norm_mod(x_ref[...], g_ref[...], sh_ref[...], sc_ref[...])
    h_ref[...] = h
    logits = jnp.dot(h, rw_ref[...], precision=HIGHEST, preferred_element_type=F32) + rb_ref[...]
    lane = lax.broadcasted_iota(I32, logits.shape, 1)
    lane_f = lane.astype(F32)
    lg = jnp.where(lane < N_EXPERTS, logits, NEG)
    v1 = jnp.max(lg, axis=-1, keepdims=True)
    i1 = jnp.min(jnp.where(lg == v1, lane_f, 128.0), axis=-1, keepdims=True)
    lg2 = jnp.where(lane_f == i1, NEG, lg)
    v2 = jnp.max(lg2, axis=-1, keepdims=True)
    i2 = jnp.min(jnp.where(lg2 == v2, lane_f, 128.0), axis=-1, keepdims=True)
    e = jnp.exp(v2 - v1)
    w1 = 1.0 / (1.0 + e)
    w2 = e / (1.0 + e)
    r_ref[...] = jnp.where(lane == 0, i1, jnp.where(lane == 1, i2, jnp.where(lane == 2, w1, jnp.where(lane == 3, w2, 0.0))))


def moe_route(x, g, shift, scale, rw_pad, rb_pad, *, per_token, tm, tpb):
    m, d = x.shape
    sh, sh_spec = _mod_arg(shift, per_token, tm, tpb)
    sc, sc_spec = _mod_arg(scale, per_token, tm, tpb)
    return pl.pallas_call(
        _route_kernel,
        grid=(m // tm,),
        in_specs=[pl.BlockSpec((tm, d), lambda i: (i, 0)), _vec_spec(d), sh_spec, sc_spec,
                  pl.BlockSpec((d, 128), lambda i: (0, 0)), _vec_spec(128)],
        out_specs=[pl.BlockSpec((tm, d), lambda i: (i, 0)), pl.BlockSpec((tm, 128), lambda i: (i, 0))],
        out_shape=[jax.ShapeDtypeStruct((m, d), F32), jax.ShapeDtypeStruct((m, 128), F32)],
        compiler_params=_cparams("parallel"),
        name="moe_route",
    )(x, g.reshape(1, d), sh, sc, rw_pad, rb_pad)


def _moe_kernel(te_ref, nv_ref, x_ref, wg_ref, wu_ref, wd_ref, o_ref, xb_ref, acc_ref):
    i = pl.program_id(0)
    f = pl.program_id(1)

    @pl.when(f == 0)
    def _():
        acc_ref[...] = jnp.zeros(acc_ref.shape, F32)
        xb_ref[...] = x_ref[...].astype(BF16)

    @pl.when(i < nv_ref[0])
    def _():
        x = xb_ref[...]
        act = _silu(_dot(x, wg_ref[...])) * _dot(x, wu_ref[...])
        acc_ref[...] += _dot(act.astype(BF16), wd_ref[...])

    @pl.when(f == pl.num_programs(1) - 1)
    def _():
        o_ref[...] = acc_ref[...]


def moe_experts(xs, tile_expert, n_valid, w_gu_b, w_down_b, *, tm, tf):
    p, d = xs.shape
    edim = w_down_b.shape[1]
    nf = edim // tf
    n_tiles = p // tm

    def wmap(off):
        def index(i, f, te, nv):
            ok = i < nv[0]
            return (te[i], 0, off + jnp.where(ok, f, nf - 1))
        return index

    def dmap(i, f, te, nv):
        return (te[i], jnp.where(i < nv[0], f, nf - 1), 0)

    grid_spec = pltpu.PrefetchScalarGridSpec(
        num_scalar_prefetch=2,
        grid=(n_tiles, nf),
        in_specs=[pl.BlockSpec((tm, d), lambda i, f, te, nv: (i, 0)),
                  pl.BlockSpec((None, d, tf), wmap(0)),
                  pl.BlockSpec((None, d, tf), wmap(nf)),
                  pl.BlockSpec((None, tf, d), dmap)],
        out_specs=pl.BlockSpec((tm, d), lambda i, f, te, nv: (i, 0)),
        scratch_shapes=[pltpu.VMEM((tm, d), BF16), pltpu.VMEM((tm, d), F32)],
    )
    return pl.pallas_call(
        _moe_kernel,
        grid_spec=grid_spec,
        out_shape=jax.ShapeDtypeStruct((p, d), F32),
        compiler_params=_cparams("parallel", "arbitrary"),
        name="moe_experts",
    )(tile_expert, n_valid, xs, w_gu_b, w_gu_b, w_down_b)


def _moe_tables(route, tm):
    m = route.shape[0]
    na = TOP_K * m
    e = route[:, :TOP_K].astype(I32).reshape(na)
    onehot = (e[:, None] == jnp.arange(N_EXPERTS, dtype=I32)[None, :]).astype(I32)
    within = jnp.sum((jnp.cumsum(onehot, axis=0) - onehot) * onehot, axis=1)
    counts = jnp.sum(onehot, axis=0)
    padded = ((counts + tm - 1) // tm) * tm
    ends = jnp.cumsum(padded)
    starts = ends - padded
    dest = starts[e] + within
    n_slots = (-(-na // tm) + N_EXPERTS) * tm
    src_tok = jnp.zeros((n_slots,), I32).at[dest].set(jnp.arange(na, dtype=I32) // TOP_K,
                                                       mode='promise_in_bounds', unique_indices=True)
    n_tiles = n_slots // tm
    n_valid = (ends[-1] // tm).astype(I32)
    tile_start = jnp.arange(n_tiles, dtype=I32) * tm
    tile_expert = jnp.sum((tile_start[:, None] >= ends[None, :]).astype(I32), axis=1)
    last = jnp.take(tile_expert, jnp.maximum(n_valid - 1, 0))
    tile_expert = jnp.where(jnp.arange(n_tiles) < n_valid, tile_expert, last).astype(I32)
    return src_tok, tile_expert, n_valid.reshape(1), dest


def _split_mod(mod):
    return [mod[:, i * D_MODEL:(i + 1) * D_MODEL] for i in range(mod.shape[1] // D_MODEL)]


def kernel(x_prompt, x_sample, c_prompt, c_sample, state_ret, cache_kv, cache_win, page_table, w_mod, b_mod, norm_g, ret_w_in, ret_gn_g, ret_w_out, kv_w_mod, kv_b_mod, kv_norm_g, kv_w, cmp_pos, cmp_w1, cmp_b1, cmp_w2, nsa_w_in, nsa_w_out, ffn_w_gu, ffn_w_down, moe_router_w, moe_router_b, moe_w_gu, moe_w_down):
    bp, t, d = x_prompt.shape
    bs, ts, _ = x_sample.shape
    mp, ms = bp * t, bs * ts
    n_pool, page = cache_kv.shape[:2]
    past = page_table.shape[1] * page
    g, r, hd = NSA_KV_HEADS, NSA_GROUP, NSA_HEAD_DIM

    ret_w_in_b = ret_w_in[0].astype(BF16)
    ret_w_out_b = ret_w_out[0].astype(BF16)
    kv_w_b = kv_w.astype(BF16)
    kv_wt_b = kv_w.T.astype(BF16)
    nq = NSA_HEADS * hd
    gate_w = nsa_w_in[0][:, nq:].reshape(d, g, r * N_BRANCH)
    gate_w = jnp.pad(gate_w, ((0, 0), (0, 0), (0, LANES - r * N_BRANCH))).reshape(d, g * LANES)
    nsa_w_in_b = jnp.concatenate([nsa_w_in[0][:, :nq], gate_w], axis=1).astype(BF16)
    gate_wt = nsa_w_in[0][:, nq:].T.reshape(g, r * N_BRANCH, d)
    gate_wt = jnp.pad(gate_wt, ((0, 0), (0, GATE_ROWS - r * N_BRANCH), (0, 0))).reshape(g * GATE_ROWS, d)
    nsa_wt_b = jnp.concatenate([nsa_w_in[0][:, :nq].T, gate_wt], axis=0).astype(BF16)
    def pad_heads(w):
        return jnp.pad(w.reshape(d, g, hd), ((0, 0), (0, 0), (0, LANES - hd))).reshape(d, g * LANES)
    kpad_w_b = jnp.concatenate([pad_heads(kv_w[:, 2 * KVW:3 * KVW]), pad_heads(kv_w[:, 4 * KVW:5 * KVW])],
                               axis=1).astype(BF16)
    nsa_w_out_b = nsa_w_out[0].astype(BF16)
    ffn_w_gu_b = ffn_w_gu[0].astype(BF16)
    ffn_w_down_b = ffn_w_down[0].astype(BF16)
    moe_w_gu_b = moe_w_gu[0].astype(BF16)
    moe_w_down_b = moe_w_down[0].astype(BF16)
    x_prompt, moe_w_gu_b, moe_w_down_b = lax.optimization_barrier((x_prompt, moe_w_gu_b, moe_w_down_b))
    rw_pad = jnp.pad(moe_router_w[0], ((0, 0), (0, 128 - N_EXPERTS)))
    rb_pad = jnp.pad(moe_router_b[0], (0, 128 - N_EXPERTS)).reshape(1, 128)
    cw = _compress_weights(cmp_pos, cmp_w1, cmp_b1, cmp_w2)

    c_all = jnp.concatenate([c_prompt, c_sample], axis=0)
    mods = [cond_matmul(c_all, w_mod, b_mod, layer) for layer in range(w_mod.shape[0])]
    kv_mod = cond_matmul(c_all, kv_w_mod[None], kv_b_mod[None], 0)

    tm_p = min(512, t)
    tm_f = min(1024, t)
    groups = {
        'p': dict(x=x_prompt.reshape(mp, d), b=bp, t=t, per_token=False, tm=tm_p, tpb=t // tm_p,
                  tm_f=tm_f, tpb_f=t // tm_f,
                  mod=lambda a: a[:bp]),
        's': dict(x=x_sample.reshape(ms, d), b=bs, t=ts, per_token=True, tm=ms, tpb=1, tm_f=ms, tpb_f=1,
                  mod=lambda a: jnp.repeat(a[bp:], ts, axis=0)),
    }
    out = {}

    for name, gr in groups.items():
        m0 = [gr['mod'](a) for a in _split_mod(mods[0])]
        if name == 'p':
            pos = jnp.arange(t)
            s0 = jnp.zeros((bp, RET_HEADS, RET_DK, RET_DV), F32)
        else:
            pos = past + jnp.arange(ts)
            s0 = state_ret[0]
        x, s_new = retention_sublayer(gr['x'], norm_g[0, 0], m0[0], m0[1], ret_w_in_b, pos, s0, ret_gn_g[0],
                                      ret_w_out_b, m0[2], norm_g[0, 1], b=gr['b'], per_token=gr['per_token'],
                                      chunk=math.gcd(gr['t'], RET_CHUNK))
        out['ret_' + name] = s_new[None]
        x = ffn_sublayer(x, norm_g[0, 2], m0[3], m0[4], ffn_w_gu_b, ffn_w_down_b, m0[5], norm_g[0, 3],
                         per_token=gr['per_token'], tm=gr['tm_f'], tpb=gr['tpb_f'],
                         tf=_largest_tile(ffn_w_down.shape[1], FFN_HIDDEN_TILE_CAP))
        gr['x1'] = x

    def attention(name):
        gr = groups[name]
        kw = dict(per_token=gr['per_token'], tm=gr['tm'], tpb=gr['tpb'])
        m1 = [gr['mod'](a) for a in _split_mod(mods[1])]
        kvm = [gr['mod'](a) for a in _split_mod(kv_mod)]
        gr['m1'] = m1
        x = gr['x1']
        b_, t_ = gr['b'], gr['t']
        if name == 'p':
            kvt, wint = norm_mod_matmul_t(x, kv_norm_g, kvm[0], kvm[1], kv_wt_b,
                                          (N_PAGED_ROWS * KVW, (N_KV_ROWS - N_PAGED_ROWS) * KVW), b=b_, tm=gr['tm_f'])
            out['kv_p'] = kvt.reshape(b_, N_PAGED_ROWS, g, hd, t_).transpose(0, 4, 1, 2, 3)
            n_keep = min(WINDOW, t_)
            out['win_p'] = wint[:, :, t_ - n_keep:].reshape(b_, 2, g, hd, n_keep).transpose(0, 4, 1, 2, 3)
            kpad = norm_mod_matmul(x, kv_norm_g, kvm[0], kvm[1], kpad_w_b, out_dtype=BF16,
                                   per_token=gr['per_token'], tm=gr['tm_f'], tpb=gr['tpb_f'])
            qt, glt = norm_mod_matmul_t(x, norm_g[1, 0], m1[0], m1[1], nsa_wt_b, (nq, g * GATE_ROWS),
                                        b=b_, tm=gr['tm_f'])
            cmp = compress_prompt(kvt, cw).reshape(2, b_, -1, g, hd)
            kc = jnp.pad(cmp[0].transpose(0, 2, 1, 3), ((0, 0), (0, 0), (0, 0), (0, LANES - hd)))
            o = nsa_prompt(qt, glt, kc, cmp[1].transpose(0, 2, 3, 1), kpad, kvt, wint)
        else:
            proj = norm_mod_matmul(x, norm_g[1, 0], m1[0], m1[1], nsa_w_in_b, **kw)
            rows = norm_mod_matmul(x, kv_norm_g, kvm[0], kvm[1], kv_w_b, **kw)
            q = (proj[:, :NSA_HEADS * hd] * (hd ** -0.5)).reshape(b_, t_, g, r, hd)
            gl = proj[:, NSA_HEADS * hd:].reshape(b_, t_, g, LANES)[..., :r * N_BRANCH]
            gl = gl.reshape(b_, t_, g, r, N_BRANCH)
            rows3 = rows.reshape(b_, t_, N_KV_ROWS * KVW)
            rows6 = rows.reshape(b_, t_, N_KV_ROWS, g, hd)
            out['kv_s'] = rows6[:, :, :N_PAGED_ROWS]
            cache_t = cache_kv.transpose(0, 2, 3, 4, 1).reshape(n_pool, N_PAGED_ROWS, KVW, page)
            win_t5 = cache_win.transpose(0, 2, 3, 4, 1)
            n_win = cache_win.shape[1]
            new_win_t = rows6[:, :, N_PAGED_ROWS:].transpose(0, 2, 3, 4, 1)
            out['win_s'] = jnp.concatenate([win_t5, new_win_t], axis=-1)[..., t_:].transpose(0, 4, 1, 2, 3)
            new_sub = jnp.pad(rows3[:, :, :2 * KVW], ((0, 0), (0, CMP_STRIDE - t_), (0, 0)))
            new_sub = new_sub.reshape(b_, CMP_STRIDE, 2, KVW).transpose(2, 0, 1, 3)
            cmp = compress_sample(cache_t, page_table, new_sub, cw)
            eye = jnp.eye(g, dtype=F32)
            qrows = q.transpose(0, 2, 3, 1, 4).reshape(b_, g, r * t_, hd)
            q4 = jnp.einsum('bgxd,gj->bgxjd', qrows, eye).reshape(b_, g * r * t_, KVW)
            glr = gl.transpose(0, 2, 3, 1, 4).reshape(b_, g * r * t_, N_BRANCH)
            ocmp, sel = nsa_sample_select(q4, cmp, t=t_, past=past)
            o4 = nsa_sample_attend(q4, sel, ocmp, glr, cache_t, page_table, rows3,
                                   win_t5.reshape(b_, 2, KVW, n_win), t=t_, past=past)
            o4 = o4.reshape(b_, g, r, t_, g, hd)
            o = jnp.einsum('bgrtjd,gj->btgrd', o4, eye).reshape(ms, NSA_HEADS * hd)
        gr['x2'] = matmul_norm_residual(o, nsa_w_out_b, x, m1[2], norm_g[1, 1], **kw)

    take = lambda a, idx: a.at[idx].get(mode='promise_in_bounds')

    def moe_plan(name):
        gr = groups[name]
        kw = dict(per_token=gr['per_token'], tm=gr['tm'], tpb=gr['tpb'])
        h, route = moe_route(gr['x2'], norm_g[1, 2], gr['m1'][3], gr['m1'][4], rw_pad, rb_pad, **kw)
        tm_e = 512 if h.shape[0] >= 4096 else 128
        src_tok, tile_expert, n_valid, dest = _moe_tables(route, tm_e)
        return dict(h=h, src_tok=src_tok, route=route, tile_expert=tile_expert, n_valid=n_valid,
                    dest=dest.reshape(-1, TOP_K), tm_e=tm_e)

    def moe_finish(name, dp, xs):
        gr = groups[name]
        kw = dict(per_token=gr['per_token'], tm=gr['tm'], tpb=gr['tpb'])
        ys = moe_experts(xs, dp['tile_expert'], dp['n_valid'], moe_w_gu_b, moe_w_down_b, tm=dp['tm_e'],
                         tf=_largest_tile(moe_w_down.shape[2], MOE_HIDDEN_TILE_CAP))
        out['y_' + name] = moe_combine(take(ys, dp['dest'][:, 0]), take(ys, dp['dest'][:, 1]), dp['route'],
                                       gr['x2'], gr['m1'][5], norm_g[1, 3], **kw)

    attention('p')
    plan_p = moe_plan('p')
    plan_p['src_tok'], groups['s']['x1'] = lax.optimization_barrier((plan_p['src_tok'], groups['s']['x1']))
    xs_p = take(plan_p['h'], plan_p['src_tok'])
    attention('s')
    xs_p, groups['s']['x2'] = lax.optimization_barrier((xs_p, groups['s']['x2']))
    moe_finish('p', plan_p, xs_p)
    plan_s = moe_plan('s')
    moe_finish('s', plan_s, take(plan_s['h'], plan_s['src_tok']))

    return (out['y_p'].reshape(bp, t, d), out['y_s'].reshape(bs, ts, d),
            out['ret_p'], out['ret_s'], out['kv_p'], out['kv_s'], out['win_p'], out['win_s'])
```
